```python
import math
import jax, jax.numpy as jnp
from jax import lax
import numpy as np

D_MODEL = 2048
BATCH = 4
SEQ = 4096
DEPTH = 2

PLE_DIM = 256
BRANCH_W = 1024
N_BRANCH = 3
CONV_W = BRANCH_W
CONV_K = 3
LRU_W = BRANCH_W
LRU_HEADS = 8
LRU_HD = LRU_W // LRU_HEADS
LRU_CONV_K = 4
LRU_C = 8.0
NSA_HEADS = 16
NSA_KV_HEADS = 4
NSA_HD = 64
NSA_W = NSA_HEADS * NSA_HD
KV_W = NSA_KV_HEADS * NSA_HD
N_NSA_BRANCH = 3
CMP_BLK = 32
CMP_STRIDE = 16
CMP_HIDDEN = 128
SLC_BLK = 64
SLC_TOPN = 16
WINDOW = 512
Q_CHUNK = 64
IN_COLS = 3 * CONV_W + 2 * LRU_W + NSA_W + 6 * KV_W + N_NSA_BRANCH * NSA_HEADS + N_BRANCH * D_MODEL
N_EXPERTS = 16
N_GROUPS = 4
EXPERTS_PER_GROUP = N_EXPERTS // N_GROUPS
TOP_K = 2
D_EXPERT = 1024
MOE_BLK = 256
ALPHA = (2 * DEPTH) ** 0.25
BETA = (8 * DEPTH) ** -0.25
LN_EPS = 1e-5
NEG = -1e30
FORCE = 1e9

kernel_name = 'hybrid_conv_lru_nsa_moe_deepnorm'


def layer_norm(x, g, b):
    x32 = x.astype(jnp.float32)
    mu = x32.mean(-1, keepdims=True)
    var = jnp.square(x32 - mu).mean(-1, keepdims=True)
    y = (x32 - mu) * lax.rsqrt(var + LN_EPS) * g.astype(jnp.float32) + b.astype(jnp.float32)
    return y.astype(x.dtype)


def causal_dwconv(u, w, b):
    k, c = w.shape
    y = lax.conv_general_dilated(u, w[:, None, :].astype(u.dtype), window_strides=(1,), padding=[(k - 1, 0)],
                                 dimension_numbers=('NWC', 'WIO', 'NWC'), feature_group_count=c)
    return y + b


def short_conv_mixer(u, gate_b, gate_c, w, b):
    return gate_b * causal_dwconv(gate_c * u, w, b)


def _lin_rec(c1, c2):
    a1, b1 = c1
    a2, b2 = c2
    return a1 * a2, a2 * b1 + b2


def rg_lru_mixer(gate_in, u, conv_w, conv_b, w_a, b_a, w_x, b_x, lam):
    bsz, seq, _ = u.shape
    xc = causal_dwconv(u, conv_w, conv_b)
    xh = xc.reshape(bsz, seq, LRU_HEADS, LRU_HD)
    r = jax.nn.sigmoid(jnp.einsum('bshi,hij->bshj', xh, w_a) + b_a).reshape(bsz, seq, LRU_W)
    i = jax.nn.sigmoid(jnp.einsum('bshi,hij->bshj', xh, w_x) + b_x).reshape(bsz, seq, LRU_W)
    log_a = -LRU_C * r.astype(jnp.float32) * jax.nn.softplus(-lam.astype(jnp.float32))
    a = jnp.exp(log_a)
    b_t = jnp.sqrt(-jnp.expm1(2.0 * log_a)) * (i * xc).astype(jnp.float32)
    _, h = lax.associative_scan(_lin_rec, (a, b_t), axis=1)
    return h.astype(u.dtype) * jax.nn.gelu(gate_in)


def nsa_compress(kv, pos, w1, b1, w2, blk_idx):
    bsz = kv.shape[0]
    nc = blk_idx.shape[0]
    blocks = kv[:, blk_idx] + pos[:, None, :]
    blocks = blocks.transpose(0, 1, 3, 2, 4).reshape(bsz, nc, NSA_KV_HEADS, CMP_BLK * NSA_HD)
    return jax.nn.gelu(blocks @ w1 + b1) @ w2


def nsa_mixer(q, kc_in, vc_in, ks_in, vs_in, kw_in, vw_in, gate_logits, cmp_pos, phi_w1, phi_b1, phi_w2):
    bsz, seq = q.shape[:2]
    dt = q.dtype
    grp = NSA_HEADS // NSA_KV_HEADS
    n_cmp = (seq - CMP_BLK) // CMP_STRIDE + 1
    n_slc = seq // SLC_BLK
    top_n = min(SLC_TOPN, n_slc)
    n_chunks = seq // Q_CHUNK
    c_start = np.arange(n_cmp) * CMP_STRIDE
    blk_idx = c_start[:, None] + np.arange(CMP_BLK)[None, :]
    cmp_end = jnp.asarray(c_start + CMP_BLK - 1, jnp.int32)
    s_start = np.arange(n_slc) * SLC_BLK
    overlap = np.clip(np.minimum(c_start[:, None] + CMP_BLK, s_start[None, :] + SLC_BLK)
                      - np.maximum(c_start[:, None], s_start[None, :]), 0, None)
    cmp_to_slc = jnp.asarray(overlap / CMP_BLK, jnp.float32)
    kc = nsa_compress(kc_in, cmp_pos[0], phi_w1[0], phi_b1[0], phi_w2[0], blk_idx)
    vc = nsa_compress(vc_in, cmp_pos[1], phi_w1[1], phi_b1[1], phi_w2[1], blk_idx)
    ks_blk = ks_in.reshape(bsz, n_slc, SLC_BLK, NSA_KV_HEADS, NSA_HD).transpose(0, 3, 1, 2, 4)
    vs_blk = vs_in.reshape(bsz, n_slc, SLC_BLK, NSA_KV_HEADS, NSA_HD).transpose(0, 3, 1, 2, 4)
    kw_pad = jnp.pad(kw_in, ((0, 0), (WINDOW, 0), (0, 0), (0, 0)))
    vw_pad = jnp.pad(vw_in, ((0, 0), (WINDOW, 0), (0, 0), (0, 0)))
    slopes = jnp.asarray(2.0 ** (-8.0 * np.arange(1, NSA_HEADS + 1) / NSA_HEADS), jnp.float32).reshape(NSA_KV_HEADS, grp)
    q = q * (NSA_HD ** -0.5)
    gates = jax.nn.sigmoid(gate_logits).reshape(bsz, seq, NSA_KV_HEADS, grp, N_NSA_BRANCH)
    b_idx = jnp.arange(bsz)[:, None, None, None]
    g_idx = jnp.arange(NSA_KV_HEADS)[None, :, None, None]
    slc_j = jnp.arange(n_slc)

    def attend_chunk(c):
        start = c * Q_CHUNK
        t = start + jnp.arange(Q_CHUNK)
        qc = lax.dynamic_slice_in_dim(q, start, Q_CHUNK, axis=1).reshape(bsz, Q_CHUNK, NSA_KV_HEADS, grp, NSA_HD)
        dist = (t[:, None] - cmp_end[None, :]).astype(jnp.float32)
        ok = dist >= 0
        s = jnp.einsum('bqgrd,bngd->bgrqn', qc, kc).astype(jnp.float32) - slopes[:, :, None, None] * dist
        p_cmp = jnp.where(ok, jax.nn.softmax(jnp.where(ok, s, NEG), axis=-1), 0.0)
        o_cmp = jnp.einsum('bgrqn,bngd->bqgrd', p_cmp.astype(dt), vc)
        imp = jnp.einsum('bgrqn,nj->bgqj', p_cmp, cmp_to_slc)
        forced = (slc_j[None, :] == 0) | (slc_j[None, :] == (t // SLC_BLK)[:, None])
        causal = slc_j[None, :] * SLC_BLK <= t[:, None]
        imp = jnp.where(forced, FORCE, jnp.where(causal, imp, -FORCE))
        _, idx = lax.top_k(imp, top_n)
        kb = ks_blk[b_idx, g_idx, idx]
        vb = vs_blk[b_idx, g_idx, idx]
        kpos = idx[..., None] * SLC_BLK + jnp.arange(SLC_BLK)
        dist = (t[None, None, :, None, None] - kpos).astype(jnp.float32)[:, :, None]
        s = jnp.einsum('bqgrd,bgqnsd->bgrqns', qc, kb).astype(jnp.float32) - slopes[None, :, :, None, None, None] * dist
        s = jnp.where(dist >= 0, s, NEG).reshape(bsz, NSA_KV_HEADS, grp, Q_CHUNK, top_n * SLC_BLK)
        p_slc = jax.nn.softmax(s, axis=-1).reshape(bsz, NSA_KV_HEADS, grp, Q_CHUNK, top_n, SLC_BLK)
        o_slc = jnp.einsum('bgrqns,bgqnsd->bqgrd', p_slc.astype(dt), vb)
        kw = lax.dynamic_slice_in_dim(kw_pad, start, WINDOW + Q_CHUNK, axis=1)
        vw = lax.dynamic_slice_in_dim(vw_pad, start, WINDOW + Q_CHUNK, axis=1)
        kpos_w = start - WINDOW + jnp.arange(WINDOW + Q_CHUNK)
        dist_w = t[:, None] - kpos_w[None, :]
        ok_w = (dist_w >= 0) & (dist_w < WINDOW) & (kpos_w[None, :] >= 0)
        s = jnp.einsum('bqgrd,bkgd->bgrqk', qc, kw).astype(jnp.float32) - slopes[:, :, None, None] * dist_w.astype(jnp.float32)
        p_win = jax.nn.softmax(jnp.where(ok_w, s, NEG), axis=-1)
        o_win = jnp.einsum('bgrqk,bkgd->bqgrd', p_win.astype(dt), vw)
        g = lax.dynamic_slice_in_dim(gates, start, Q_CHUNK, axis=1)
        o = g[..., 0:1] * o_cmp + g[..., 1:2] * o_slc + g[..., 2:3] * o_win
        return o.reshape(bsz, Q_CHUNK, NSA_W)

    out = lax.map(attend_chunk, jnp.arange(n_chunks))
    return out.transpose(1, 0, 2, 3).reshape(bsz, seq, NSA_W)


def token_mixer(x, w_in, conv_a_w, conv_a_b, lru_conv_w, lru_conv_b, lru_wa, lru_ba, lru_wx, lru_bx, lru_lam,
                cmp_pos, phi_w1, phi_b1, phi_w2, w_branch, w_out):
    bsz, seq, _ = x.shape
    widths = [CONV_W] * 3 + [LRU_W] * 2 + [NSA_W] + [KV_W] * 6 + [N_NSA_BRANCH * NSA_HEADS, N_BRANCH * D_MODEL]
    points = np.cumsum(widths)[:-1].tolist()
    z = x @ w_in
    (a_in, a_b, a_c, r_gate, r_in, q, kc, vc, ks, vs, kw, vw, nsa_g, merge_g) = jnp.split(z, points, axis=-1)
    y_a = short_conv_mixer(a_in, a_b, a_c, conv_a_w, conv_a_b)
    y_b = rg_lru_mixer(r_gate, r_in, lru_conv_w, lru_conv_b, lru_wa, lru_ba, lru_wx, lru_bx, lru_lam)
    kv_heads = lambda t: t.reshape(bsz, seq, NSA_KV_HEADS, NSA_HD)
    y_c = nsa_mixer(q.reshape(bsz, seq, NSA_HEADS, NSA_HD), kv_heads(kc), kv_heads(vc), kv_heads(ks), kv_heads(vs),
                    kv_heads(kw), kv_heads(vw), nsa_g.reshape(bsz, seq, NSA_HEADS, N_NSA_BRANCH),
                    cmp_pos, phi_w1, phi_b1, phi_w2)
    gl = jax.nn.sigmoid(merge_g).reshape(bsz, seq, N_BRANCH, D_MODEL)
    merged = (gl[:, :, 0] * (y_a @ w_branch[0]) + gl[:, :, 1] * (y_b @ w_branch[1])
              + gl[:, :, 2] * (y_c @ w_branch[2]))
    return merged @ w_out


def grouped_moe(x, w_router, b_router, w_gate_up, w_down):
    bsz, seq, d = x.shape
    n_tok = bsz * seq
    n_assign = n_tok * TOP_K
    xf = x.reshape(n_tok, d)
    affinity = jax.nn.sigmoid((xf @ w_router).astype(jnp.float32))
    select = affinity + b_router.astype(jnp.float32)
    group_score = lax.top_k(select.reshape(n_tok, N_GROUPS, EXPERTS_PER_GROUP), 2)[0].sum(-1)
    best_group = jnp.argmax(group_score, axis=-1)
    in_group = (jnp.arange(N_EXPERTS) // EXPERTS_PER_GROUP)[None, :] == best_group[:, None]
    _, expert_idx = lax.top_k(jnp.where(in_group, select, NEG), TOP_K)
    gate = jnp.take_along_axis(affinity, expert_idx, axis=-1)
    gate = gate / gate.sum(-1, keepdims=True)
    e_flat = expert_idx.reshape(n_assign)
    order = jnp.argsort(e_flat)
    e_sorted = e_flat[order]
    tok_sorted = (order // TOP_K).astype(jnp.int32)
    counts = jnp.bincount(e_flat, length=N_EXPERTS)
    padded = (counts + MOE_BLK - 1) // MOE_BLK * MOE_BLK
    pad_end = jnp.cumsum(padded)
    pad_start = pad_end - padded
    raw_start = jnp.cumsum(counts) - counts
    dest = pad_start[e_sorted] + jnp.arange(n_assign) - raw_start[e_sorted]
    n_rows = n_assign + N_EXPERTS * MOE_BLK
    n_blocks = n_rows // MOE_BLK
    row_tok = jnp.zeros((n_rows,), jnp.int32).at[dest].set(tok_sorted)
    block_expert = jnp.minimum(jnp.searchsorted(pad_end, jnp.arange(n_blocks) * MOE_BLK, side='right'), N_EXPERTS - 1)

    def expert_block(b):
        xb = xf[lax.dynamic_slice_in_dim(row_tok, b * MOE_BLK, MOE_BLK)]
        e = block_expert[b]
        g, u = jnp.split(xb @ w_gate_up[e], 2, axis=-1)
        return (jax.nn.silu(g) * u) @ w_down[e]

    rows = lax.map(expert_block, jnp.arange(n_blocks)).reshape(n_rows, d)
    dest_orig = jnp.zeros((n_assign,), dest.dtype).at[order].set(dest)
    y = rows[dest_orig].reshape(n_tok, TOP_K, d)
    return jnp.einsum('tk,tkd->td', gate.astype(x.dtype), y).reshape(bsz, seq, d)


def setup_inputs(seed: int = 0) -> dict:
    key = jax.random.key(seed)
    ks = jax.random.split(key, 26)
    f32 = jnp.float32

    def nrm(k, shape, scale):
        return jax.random.normal(k, shape, f32) * scale

    u = jax.random.uniform(ks[11], (DEPTH, LRU_W), f32, 0.9, 0.999)
    a0 = u ** (1.0 / LRU_C)
    return {
        'x': nrm(ks[0], (BATCH, SEQ, D_MODEL), 1.0),
        'p': nrm(ks[1], (DEPTH, BATCH, SEQ, PLE_DIM), 1.0),
        'w_in': nrm(ks[2], (DEPTH, D_MODEL, IN_COLS), D_MODEL ** -0.5),
        'conv_a_w': nrm(ks[3], (DEPTH, CONV_K, CONV_W), 0.5),
        'conv_a_b': nrm(ks[4], (DEPTH, CONV_W), 0.02),
        'lru_conv_w': nrm(ks[5], (DEPTH, LRU_CONV_K, LRU_W), 0.5),
        'lru_conv_b': nrm(ks[6], (DEPTH, LRU_W), 0.02),
        'lru_wa': nrm(ks[7], (DEPTH, LRU_HEADS, LRU_HD, LRU_HD), LRU_HD ** -0.5),
        'lru_ba': nrm(ks[8], (DEPTH, LRU_HEADS, LRU_HD), 0.02),
        'lru_wx': nrm(ks[9], (DEPTH, LRU_HEADS, LRU_HD, LRU_HD), LRU_HD ** -0.5),
        'lru_bx': nrm(ks[10], (DEPTH, LRU_HEADS, LRU_HD), 0.02),
        'lru_lam': jnp.log(a0) - jnp.log1p(-a0),
        'cmp_pos': nrm(ks[12], (DEPTH, 2, CMP_BLK, NSA_HD), 0.02),
        'phi_w1': nrm(ks[13], (DEPTH, 2, CMP_BLK * NSA_HD, CMP_HIDDEN), (CMP_BLK * NSA_HD) ** -0.5),
        'phi_b1': nrm(ks[14], (DEPTH, 2, CMP_HIDDEN), 0.02),
        'phi_w2': nrm(ks[15], (DEPTH, 2, CMP_HIDDEN, NSA_HD), CMP_HIDDEN ** -0.5),
        'w_branch': nrm(ks[16], (DEPTH, N_BRANCH, BRANCH_W, D_MODEL), BRANCH_W ** -0.5),
        'w_out': nrm(ks[17], (DEPTH, D_MODEL, D_MODEL), BETA * D_MODEL ** -0.5),
        'ln_g': 1.0 + nrm(ks[18], (DEPTH, 2, D_MODEL), 0.02),
        'ln_b': nrm(ks[19], (DEPTH, 2, D_MODEL), 0.02),
        'w_router': nrm(ks[20], (D_MODEL, N_EXPERTS), D_MODEL ** -0.5),
        'b_router': nrm(ks[21], (N_EXPERTS,), 0.01),
        'w_gate_up': nrm(ks[22], (DEPTH, N_EXPERTS, D_MODEL, 2 * D_EXPERT), D_MODEL ** -0.5),
        'w_down': nrm(ks[23], (DEPTH, N_EXPERTS, D_EXPERT, D_MODEL), BETA * D_EXPERT ** -0.5),
        'w_ple': nrm(ks[24], (DEPTH, PLE_DIM, D_MODEL), BETA * PLE_DIM ** -0.5),
        'w_ple_gate': nrm(ks[25], (DEPTH, D_MODEL, D_MODEL), D_MODEL ** -0.5),
    }


def reference(x, p, w_in, conv_a_w, conv_a_b, lru_conv_w, lru_conv_b, lru_wa, lru_ba, lru_wx, lru_bx, lru_lam,
              cmp_pos, phi_w1, phi_b1, phi_w2, w_branch, w_out, ln_g, ln_b, w_router, b_router,
              w_gate_up, w_down, w_ple, w_ple_gate):
    for i in range(DEPTH):
        mix = token_mixer(x, w_in[i], conv_a_w[i], conv_a_b[i], lru_conv_w[i], lru_conv_b[i], lru_wa[i], lru_ba[i],
                          lru_wx[i], lru_bx[i], lru_lam[i], cmp_pos[i], phi_w1[i], phi_b1[i], phi_w2[i],
                          w_branch[i], w_out[i])
        x = layer_norm(ALPHA * x + mix, ln_g[i, 0], ln_b[i, 0])
        ffn = grouped_moe(x, w_router, b_router, w_gate_up[i], w_down[i])
        ple = jax.nn.sigmoid(x @ w_ple_gate[i]) * (p[i] @ w_ple[i])
        x = layer_norm(ALPHA * x + ffn + ple, ln_g[i, 1], ln_b[i, 1])
    return x
```

```python
import functools

import ml_dtypes
import numpy as np
import jax
import jax.numpy as jnp
from jax import lax
from jax.experimental import pallas as pl
from jax.experimental.pallas import tpu as pltpu

F32 = jnp.float32
BF16 = jnp.bfloat16

BRANCH_W = 1024
CONV_K = 3
LRU_HEADS = 8
LRU_HD = BRANCH_W // LRU_HEADS
LRU_CONV_K = 4
LRU_C = 8.0
NSA_HEADS = 16
NSA_KV_HEADS = 4
NSA_GRP = NSA_HEADS // NSA_KV_HEADS
NSA_HD = 64
NSA_W = NSA_HEADS * NSA_HD
KV_W = NSA_KV_HEADS * NSA_HD
CMP_BLK = 32
CMP_STRIDE = 16
CMP_HIDDEN = 128
SLC_BLK = 64
SLC_TOPN = 16
WINDOW = 512
N_EXPERTS = 16
N_GROUPS = 4
EXPERTS_PER_GROUP = N_EXPERTS // N_GROUPS
TOP_K = 2
LN_EPS = 1e-5
NEG = -1e30
FORCE = 1e9

LANES = 128
SUBLANES = 8
VMEM_LIMIT = 56 * 1024 * 1024

FEAT = 256
MASK_OFF = NSA_HD
ALIBI_OFF = 2 * NSA_HD

ATT_TQ = 128
ATT_TK = 512
MOE_BLK = 256


def _cparams(sem):
    return pltpu.CompilerParams(dimension_semantics=sem, vmem_limit_bytes=VMEM_LIMIT)


def _mm_kernel(x_ref, w_ref, o_ref):
    o_ref[...] = jnp.dot(x_ref[...], w_ref[...], preferred_element_type=F32).astype(o_ref.dtype)


def matmul(x, w, out_dtype, tm, tn):
    m, k = x.shape
    n = w.shape[1]
    assert m % tm == 0 and n % tn == 0
    return pl.pallas_call(
        _mm_kernel,
        grid=(m // tm, n // tn),
        in_specs=[pl.BlockSpec((tm, k), lambda i, j: (i, 0)),
                  pl.BlockSpec((k, tn), lambda i, j: (0, j))],
        out_specs=pl.BlockSpec((tm, tn), lambda i, j: (i, j)),
        out_shape=jax.ShapeDtypeStruct((m, n), out_dtype),
        compiler_params=_cparams(("parallel", "parallel")),
        name="matmul",
    )(x, w)


def _causal_conv(ext, cur, w, k):
    acc = cur * w[k - 1:k, :]
    for j in range(k - 1):
        shift = k - 1 - j
        acc = acc + pltpu.roll(ext, shift, 0)[SUBLANES:, :] * w[j:j + 1, :]
    return acc


def _conv_mixer_kernel(ain_ref, ab_ref, ac_ref, w_ref, b_ref, o_ref, halo_ref):
    @pl.when(pl.program_id(2) == 0)
    def _():
        halo_ref[...] = jnp.zeros_like(halo_ref)

    v = ac_ref[0].astype(F32) * ain_ref[0].astype(F32)
    ext = jnp.concatenate([halo_ref[...], v], axis=0)
    y = _causal_conv(ext, v, w_ref[...], CONV_K) + b_ref[...]
    o_ref[0] = (ab_ref[0].astype(F32) * y).astype(o_ref.dtype)
    halo_ref[...] = v[v.shape[0] - SUBLANES:, :]


def conv_mixer(z3, conv_w, conv_b, ts=512, cw=512):
    b, s, _ = z3.shape
    nc = BRANCH_W // cw
    return pl.pallas_call(
        _conv_mixer_kernel,
        grid=(b, nc, s // ts),
        in_specs=[pl.BlockSpec((1, ts, cw), lambda bi, c, si: (bi, si, c)),
                  pl.BlockSpec((1, ts, cw), lambda bi, c, si: (bi, si, c + nc)),
                  pl.BlockSpec((1, ts, cw), lambda bi, c, si: (bi, si, c + 2 * nc)),
                  pl.BlockSpec((CONV_K, cw), lambda bi, c, si: (0, c)),
                  pl.BlockSpec((1, cw), lambda bi, c, si: (0, c))],
        out_specs=pl.BlockSpec((1, ts, cw), lambda bi, c, si: (bi, si, c)),
        out_shape=jax.ShapeDtypeStruct((b, s, BRANCH_W), BF16),
        scratch_shapes=[pltpu.VMEM((SUBLANES, cw), F32)],
        compiler_params=_cparams(("parallel", "parallel", "arbitrary")),
        name="conv_mixer",
    )(z3, z3, z3, conv_w, conv_b.reshape(1, BRANCH_W))


def _lru_kernel(gate_ref, rin_ref, cw_ref, cb_ref, wa_ref, ba_ref, wx_ref, bx_ref, lam_ref, o_ref,
                halo_ref, carry_ref, a_ref, b_ref, h_ref):
    ts, lw = a_ref.shape

    @pl.when(pl.program_id(2) == 0)
    def _():
        halo_ref[...] = jnp.zeros_like(halo_ref)
        carry_ref[...] = jnp.zeros_like(carry_ref)

    u = rin_ref[0].astype(F32)
    ext = jnp.concatenate([halo_ref[...], u], axis=0)
    xc = _causal_conv(ext, u, cw_ref[...], LRU_CONV_K) + cb_ref[...]
    halo_ref[...] = u[ts - SUBLANES:, :]

    xcb = xc.astype(BF16)
    ra, ix = [], []
    for hh in range(lw // LRU_HD):
        xh = xcb[:, hh * LRU_HD:(hh + 1) * LRU_HD]
        ra.append(jnp.dot(xh, wa_ref[hh].astype(BF16), preferred_element_type=F32))
        ix.append(jnp.dot(xh, wx_ref[hh].astype(BF16), preferred_element_type=F32))
    r = jax.nn.sigmoid(jnp.concatenate(ra, axis=1) + ba_ref[...])
    i = jax.nn.sigmoid(jnp.concatenate(ix, axis=1) + bx_ref[...])
    neg_lam = -lam_ref[...]
    softplus = jnp.maximum(neg_lam, 0.0) + jnp.log1p(jnp.exp(-jnp.abs(neg_lam)))
    log_a = -LRU_C * r * softplus
    a_ref[...] = jnp.exp(log_a)
    th = jnp.tanh(log_a)
    b_ref[...] = jnp.sqrt(-2.0 * th / (1.0 - th)) * (i * xc)

    row = lax.broadcasted_iota(jnp.int32, (SUBLANES, lw), 0)

    def group(g, hc):
        sl = pl.ds(pl.multiple_of(g * SUBLANES, SUBLANES), SUBLANES)
        a = a_ref[sl, :]
        bv = b_ref[sl, :]
        for d in (1, 2, 4):
            keep = row >= d
            bv = jnp.where(keep, a * pltpu.roll(bv, d, 0) + bv, bv)
            a = jnp.where(keep, a * pltpu.roll(a, d, 0), a)
        h = a * hc + bv
        h_ref[sl, :] = h
        return jnp.broadcast_to(h[SUBLANES - 1:SUBLANES, :], (SUBLANES, lw))

    carry_ref[...] = lax.fori_loop(0, ts // SUBLANES, group, carry_ref[...])
    o_ref[0] = (h_ref[...] * jax.nn.gelu(gate_ref[0].astype(F32))).astype(o_ref.dtype)


def lru_mixer(z3, col0, conv_w, conv_b, wa, ba, wx, bx, lam, ts=512, lw=256):
    b, s, _ = z3.shape
    nc = BRANCH_W // lw
    c0 = col0 // lw
    hpb = lw // LRU_HD
    row = lambda v: v.reshape(1, BRANCH_W)
    vec = pl.BlockSpec((1, lw), lambda bi, c, si: (0, c))
    mat = pl.BlockSpec((hpb, LRU_HD, LRU_HD), lambda bi, c, si: (c, 0, 0))
    return pl.pallas_call(
        _lru_kernel,
        grid=(b, nc, s // ts),
        in_specs=[pl.BlockSpec((1, ts, lw), lambda bi, c, si: (bi, si, c0 + c)),
                  pl.BlockSpec((1, ts, lw), lambda bi, c, si: (bi, si, c0 + nc + c)),
                  pl.BlockSpec((LRU_CONV_K, lw), lambda bi, c, si: (0, c)),
                  vec, mat, vec, mat, vec, vec],
        out_specs=pl.BlockSpec((1, ts, lw), lambda bi, c, si: (bi, si, c)),
        out_shape=jax.ShapeDtypeStruct((b, s, BRANCH_W), BF16),
        scratch_shapes=[pltpu.VMEM((SUBLANES, lw), F32), pltpu.VMEM((SUBLANES, lw), F32),
                        pltpu.VMEM((ts, lw), F32), pltpu.VMEM((ts, lw), F32), pltpu.VMEM((ts, lw), F32)],
        compiler_params=_cparams(("parallel", "parallel", "arbitrary")),
        name="rg_lru",
    )(z3, z3, conv_w, row(conv_b), wa, row(ba), wx, row(bx), row(lam))


def _compress_kernel(slots_ref, pos_ref, w1_ref, b1_ref, w2_ref, o_ref):
    g, nslot, half = slots_ref.shape[2:]
    rows = g * nslot
    x = slots_ref[0, 0].reshape(rows, half)
    w1 = w1_ref[0].astype(BF16)
    first = jnp.dot(x, w1[:half], preferred_element_type=F32)
    second = jnp.dot(x, w1[half:], preferred_element_type=F32)
    pos = jnp.broadcast_to(pos_ref[0], (SUBLANES, 2 * half)).astype(BF16)
    posb = jnp.dot(pos, w1, preferred_element_type=F32)[0:1, :]
    hidden = first + pltpu.roll(second, rows - 1, 0) + posb + b1_ref[0]
    out = jnp.dot(jax.nn.gelu(hidden).astype(BF16), w2_ref[0].astype(BF16), preferred_element_type=F32)
    o_ref[0, 0] = out.reshape(g, nslot, NSA_HD).astype(o_ref.dtype)


def nsa_compress(slots, cmp_pos, w1, b1, w2):
    _, b, g, nslot, half = slots.shape
    pos = cmp_pos.reshape(2, 1, CMP_BLK * NSA_HD)
    return pl.pallas_call(
        _compress_kernel,
        grid=(2, b),
        in_specs=[pl.BlockSpec((1, 1, g, nslot, half), lambda kv, bi: (kv, bi, 0, 0, 0)),
                  pl.BlockSpec((1, 1, 2 * half), lambda kv, bi: (kv, 0, 0)),
                  pl.BlockSpec((1, 2 * half, CMP_HIDDEN), lambda kv, bi: (kv, 0, 0)),
                  pl.BlockSpec((1, 1, CMP_HIDDEN), lambda kv, bi: (kv, 0, 0)),
                  pl.BlockSpec((1, CMP_HIDDEN, NSA_HD), lambda kv, bi: (kv, 0, 0))],
        out_specs=pl.BlockSpec((1, 1, g, nslot, NSA_HD), lambda kv, bi: (kv, bi, 0, 0, 0)),
        out_shape=jax.ShapeDtypeStruct((2, b, g, nslot, NSA_HD), BF16),
        compiler_params=_cparams(("parallel", "parallel")),
        name="nsa_compress",
    )(slots, pos, w1, b1.reshape(2, 1, CMP_HIDDEN), w2)


def _softmax_rows(s):
    mx = jnp.max(s, axis=1, keepdims=True)
    p = jnp.exp(s - mx)
    return p, jnp.sum(p, axis=1, keepdims=True)


def _nsa_kernel(q_ref, qf_ref, kc_ref, vc_ref, ks_ref, vs_ref, kw_ref, vw_ref, gl_ref, c2s_ref, o_ref,
                qp_ref, imp_ref, *, n_cmp):
    tq = q_ref.shape[1]
    rows = NSA_GRP * tq
    seq = ks_ref.shape[3]
    n_slc = seq // SLC_BLK
    t0 = pl.program_id(2) * tq

    qp_ref[...] = qf_ref[0]
    qblk = q_ref[0]
    for r in range(NSA_GRP):
        qp_ref[r * tq:(r + 1) * tq, 0:NSA_HD] = qblk[:, r * NSA_HD:(r + 1) * NSA_HD] * (NSA_HD ** -0.5)
    qp = qp_ref[...]

    ncp = kc_ref.shape[3]
    s = jnp.dot(qp, kc_ref[0, 0], preferred_element_type=F32)
    t_row = t0 + lax.broadcasted_iota(jnp.int32, (rows, ncp), 0) % tq
    n_col = lax.broadcasted_iota(jnp.int32, (rows, ncp), 1)
    ok = (t_row >= n_col * CMP_STRIDE + (CMP_BLK - 1)) & (n_col < n_cmp)
    p, l = _softmax_rows(jnp.where(ok, s, NEG))
    p = jnp.where(ok, p, 0.0)
    l = jnp.sum(p, axis=1, keepdims=True)
    p_cmp = p / jnp.where(l > 0.0, l, 1.0)
    o_cmp = jnp.dot(p_cmp.astype(BF16), vc_ref[0, 0], preferred_element_type=F32)

    psum = p_cmp[0:tq]
    for r in range(1, NSA_GRP):
        psum = psum + p_cmp[r * tq:(r + 1) * tq]
    imp = jnp.zeros((n_slc, tq), F32)
    rest = psum
    for _ in range(3):
        part = rest.astype(BF16)
        rest = rest - part.astype(F32)
        imp = imp + lax.dot_general(c2s_ref[...], part, (((1,), (1,)), ((), ())), preferred_element_type=F32)
    j_row = lax.broadcasted_iota(jnp.int32, (n_slc, tq), 0)
    t_col = t0 + lax.broadcasted_iota(jnp.int32, (n_slc, tq), 1)
    forced = (j_row == 0) | (j_row == t_col // SLC_BLK)
    causal = j_row * SLC_BLK <= t_col
    imp = jnp.where(forced, FORCE, jnp.where(causal, imp, -FORCE))
    imp_ref[...] = imp

    def count(i, cnt):
        bi = jnp.broadcast_to(imp_ref[pl.ds(i, 1), :], (n_slc, tq))
        ahead = (bi > imp) | ((bi == imp) & (j_row > i))
        return cnt + jnp.where(ahead, 1, 0)

    rank = lax.fori_loop(0, n_slc, count, jnp.zeros((n_slc, tq), jnp.int32))
    maskT = jnp.where(rank < min(SLC_TOPN, n_slc), 0.0, NEG)
    mask = maskT.T.astype(BF16)
    for r in range(NSA_GRP):
        qp_ref[r * tq:(r + 1) * tq, MASK_OFF:MASK_OFF + n_slc] = mask
    qp = qp_ref[...]

    tk = min(ATT_TK, seq)
    t_q = t0 + lax.broadcasted_iota(jnp.int32, (rows, tk), 0) % tq
    k_off = lax.broadcasted_iota(jnp.int32, (rows, tk), 1)

    def slc_step(kt, carry, diagonal):
        m_run, l_run, acc = carry
        k0 = pl.multiple_of(kt * tk, tk)
        s = jnp.dot(qp, ks_ref[0, 0, :, pl.ds(k0, tk)], preferred_element_type=F32)
        if diagonal:
            s = jnp.where(k0 + k_off <= t_q, s, NEG)
        m_new = jnp.maximum(m_run, jnp.max(s, axis=1, keepdims=True))
        alpha = jnp.exp(m_run - m_new)
        p = jnp.exp(s - m_new)
        l_new = alpha * l_run + jnp.sum(p, axis=1, keepdims=True)
        pv = jnp.dot(p.astype(BF16), vs_ref[0, 0, pl.ds(k0, tk), :], preferred_element_type=F32)
        return m_new, l_new, alpha * acc + pv

    last = (t0 + tq - 1) // tk
    init = (jnp.full((rows, 1), NEG, F32), jnp.zeros((rows, 1), F32), jnp.zeros((rows, NSA_HD), F32))
    carry = lax.fori_loop(0, last, functools.partial(slc_step, diagonal=False), init)
    _, l_slc, acc_slc = slc_step(last, carry, diagonal=True)
    o_slc = acc_slc / l_slc

    wk = WINDOW + tq
    start = pl.multiple_of(jnp.maximum(t0 - WINDOW, 0), tq)
    s = jnp.dot(qp, kw_ref[0, 0, :, pl.ds(start, wk)], preferred_element_type=F32)
    dist = (t0 + lax.broadcasted_iota(jnp.int32, (rows, wk), 0) % tq) - (
        start + lax.broadcasted_iota(jnp.int32, (rows, wk), 1))
    p, l = _softmax_rows(jnp.where((dist >= 0) & (dist < WINDOW), s, NEG))
    o_win = jnp.dot(p.astype(BF16), vw_ref[0, 0, pl.ds(start, wk), :], preferred_element_type=F32) / l

    gates = jax.nn.sigmoid(gl_ref[0, 0].astype(F32))
    outs = []
    for r in range(NSA_GRP):
        sl = slice(r * tq, (r + 1) * tq)
        outs.append(gates[:, 3 * r:3 * r + 1] * o_cmp[sl] + gates[:, 3 * r + 1:3 * r + 2] * o_slc[sl]
                    + gates[:, 3 * r + 2:3 * r + 3] * o_win[sl])
    o_ref[0] = jnp.concatenate(outs, axis=1).astype(o_ref.dtype)


def _alibi_slopes():
    return (2.0 ** (-8.0 * np.arange(1, NSA_HEADS + 1) / NSA_HEADS)).astype(np.float32)


def _bf16_split3(v):
    parts, rest = [], np.asarray(v, np.float32)
    for _ in range(3):
        part = rest.astype(ml_dtypes.bfloat16).astype(np.float32)
        parts.append(part)
        rest = (rest.astype(np.float64) - part.astype(np.float64)).astype(np.float32)
    return parts


def _query_features(tq):
    hi, mid, lo = _bf16_split3(_alibi_slopes())
    feat = np.zeros((NSA_HEADS, FEAT), np.float32)
    for c, part in enumerate((hi, mid, lo, hi, mid, lo)):
        feat[:, ALIBI_OFF + c] = part
    feat = feat.reshape(NSA_KV_HEADS, NSA_GRP, 1, FEAT)
    feat = np.broadcast_to(feat, (NSA_KV_HEADS, NSA_GRP, tq, FEAT)).reshape(NSA_KV_HEADS, NSA_GRP * tq, FEAT)
    return jnp.asarray(feat, BF16)


def _key_features(pos, n_slc):
    pos = np.asarray(pos)
    feat = np.zeros((FEAT - NSA_HD, pos.shape[0]), np.float32)
    if n_slc:
        feat[pos // SLC_BLK, np.arange(pos.shape[0])] = 1.0
    a0 = ALIBI_OFF - NSA_HD
    feat[a0:a0 + 3] = (pos // SLC_BLK) * SLC_BLK
    feat[a0 + 3:a0 + 6] = pos % SLC_BLK
    return feat


def _with_features(k, feat):
    b, g, n, _ = k.shape
    kt = jnp.swapaxes(k, 2, 3).astype(BF16)
    f = jnp.broadcast_to(jnp.asarray(feat, BF16), (b, g) + feat.shape)
    return jnp.concatenate([kt, f], axis=2)


def nsa_attention(z3, qcol0, kc, vc, ks, vs, kw, vw, gate_logits):
    b, s, _ = z3.shape
    tq = ATT_TQ
    assert s % tq == 0 and s >= WINDOW + tq and s % SLC_BLK == 0 and tq % SLC_BLK == 0
    assert s % min(ATT_TK, s) == 0 and min(ATT_TK, s) % tq == 0
    n_slc = s // SLC_BLK
    assert n_slc <= NSA_HD
    n_cmp = (s - CMP_BLK) // CMP_STRIDE + 1
    ncp = kc.shape[2]
    gw = NSA_GRP * NSA_HD
    rows = NSA_GRP * tq

    cmp_end = np.arange(ncp) * CMP_STRIDE + CMP_BLK - 1
    kc_t = _with_features(kc, _key_features(cmp_end, 0))
    seq_feat = _key_features(np.arange(s), n_slc)
    ks_t = _with_features(ks, seq_feat)
    win_feat = seq_feat.copy()
    win_feat[:n_slc] = 0.0
    kw_t = _with_features(kw, win_feat)

    c_start = np.arange(ncp) * CMP_STRIDE
    s_start = np.arange(n_slc) * SLC_BLK
    overlap = np.clip(np.minimum(c_start[:, None] + CMP_BLK, s_start[None, :] + SLC_BLK)
                      - np.maximum(c_start[:, None], s_start[None, :]), 0, None) / CMP_BLK
    overlap[n_cmp:] = 0.0
    c2s_t = jnp.asarray(overlap.T, BF16)

    kv_spec = lambda shape: pl.BlockSpec((1, 1) + shape, lambda bi, g, qi: (bi, g, 0, 0))
    return pl.pallas_call(
        functools.partial(_nsa_kernel, n_cmp=n_cmp),
        grid=(b, NSA_KV_HEADS, s // tq),
        in_specs=[pl.BlockSpec((1, tq, gw), lambda bi, g, qi: (bi, qi, qcol0 // gw + g)),
                  pl.BlockSpec((1, rows, FEAT), lambda bi, g, qi: (g, 0, 0)),
                  kv_spec((FEAT, ncp)), kv_spec((ncp, NSA_HD)),
                  kv_spec((FEAT, s)), kv_spec((s, NSA_HD)),
                  kv_spec((FEAT, s)), kv_spec((s, NSA_HD)),
                  pl.BlockSpec((1, 1, tq, NSA_GRP * 3), lambda bi, g, qi: (bi, g, qi, 0)),
                  pl.BlockSpec((n_slc, ncp), lambda bi, g, qi: (0, 0))],
        out_specs=pl.BlockSpec((1, tq, gw), lambda bi, g, qi: (bi, qi, g)),
        out_shape=jax.ShapeDtypeStruct((b, s, NSA_W), BF16),
        scratch_shapes=[pltpu.VMEM((rows, FEAT), BF16), pltpu.VMEM((n_slc, tq), F32)],
        compiler_params=_cparams(("parallel", "parallel", "arbitrary")),
        name="nsa_attention",
    )(z3, _query_features(tq), kc_t, vc, ks_t, vs, kw_t, vw, gate_logits, c2s_t)


def _merge_kernel(ya_ref, yb_ref, yc_ref, wb_ref, ga_ref, gb_ref, gc_ref, o_ref):
    acc = None
    for i, (y_ref, g_ref) in enumerate(((ya_ref, ga_ref), (yb_ref, gb_ref), (yc_ref, gc_ref))):
        term = jax.nn.sigmoid(g_ref[...].astype(F32)) * jnp.dot(y_ref[...], wb_ref[i], preferred_element_type=F32)
        acc = term if acc is None else acc + term
    o_ref[...] = acc.astype(o_ref.dtype)


def branch_merge(ya, yb, yc, wb, zmerge, tm=1024, tn=512):
    t = ya.shape[0]
    d = wb.shape[2]
    nj = d // tn
    y_spec = pl.BlockSpec((tm, BRANCH_W), lambda i, j: (i, 0))
    gate = lambda br: pl.BlockSpec((tm, tn), lambda i, j: (i, j + br * nj))
    return pl.pallas_call(
        _merge_kernel,
        grid=(t // tm, nj),
        in_specs=[y_spec, y_spec, y_spec,
                  pl.BlockSpec((3, BRANCH_W, tn), lambda i, j: (0, 0, j)),
                  gate(0), gate(1), gate(2)],
        out_specs=pl.BlockSpec((tm, tn), lambda i, j: (i, j)),
        out_shape=jax.ShapeDtypeStruct((t, d), BF16),
        compiler_params=_cparams(("parallel", "parallel")),
        name="branch_merge",
    )(ya, yb, yc, wb, zmerge, zmerge, zmerge)


def _layer_norm(y, g, b):
    mu = jnp.mean(y, axis=-1, keepdims=True)
    yc = y - mu
    var = jnp.mean(yc * yc, axis=-1, keepdims=True)
    return yc * lax.rsqrt(var + LN_EPS) * g + b


def _outproj_ln_kernel(m_ref, w_ref, x_ref, g_ref, b_ref, o_ref, ob_ref, *, alpha):
    mix = jnp.dot(m_ref[...], w_ref[...], preferred_element_type=F32)
    y = _layer_norm(alpha * x_ref[...] + mix, g_ref[...], b_ref[...])
    o_ref[...] = y
    ob_ref[...] = y.astype(BF16)


def outproj_ln(merged, w_out, x, g, b, alpha, tm=256):
    t, d = x.shape
    row = pl.BlockSpec((tm, d), lambda i: (i, 0))
    vec = pl.BlockSpec((1, d), lambda i: (0, 0))
    return pl.pallas_call(
        functools.partial(_outproj_ln_kernel, alpha=alpha),
        grid=(t // tm,),
        in_specs=[row, pl.BlockSpec((d, d), lambda i: (0, 0)), row, vec, vec],
        out_specs=[row, row],
        out_shape=[jax.ShapeDtypeStruct((t, d), F32), jax.ShapeDtypeStruct((t, d), BF16)],
        compiler_params=_cparams(("parallel",)),
        name="outproj_ln",
    )(merged, w_out, x, g.reshape(1, d), b.reshape(1, d))


def _final_kernel(x_ref, xb_ref, wg_ref, p_ref, wp_ref, y0_ref, y1_ref, gate_ref, g_ref, b_ref, o_ref, ob_ref,
                  *, alpha):
    ple = jax.nn.sigmoid(jnp.dot(xb_ref[...], wg_ref[...], preferred_element_type=F32)) * jnp.dot(
        p_ref[...], wp_ref[...], preferred_element_type=F32)
    gate = gate_ref[...]
    ffn = gate[:, 0:1] * y0_ref[...].astype(F32) + gate[:, 1:2] * y1_ref[...].astype(F32)
    y = _layer_norm(alpha * x_ref[...] + ffn + ple, g_ref[...], b_ref[...])
    o_ref[...] = y
    ob_ref[...] = y.astype(BF16)


def ple_combine_ln(x, xb, w_ple_gate, p, w_ple, y0, y1, gate, g, b, alpha, tm=256):
    t, d = x.shape
    pd = p.shape[1]
    row = pl.BlockSpec((tm, d), lambda i: (i, 0))
    vec = pl.BlockSpec((1, d), lambda i: (0, 0))
    return pl.pallas_call(
        functools.partial(_final_kernel, alpha=alpha),
        grid=(t // tm,),
        in_specs=[row, row, pl.BlockSpec((d, d), lambda i: (0, 0)),
                  pl.BlockSpec((tm, pd), lambda i: (i, 0)), pl.BlockSpec((pd, d), lambda i: (0, 0)),
                  row, row, pl.BlockSpec((tm, TOP_K), lambda i: (i, 0)), vec, vec],
        out_specs=[row, row],
        out_shape=[jax.ShapeDtypeStruct((t, d), F32), jax.ShapeDtypeStruct((t, d), BF16)],
        compiler_params=_cparams(("parallel",)),
        name="ple_combine_ln",
    )(x, xb, w_ple_gate, p, w_ple, y0, y1, gate, g.reshape(1, d), b.reshape(1, d))


def _router_kernel(x_ref, wr_ref, br_ref, idx_ref, gate_ref):
    tm = x_ref.shape[0]
    logits = lax.dot_general(wr_ref[...], x_ref[...], (((1,), (1,)), ((), ())),
                             precision=lax.Precision.HIGHEST, preferred_element_type=F32)
    aff = jax.nn.sigmoid(logits)
    sel = aff + br_ref[...]
    scores = []
    for g in range(N_GROUPS):
        v = [sel[g * EXPERTS_PER_GROUP + e:g * EXPERTS_PER_GROUP + e + 1, :] for e in range(EXPERTS_PER_GROUP)]
        best = None
        for a in range(EXPERTS_PER_GROUP):
            for c in range(a + 1, EXPERTS_PER_GROUP):
                pair = v[a] + v[c]
                best = pair if best is None else jnp.maximum(best, pair)
        scores.append(best)
    best_score, best_group = scores[0], jnp.zeros((1, tm), jnp.int32)
    for g in range(1, N_GROUPS):
        better = scores[g] > best_score
        best_score = jnp.where(better, scores[g], best_score)
        best_group = jnp.where(better, g, best_group)
    e_row = lax.broadcasted_iota(jnp.int32, (N_EXPERTS, tm), 0)
    cand = jnp.where(e_row // EXPERTS_PER_GROUP == best_group, sel, NEG)
    picks, gates = [], []
    for _ in range(TOP_K):
        mx = jnp.max(cand, axis=0, keepdims=True)
        pick = jnp.min(jnp.where(cand == mx, e_row, N_EXPERTS), axis=0, keepdims=True)
        hit = e_row == pick
        picks.append(pick)
        gates.append(jnp.sum(jnp.where(hit, aff, 0.0), axis=0, keepdims=True))
        cand = jnp.where(hit, -jnp.inf, cand)
    total = gates[0] + gates[1]
    idx_ref[...] = jnp.concatenate(picks, axis=0)
    gate_ref[...] = jnp.concatenate([gates[0] / total, gates[1] / total], axis=0)


def router(x, w_router, b_router, tm=1024):
    t, d = x.shape
    return pl.pallas_call(
        _router_kernel,
        grid=(t // tm,),
        in_specs=[pl.BlockSpec((tm, d), lambda i: (i, 0)),
                  pl.BlockSpec((N_EXPERTS, d), lambda i: (0, 0)),
                  pl.BlockSpec((N_EXPERTS, 1), lambda i: (0, 0))],
        out_specs=[pl.BlockSpec((TOP_K, tm), lambda i: (0, i)), pl.BlockSpec((TOP_K, tm), lambda i: (0, i))],
        out_shape=[jax.ShapeDtypeStruct((TOP_K, t), jnp.int32), jax.ShapeDtypeStruct((TOP_K, t), F32)],
        compiler_params=_cparams(("parallel",)),
        name="router",
    )(x, w_router.T, b_router.reshape(N_EXPERTS, 1))


def _expert_kernel(be_ref, x_ref, wgu_ref, wd_ref, o_ref):
    de = wd_ref.shape[1]
    gu = jnp.dot(x_ref[...], wgu_ref[0], preferred_element_type=F32)
    act = (jax.nn.silu(gu[:, :de]) * gu[:, de:]).astype(BF16)
    o_ref[...] = jnp.dot(act, wd_ref[0], preferred_element_type=F32).astype(o_ref.dtype)


def expert_ffn(block_expert, x_sorted, w_gate_up, w_down):
    n_rows, d = x_sorted.shape
    de = w_down.shape[1]
    return pl.pallas_call(
        _expert_kernel,
        grid_spec=pltpu.PrefetchScalarGridSpec(
            num_scalar_prefetch=1,
            grid=(n_rows // MOE_BLK,),
            in_specs=[pl.BlockSpec((MOE_BLK, d), lambda i, be: (i, 0)),
                      pl.BlockSpec((1, d, 2 * de), lambda i, be: (be[i], 0, 0)),
                      pl.BlockSpec((1, de, d), lambda i, be: (be[i], 0, 0))],
            out_specs=pl.BlockSpec((MOE_BLK, d), lambda i, be: (i, 0))),
        out_shape=jax.ShapeDtypeStruct((n_rows, d), BF16),
        compiler_params=_cparams(("arbitrary",)),
        name="expert_ffn",
    )(block_expert, x_sorted, w_gate_up, w_down)


def _dispatch_tables(expert_idx):
    n_tok = expert_idx.shape[0]
    n_assign = n_tok * TOP_K
    e_flat = expert_idx.reshape(n_assign)
    onehot = (e_flat[:, None] == jnp.arange(N_EXPERTS)[None, :]).astype(jnp.int32)
    csum = jnp.cumsum(onehot, axis=0)
    rank = jnp.take_along_axis(csum, e_flat[:, None], axis=1)[:, 0] - 1
    counts = csum[-1]
    padded = (counts + MOE_BLK - 1) // MOE_BLK * MOE_BLK
    pad_end = jnp.cumsum(padded)
    dest = (pad_end - padded)[e_flat] + rank
    n_rows = n_assign + N_EXPERTS * MOE_BLK
    row_tok = jnp.zeros((n_rows,), jnp.int32).at[dest].set(jnp.arange(n_assign, dtype=jnp.int32) // TOP_K)
    block_expert = jnp.minimum(
        jnp.searchsorted(pad_end, jnp.arange(n_rows // MOE_BLK) * MOE_BLK, side="right"), N_EXPERTS - 1)
    return row_tok, block_expert.astype(jnp.int32), dest.reshape(n_tok, TOP_K)


def _token_mixer(xb, bsz, seq, w_in, conv_a_w, conv_a_b, lru_conv_w, lru_conv_b, lru_wa, lru_ba, lru_wx, lru_bx,
                 lru_lam, cmp_pos, phi_w1, phi_b1, phi_w2, w_branch):
    d = xb.shape[1]
    n_tok = bsz * seq
    main_w = 3 * BRANCH_W + 2 * BRANCH_W + NSA_W + 6 * KV_W
    ng = 3 * NSA_HEADS
    kv0 = main_w - 6 * KV_W
    w_main = w_in[:, :main_w].astype(BF16)
    w_gate = jnp.pad(w_in[:, main_w:main_w + ng], ((0, 0), (0, LANES - ng))).astype(BF16)
    w_merge = w_in[:, main_w + ng:].astype(BF16)

    zmain = matmul(xb, w_main, BF16, 1024, 1536)
    zgate = matmul(xb, w_gate, F32, 1024, LANES)
    zmerge = matmul(xb, w_merge, BF16, 1024, 1024)
    z3 = zmain.reshape(bsz, seq, main_w)

    y_a = conv_mixer(z3, conv_a_w, conv_a_b)
    y_b = lru_mixer(z3, 3 * BRANCH_W, lru_conv_w, lru_conv_b, lru_wa, lru_ba, lru_wx, lru_bx, lru_lam)

    kv = z3[:, :, kv0:].reshape(bsz, seq, 6, NSA_KV_HEADS, NSA_HD).transpose(2, 0, 3, 1, 4)
    nslot = seq // CMP_STRIDE
    slots = kv[0:2].reshape(2, bsz, NSA_KV_HEADS, nslot, CMP_STRIDE * NSA_HD)
    cmp = nsa_compress(slots, cmp_pos, phi_w1, phi_b1, phi_w2)
    gate_logits = zgate[:, :ng].reshape(bsz, seq, NSA_KV_HEADS, NSA_GRP * 3).transpose(0, 2, 1, 3)
    y_c = nsa_attention(z3, 5 * BRANCH_W, cmp[0], cmp[1], kv[2], kv[3], kv[4], kv[5], gate_logits)

    flat = lambda y: y.reshape(n_tok, BRANCH_W)
    return branch_merge(flat(y_a), flat(y_b), flat(y_c), w_branch.astype(BF16), zmerge)


def _moe(x, xb, w_router, b_router, w_gate_up, w_down):
    idx_t, gate_t = router(x, w_router, b_router)
    row_tok, block_expert, dest = _dispatch_tables(idx_t.T)
    rows = expert_ffn(block_expert, xb[row_tok], w_gate_up.astype(BF16), w_down.astype(BF16))
    return rows[dest[:, 0]], rows[dest[:, 1]], gate_t.T


def kernel(x, p, w_in, conv_a_w, conv_a_b, lru_conv_w, lru_conv_b, lru_wa, lru_ba, lru_wx, lru_bx, lru_lam, cmp_pos, phi_w1, phi_b1, phi_w2, w_branch, w_out, ln_g, ln_b, w_router, b_router, w_gate_up, w_down, w_ple, w_ple_gate):
    bsz, seq, d = x.shape
    depth = w_in.shape[0]
    n_tok = bsz * seq
    alpha = (2 * depth) ** 0.25
    xf = x.reshape(n_tok, d)
    xb = xf.astype(BF16)
    for i in range(depth):
        merged = _token_mixer(xb, bsz, seq, w_in[i], conv_a_w[i], conv_a_b[i], lru_conv_w[i], lru_conv_b[i],
                              lru_wa[i], lru_ba[i], lru_wx[i], lru_bx[i], lru_lam[i], cmp_pos[i], phi_w1[i],
                              phi_b1[i], phi_w2[i], w_branch[i])
        xf, xb = outproj_ln(merged, w_out[i].astype(BF16), xf, ln_g[i, 0], ln_b[i, 0], alpha)
        y0, y1, gate = _moe(xf, xb, w_router, b_router, w_gate_up[i], w_down[i])
        xf, xb = ple_combine_ln(xf, xb, w_ple_gate[i].astype(BF16), p[i].reshape(n_tok, -1).astype(BF16),
                                w_ple[i].astype(BF16), y0, y1, gate, ln_g[i, 1], ln_b[i, 1], alpha)
    return xf.reshape(bsz, seq, d)
```

```python
import functools

import ml_dtypes
import numpy as np
import jax
import jax.numpy as jnp
from jax import lax
from jax.experimental import pallas as pl
from jax.experimental.pallas import tpu as pltpu

F32 = jnp.float32
BF16 = jnp.bfloat16

BRANCH_W = 1024
CONV_K = 3
LRU_HEADS = 8
LRU_HD = BRANCH_W // LRU_HEADS
LRU_CONV_K = 4
LRU_C = 8.0
NSA_HEADS = 16
NSA_KV_HEADS = 4
NSA_GRP = NSA_HEADS // NSA_KV_HEADS
NSA_HD = 64
NSA_W = NSA_HEADS * NSA_HD
KV_W = NSA_KV_HEADS * NSA_HD
CMP_BLK = 32
CMP_STRIDE = 16
CMP_HIDDEN = 128
SLC_BLK = 64
SLC_TOPN = 16
WINDOW = 512
N_EXPERTS = 16
N_GROUPS = 4
EXPERTS_PER_GROUP = N_EXPERTS // N_GROUPS
TOP_K = 2
LN_EPS = 1e-5
NEG = -1e30
FORCE = 1e9

LANES = 128
SUBLANES = 8
VMEM_LIMIT = 56 * 1024 * 1024

FEAT = 256
MASK_OFF = NSA_HD
ALIBI_OFF = 2 * NSA_HD
PAD_OFF = ALIBI_OFF + 6

ATT_TQ = 128
ATT_TK = 512
MOE_BLK = 256


def _cparams(sem):
    return pltpu.CompilerParams(dimension_semantics=sem, vmem_limit_bytes=VMEM_LIMIT)


def _mm_kernel(x_ref, w_ref, o_ref):
    o_ref[...] = jnp.dot(x_ref[...], w_ref[...], preferred_element_type=F32).astype(o_ref.dtype)


def matmul(x, w, out_dtype, tm, tn):
    m, k = x.shape
    n = w.shape[1]
    assert m % tm == 0 and n % tn == 0
    return pl.pallas_call(
        _mm_kernel,
        grid=(m // tm, n // tn),
        in_specs=[pl.BlockSpec((tm, k), lambda i, j: (i, 0)),
                  pl.BlockSpec((k, tn), lambda i, j: (0, j))],
        out_specs=pl.BlockSpec((tm, tn), lambda i, j: (i, j)),
        out_shape=jax.ShapeDtypeStruct((m, n), out_dtype),
        compiler_params=_cparams(("parallel", "parallel")),
        name="matmul",
    )(x, w)


def _causal_conv(ext, cur, w, k):
    acc = cur * w[k - 1:k, :]
    for j in range(k - 1):
        shift = k - 1 - j
        acc = acc + pltpu.roll(ext, shift, 0)[SUBLANES:, :] * w[j:j + 1, :]
    return acc


def _conv_mixer_kernel(ain_ref, ab_ref, ac_ref, w_ref, b_ref, o_ref, halo_ref):
    @pl.when(pl.program_id(2) == 0)
    def _():
        halo_ref[...] = jnp.zeros_like(halo_ref)

    v = ac_ref[0].astype(F32) * ain_ref[0].astype(F32)
    ext = jnp.concatenate([halo_ref[...], v], axis=0)
    y = _causal_conv(ext, v, w_ref[...], CONV_K) + b_ref[...]
    o_ref[0] = (ab_ref[0].astype(F32) * y).astype(o_ref.dtype)
    halo_ref[...] = v[v.shape[0] - SUBLANES:, :]


def conv_mixer(z3, conv_w, conv_b, ts=512, cw=512):
    b, s, _ = z3.shape
    nc = BRANCH_W // cw
    return pl.pallas_call(
        _conv_mixer_kernel,
        grid=(b, nc, s // ts),
        in_specs=[pl.BlockSpec((1, ts, cw), lambda bi, c, si: (bi, si, c)),
                  pl.BlockSpec((1, ts, cw), lambda bi, c, si: (bi, si, c + nc)),
                  pl.BlockSpec((1, ts, cw), lambda bi, c, si: (bi, si, c + 2 * nc)),
                  pl.BlockSpec((CONV_K, cw), lambda bi, c, si: (0, c)),
                  pl.BlockSpec((1, cw), lambda bi, c, si: (0, c))],
        out_specs=pl.BlockSpec((1, ts, cw), lambda bi, c, si: (bi, si, c)),
        out_shape=jax.ShapeDtypeStruct((b, s, BRANCH_W), BF16),
        scratch_shapes=[pltpu.VMEM((SUBLANES, cw), F32)],
        compiler_params=_cparams(("parallel", "parallel", "arbitrary")),
        name="conv_mixer",
    )(z3, z3, z3, conv_w, conv_b.reshape(1, BRANCH_W))


def _lru_kernel(gate_ref, rin_ref, cw_ref, cb_ref, wa_ref, ba_ref, wx_ref, bx_ref, lam_ref, o_ref,
                halo_ref, carry_ref, a_ref, b_ref, h_ref):
    ts, lw = a_ref.shape

    @pl.when(pl.program_id(2) == 0)
    def _():
        halo_ref[...] = jnp.zeros_like(halo_ref)
        carry_ref[...] = jnp.zeros_like(carry_ref)

    u = rin_ref[0].astype(F32)
    ext = jnp.concatenate([halo_ref[...], u], axis=0)
    xc = _causal_conv(ext, u, cw_ref[...], LRU_CONV_K) + cb_ref[...]
    halo_ref[...] = u[ts - SUBLANES:, :]

    xcb = xc.astype(BF16)
    ra, ix = [], []
    for hh in range(lw // LRU_HD):
        xh = xcb[:, hh * LRU_HD:(hh + 1) * LRU_HD]
        ra.append(jnp.dot(xh, wa_ref[hh].astype(BF16), preferred_element_type=F32))
        ix.append(jnp.dot(xh, wx_ref[hh].astype(BF16), preferred_element_type=F32))
    r = jax.nn.sigmoid(jnp.concatenate(ra, axis=1) + ba_ref[...])
    i = jax.nn.sigmoid(jnp.concatenate(ix, axis=1) + bx_ref[...])
    neg_lam = -lam_ref[...]
    softplus = jnp.maximum(neg_lam, 0.0) + jnp.log1p(jnp.exp(-jnp.abs(neg_lam)))
    log_a = -LRU_C * r * softplus
    a_ref[...] = jnp.exp(log_a)
    th = jnp.tanh(log_a)
    b_ref[...] = jnp.sqrt(-2.0 * th / (1.0 - th)) * (i * xc)

    row = lax.broadcasted_iota(jnp.int32, (SUBLANES, lw), 0)

    def group(g, hc):
        sl = pl.ds(pl.multiple_of(g * SUBLANES, SUBLANES), SUBLANES)
        a = a_ref[sl, :]
        bv = b_ref[sl, :]
        for d in (1, 2, 4):
            keep = row >= d
            bv = jnp.where(keep, a * pltpu.roll(bv, d, 0) + bv, bv)
            a = jnp.where(keep, a * pltpu.roll(a, d, 0), a)
        h = a * hc + bv
        h_ref[sl, :] = h
        return jnp.broadcast_to(h[SUBLANES - 1:SUBLANES, :], (SUBLANES, lw))

    carry_ref[...] = lax.fori_loop(0, ts // SUBLANES, group, carry_ref[...])
    o_ref[0] = (h_ref[...] * jax.nn.gelu(gate_ref[0].astype(F32))).astype(o_ref.dtype)


def lru_mixer(z3, col0, conv_w, conv_b, wa, ba, wx, bx, lam, ts=512, lw=256):
    b, s, _ = z3.shape
    nc = BRANCH_W // lw
    c0 = col0 // lw
    hpb = lw // LRU_HD
    row = lambda v: v.reshape(1, BRANCH_W)
    vec = pl.BlockSpec((1, lw), lambda bi, c, si: (0, c))
    mat = pl.BlockSpec((hpb, LRU_HD, LRU_HD), lambda bi, c, si: (c, 0, 0))
    return pl.pallas_call(
        _lru_kernel,
        grid=(b, nc, s // ts),
        in_specs=[pl.BlockSpec((1, ts, lw), lambda bi, c, si: (bi, si, c0 + c)),
                  pl.BlockSpec((1, ts, lw), lambda bi, c, si: (bi, si, c0 + nc + c)),
                  pl.BlockSpec((LRU_CONV_K, lw), lambda bi, c, si: (0, c)),
                  vec, mat, vec, mat, vec, vec],
        out_specs=pl.BlockSpec((1, ts, lw), lambda bi, c, si: (bi, si, c)),
        out_shape=jax.ShapeDtypeStruct((b, s, BRANCH_W), BF16),
        scratch_shapes=[pltpu.VMEM((SUBLANES, lw), F32), pltpu.VMEM((SUBLANES, lw), F32),
                        pltpu.VMEM((ts, lw), F32), pltpu.VMEM((ts, lw), F32), pltpu.VMEM((ts, lw), F32)],
        compiler_params=_cparams(("parallel", "parallel", "arbitrary")),
        name="rg_lru",
    )(z3, z3, conv_w, row(conv_b), wa, row(ba), wx, row(bx), row(lam))


def _compress_kernel(slots_ref, pos_ref, w1_ref, b1_ref, w2_ref, o_ref):
    g, nslot, half = slots_ref.shape[2:]
    rows = g * nslot
    x = slots_ref[0, 0].reshape(rows, half)
    w1 = w1_ref[0].astype(BF16)
    first = jnp.dot(x, w1[:half], preferred_element_type=F32)
    second = jnp.dot(x, w1[half:], preferred_element_type=F32)
    pos = jnp.broadcast_to(pos_ref[0], (SUBLANES, 2 * half)).astype(BF16)
    posb = jnp.dot(pos, w1, preferred_element_type=F32)[0:1, :]
    hidden = first + pltpu.roll(second, rows - 1, 0) + posb + b1_ref[0]
    out = jnp.dot(jax.nn.gelu(hidden).astype(BF16), w2_ref[0].astype(BF16), preferred_element_type=F32)
    o_ref[0, 0] = out.reshape(g, nslot, NSA_HD).astype(o_ref.dtype)


def nsa_compress(slots, cmp_pos, w1, b1, w2):
    _, b, g, nslot, half = slots.shape
    pos = cmp_pos.reshape(2, 1, CMP_BLK * NSA_HD)
    return pl.pallas_call(
        _compress_kernel,
        grid=(2, b),
        in_specs=[pl.BlockSpec((1, 1, g, nslot, half), lambda kv, bi: (kv, bi, 0, 0, 0)),
                  pl.BlockSpec((1, 1, 2 * half), lambda kv, bi: (kv, 0, 0)),
                  pl.BlockSpec((1, 2 * half, CMP_HIDDEN), lambda kv, bi: (kv, 0, 0)),
                  pl.BlockSpec((1, 1, CMP_HIDDEN), lambda kv, bi: (kv, 0, 0)),
                  pl.BlockSpec((1, CMP_HIDDEN, NSA_HD), lambda kv, bi: (kv, 0, 0))],
        out_specs=pl.BlockSpec((1, 1, g, nslot, NSA_HD), lambda kv, bi: (kv, bi, 0, 0, 0)),
        out_shape=jax.ShapeDtypeStruct((2, b, g, nslot, NSA_HD), BF16),
        compiler_params=_cparams(("parallel", "parallel")),
        name="nsa_compress",
    )(slots, pos, w1, b1.reshape(2, 1, CMP_HIDDEN), w2)


def _nsa_kernel(q_ref, qf_ref, kc_ref, vc_ref, ks_ref, vs_ref, kw_ref, vw_ref, gl_ref, c2s_ref, bias_ref, o_ref,
                qp_ref, imp_ref, *, n_cmp):
    tq = q_ref.shape[1]
    lanes = NSA_GRP * tq
    seq = ks_ref.shape[2]
    n_slc = seq // SLC_BLK
    t0 = pl.multiple_of(pl.program_id(2) * tq, tq)
    head = lambda r: slice(r * tq, (r + 1) * tq)

    def scores(k_tile, bias=None):
        s = jnp.dot(k_tile, qp_ref[...], preferred_element_type=F32)
        return s if bias is None else s + bias

    qt = q_ref[0].astype(F32).T * (NSA_HD ** -0.5)
    qp_ref[...] = qf_ref[0]
    for r in range(NSA_GRP):
        qp_ref[0:NSA_HD, head(r)] = qt[r * NSA_HD:(r + 1) * NSA_HD, :].astype(BF16)

    s_cmp = scores(kc_ref[0, 0])
    win = []
    for off, size, bias in ((0, tq, bias_ref[1]), (tq, WINDOW - tq, None), (WINDOW, tq, bias_ref[0])):
        k0 = pl.multiple_of(t0 + off, tq)
        win.append((scores(kw_ref[0, 0, pl.ds(k0, size), :], bias), vw_ref[0, 0, :, pl.ds(k0, size)]))

    ncp = kc_ref.shape[2]
    n_row = lax.broadcasted_iota(jnp.int32, (ncp, lanes), 0)
    t_lane = t0 + lax.broadcasted_iota(jnp.int32, (ncp, lanes), 1) % tq
    ok = (t_lane >= n_row * CMP_STRIDE + (CMP_BLK - 1)) & (n_row < n_cmp)
    s = jnp.where(ok, s_cmp, NEG)
    p = jnp.where(ok, jnp.exp(s - jnp.max(s, axis=0, keepdims=True)), 0.0)
    l = jnp.sum(p, axis=0, keepdims=True)
    p_cmp = p * (1.0 / jnp.where(l > 0.0, l, 1.0))
    o_cmp = jnp.dot(vc_ref[0, 0], p_cmp.astype(BF16), preferred_element_type=F32)

    rest = p_cmp[:, head(0)]
    for r in range(1, NSA_GRP):
        rest = rest + p_cmp[:, head(r)]
    imp = jnp.zeros((n_slc, tq), F32)
    for _ in range(3):
        part = rest.astype(BF16)
        rest = rest - part.astype(F32)
        imp = imp + jnp.dot(c2s_ref[...], part, preferred_element_type=F32)
    j_row = lax.broadcasted_iota(jnp.int32, (n_slc, tq), 0)
    t_col = t0 + lax.broadcasted_iota(jnp.int32, (n_slc, tq), 1)
    forced = (j_row == 0) | (j_row == t_col // SLC_BLK)
    imp = jnp.where(forced, FORCE, jnp.where(j_row * SLC_BLK <= t_col, imp, -FORCE))
    imp_ref[...] = imp

    mx = functools.reduce(jnp.maximum, [jnp.max(s, axis=0, keepdims=True) for s, _ in win])
    l_win = jnp.zeros((1, lanes), F32)
    o_win = jnp.zeros((NSA_HD, lanes), F32)
    for s, v_tile in win:
        p = jnp.exp(s - mx)
        l_win = l_win + jnp.sum(p, axis=0, keepdims=True)
        o_win = o_win + jnp.dot(v_tile, p.astype(BF16), preferred_element_type=F32)
    o_win = o_win * (1.0 / l_win)

    n_grp = n_slc // SUBLANES
    vals = [imp[v * SUBLANES:(v + 1) * SUBLANES, :] for v in range(n_grp)]
    sub = lax.broadcasted_iota(jnp.int32, (SUBLANES, tq), 0)
    rank = [jnp.zeros((SUBLANES, tq), jnp.int32) for _ in range(n_grp)]
    for i in range(n_slc):
        bi = jnp.broadcast_to(imp_ref[i:i + 1, :], (SUBLANES, tq))
        for v in range(n_grp):
            if v < i // SUBLANES:
                ahead = bi > vals[v]
            elif v > i // SUBLANES:
                ahead = bi >= vals[v]
            else:
                ahead = (bi > vals[v]) | ((bi == vals[v]) & (sub > i % SUBLANES))
            rank[v] = rank[v] + jnp.where(ahead, 1, 0)
    mask = jnp.where(jnp.concatenate(rank, axis=0) < min(SLC_TOPN, n_slc), 0.0, NEG).astype(BF16)
    for r in range(NSA_GRP):
        qp_ref[MASK_OFF:MASK_OFF + n_slc, head(r)] = mask

    tk = min(ATT_TK, seq)

    def online(carry, tiles):
        ss = []
        for k0, bias in tiles:
            s = scores(ks_ref[0, 0, pl.ds(k0, tk), :], bias)
            ss.append((s, vs_ref[0, 0, :, pl.ds(k0, tk)]))
        out = list(carry)
        for c, (s, v_tile) in enumerate(ss):
            m_run, l_run, acc = carry[c]
            m_new = jnp.maximum(m_run, jnp.max(s, axis=0, keepdims=True))
            alpha = jnp.exp(m_run - m_new)
            p = jnp.exp(s - m_new)
            l_new = alpha * l_run + jnp.sum(p, axis=0, keepdims=True)
            pv = jnp.dot(v_tile, p.astype(BF16), preferred_element_type=F32)
            out[c] = (m_new, l_new, alpha * acc + pv)
        return tuple(out)

    def pair(i, carry):
        k0 = pl.multiple_of(2 * i * tk, tk)
        return online(carry, ((k0, None), (k0 + tk, None)))

    init = (jnp.full((1, lanes), NEG, F32), jnp.zeros((1, lanes), F32), jnp.zeros((NSA_HD, lanes), F32))
    n_full = t0 // tk
    carry = lax.fori_loop(0, n_full // 2, pair, (init, init))
    k_last = pl.multiple_of(n_full * tk, tk)
    key_pos = k_last + lax.broadcasted_iota(jnp.int32, (tk, lanes), 0)
    qry_pos = t0 + lax.broadcasted_iota(jnp.int32, (tk, lanes), 1) % tq
    causal = jnp.where(key_pos <= qry_pos, 0.0, NEG)
    carry = lax.cond(n_full % 2 == 1,
                     lambda c: online(c, ((pl.multiple_of(k_last - tk, tk), None), (k_last, causal))),
                     lambda c: online(c, ((k_last, causal),)),
                     carry)
    (m_e, l_e, acc_e), (m_o, l_o, acc_o) = carry
    m_all = jnp.maximum(m_e, m_o)
    w_e = jnp.exp(m_e - m_all)
    w_o = jnp.exp(m_o - m_all)
    o_slc = (w_e * acc_e + w_o * acc_o) * (1.0 / (w_e * l_e + w_o * l_o))

    gates = jax.nn.sigmoid(gl_ref[0, 0].astype(F32))
    outs = []
    for r in range(NSA_GRP):
        outs.append(gates[3 * r:3 * r + 1] * o_cmp[:, head(r)] + gates[3 * r + 1:3 * r + 2] * o_slc[:, head(r)]
                    + gates[3 * r + 2:3 * r + 3] * o_win[:, head(r)])
    o_ref[0] = jnp.concatenate(outs, axis=0).T.astype(o_ref.dtype)


def _alibi_slopes():
    return (2.0 ** (-8.0 * np.arange(1, NSA_HEADS + 1) / NSA_HEADS)).astype(np.float32)


def _bf16_split3(v):
    parts, rest = [], np.asarray(v, np.float32)
    for _ in range(3):
        part = rest.astype(ml_dtypes.bfloat16).astype(np.float32)
        parts.append(part)
        rest = (rest.astype(np.float64) - part.astype(np.float64)).astype(np.float32)
    return parts


def _query_features(tq):
    feat = np.zeros((NSA_HEADS, FEAT), np.float32)
    hi, mid, lo = _bf16_split3(_alibi_slopes())
    for c, part in enumerate((hi, mid, lo, hi, mid, lo)):
        feat[:, ALIBI_OFF + c] = part
    feat[:, PAD_OFF] = NEG
    feat = np.repeat(feat.reshape(NSA_KV_HEADS, NSA_GRP, FEAT), tq, axis=1)
    return jnp.asarray(feat.transpose(0, 2, 1), BF16)


def _key_features(pos, n_slc):
    pos = np.asarray(pos)
    real = pos >= 0
    feat = np.zeros((pos.shape[0], FEAT - NSA_HD), np.float32)
    if n_slc:
        feat[real, pos[real] // SLC_BLK] = 1.0
    a0 = ALIBI_OFF - NSA_HD
    feat[real, a0:a0 + 3] = ((pos[real] // SLC_BLK) * SLC_BLK)[:, None]
    feat[real, a0 + 3:a0 + 6] = (pos[real] % SLC_BLK)[:, None]
    feat[~real, PAD_OFF - NSA_HD] = 1.0
    return feat


def _with_features(k, feat):
    b, g, _, _ = k.shape
    f = jnp.broadcast_to(jnp.asarray(feat, BF16), (b, g) + feat.shape)
    return jnp.concatenate([k.astype(BF16), f], axis=3)


def nsa_attention(z3, qcol0, kc, vc, ks, vs, kw, vw, gate_logits):
    b, s, _ = z3.shape
    tq = ATT_TQ
    tk = min(ATT_TK, s)
    assert s % tk == 0 and tk % tq == 0 and tq % SLC_BLK == 0 and WINDOW % tq == 0 and WINDOW > tq
    n_slc = s // SLC_BLK
    assert n_slc <= ALIBI_OFF - MASK_OFF and n_slc % SUBLANES == 0
    n_cmp = (s - CMP_BLK) // CMP_STRIDE + 1
    ncp = kc.shape[2]
    gw = NSA_GRP * NSA_HD
    lanes = NSA_GRP * tq

    kc_f = _with_features(kc, _key_features(np.arange(ncp) * CMP_STRIDE + CMP_BLK - 1, 0))
    ks_f = _with_features(ks, _key_features(np.arange(s), n_slc))
    front = ((0, 0), (0, 0), (WINDOW, 0), (0, 0))
    kw_f = _with_features(jnp.pad(kw, front), _key_features(np.arange(-WINDOW, s), 0))
    t = lambda v: jnp.swapaxes(v, 2, 3)
    vw_t = t(jnp.pad(vw, front))

    c_start = np.arange(ncp) * CMP_STRIDE
    s_start = np.arange(n_slc) * SLC_BLK
    overlap = np.clip(np.minimum(c_start[:, None] + CMP_BLK, s_start[None, :] + SLC_BLK)
                      - np.maximum(c_start[:, None], s_start[None, :]), 0, None) / CMP_BLK
    overlap[n_cmp:] = 0.0
    c2s_t = jnp.asarray(overlap.T, BF16)

    key = np.arange(tq)[:, None]
    qry = np.tile(np.arange(tq), NSA_GRP)[None, :]
    bias = jnp.asarray(np.stack([np.where(key <= qry, 0.0, NEG), np.where(key > qry, 0.0, NEG)]), F32)

    blk = lambda *shape: pl.BlockSpec((1, 1) + shape, lambda bi, g, qi: (bi, g, 0, 0))
    const = lambda *shape: pl.BlockSpec(shape, lambda bi, g, qi: (0,) * len(shape))
    return pl.pallas_call(
        functools.partial(_nsa_kernel, n_cmp=n_cmp),
        grid=(b, NSA_KV_HEADS, s // tq),
        in_specs=[pl.BlockSpec((1, tq, gw), lambda bi, g, qi: (bi, qi, qcol0 // gw + g)),
                  pl.BlockSpec((1, FEAT, lanes), lambda bi, g, qi: (g, 0, 0)),
                  blk(ncp, FEAT), blk(NSA_HD, ncp),
                  blk(s, FEAT), blk(NSA_HD, s),
                  blk(s + WINDOW, FEAT), blk(NSA_HD, s + WINDOW),
                  pl.BlockSpec((1, 1, NSA_GRP * 3, tq), lambda bi, g, qi: (bi, g, 0, qi)),
                  const(n_slc, ncp), const(2, tq, lanes)],
        out_specs=pl.BlockSpec((1, tq, gw), lambda bi, g, qi: (bi, qi, g)),
        out_shape=jax.ShapeDtypeStruct((b, s, NSA_W), BF16),
        scratch_shapes=[pltpu.VMEM((FEAT, lanes), BF16), pltpu.VMEM((n_slc, tq), F32)],
        compiler_params=_cparams(("parallel", "parallel", "arbitrary")),
        name="nsa_attention",
    )(z3, _query_features(tq), kc_f, t(vc), ks_f, t(vs), kw_f, vw_t, gate_logits, c2s_t, bias)


def _merge_kernel(ya_ref, yb_ref, yc_ref, wb_ref, ga_ref, gb_ref, gc_ref, o_ref):
    acc = None
    for i, (y_ref, g_ref) in enumerate(((ya_ref, ga_ref), (yb_ref, gb_ref), (yc_ref, gc_ref))):
        term = jax.nn.sigmoid(g_ref[...].astype(F32)) * jnp.dot(y_ref[...], wb_ref[i], preferred_element_type=F32)
        acc = term if acc is None else acc + term
    o_ref[...] = acc.astype(o_ref.dtype)


def branch_merge(ya, yb, yc, wb, zmerge, tm=1024, tn=512):
    t = ya.shape[0]
    d = wb.shape[2]
    nj = d // tn
    y_spec = pl.BlockSpec((tm, BRANCH_W), lambda i, j: (i, 0))
    gate = lambda br: pl.BlockSpec((tm, tn), lambda i, j: (i, j + br * nj))
    return pl.pallas_call(
        _merge_kernel,
        grid=(t // tm, nj),
        in_specs=[y_spec, y_spec, y_spec,
                  pl.BlockSpec((3, BRANCH_W, tn), lambda i, j: (0, 0, j)),
                  gate(0), gate(1), gate(2)],
        out_specs=pl.BlockSpec((tm, tn), lambda i, j: (i, j)),
        out_shape=jax.ShapeDtypeStruct((t, d), BF16),
        compiler_params=_cparams(("parallel", "parallel")),
        name="branch_merge",
    )(ya, yb, yc, wb, zmerge, zmerge, zmerge)


def _layer_norm(y, g, b):
    mu = jnp.mean(y, axis=-1, keepdims=True)
    yc = y - mu
    var = jnp.mean(yc * yc, axis=-1, keepdims=True)
    return yc * lax.rsqrt(var + LN_EPS) * g + b


def _outproj_ln_kernel(m_ref, w_ref, x_ref, g_ref, b_ref, o_ref, ob_ref, *, alpha):
    mix = jnp.dot(m_ref[...], w_ref[...], preferred_element_type=F32)
    y = _layer_norm(alpha * x_ref[...] + mix, g_ref[...], b_ref[...])
    o_ref[...] = y
    ob_ref[...] = y.astype(BF16)


def outproj_ln(merged, w_out, x, g, b, alpha, tm=256):
    t, d = x.shape
    row = pl.BlockSpec((tm, d), lambda i: (i, 0))
    vec = pl.BlockSpec((1, d), lambda i: (0, 0))
    return pl.pallas_call(
        functools.partial(_outproj_ln_kernel, alpha=alpha),
        grid=(t // tm,),
        in_specs=[row, pl.BlockSpec((d, d), lambda i: (0, 0)), row, vec, vec],
        out_specs=[row, row],
        out_shape=[jax.ShapeDtypeStruct((t, d), F32), jax.ShapeDtypeStruct((t, d), BF16)],
        compiler_params=_cparams(("parallel",)),
        name="outproj_ln",
    )(merged, w_out, x, g.reshape(1, d), b.reshape(1, d))


def _final_kernel(x_ref, xb_ref, wg_ref, p_ref, wp_ref, y0_ref, y1_ref, gate_ref, g_ref, b_ref, o_ref, ob_ref,
                  *, alpha):
    ple = jax.nn.sigmoid(jnp.dot(xb_ref[...], wg_ref[...], preferred_element_type=F32)) * jnp.dot(
        p_ref[...], wp_ref[...], preferred_element_type=F32)
    gate = gate_ref[...]
    ffn = gate[:, 0:1] * y0_ref[...].astype(F32) + gate[:, 1:2] * y1_ref[...].astype(F32)
    y = _layer_norm(alpha * x_ref[...] + ffn + ple, g_ref[...], b_ref[...])
    o_ref[...] = y
    ob_ref[...] = y.astype(BF16)


def ple_combine_ln(x, xb, w_ple_gate, p, w_ple, y0, y1, gate, g, b, alpha, tm=256):
    t, d = x.shape
    pd = p.shape[1]
    row = pl.BlockSpec((tm, d), lambda i: (i, 0))
    vec = pl.BlockSpec((1, d), lambda i: (0, 0))
    return pl.pallas_call(
        functools.partial(_final_kernel, alpha=alpha),
        grid=(t // tm,),
        in_specs=[row, row, pl.BlockSpec((d, d), lambda i: (0, 0)),
                  pl.BlockSpec((tm, pd), lambda i: (i, 0)), pl.BlockSpec((pd, d), lambda i: (0, 0)),
                  row, row, pl.BlockSpec((tm, TOP_K), lambda i: (i, 0)), vec, vec],
        out_specs=[row, row],
        out_shape=[jax.ShapeDtypeStruct((t, d), F32), jax.ShapeDtypeStruct((t, d), BF16)],
        compiler_params=_cparams(("parallel",)),
        name="ple_combine_ln",
    )(x, xb, w_ple_gate, p, w_ple, y0, y1, gate, g.reshape(1, d), b.reshape(1, d))


def _router_kernel(x_ref, wr_ref, br_ref, idx_ref, gate_ref):
    tm = x_ref.shape[0]
    logits = lax.dot_general(wr_ref[...], x_ref[...], (((1,), (1,)), ((), ())),
                             precision=lax.Precision.HIGHEST, preferred_element_type=F32)
    aff = jax.nn.sigmoid(logits)
    sel = aff + br_ref[...]
    scores = []
    for g in range(N_GROUPS):
        v = [sel[g * EXPERTS_PER_GROUP + e:g * EXPERTS_PER_GROUP + e + 1, :] for e in range(EXPERTS_PER_GROUP)]
        best = None
        for a in range(EXPERTS_PER_GROUP):
            for c in range(a + 1, EXPERTS_PER_GROUP):
                pair = v[a] + v[c]
                best = pair if best is None else jnp.maximum(best, pair)
        scores.append(best)
    best_score, best_group = scores[0], jnp.zeros((1, tm), jnp.int32)
    for g in range(1, N_GROUPS):
        better = scores[g] > best_score
        best_score = jnp.where(better, scores[g], best_score)
        best_group = jnp.where(better, g, best_group)
    e_row = lax.broadcasted_iota(jnp.int32, (N_EXPERTS, tm), 0)
    cand = jnp.where(e_row // EXPERTS_PER_GROUP == best_group, sel, NEG)
    picks, gates = [], []
    for _ in range(TOP_K):
        mx = jnp.max(cand, axis=0, keepdims=True)
        pick = jnp.min(jnp.where(cand == mx, e_row, N_EXPERTS), axis=0, keepdims=True)
        hit = e_row == pick
        picks.append(pick)
        gates.append(jnp.sum(jnp.where(hit, aff, 0.0), axis=0, keepdims=True))
        cand = jnp.where(hit, -jnp.inf, cand)
    total = gates[0] + gates[1]
    idx_ref[...] = jnp.concatenate(picks, axis=0)
    gate_ref[...] = jnp.concatenate([gates[0] / total, gates[1] / total], axis=0)


def router(x, w_router, b_router, tm=1024):
    t, d = x.shape
    return pl.pallas_call(
        _router_kernel,
        grid=(t // tm,),
        in_specs=[pl.BlockSpec((tm, d), lambda i: (i, 0)),
                  pl.BlockSpec((N_EXPERTS, d), lambda i: (0, 0)),
                  pl.BlockSpec((N_EXPERTS, 1), lambda i: (0, 0))],
        out_specs=[pl.BlockSpec((TOP_K, tm), lambda i: (0, i)), pl.BlockSpec((TOP_K, tm), lambda i: (0, i))],
        out_shape=[jax.ShapeDtypeStruct((TOP_K, t), jnp.int32), jax.ShapeDtypeStruct((TOP_K, t), F32)],
        compiler_params=_cparams(("parallel",)),
        name="router",
    )(x, w_router.T, b_router.reshape(N_EXPERTS, 1))


def _expert_kernel(be_ref, x_ref, wgu_ref, wd_ref, o_ref):
    de = wd_ref.shape[1]
    gu = jnp.dot(x_ref[...], wgu_ref[0], preferred_element_type=F32)
    act = (jax.nn.silu(gu[:, :de]) * gu[:, de:]).astype(BF16)
    o_ref[...] = jnp.dot(act, wd_ref[0], preferred_element_type=F32).astype(o_ref.dtype)


def expert_ffn(block_expert, x_sorted, w_gate_up, w_down):
    n_rows, d = x_sorted.shape
    de = w_down.shape[1]
    return pl.pallas_call(
        _expert_kernel,
        grid_spec=pltpu.PrefetchScalarGridSpec(
            num_scalar_prefetch=1,
            grid=(n_rows // MOE_BLK,),
            in_specs=[pl.BlockSpec((MOE_BLK, d), lambda i, be: (i, 0)),
                      pl.BlockSpec((1, d, 2 * de), lambda i, be: (be[i], 0, 0)),
                      pl.BlockSpec((1, de, d), lambda i, be: (be[i], 0, 0))],
            out_specs=pl.BlockSpec((MOE_BLK, d), lambda i, be: (i, 0))),
        out_shape=jax.ShapeDtypeStruct((n_rows, d), BF16),
        compiler_params=_cparams(("arbitrary",)),
        name="expert_ffn",
    )(block_expert, x_sorted, w_gate_up, w_down)


def _dispatch_tables(expert_idx):
    n_tok = expert_idx.shape[0]
    n_assign = n_tok * TOP_K
    e_flat = expert_idx.reshape(n_assign)
    onehot = (e_flat[:, None] == jnp.arange(N_EXPERTS)[None, :]).astype(jnp.int32)
    csum = jnp.cumsum(onehot, axis=0)
    rank = jnp.take_along_axis(csum, e_flat[:, None], axis=1)[:, 0] - 1
    counts = csum[-1]
    padded = (counts + MOE_BLK - 1) // MOE_BLK * MOE_BLK
    pad_end = jnp.cumsum(padded)
    dest = (pad_end - padded)[e_flat] + rank
    n_rows = n_assign + N_EXPERTS * MOE_BLK
    row_tok = jnp.zeros((n_rows,), jnp.int32).at[dest].set(jnp.arange(n_assign, dtype=jnp.int32) // TOP_K)
    block_expert = jnp.minimum(
        jnp.searchsorted(pad_end, jnp.arange(n_rows // MOE_BLK) * MOE_BLK, side="right"), N_EXPERTS - 1)
    return row_tok, block_expert.astype(jnp.int32), dest.reshape(n_tok, TOP_K)


def _token_mixer(xb, bsz, seq, w_in, conv_a_w, conv_a_b, lru_conv_w, lru_conv_b, lru_wa, lru_ba, lru_wx, lru_bx,
                 lru_lam, cmp_pos, phi_w1, phi_b1, phi_w2, w_branch):
    d = xb.shape[1]
    n_tok = bsz * seq
    main_w = 3 * BRANCH_W + 2 * BRANCH_W + NSA_W + 6 * KV_W
    ng = 3 * NSA_HEADS
    kv0 = main_w - 6 * KV_W
    w_main = w_in[:, :main_w].astype(BF16)
    w_gate = jnp.pad(w_in[:, main_w:main_w + ng], ((0, 0), (0, LANES - ng))).astype(BF16)
    w_merge = w_in[:, main_w + ng:].astype(BF16)

    zmain = matmul(xb, w_main, BF16, 1024, 1536)
    zgate = matmul(xb, w_gate, F32, 1024, LANES)
    zmerge = matmul(xb, w_merge, BF16, 1024, 1024)
    z3 = zmain.reshape(bsz, seq, main_w)

    y_a = conv_mixer(z3, conv_a_w, conv_a_b)
    y_b = lru_mixer(z3, 3 * BRANCH_W, lru_conv_w, lru_conv_b, lru_wa, lru_ba, lru_wx, lru_bx, lru_lam)

    kv = z3[:, :, kv0:].reshape(bsz, seq, 6, NSA_KV_HEADS, NSA_HD).transpose(2, 0, 3, 1, 4)
    nslot = seq // CMP_STRIDE
    slots = kv[0:2].reshape(2, bsz, NSA_KV_HEADS, nslot, CMP_STRIDE * NSA_HD)
    cmp = nsa_compress(slots, cmp_pos, phi_w1, phi_b1, phi_w2)
    gate_logits = zgate[:, :ng].reshape(bsz, seq, NSA_KV_HEADS, NSA_GRP * 3).transpose(0, 2, 3, 1)
    y_c = nsa_attention(z3, 5 * BRANCH_W, cmp[0], cmp[1], kv[2], kv[3], kv[4], kv[5], gate_logits)

    flat = lambda y: y.reshape(n_tok, BRANCH_W)
    return branch_merge(flat(y_a), flat(y_b), flat(y_c), w_branch.astype(BF16), zmerge)


def _moe(x, xb, w_router, b_router, w_gate_up, w_down):
    idx_t, gate_t = router(x, w_router, b_router)
    row_tok, block_expert, dest = _dispatch_tables(idx_t.T)
    rows = expert_ffn(block_expert, xb[row_tok], w_gate_up.astype(BF16), w_down.astype(BF16))
    return rows[dest[:, 0]], rows[dest[:, 1]], gate_t.T


def kernel(x, p, w_in, conv_a_w, conv_a_b, lru_conv_w, lru_conv_b, lru_wa, lru_ba, lru_wx, lru_bx, lru_lam, cmp_pos, phi_w1, phi_b1, phi_w2, w_branch, w_out, ln_g, ln_b, w_router, b_router, w_gate_up, w_down, w_ple, w_ple_gate):
    bsz, seq, d = x.shape
    depth = w_in.shape[0]
    n_tok = bsz * seq
    alpha = (2 * depth) ** 0.25
    xf = x.reshape(n_tok, d)
    xb = xf.astype(BF16)
    for i in range(depth):
        merged = _token_mixer(xb, bsz, seq, w_in[i], conv_a_w[i], conv_a_b[i], lru_conv_w[i], lru_conv_b[i],
                              lru_wa[i], lru_ba[i], lru_wx[i], lru_bx[i], lru_lam[i], cmp_pos[i], phi_w1[i],
                              phi_b1[i], phi_w2[i], w_branch[i])
        xf, xb = outproj_ln(merged, w_out[i].astype(BF16), xf, ln_g[i, 0], ln_b[i, 0], alpha)
        y0, y1, gate = _moe(xf, xb, w_router, b_router, w_gate_up[i], w_down[i])
        xf, xb = ple_combine_ln(xf, xb, w_ple_gate[i].astype(BF16), p[i].reshape(n_tok, -1).astype(BF16),
                                w_ple[i].astype(BF16), y0, y1, gate, ln_g[i, 1], ln_b[i, 1], alpha)
    return xf.reshape(bsz, seq, d)
```

```python
import functools

import ml_dtypes
import numpy as np
import jax
import jax.numpy as jnp
from jax import lax
from jax.experimental import pallas as pl
from jax.experimental.pallas import tpu as pltpu

F32 = jnp.float32
BF16 = jnp.bfloat16

BRANCH_W = 1024
CONV_K = 3
LRU_HEADS = 8
LRU_HD = BRANCH_W // LRU_HEADS
LRU_CONV_K = 4
LRU_C = 8.0
NSA_HEADS = 16
NSA_KV_HEADS = 4
NSA_GRP = NSA_HEADS // NSA_KV_HEADS
NSA_HD = 64
NSA_W = NSA_HEADS * NSA_HD
KV_W = NSA_KV_HEADS * NSA_HD
CMP_BLK = 32
CMP_STRIDE = 16
CMP_HIDDEN = 128
SLC_BLK = 64
SLC_TOPN = 16
WINDOW = 512
N_EXPERTS = 16
N_GROUPS = 4
EXPERTS_PER_GROUP = N_EXPERTS // N_GROUPS
TOP_K = 2
LN_EPS = 1e-5
NEG = -1e30
FORCE = 1e9

LANES = 128
SUBLANES = 8
VMEM_LIMIT = 56 * 1024 * 1024

FEAT = 256
MASK_OFF = NSA_HD
ALIBI_OFF = 2 * NSA_HD
PAD_OFF = ALIBI_OFF + 6
V_ROWS = NSA_HD + 16
LOG2E = 1.4426950408889634

ATT_TQ = 128
ATT_TK = 512
EXP_CHUNK = 64
MOE_BLK = 256
EXPERT_CHUNKS = 2


def _cparams(sem):
    return pltpu.CompilerParams(dimension_semantics=sem, vmem_limit_bytes=VMEM_LIMIT)


def _mm_kernel(x_ref, w_ref, o_ref):
    o_ref[...] = jnp.dot(x_ref[...], w_ref[...], preferred_element_type=F32).astype(o_ref.dtype)


def matmul(x, w, out_dtype, tm, tn):
    m, k = x.shape
    n = w.shape[1]
    assert m % tm == 0 and n % tn == 0
    return pl.pallas_call(
        _mm_kernel,
        grid=(m // tm, n // tn),
        in_specs=[pl.BlockSpec((tm, k), lambda i, j: (i, 0)),
                  pl.BlockSpec((k, tn), lambda i, j: (0, j))],
        out_specs=pl.BlockSpec((tm, tn), lambda i, j: (i, j)),
        out_shape=jax.ShapeDtypeStruct((m, n), out_dtype),
        compiler_params=_cparams(("parallel", "parallel")),
        name="matmul",
    )(x, w)


def _mm_wres_kernel(x_ref, w_ref, o_ref, wb_ref):
    @pl.when(pl.program_id(1) == 0)
    def _():
        wb_ref[...] = w_ref[0].astype(BF16)

    o_ref[...] = jnp.dot(x_ref[...], wb_ref[...], preferred_element_type=F32).astype(o_ref.dtype)


def matmul_wres(x, w3, layer, col0, n, out_dtype, tm, tn):
    m, k = x.shape
    assert m % tm == 0 and n % tn == 0 and col0 % tn == 0
    return pl.pallas_call(
        _mm_wres_kernel,
        grid=(n // tn, m // tm),
        in_specs=[pl.BlockSpec((tm, k), lambda j, i: (i, 0)),
                  pl.BlockSpec((1, k, tn), lambda j, i: (layer, 0, col0 // tn + j))],
        out_specs=pl.BlockSpec((tm, tn), lambda j, i: (i, j)),
        out_shape=jax.ShapeDtypeStruct((m, n), out_dtype),
        scratch_shapes=[pltpu.VMEM((k, tn), BF16)],
        compiler_params=_cparams(("parallel", "arbitrary")),
        name="matmul_wres",
    )(x, w3)


def _causal_conv(ext, cur, w, k):
    acc = cur * w[k - 1:k, :]
    for j in range(k - 1):
        shift = k - 1 - j
        acc = acc + pltpu.roll(ext, shift, 0)[SUBLANES:, :] * w[j:j + 1, :]
    return acc


def _conv_mixer_kernel(ain_ref, ab_ref, ac_ref, w_ref, b_ref, o_ref, halo_ref):
    @pl.when(pl.program_id(2) == 0)
    def _():
        halo_ref[...] = jnp.zeros_like(halo_ref)

    v = ac_ref[0].astype(F32) * ain_ref[0].astype(F32)
    ext = jnp.concatenate([halo_ref[...], v], axis=0)
    y = _causal_conv(ext, v, w_ref[...], CONV_K) + b_ref[...]
    o_ref[0] = (ab_ref[0].astype(F32) * y).astype(o_ref.dtype)
    halo_ref[...] = v[v.shape[0] - SUBLANES:, :]


def conv_mixer(z3, conv_w, conv_b, ts=512, cw=512):
    b, s, _ = z3.shape
    nc = BRANCH_W // cw
    return pl.pallas_call(
        _conv_mixer_kernel,
        grid=(b, nc, s // ts),
        in_specs=[pl.BlockSpec((1, ts, cw), lambda bi, c, si: (bi, si, c)),
                  pl.BlockSpec((1, ts, cw), lambda bi, c, si: (bi, si, c + nc)),
                  pl.BlockSpec((1, ts, cw), lambda bi, c, si: (bi, si, c + 2 * nc)),
                  pl.BlockSpec((CONV_K, cw), lambda bi, c, si: (0, c)),
                  pl.BlockSpec((1, cw), lambda bi, c, si: (0, c))],
        out_specs=pl.BlockSpec((1, ts, cw), lambda bi, c, si: (bi, si, c)),
        out_shape=jax.ShapeDtypeStruct((b, s, BRANCH_W), BF16),
        scratch_shapes=[pltpu.VMEM((SUBLANES, cw), F32)],
        compiler_params=_cparams(("parallel", "parallel", "arbitrary")),
        name="conv_mixer",
    )(z3, z3, z3, conv_w, conv_b.reshape(1, BRANCH_W))


def _lru_kernel(gate_ref, rin_ref, cw_ref, cb_ref, wa_ref, ba_ref, wx_ref, bx_ref, lam_ref, o_ref,
                halo_ref, carry_ref, a_ref, b_ref, h_ref):
    ts, lw = a_ref.shape

    @pl.when(pl.program_id(2) == 0)
    def _():
        halo_ref[...] = jnp.zeros_like(halo_ref)
        carry_ref[...] = jnp.zeros_like(carry_ref)

    u = rin_ref[0].astype(F32)
    ext = jnp.concatenate([halo_ref[...], u], axis=0)
    xc = _causal_conv(ext, u, cw_ref[...], LRU_CONV_K) + cb_ref[...]
    halo_ref[...] = u[ts - SUBLANES:, :]

    xcb = xc.astype(BF16)
    ra, ix = [], []
    for hh in range(lw // LRU_HD):
        xh = xcb[:, hh * LRU_HD:(hh + 1) * LRU_HD]
        ra.append(jnp.dot(xh, wa_ref[hh].astype(BF16), preferred_element_type=F32))
        ix.append(jnp.dot(xh, wx_ref[hh].astype(BF16), preferred_element_type=F32))
    r = jax.nn.sigmoid(jnp.concatenate(ra, axis=1) + ba_ref[...])
    i = jax.nn.sigmoid(jnp.concatenate(ix, axis=1) + bx_ref[...])
    neg_lam = -lam_ref[...]
    softplus = jnp.maximum(neg_lam, 0.0) + jnp.log1p(jnp.exp(-jnp.abs(neg_lam)))
    log_a = -LRU_C * r * softplus
    a_ref[...] = jnp.exp(log_a)
    th = jnp.tanh(log_a)
    b_ref[...] = jnp.sqrt(-2.0 * th / (1.0 - th)) * (i * xc)

    row = lax.broadcasted_iota(jnp.int32, (SUBLANES, lw), 0)

    def group(g, hc):
        sl = pl.ds(pl.multiple_of(g * SUBLANES, SUBLANES), SUBLANES)
        a = a_ref[sl, :]
        bv = b_ref[sl, :]
        for d in (1, 2, 4):
            keep = row >= d
            bv = jnp.where(keep, a * pltpu.roll(bv, d, 0) + bv, bv)
            a = jnp.where(keep, a * pltpu.roll(a, d, 0), a)
        h = a * hc + bv
        h_ref[sl, :] = h
        return jnp.broadcast_to(h[SUBLANES - 1:SUBLANES, :], (SUBLANES, lw))

    carry_ref[...] = lax.fori_loop(0, ts // SUBLANES, group, carry_ref[...])
    o_ref[0] = (h_ref[...] * jax.nn.gelu(gate_ref[0].astype(F32))).astype(o_ref.dtype)


def lru_mixer(z3, col0, conv_w, conv_b, wa, ba, wx, bx, lam, ts=512, lw=256):
    b, s, _ = z3.shape
    nc = BRANCH_W // lw
    c0 = col0 // lw
    hpb = lw // LRU_HD
    row = lambda v: v.reshape(1, BRANCH_W)
    vec = pl.BlockSpec((1, lw), lambda bi, c, si: (0, c))
    mat = pl.BlockSpec((hpb, LRU_HD, LRU_HD), lambda bi, c, si: (c, 0, 0))
    return pl.pallas_call(
        _lru_kernel,
        grid=(b, nc, s // ts),
        in_specs=[pl.BlockSpec((1, ts, lw), lambda bi, c, si: (bi, si, c0 + c)),
                  pl.BlockSpec((1, ts, lw), lambda bi, c, si: (bi, si, c0 + nc + c)),
                  pl.BlockSpec((LRU_CONV_K, lw), lambda bi, c, si: (0, c)),
                  vec, mat, vec, mat, vec, vec],
        out_specs=pl.BlockSpec((1, ts, lw), lambda bi, c, si: (bi, si, c)),
        out_shape=jax.ShapeDtypeStruct((b, s, BRANCH_W), BF16),
        scratch_shapes=[pltpu.VMEM((SUBLANES, lw), F32), pltpu.VMEM((SUBLANES, lw), F32),
                        pltpu.VMEM((ts, lw), F32), pltpu.VMEM((ts, lw), F32), pltpu.VMEM((ts, lw), F32)],
        compiler_params=_cparams(("parallel", "parallel", "arbitrary")),
        name="rg_lru",
    )(z3, z3, conv_w, row(conv_b), wa, row(ba), wx, row(bx), row(lam))


def _compress_kernel(slots_ref, pos_ref, w1_ref, b1_ref, w2_ref, o_ref):
    g, nslot, half = slots_ref.shape[2:]
    rows = g * nslot
    x = slots_ref[0, 0].reshape(rows, half)
    w1 = w1_ref[0].astype(BF16)
    first = jnp.dot(x, w1[:half], preferred_element_type=F32)
    second = jnp.dot(x, w1[half:], preferred_element_type=F32)
    pos = jnp.broadcast_to(pos_ref[0], (SUBLANES, 2 * half)).astype(BF16)
    posb = jnp.dot(pos, w1, preferred_element_type=F32)[0:1, :]
    hidden = first + pltpu.roll(second, rows - 1, 0) + posb + b1_ref[0]
    out = jnp.dot(jax.nn.gelu(hidden).astype(BF16), w2_ref[0].astype(BF16), preferred_element_type=F32)
    o_ref[0, 0] = out.reshape(g, nslot, NSA_HD).astype(o_ref.dtype)


def nsa_compress(slots, cmp_pos, w1, b1, w2):
    _, b, g, nslot, half = slots.shape
    pos = cmp_pos.reshape(2, 1, CMP_BLK * NSA_HD)
    return pl.pallas_call(
        _compress_kernel,
        grid=(2, b),
        in_specs=[pl.BlockSpec((1, 1, g, nslot, half), lambda kv, bi: (kv, bi, 0, 0, 0)),
                  pl.BlockSpec((1, 1, 2 * half), lambda kv, bi: (kv, 0, 0)),
                  pl.BlockSpec((1, 2 * half, CMP_HIDDEN), lambda kv, bi: (kv, 0, 0)),
                  pl.BlockSpec((1, 1, CMP_HIDDEN), lambda kv, bi: (kv, 0, 0)),
                  pl.BlockSpec((1, CMP_HIDDEN, NSA_HD), lambda kv, bi: (kv, 0, 0))],
        out_specs=pl.BlockSpec((1, 1, g, nslot, NSA_HD), lambda kv, bi: (kv, bi, 0, 0, 0)),
        out_shape=jax.ShapeDtypeStruct((2, b, g, nslot, NSA_HD), BF16),
        compiler_params=_cparams(("parallel", "parallel")),
        name="nsa_compress",
    )(slots, pos, w1, b1.reshape(2, 1, CMP_HIDDEN), w2)


def _exp2_bf16(s, m):
    rows = s.shape[0]
    chunk = min(EXP_CHUNK, rows)
    return jnp.concatenate([jnp.exp2((s[r:r + chunk] - m).astype(BF16)) for r in range(0, rows, chunk)], axis=0)


def _nsa_kernel(q_ref, qf_ref, kc_ref, vc_ref, ks_ref, vs_ref, kw_ref, vw_ref, gl_ref, c2s_ref, bias_ref, o_ref,
                qp_ref, imp_ref, *, n_cmp):
    tq = q_ref.shape[1]
    lanes = NSA_GRP * tq
    seq = ks_ref.shape[2]
    n_slc = seq // SLC_BLK
    t0 = pl.multiple_of(pl.program_id(2) * tq, tq)
    head = lambda r: slice(r * tq, (r + 1) * tq)

    def scores(k_tile, bias=None):
        s = jnp.dot(k_tile, qp_ref[...], preferred_element_type=F32)
        return s if bias is None else s + bias

    qt = q_ref[0].astype(F32).T * (NSA_HD ** -0.5 * LOG2E)
    qp_ref[...] = qf_ref[0]
    for r in range(NSA_GRP):
        qp_ref[0:NSA_HD, head(r)] = qt[r * NSA_HD:(r + 1) * NSA_HD, :].astype(BF16)

    s_cmp = scores(kc_ref[0, 0])
    win = []
    for off, size, bias in ((0, tq, bias_ref[1]), (tq, WINDOW - tq, None), (WINDOW, tq, bias_ref[0])):
        k0 = pl.multiple_of(t0 + off, tq)
        win.append((scores(kw_ref[0, 0, pl.ds(k0, size), :], bias), vw_ref[0, 0, :, pl.ds(k0, size)]))

    ncp = kc_ref.shape[2]
    n_row = lax.broadcasted_iota(jnp.int32, (ncp, lanes), 0)
    t_lane = t0 + lax.broadcasted_iota(jnp.int32, (ncp, lanes), 1) % tq
    ok = (t_lane >= n_row * CMP_STRIDE + (CMP_BLK - 1)) & (n_row < n_cmp)
    s = jnp.where(ok, s_cmp, NEG)
    p = jnp.where(ok, jnp.exp2(s - jnp.max(s, axis=0, keepdims=True)), 0.0)
    l = jnp.sum(p, axis=0, keepdims=True)
    p_cmp = p * (1.0 / jnp.where(l > 0.0, l, 1.0))
    o_cmp = jnp.dot(vc_ref[0, 0], p_cmp.astype(BF16), preferred_element_type=F32)

    rest = p_cmp[:, head(0)]
    for r in range(1, NSA_GRP):
        rest = rest + p_cmp[:, head(r)]
    imp = jnp.zeros((n_slc, tq), F32)
    for _ in range(3):
        part = rest.astype(BF16)
        rest = rest - part.astype(F32)
        imp = imp + jnp.dot(c2s_ref[...], part, preferred_element_type=F32)
    j_row = lax.broadcasted_iota(jnp.int32, (n_slc, tq), 0)
    t_col = t0 + lax.broadcasted_iota(jnp.int32, (n_slc, tq), 1)
    forced = (j_row == 0) | (j_row == t_col // SLC_BLK)
    imp = jnp.where(forced, FORCE, jnp.where(j_row * SLC_BLK <= t_col, imp, -FORCE))
    imp_ref[...] = imp

    mx = functools.reduce(jnp.maximum, [jnp.max(s, axis=0, keepdims=True) for s, _ in win])
    o_win = jnp.zeros((V_ROWS, lanes), F32)
    for s, v_tile in win:
        o_win = o_win + jnp.dot(v_tile, _exp2_bf16(s, mx), preferred_element_type=F32)
    o_win = o_win[0:NSA_HD] * (1.0 / o_win[NSA_HD:NSA_HD + 1])

    n_grp = n_slc // SUBLANES
    vals = [imp[v * SUBLANES:(v + 1) * SUBLANES, :] for v in range(n_grp)]
    sub = lax.broadcasted_iota(jnp.int32, (SUBLANES, tq), 0)
    rank = [jnp.zeros((SUBLANES, tq), jnp.int32) for _ in range(n_grp)]
    for i in range(n_slc):
        bi = jnp.broadcast_to(imp_ref[i:i + 1, :], (SUBLANES, tq))
        for v in range(n_grp):
            if v < i // SUBLANES:
                ahead = bi > vals[v]
            elif v > i // SUBLANES:
                ahead = bi >= vals[v]
            else:
                ahead = (bi > vals[v]) | ((bi == vals[v]) & (sub > i % SUBLANES))
            rank[v] = rank[v] + jnp.where(ahead, 1, 0)
    mask = jnp.where(jnp.concatenate(rank, axis=0) < min(SLC_TOPN, n_slc), 0.0, NEG).astype(BF16)
    for r in range(NSA_GRP):
        qp_ref[MASK_OFF:MASK_OFF + n_slc, head(r)] = mask

    tk = min(ATT_TK, seq)

    def online(carry, tiles):
        ss = []
        for k0, bias in tiles:
            s = scores(ks_ref[0, 0, pl.ds(k0, tk), :], bias)
            ss.append((s, vs_ref[0, 0, :, pl.ds(k0, tk)]))
        out = list(carry)
        for c, (s, v_tile) in enumerate(ss):
            m_run, acc = carry[c]
            m_new = jnp.maximum(m_run, jnp.max(s, axis=0, keepdims=True))
            pv = jnp.dot(v_tile, _exp2_bf16(s, m_new), preferred_element_type=F32)
            out[c] = (m_new, jnp.exp2(m_run - m_new) * acc + pv)
        return tuple(out)

    def pair(i, carry):
        k0 = pl.multiple_of(2 * i * tk, tk)
        return online(carry, ((k0, None), (k0 + tk, None)))

    init = (jnp.full((1, lanes), NEG, F32), jnp.zeros((V_ROWS, lanes), F32))
    n_full = t0 // tk
    carry = lax.fori_loop(0, n_full // 2, pair, (init, init))
    k_last = pl.multiple_of(n_full * tk, tk)
    key_pos = k_last + lax.broadcasted_iota(jnp.int32, (tk, lanes), 0)
    qry_pos = t0 + lax.broadcasted_iota(jnp.int32, (tk, lanes), 1) % tq
    causal = jnp.where(key_pos <= qry_pos, 0.0, NEG)
    carry = lax.cond(n_full % 2 == 1,
                     lambda c: online(c, ((pl.multiple_of(k_last - tk, tk), None), (k_last, causal))),
                     lambda c: online(c, ((k_last, causal),)),
                     carry)
    (m_e, acc_e), (m_o, acc_o) = carry
    m_all = jnp.maximum(m_e, m_o)
    o_slc = jnp.exp2(m_e - m_all) * acc_e + jnp.exp2(m_o - m_all) * acc_o
    o_slc = o_slc[0:NSA_HD] * (1.0 / o_slc[NSA_HD:NSA_HD + 1])

    gates = jax.nn.sigmoid(gl_ref[0, 0].astype(F32))
    outs = []
    for r in range(NSA_GRP):
        outs.append(gates[3 * r:3 * r + 1] * o_cmp[:, head(r)] + gates[3 * r + 1:3 * r + 2] * o_slc[:, head(r)]
                    + gates[3 * r + 2:3 * r + 3] * o_win[:, head(r)])
    o_ref[0] = jnp.concatenate(outs, axis=0).T.astype(o_ref.dtype)


def _alibi_slopes():
    return (2.0 ** (-8.0 * np.arange(1, NSA_HEADS + 1) / NSA_HEADS)).astype(np.float32)


def _bf16_split3(v):
    parts, rest = [], np.asarray(v, np.float32)
    for _ in range(3):
        part = rest.astype(ml_dtypes.bfloat16).astype(np.float32)
        parts.append(part)
        rest = (rest.astype(np.float64) - part.astype(np.float64)).astype(np.float32)
    return parts


def _query_features(tq):
    feat = np.zeros((NSA_HEADS, FEAT), np.float32)
    hi, mid, lo = _bf16_split3(_alibi_slopes() * np.float32(LOG2E))
    for c, part in enumerate((hi, mid, lo, hi, mid, lo)):
        feat[:, ALIBI_OFF + c] = part
    feat[:, PAD_OFF] = NEG
    feat = np.repeat(feat.reshape(NSA_KV_HEADS, NSA_GRP, FEAT), tq, axis=1)
    return jnp.asarray(feat.transpose(0, 2, 1), BF16)


def _key_features(pos, n_slc):
    pos = np.asarray(pos)
    real = pos >= 0
    feat = np.zeros((pos.shape[0], FEAT - NSA_HD), np.float32)
    if n_slc:
        feat[real, pos[real] // SLC_BLK] = 1.0
    a0 = ALIBI_OFF - NSA_HD
    feat[real, a0:a0 + 3] = ((pos[real] // SLC_BLK) * SLC_BLK)[:, None]
    feat[real, a0 + 3:a0 + 6] = (pos[real] % SLC_BLK)[:, None]
    feat[~real, PAD_OFF - NSA_HD] = 1.0
    return feat


def _values_t(v):
    b, g, n, _ = v.shape
    extra = jnp.zeros((b, g, V_ROWS - NSA_HD, n), v.dtype).at[:, :, 0].set(1.0)
    return jnp.concatenate([jnp.swapaxes(v, 2, 3), extra], axis=2)


def _with_features(k, feat):
    b, g, _, _ = k.shape
    f = jnp.broadcast_to(jnp.asarray(feat, BF16), (b, g) + feat.shape)
    return jnp.concatenate([k.astype(BF16), f], axis=3)


def nsa_attention(z3, qcol0, kc, vc, ks, vs, kw, vw, gate_logits):
    b, s, _ = z3.shape
    tq = ATT_TQ
    tk = min(ATT_TK, s)
    assert s % tk == 0 and tk % tq == 0 and tq % SLC_BLK == 0 and WINDOW % tq == 0 and WINDOW > tq
    n_slc = s // SLC_BLK
    assert n_slc <= ALIBI_OFF - MASK_OFF and n_slc % SUBLANES == 0
    n_cmp = (s - CMP_BLK) // CMP_STRIDE + 1
    ncp = kc.shape[2]
    gw = NSA_GRP * NSA_HD
    lanes = NSA_GRP * tq

    kc_f = _with_features(kc, _key_features(np.arange(ncp) * CMP_STRIDE + CMP_BLK - 1, 0))
    ks_f = _with_features(ks, _key_features(np.arange(s), n_slc))
    front = ((0, 0), (0, 0), (WINDOW, 0), (0, 0))
    kw_f = _with_features(jnp.pad(kw, front), _key_features(np.arange(-WINDOW, s), 0))
    t = lambda v: jnp.swapaxes(v, 2, 3)
    vw_t = _values_t(jnp.pad(vw, front))

    c_start = np.arange(ncp) * CMP_STRIDE
    s_start = np.arange(n_slc) * SLC_BLK
    overlap = np.clip(np.minimum(c_start[:, None] + CMP_BLK, s_start[None, :] + SLC_BLK)
                      - np.maximum(c_start[:, None], s_start[None, :]), 0, None) / CMP_BLK
    overlap[n_cmp:] = 0.0
    c2s_t = jnp.asarray(overlap.T, BF16)

    key = np.arange(tq)[:, None]
    qry = np.tile(np.arange(tq), NSA_GRP)[None, :]
    bias = jnp.asarray(np.stack([np.where(key <= qry, 0.0, NEG), np.where(key > qry, 0.0, NEG)]), F32)

    blk = lambda *shape: pl.BlockSpec((1, 1) + shape, lambda bi, g, qi: (bi, g, 0, 0))
    const = lambda *shape: pl.BlockSpec(shape, lambda bi, g, qi: (0,) * len(shape))
    return pl.pallas_call(
        functools.partial(_nsa_kernel, n_cmp=n_cmp),
        grid=(b, NSA_KV_HEADS, s // tq),
        in_specs=[pl.BlockSpec((1, tq, gw), lambda bi, g, qi: (bi, qi, qcol0 // gw + g)),
                  pl.BlockSpec((1, FEAT, lanes), lambda bi, g, qi: (g, 0, 0)),
                  blk(ncp, FEAT), blk(NSA_HD, ncp),
                  blk(s, FEAT), blk(V_ROWS, s),
                  blk(s + WINDOW, FEAT), blk(V_ROWS, s + WINDOW),
                  pl.BlockSpec((1, 1, NSA_GRP * 3, tq), lambda bi, g, qi: (bi, g, 0, qi)),
                  const(n_slc, ncp), const(2, tq, lanes)],
        out_specs=pl.BlockSpec((1, tq, gw), lambda bi, g, qi: (bi, qi, g)),
        out_shape=jax.ShapeDtypeStruct((b, s, NSA_W), BF16),
        scratch_shapes=[pltpu.VMEM((FEAT, lanes), BF16), pltpu.VMEM((n_slc, tq), F32)],
        compiler_params=_cparams(("parallel", "parallel", "arbitrary")),
        name="nsa_attention",
    )(z3, _query_features(tq), kc_f, t(vc), ks_f, _values_t(vs), kw_f, vw_t, gate_logits, c2s_t, bias)


def _merge_kernel(ya_ref, yb_ref, yc_ref, wb_ref, ga_ref, gb_ref, gc_ref, o_ref):
    acc = None
    for i, (y_ref, g_ref) in enumerate(((ya_ref, ga_ref), (yb_ref, gb_ref), (yc_ref, gc_ref))):
        term = jax.nn.sigmoid(g_ref[...].astype(F32)) * jnp.dot(y_ref[...], wb_ref[i], preferred_element_type=F32)
        acc = term if acc is None else acc + term
    o_ref[...] = acc.astype(o_ref.dtype)


def branch_merge(ya, yb, yc, wb, zmerge, tm=1024, tn=512):
    t = ya.shape[0]
    d = wb.shape[2]
    nj = d // tn
    y_spec = pl.BlockSpec((tm, BRANCH_W), lambda i, j: (i, 0))
    gate = lambda br: pl.BlockSpec((tm, tn), lambda i, j: (i, j + br * nj))
    return pl.pallas_call(
        _merge_kernel,
        grid=(t // tm, nj),
        in_specs=[y_spec, y_spec, y_spec,
                  pl.BlockSpec((3, BRANCH_W, tn), lambda i, j: (0, 0, j)),
                  gate(0), gate(1), gate(2)],
        out_specs=pl.BlockSpec((tm, tn), lambda i, j: (i, j)),
        out_shape=jax.ShapeDtypeStruct((t, d), BF16),
        compiler_params=_cparams(("parallel", "parallel")),
        name="branch_merge",
    )(ya, yb, yc, wb, zmerge, zmerge, zmerge)


def _layer_norm(y, g, b):
    mu = jnp.mean(y, axis=-1, keepdims=True)
    yc = y - mu
    var = jnp.mean(yc * yc, axis=-1, keepdims=True)
    return yc * lax.rsqrt(var + LN_EPS) * g + b


def _outproj_ln_kernel(m_ref, w_ref, x_ref, g_ref, b_ref, o_ref, ob_ref, *, alpha):
    mix = jnp.dot(m_ref[...], w_ref[...], preferred_element_type=F32)
    y = _layer_norm(alpha * x_ref[...] + mix, g_ref[...], b_ref[...])
    o_ref[...] = y
    ob_ref[...] = y.astype(BF16)


def outproj_ln(merged, w_out, x, g, b, alpha, tm=256):
    t, d = x.shape
    row = pl.BlockSpec((tm, d), lambda i: (i, 0))
    vec = pl.BlockSpec((1, d), lambda i: (0, 0))
    return pl.pallas_call(
        functools.partial(_outproj_ln_kernel, alpha=alpha),
        grid=(t // tm,),
        in_specs=[row, pl.BlockSpec((d, d), lambda i: (0, 0)), row, vec, vec],
        out_specs=[row, row],
        out_shape=[jax.ShapeDtypeStruct((t, d), F32), jax.ShapeDtypeStruct((t, d), BF16)],
        compiler_params=_cparams(("parallel",)),
        name="outproj_ln",
    )(merged, w_out, x, g.reshape(1, d), b.reshape(1, d))


def _final_kernel(x_ref, xb_ref, wg_ref, p_ref, wp_ref, y0_ref, y1_ref, gate_ref, g_ref, b_ref, o_ref, ob_ref,
                  *, alpha):
    ple = jax.nn.sigmoid(jnp.dot(xb_ref[...], wg_ref[...], preferred_element_type=F32)) * jnp.dot(
        p_ref[...], wp_ref[...], preferred_element_type=F32)
    gate = gate_ref[...]
    ffn = gate[:, 0:1] * y0_ref[...].astype(F32) + gate[:, 1:2] * y1_ref[...].astype(F32)
    y = _layer_norm(alpha * x_ref[...] + ffn + ple, g_ref[...], b_ref[...])
    o_ref[...] = y
    ob_ref[...] = y.astype(BF16)


def ple_combine_ln(x, xb, w_ple_gate, p, w_ple, y0, y1, gate, g, b, alpha, tm=256):
    t, d = x.shape
    pd = p.shape[1]
    row = pl.BlockSpec((tm, d), lambda i: (i, 0))
    vec = pl.BlockSpec((1, d), lambda i: (0, 0))
    return pl.pallas_call(
        functools.partial(_final_kernel, alpha=alpha),
        grid=(t // tm,),
        in_specs=[row, row, pl.BlockSpec((d, d), lambda i: (0, 0)),
                  pl.BlockSpec((tm, pd), lambda i: (i, 0)), pl.BlockSpec((pd, d), lambda i: (0, 0)),
                  row, row, pl.BlockSpec((tm, TOP_K), lambda i: (i, 0)), vec, vec],
        out_specs=[row, row],
        out_shape=[jax.ShapeDtypeStruct((t, d), F32), jax.ShapeDtypeStruct((t, d), BF16)],
        compiler_params=_cparams(("parallel",)),
        name="ple_combine_ln",
    )(x, xb, w_ple_gate, p, w_ple, y0, y1, gate, g.reshape(1, d), b.reshape(1, d))


def _router_kernel(x_ref, wr_ref, br_ref, idx_ref, gate_ref):
    tm = x_ref.shape[0]
    logits = lax.dot_general(wr_ref[...], x_ref[...], (((1,), (1,)), ((), ())),
                             precision=lax.Precision.HIGHEST, preferred_element_type=F32)
    aff = jax.nn.sigmoid(logits)
    sel = aff + br_ref[...]
    scores = []
    for g in range(N_GROUPS):
        v = [sel[g * EXPERTS_PER_GROUP + e:g * EXPERTS_PER_GROUP + e + 1, :] for e in range(EXPERTS_PER_GROUP)]
        best = None
        for a in range(EXPERTS_PER_GROUP):
            for c in range(a + 1, EXPERTS_PER_GROUP):
                pair = v[a] + v[c]
                best = pair if best is None else jnp.maximum(best, pair)
        scores.append(best)
    best_score, best_group = scores[0], jnp.zeros((1, tm), jnp.int32)
    for g in range(1, N_GROUPS):
        better = scores[g] > best_score
        best_score = jnp.where(better, scores[g], best_score)
        best_group = jnp.where(better, g, best_group)
    e_row = lax.broadcasted_iota(jnp.int32, (N_EXPERTS, tm), 0)
    cand = jnp.where(e_row // EXPERTS_PER_GROUP == best_group, sel, NEG)
    picks, gates = [], []
    for _ in range(TOP_K):
        mx = jnp.max(cand, axis=0, keepdims=True)
        pick = jnp.min(jnp.where(cand == mx, e_row, N_EXPERTS), axis=0, keepdims=True)
        hit = e_row == pick
        picks.append(pick)
        gates.append(jnp.sum(jnp.where(hit, aff, 0.0), axis=0, keepdims=True))
        cand = jnp.where(hit, -jnp.inf, cand)
    total = gates[0] + gates[1]
    idx_ref[...] = jnp.concatenate(picks, axis=0)
    gate_ref[...] = jnp.concatenate([gates[0] / total, gates[1] / total], axis=0)


def router(x, w_router, b_router, tm=1024):
    t, d = x.shape
    return pl.pallas_call(
        _router_kernel,
        grid=(t // tm,),
        in_specs=[pl.BlockSpec((tm, d), lambda i: (i, 0)),
                  pl.BlockSpec((N_EXPERTS, d), lambda i: (0, 0)),
                  pl.BlockSpec((N_EXPERTS, 1), lambda i: (0, 0))],
        out_specs=[pl.BlockSpec((TOP_K, tm), lambda i: (0, i)), pl.BlockSpec((TOP_K, tm), lambda i: (0, i))],
        out_shape=[jax.ShapeDtypeStruct((TOP_K, t), jnp.int32), jax.ShapeDtypeStruct((TOP_K, t), F32)],
        compiler_params=_cparams(("parallel",)),
        name="router",
    )(x, w_router.T, b_router.reshape(N_EXPERTS, 1))


def _expert_kernel(be_ref, nu_ref, x_ref, wg_ref, wu_ref, wd_ref, *rest):
    prev_ref = rest[0] if len(rest) == 5 else None
    o_ref, wgb_ref, wub_ref, wdb_ref = rest[-4:]
    i = pl.program_id(0)

    @pl.when((i == 0) | (be_ref[i] != be_ref[jnp.maximum(i - 1, 0)]))
    def _():
        wgb_ref[...] = wg_ref[0, 0].astype(BF16)
        wub_ref[...] = wu_ref[0, 0].astype(BF16)
        wdb_ref[...] = wd_ref[0, 0].astype(BF16)

    @pl.when(i < nu_ref[0])
    def _():
        x = x_ref[...]
        g = jnp.dot(x, wgb_ref[...], preferred_element_type=F32)
        u = jnp.dot(x, wub_ref[...], preferred_element_type=F32)
        out = jnp.dot((jax.nn.silu(g) * u).astype(BF16), wdb_ref[...], preferred_element_type=F32)
        if prev_ref is not None:
            out = out + prev_ref[...]
        o_ref[...] = out.astype(o_ref.dtype)

    @pl.when(i >= nu_ref[0])
    def _():
        o_ref[...] = jnp.zeros_like(o_ref)


def expert_ffn(block_expert, n_used, x_sorted, w_gate_up, w_down, layer):
    n_rows, d = x_sorted.shape
    de = w_down.shape[2]
    ce = de // EXPERT_CHUNKS
    out = None
    for c in range(EXPERT_CHUNKS):
        last = c == EXPERT_CHUNKS - 1
        row = pl.BlockSpec((MOE_BLK, d), lambda i, be, nu: (i, 0))
        in_specs = [row,
                    pl.BlockSpec((1, 1, d, ce), lambda i, be, nu, c=c: (layer, be[i], 0, c)),
                    pl.BlockSpec((1, 1, d, ce), lambda i, be, nu, c=c: (layer, be[i], 0, EXPERT_CHUNKS + c)),
                    pl.BlockSpec((1, 1, ce, d), lambda i, be, nu, c=c: (layer, be[i], c, 0))]
        args = [x_sorted, w_gate_up, w_gate_up, w_down]
        if out is not None:
            in_specs.append(row)
            args.append(out)
        out = pl.pallas_call(
            _expert_kernel,
            grid_spec=pltpu.PrefetchScalarGridSpec(
                num_scalar_prefetch=2,
                grid=(n_rows // MOE_BLK,),
                in_specs=in_specs,
                out_specs=row,
                scratch_shapes=[pltpu.VMEM((d, ce), BF16), pltpu.VMEM((d, ce), BF16), pltpu.VMEM((ce, d), BF16)]),
            out_shape=jax.ShapeDtypeStruct((n_rows, d), BF16 if last else F32),
            compiler_params=_cparams(("arbitrary",)),
            name="expert_ffn",
        )(block_expert, n_used, *args)
    return out


def _dispatch_tables(expert_idx):
    n_tok = expert_idx.shape[0]
    n_assign = n_tok * TOP_K
    e_flat = expert_idx.reshape(n_assign)
    onehot = (e_flat[:, None] == jnp.arange(N_EXPERTS)[None, :]).astype(jnp.int32)
    csum = jnp.cumsum(onehot, axis=0)
    rank = jnp.take_along_axis(csum, e_flat[:, None], axis=1)[:, 0] - 1
    counts = csum[-1]
    padded = (counts + MOE_BLK - 1) // MOE_BLK * MOE_BLK
    pad_end = jnp.cumsum(padded)
    dest = (pad_end - padded)[e_flat] + rank
    n_rows = n_assign + N_EXPERTS * MOE_BLK
    row_tok = jnp.zeros((n_rows,), jnp.int32).at[dest].set(jnp.arange(n_assign, dtype=jnp.int32) // TOP_K)
    block_expert = jnp.minimum(
        jnp.searchsorted(pad_end, jnp.arange(n_rows // MOE_BLK) * MOE_BLK, side="right"), N_EXPERTS - 1)
    n_used = (pad_end[-1:] // MOE_BLK).astype(jnp.int32)
    return row_tok, block_expert.astype(jnp.int32), n_used, dest.reshape(n_tok, TOP_K)


def _token_mixer(xb, bsz, seq, w_in, layer, conv_a_w, conv_a_b, lru_conv_w, lru_conv_b, lru_wa, lru_ba, lru_wx, lru_bx,
                 lru_lam, cmp_pos, phi_w1, phi_b1, phi_w2, w_branch):
    d = xb.shape[1]
    n_tok = bsz * seq
    main_w = 3 * BRANCH_W + 2 * BRANCH_W + NSA_W + 6 * KV_W
    ng = 3 * NSA_HEADS
    kv0 = main_w - 6 * KV_W
    zmain = matmul_wres(xb, w_in, layer, 0, main_w, BF16, 1024, 768)
    zgate = matmul_wres(xb, w_in, layer, main_w, LANES, F32, 1024, LANES)
    zmerge = matmul(xb, w_in[layer, :, main_w + ng:].astype(BF16), BF16, 1024, 1024)
    z3 = zmain.reshape(bsz, seq, main_w)

    y_a = conv_mixer(z3, conv_a_w, conv_a_b)
    y_b = lru_mixer(z3, 3 * BRANCH_W, lru_conv_w, lru_conv_b, lru_wa, lru_ba, lru_wx, lru_bx, lru_lam)

    kv = z3[:, :, kv0:].reshape(bsz, seq, 6, NSA_KV_HEADS, NSA_HD).transpose(2, 0, 3, 1, 4)
    nslot = seq // CMP_STRIDE
    slots = kv[0:2].reshape(2, bsz, NSA_KV_HEADS, nslot, CMP_STRIDE * NSA_HD)
    cmp = nsa_compress(slots, cmp_pos, phi_w1, phi_b1, phi_w2)
    gate_logits = zgate[:, :ng].reshape(bsz, seq, NSA_KV_HEADS, NSA_GRP * 3).transpose(0, 2, 3, 1)
    y_c = nsa_attention(z3, 5 * BRANCH_W, cmp[0], cmp[1], kv[2], kv[3], kv[4], kv[5], gate_logits)

    flat = lambda y: y.reshape(n_tok, BRANCH_W)
    return branch_merge(flat(y_a), flat(y_b), flat(y_c), w_branch.astype(BF16), zmerge)


def _moe(x, xb, w_router, b_router, w_gate_up, w_down, layer):
    idx_t, gate_t = router(x, w_router, b_router)
    row_tok, block_expert, n_used, dest = _dispatch_tables(idx_t.T)
    rows = expert_ffn(block_expert, n_used, xb[row_tok], w_gate_up, w_down, layer)
    return rows[dest[:, 0]], rows[dest[:, 1]], gate_t.T


def kernel(x, p, w_in, conv_a_w, conv_a_b, lru_conv_w, lru_conv_b, lru_wa, lru_ba, lru_wx, lru_bx, lru_lam, cmp_pos, phi_w1, phi_b1, phi_w2, w_branch, w_out, ln_g, ln_b, w_router, b_router, w_gate_up, w_down, w_ple, w_ple_gate):
    bsz, seq, d = x.shape
    depth = w_in.shape[0]
    n_tok = bsz * seq
    alpha = (2 * depth) ** 0.25
    xf = x.reshape(n_tok, d)
    xb = xf.astype(BF16)
    for i in range(depth):
        merged = _token_mixer(xb, bsz, seq, w_in, i, conv_a_w[i], conv_a_b[i], lru_conv_w[i], lru_conv_b[i],
                              lru_wa[i], lru_ba[i], lru_wx[i], lru_bx[i], lru_lam[i], cmp_pos[i], phi_w1[i],
                              phi_b1[i], phi_w2[i], w_branch[i])
        xf, xb = outproj_ln(merged, w_out[i].astype(BF16), xf, ln_g[i, 0], ln_b[i, 0], alpha)
        y0, y1, gate = _moe(xf, xb, w_router, b_router, w_gate_up, w_down, i)
        xf, xb = ple_combine_ln(xf, xb, w_ple_gate[i].astype(BF16), p[i].reshape(n_tok, -1).astype(BF16),
                                w_ple[i].astype(BF16), y0, y1, gate, ln_g[i, 1], ln_b[i, 1], alpha)
    return xf.reshape(bsz, seq, d)
```

```python
import functools

import ml_dtypes
import numpy as np
import jax
import jax.numpy as jnp
from jax import lax
from jax.experimental import pallas as pl
from jax.experimental.pallas import tpu as pltpu

F32 = jnp.float32
BF16 = jnp.bfloat16

BRANCH_W = 1024
CONV_K = 3
LRU_HEADS = 8
LRU_HD = BRANCH_W // LRU_HEADS
LRU_CONV_K = 4
LRU_C = 8.0
NSA_HEADS = 16
NSA_KV_HEADS = 4
NSA_GRP = NSA_HEADS // NSA_KV_HEADS
NSA_HD = 64
NSA_W = NSA_HEADS * NSA_HD
KV_W = NSA_KV_HEADS * NSA_HD
CMP_BLK = 32
CMP_STRIDE = 16
CMP_HIDDEN = 128
SLC_BLK = 64
SLC_TOPN = 16
WINDOW = 512
N_EXPERTS = 16
N_GROUPS = 4
EXPERTS_PER_GROUP = N_EXPERTS // N_GROUPS
TOP_K = 2
LN_EPS = 1e-5
NEG = -1e30
FORCE = 1e9

LANES = 128
SUBLANES = 8
VMEM_LIMIT = 56 * 1024 * 1024

FEAT = 256
MASK_OFF = NSA_HD
ALIBI_OFF = 2 * NSA_HD
PAD_OFF = ALIBI_OFF + 6
V_ROWS = NSA_HD + 16
LOG2E = 1.4426950408889634

ATT_TQ = 128
ATT_TK = 512
EXP_CHUNK = 64
MOE_BLK = 256
EXPERT_CHUNKS = 2


def _cparams(sem):
    return pltpu.CompilerParams(dimension_semantics=sem, vmem_limit_bytes=VMEM_LIMIT)


def _mm_wres_kernel(x_ref, w_ref, *rest, shift):
    o_ref, wb_ref = rest[-2:]

    @pl.when(pl.program_id(1) == 0)
    def _():
        w = w_ref[0]
        if shift:
            tn = w.shape[1]
            w = jnp.concatenate([w, rest[0][0]], axis=1)[:, shift:shift + tn]
        wb_ref[...] = w.astype(BF16)

    o_ref[...] = jnp.dot(x_ref[...], wb_ref[...], preferred_element_type=F32).astype(o_ref.dtype)


def matmul_wres(x, w3, layer, col0, n, out_dtype, tm, tn):
    m, k = x.shape
    shift = col0 % LANES
    base = col0 - shift
    assert m % tm == 0 and n % tn == 0 and base % tn == 0 and tn % LANES == 0
    in_specs = [pl.BlockSpec((tm, k), lambda j, i: (i, 0)),
                pl.BlockSpec((1, k, tn), lambda j, i: (layer, 0, base // tn + j))]
    args = [x, w3]
    if shift:
        in_specs.append(pl.BlockSpec((1, k, LANES), lambda j, i: (layer, 0, (base + (j + 1) * tn) // LANES)))
        args.append(w3)
    return pl.pallas_call(
        functools.partial(_mm_wres_kernel, shift=shift),
        grid=(n // tn, m // tm),
        in_specs=in_specs,
        out_specs=pl.BlockSpec((tm, tn), lambda j, i: (i, j)),
        out_shape=jax.ShapeDtypeStruct((m, n), out_dtype),
        scratch_shapes=[pltpu.VMEM((k, tn), BF16)],
        compiler_params=_cparams(("parallel", "arbitrary")),
        name="matmul_wres",
    )(*args)


def _causal_conv(ext, cur, w, k):
    acc = cur * w[k - 1:k, :]
    for j in range(k - 1):
        shift = k - 1 - j
        acc = acc + pltpu.roll(ext, shift, 0)[SUBLANES:, :] * w[j:j + 1, :]
    return acc


def _conv_mixer_kernel(ain_ref, ab_ref, ac_ref, w_ref, b_ref, o_ref, halo_ref):
    @pl.when(pl.program_id(2) == 0)
    def _():
        halo_ref[...] = jnp.zeros_like(halo_ref)

    v = ac_ref[0].astype(F32) * ain_ref[0].astype(F32)
    ext = jnp.concatenate([halo_ref[...], v], axis=0)
    y = _causal_conv(ext, v, w_ref[...], CONV_K) + b_ref[...]
    o_ref[0] = (ab_ref[0].astype(F32) * y).astype(o_ref.dtype)
    halo_ref[...] = v[v.shape[0] - SUBLANES:, :]


def conv_mixer(z3, conv_w, conv_b, ts=512, cw=512):
    b, s, _ = z3.shape
    nc = BRANCH_W // cw
    return pl.pallas_call(
        _conv_mixer_kernel,
        grid=(b, nc, s // ts),
        in_specs=[pl.BlockSpec((1, ts, cw), lambda bi, c, si: (bi, si, c)),
                  pl.BlockSpec((1, ts, cw), lambda bi, c, si: (bi, si, c + nc)),
                  pl.BlockSpec((1, ts, cw), lambda bi, c, si: (bi, si, c + 2 * nc)),
                  pl.BlockSpec((CONV_K, cw), lambda bi, c, si: (0, c)),
                  pl.BlockSpec((1, cw), lambda bi, c, si: (0, c))],
        out_specs=pl.BlockSpec((1, ts, cw), lambda bi, c, si: (bi, si, c)),
        out_shape=jax.ShapeDtypeStruct((b, s, BRANCH_W), BF16),
        scratch_shapes=[pltpu.VMEM((SUBLANES, cw), F32)],
        compiler_params=_cparams(("parallel", "parallel", "arbitrary")),
        name="conv_mixer",
    )(z3, z3, z3, conv_w, conv_b.reshape(1, BRANCH_W))


def _lru_kernel(gate_ref, rin_ref, cw_ref, cb_ref, wa_ref, ba_ref, wx_ref, bx_ref, lam_ref, o_ref,
                halo_ref, carry_ref, a_ref, b_ref, h_ref):
    ts, lw = a_ref.shape

    @pl.when(pl.program_id(2) == 0)
    def _():
        halo_ref[...] = jnp.zeros_like(halo_ref)
        carry_ref[...] = jnp.zeros_like(carry_ref)

    u = rin_ref[0].astype(F32)
    ext = jnp.concatenate([halo_ref[...], u], axis=0)
    xc = _causal_conv(ext, u, cw_ref[...], LRU_CONV_K) + cb_ref[...]
    halo_ref[...] = u[ts - SUBLANES:, :]

    xcb = xc.astype(BF16)
    ra, ix = [], []
    for hh in range(lw // LRU_HD):
        xh = xcb[:, hh * LRU_HD:(hh + 1) * LRU_HD]
        ra.append(jnp.dot(xh, wa_ref[hh].astype(BF16), preferred_element_type=F32))
        ix.append(jnp.dot(xh, wx_ref[hh].astype(BF16), preferred_element_type=F32))
    r = jax.nn.sigmoid(jnp.concatenate(ra, axis=1) + ba_ref[...])
    i = jax.nn.sigmoid(jnp.concatenate(ix, axis=1) + bx_ref[...])
    neg_lam = -lam_ref[...]
    softplus = jnp.maximum(neg_lam, 0.0) + jnp.log1p(jnp.exp(-jnp.abs(neg_lam)))
    log_a = -LRU_C * r * softplus
    a_ref[...] = jnp.exp(log_a)
    th = jnp.tanh(log_a)
    b_ref[...] = jnp.sqrt(-2.0 * th / (1.0 - th)) * (i * xc)

    row = lax.broadcasted_iota(jnp.int32, (SUBLANES, lw), 0)

    def group(g, hc):
        sl = pl.ds(pl.multiple_of(g * SUBLANES, SUBLANES), SUBLANES)
        a = a_ref[sl, :]
        bv = b_ref[sl, :]
        for d in (1, 2, 4):
            keep = row >= d
            bv = jnp.where(keep, a * pltpu.roll(bv, d, 0) + bv, bv)
            a = jnp.where(keep, a * pltpu.roll(a, d, 0), a)
        h = a * hc + bv
        h_ref[sl, :] = h
        return jnp.broadcast_to(h[SUBLANES - 1:SUBLANES, :], (SUBLANES, lw))

    carry_ref[...] = lax.fori_loop(0, ts // SUBLANES, group, carry_ref[...])
    o_ref[0] = (h_ref[...] * jax.nn.gelu(gate_ref[0].astype(F32))).astype(o_ref.dtype)


def lru_mixer(z3, col0, conv_w, conv_b, wa, ba, wx, bx, lam, ts=512, lw=256):
    b, s, _ = z3.shape
    nc = BRANCH_W // lw
    c0 = col0 // lw
    hpb = lw // LRU_HD
    row = lambda v: v.reshape(1, BRANCH_W)
    vec = pl.BlockSpec((1, lw), lambda bi, c, si: (0, c))
    mat = pl.BlockSpec((hpb, LRU_HD, LRU_HD), lambda bi, c, si: (c, 0, 0))
    return pl.pallas_call(
        _lru_kernel,
        grid=(b, nc, s // ts),
        in_specs=[pl.BlockSpec((1, ts, lw), lambda bi, c, si: (bi, si, c0 + c)),
                  pl.BlockSpec((1, ts, lw), lambda bi, c, si: (bi, si, c0 + nc + c)),
                  pl.BlockSpec((LRU_CONV_K, lw), lambda bi, c, si: (0, c)),
                  vec, mat, vec, mat, vec, vec],
        out_specs=pl.BlockSpec((1, ts, lw), lambda bi, c, si: (bi, si, c)),
        out_shape=jax.ShapeDtypeStruct((b, s, BRANCH_W), BF16),
        scratch_shapes=[pltpu.VMEM((SUBLANES, lw), F32), pltpu.VMEM((SUBLANES, lw), F32),
                        pltpu.VMEM((ts, lw), F32), pltpu.VMEM((ts, lw), F32), pltpu.VMEM((ts, lw), F32)],
        compiler_params=_cparams(("parallel", "parallel", "arbitrary")),
        name="rg_lru",
    )(z3, z3, conv_w, row(conv_b), wa, row(ba), wx, row(bx), row(lam))


def _compress_kernel(slots_ref, pos_ref, w1_ref, b1_ref, w2_ref, o_ref):
    g, nslot, half = slots_ref.shape[2:]
    rows = g * nslot
    x = slots_ref[0, 0].reshape(rows, half)
    w1 = w1_ref[0].astype(BF16)
    first = jnp.dot(x, w1[:half], preferred_element_type=F32)
    second = jnp.dot(x, w1[half:], preferred_element_type=F32)
    pos = jnp.broadcast_to(pos_ref[0], (SUBLANES, 2 * half)).astype(BF16)
    posb = jnp.dot(pos, w1, preferred_element_type=F32)[0:1, :]
    hidden = first + pltpu.roll(second, rows - 1, 0) + posb + b1_ref[0]
    out = jnp.dot(jax.nn.gelu(hidden).astype(BF16), w2_ref[0].astype(BF16), preferred_element_type=F32)
    o_ref[0, 0] = out.reshape(g, nslot, NSA_HD).astype(o_ref.dtype)


def nsa_compress(slots, cmp_pos, w1, b1, w2):
    _, b, g, nslot, half = slots.shape
    pos = cmp_pos.reshape(2, 1, CMP_BLK * NSA_HD)
    return pl.pallas_call(
        _compress_kernel,
        grid=(2, b),
        in_specs=[pl.BlockSpec((1, 1, g, nslot, half), lambda kv, bi: (kv, bi, 0, 0, 0)),
                  pl.BlockSpec((1, 1, 2 * half), lambda kv, bi: (kv, 0, 0)),
                  pl.BlockSpec((1, 2 * half, CMP_HIDDEN), lambda kv, bi: (kv, 0, 0)),
                  pl.BlockSpec((1, 1, CMP_HIDDEN), lambda kv, bi: (kv, 0, 0)),
                  pl.BlockSpec((1, CMP_HIDDEN, NSA_HD), lambda kv, bi: (kv, 0, 0))],
        out_specs=pl.BlockSpec((1, 1, g, nslot, NSA_HD), lambda kv, bi: (kv, bi, 0, 0, 0)),
        out_shape=jax.ShapeDtypeStruct((2, b, g, nslot, NSA_HD), BF16),
        compiler_params=_cparams(("parallel", "parallel")),
        name="nsa_compress",
    )(slots, pos, w1, b1.reshape(2, 1, CMP_HIDDEN), w2)


def _exp2_bf16(s, m):
    rows = s.shape[0]
    chunk = min(EXP_CHUNK, rows)
    return jnp.concatenate([jnp.exp2((s[r:r + chunk, :] - m).astype(BF16)) for r in range(0, rows, chunk)], axis=0)


def _nsa_kernel(q_ref, qf_ref, kc_ref, vc_ref, ks_ref, vs_ref, kw_ref, vw_ref, gl_ref, c2s_ref, bias_ref, o_ref,
                qp_ref, imp_ref, s0_ref, s1_ref, mx_ref, m_ref, acc_ref, *, n_cmp):
    tq = q_ref.shape[1]
    lanes = NSA_GRP * tq
    seq = ks_ref.shape[2]
    n_slc = seq // SLC_BLK
    t0 = pl.multiple_of(pl.program_id(2) * tq, tq)
    head = lambda r: slice(r * tq, (r + 1) * tq)

    def scores(k_tile, bias=None):
        s = jnp.dot(k_tile, qp_ref[...], preferred_element_type=F32)
        return s if bias is None else s + bias

    qt = q_ref[0].astype(F32).T * (NSA_HD ** -0.5 * LOG2E)
    qp_ref[...] = qf_ref[0]
    for r in range(NSA_GRP):
        qp_ref[0:NSA_HD, head(r)] = qt[r * NSA_HD:(r + 1) * NSA_HD, :].astype(BF16)

    s_cmp = scores(kc_ref[0, 0])
    win = []
    for off, size, bias in ((0, tq, bias_ref[1]), (tq, WINDOW - tq, None), (WINDOW, tq, bias_ref[0])):
        k0 = pl.multiple_of(t0 + off, tq)
        win.append((scores(kw_ref[0, 0, pl.ds(k0, size), :], bias), vw_ref[0, 0, :, pl.ds(k0, size)]))

    ncp = kc_ref.shape[2]
    n_row = lax.broadcasted_iota(jnp.int32, (ncp, lanes), 0)
    t_lane = t0 + lax.broadcasted_iota(jnp.int32, (ncp, lanes), 1) % tq
    ok = (t_lane >= n_row * CMP_STRIDE + (CMP_BLK - 1)) & (n_row < n_cmp)
    s = jnp.where(ok, s_cmp, NEG)
    p = jnp.where(ok, jnp.exp2(s - jnp.max(s, axis=0, keepdims=True)), 0.0)
    l = jnp.sum(p, axis=0, keepdims=True)
    p_cmp = p * (1.0 / jnp.where(l > 0.0, l, 1.0))
    o_cmp = jnp.dot(vc_ref[0, 0], p_cmp.astype(BF16), preferred_element_type=F32)

    rest = p_cmp[:, head(0)]
    for r in range(1, NSA_GRP):
        rest = rest + p_cmp[:, head(r)]
    imp = jnp.zeros((n_slc, tq), F32)
    for _ in range(3):
        part = rest.astype(BF16)
        rest = rest - part.astype(F32)
        imp = imp + jnp.dot(c2s_ref[...], part, preferred_element_type=F32)
    j_row = lax.broadcasted_iota(jnp.int32, (n_slc, tq), 0)
    t_col = t0 + lax.broadcasted_iota(jnp.int32, (n_slc, tq), 1)
    forced = (j_row == 0) | (j_row == t_col // SLC_BLK)
    causal = j_row * SLC_BLK <= t_col
    imp = jnp.where(forced, FORCE, jnp.where(causal, imp, -FORCE))
    imp_ref[...] = imp

    mx = functools.reduce(jnp.maximum, [jnp.max(s, axis=0, keepdims=True) for s, _ in win])
    o_win = jnp.zeros((V_ROWS, lanes), F32)
    for s, v_tile in win:
        o_win = o_win + jnp.dot(v_tile, _exp2_bf16(s, mx), preferred_element_type=F32)
    o_win = o_win[0:NSA_HD] * (1.0 / o_win[NSA_HD:NSA_HD + 1])

    n_grp = n_slc // SUBLANES
    vals = [imp[v * SUBLANES:(v + 1) * SUBLANES, :] for v in range(n_grp)]
    sub = lax.broadcasted_iota(jnp.int32, (SUBLANES, tq), 0)
    rank = [jnp.zeros((SUBLANES, tq), jnp.int32) for _ in range(n_grp)]
    for i in range(n_slc):
        bi = jnp.broadcast_to(imp_ref[i:i + 1, :], (SUBLANES, tq))
        for v in range(n_grp):
            if v < i // SUBLANES:
                ahead = bi > vals[v]
            elif v > i // SUBLANES:
                ahead = bi >= vals[v]
            else:
                ahead = (bi > vals[v]) | ((bi == vals[v]) & (sub > i % SUBLANES))
            rank[v] = rank[v] + jnp.where(ahead, 1, 0)
    chosen = (jnp.concatenate(rank, axis=0) < min(SLC_TOPN, n_slc)) & causal
    mask = jnp.where(chosen, 0.0, NEG).astype(BF16)
    for r in range(NSA_GRP):
        qp_ref[MASK_OFF:MASK_OFF + n_slc, head(r)] = mask

    tk = min(ATT_TK, seq)
    n_full = t0 // tk
    last = n_full // 2

    s_bufs = (s0_ref, s1_ref)

    def pair_scores(step, buf):
        for par in range(2):
            k0 = pl.multiple_of((2 * step + par) * tk, tk)
            s = scores(ks_ref[0, 0, pl.ds(k0, tk), :])
            s_bufs[buf][par] = s
            mx_ref[buf, par] = jnp.max(s, axis=0, keepdims=True)

    def pair_update(step, buf, prefetch):
        m_new = [jnp.maximum(m_ref[par], mx_ref[buf, par]) for par in range(2)]
        probs = [_exp2_bf16(s_bufs[buf].at[par], m_new[par]) for par in range(2)]
        if prefetch:
            pair_scores(step + 1, 1 - buf)
        for par in range(2):
            k0 = pl.multiple_of((2 * step + par) * tk, tk)
            pv = jnp.dot(vs_ref[0, 0, :, pl.ds(k0, tk)], probs[par], preferred_element_type=F32)
            acc_ref[par] = jnp.exp2(m_ref[par] - m_new[par]) * acc_ref[par] + pv
            m_ref[par] = m_new[par]

    def mask_diagonal(buf):
        off = pl.multiple_of(t0 - n_full * tk, tq)
        for par in range(2):
            @pl.when(n_full % 2 == par)
            def _():
                tile = s_bufs[buf].at[par]
                tile[pl.ds(off, tq), :] = tile[pl.ds(off, tq), :] + bias_ref[0]
                mx_ref[buf, par] = jnp.max(tile[...], axis=0, keepdims=True)

    def half_trip(step, buf):
        @pl.when(step == last)
        def _():
            mask_diagonal(buf)

        @pl.when(step < last)
        def _():
            pair_update(step, buf, prefetch=True)

        @pl.when(step == last)
        def _():
            pair_update(step, buf, prefetch=False)

    def trip(i, carry):
        half_trip(2 * i, 0)
        half_trip(2 * i + 1, 1)
        return carry

    m_ref[...] = jnp.full(m_ref.shape, NEG, F32)
    acc_ref[...] = jnp.zeros(acc_ref.shape, F32)
    pair_scores(0, 0)
    lax.fori_loop(0, last // 2 + 1, trip, 0)
    m_all = jnp.maximum(m_ref[0], m_ref[1])
    o_slc = jnp.exp2(m_ref[0] - m_all) * acc_ref[0] + jnp.exp2(m_ref[1] - m_all) * acc_ref[1]
    o_slc = o_slc[0:NSA_HD] * (1.0 / o_slc[NSA_HD:NSA_HD + 1])

    gates = jax.nn.sigmoid(gl_ref[0, 0].astype(F32))
    outs = []
    for r in range(NSA_GRP):
        outs.append(gates[3 * r:3 * r + 1] * o_cmp[:, head(r)] + gates[3 * r + 1:3 * r + 2] * o_slc[:, head(r)]
                    + gates[3 * r + 2:3 * r + 3] * o_win[:, head(r)])
    o_ref[0] = jnp.concatenate(outs, axis=0).T.astype(o_ref.dtype)


def _alibi_slopes():
    return (2.0 ** (-8.0 * np.arange(1, NSA_HEADS + 1) / NSA_HEADS)).astype(np.float32)


def _bf16_split3(v):
    parts, rest = [], np.asarray(v, np.float32)
    for _ in range(3):
        part = rest.astype(ml_dtypes.bfloat16).astype(np.float32)
        parts.append(part)
        rest = (rest.astype(np.float64) - part.astype(np.float64)).astype(np.float32)
    return parts


def _query_features(tq):
    feat = np.zeros((NSA_HEADS, FEAT), np.float32)
    hi, mid, lo = _bf16_split3(_alibi_slopes() * np.float32(LOG2E))
    for c, part in enumerate((hi, mid, lo, hi, mid, lo)):
        feat[:, ALIBI_OFF + c] = part
    feat[:, PAD_OFF] = NEG
    feat = np.repeat(feat.reshape(NSA_KV_HEADS, NSA_GRP, FEAT), tq, axis=1)
    return jnp.asarray(feat.transpose(0, 2, 1), BF16)


def _key_features(pos, n_slc):
    pos = np.asarray(pos)
    real = pos >= 0
    feat = np.zeros((pos.shape[0], FEAT - NSA_HD), np.float32)
    if n_slc:
        feat[real, pos[real] // SLC_BLK] = 1.0
    a0 = ALIBI_OFF - NSA_HD
    feat[real, a0:a0 + 3] = ((pos[real] // SLC_BLK) * SLC_BLK)[:, None]
    feat[real, a0 + 3:a0 + 6] = (pos[real] % SLC_BLK)[:, None]
    feat[~real, PAD_OFF - NSA_HD] = 1.0
    return feat


def _values_t(v):
    b, g, n, _ = v.shape
    extra = jnp.zeros((b, g, V_ROWS - NSA_HD, n), v.dtype).at[:, :, 0].set(1.0)
    return jnp.concatenate([jnp.swapaxes(v, 2, 3), extra], axis=2)


def _with_features(k, feat):
    b, g, _, _ = k.shape
    f = jnp.broadcast_to(jnp.asarray(feat, BF16), (b, g) + feat.shape)
    return jnp.concatenate([k.astype(BF16), f], axis=3)


def nsa_attention(z3, qcol0, kc, vc, ks, vs, kw, vw, gate_logits):
    b, s, _ = z3.shape
    tq = ATT_TQ
    tk = min(ATT_TK, s)
    assert s % (2 * tk) == 0 and tk % tq == 0 and tq % SLC_BLK == 0 and WINDOW % tq == 0 and WINDOW > tq
    n_slc = s // SLC_BLK
    assert n_slc <= ALIBI_OFF - MASK_OFF and n_slc % SUBLANES == 0
    n_cmp = (s - CMP_BLK) // CMP_STRIDE + 1
    ncp = kc.shape[2]
    gw = NSA_GRP * NSA_HD
    lanes = NSA_GRP * tq

    kc_f = _with_features(kc, _key_features(np.arange(ncp) * CMP_STRIDE + CMP_BLK - 1, 0))
    ks_f = _with_features(ks, _key_features(np.arange(s), n_slc))
    front = ((0, 0), (0, 0), (WINDOW, 0), (0, 0))
    kw_f = _with_features(jnp.pad(kw, front), _key_features(np.arange(-WINDOW, s), 0))
    t = lambda v: jnp.swapaxes(v, 2, 3)
    vw_t = _values_t(jnp.pad(vw, front))

    c_start = np.arange(ncp) * CMP_STRIDE
    s_start = np.arange(n_slc) * SLC_BLK
    overlap = np.clip(np.minimum(c_start[:, None] + CMP_BLK, s_start[None, :] + SLC_BLK)
                      - np.maximum(c_start[:, None], s_start[None, :]), 0, None) / CMP_BLK
    overlap[n_cmp:] = 0.0
    c2s_t = jnp.asarray(overlap.T, BF16)

    key = np.arange(tq)[:, None]
    qry = np.tile(np.arange(tq), NSA_GRP)[None, :]
    bias = jnp.asarray(np.stack([np.where(key <= qry, 0.0, NEG), np.where(key > qry, 0.0, NEG)]), F32)

    blk = lambda *shape: pl.BlockSpec((1, 1) + shape, lambda bi, g, qi: (bi, g, 0, 0))
    const = lambda *shape: pl.BlockSpec(shape, lambda bi, g, qi: (0,) * len(shape))
    return pl.pallas_call(
        functools.partial(_nsa_kernel, n_cmp=n_cmp),
        grid=(b, NSA_KV_HEADS, s // tq),
        in_specs=[pl.BlockSpec((1, tq, gw), lambda bi, g, qi: (bi, qi, qcol0 // gw + g)),
                  pl.BlockSpec((1, FEAT, lanes), lambda bi, g, qi: (g, 0, 0)),
                  blk(ncp, FEAT), blk(NSA_HD, ncp),
                  blk(s, FEAT), blk(V_ROWS, s),
                  blk(s + WINDOW, FEAT), blk(V_ROWS, s + WINDOW),
                  pl.BlockSpec((1, 1, NSA_GRP * 3, tq), lambda bi, g, qi: (bi, g, 0, qi)),
                  const(n_slc, ncp), const(2, tq, lanes)],
        out_specs=pl.BlockSpec((1, tq, gw), lambda bi, g, qi: (bi, qi, g)),
        out_shape=jax.ShapeDtypeStruct((b, s, NSA_W), BF16),
        scratch_shapes=[pltpu.VMEM((FEAT, lanes), BF16), pltpu.VMEM((n_slc, tq), F32),
                        pltpu.VMEM((2, tk, lanes), F32), pltpu.VMEM((2, tk, lanes), F32),
                        pltpu.VMEM((2, 2, 1, lanes), F32),
                        pltpu.VMEM((2, 1, lanes), F32), pltpu.VMEM((2, V_ROWS, lanes), F32)],
        compiler_params=_cparams(("parallel", "parallel", "arbitrary")),
        name="nsa_attention",
    )(z3, _query_features(tq), kc_f, t(vc), ks_f, _values_t(vs), kw_f, vw_t, gate_logits, c2s_t, bias)


def _merge_kernel(ya_ref, yb_ref, yc_ref, wb_ref, ga_ref, gb_ref, gc_ref, o_ref):
    acc = None
    for i, (y_ref, g_ref) in enumerate(((ya_ref, ga_ref), (yb_ref, gb_ref), (yc_ref, gc_ref))):
        term = jax.nn.sigmoid(g_ref[...].astype(F32)) * jnp.dot(y_ref[...], wb_ref[i], preferred_element_type=F32)
        acc = term if acc is None else acc + term
    o_ref[...] = acc.astype(o_ref.dtype)


def branch_merge(ya, yb, yc, wb, zmerge, tm=1024, tn=512):
    t = ya.shape[0]
    d = wb.shape[2]
    nj = d // tn
    y_spec = pl.BlockSpec((tm, BRANCH_W), lambda i, j: (i, 0))
    gate = lambda br: pl.BlockSpec((tm, tn), lambda i, j: (i, j + br * nj))
    return pl.pallas_call(
        _merge_kernel,
        grid=(t // tm, nj),
        in_specs=[y_spec, y_spec, y_spec,
                  pl.BlockSpec((3, BRANCH_W, tn), lambda i, j: (0, 0, j)),
                  gate(0), gate(1), gate(2)],
        out_specs=pl.BlockSpec((tm, tn), lambda i, j: (i, j)),
        out_shape=jax.ShapeDtypeStruct((t, d), BF16),
        compiler_params=_cparams(("parallel", "parallel")),
        name="branch_merge",
    )(ya, yb, yc, wb, zmerge, zmerge, zmerge)


def _layer_norm(y, g, b):
    mu = jnp.mean(y, axis=-1, keepdims=True)
    yc = y - mu
    var = jnp.mean(yc * yc, axis=-1, keepdims=True)
    return yc * lax.rsqrt(var + LN_EPS) * g + b


def _outproj_ln_kernel(m_ref, w_ref, x_ref, g_ref, b_ref, o_ref, ob_ref, *, alpha):
    mix = jnp.dot(m_ref[...], w_ref[...], preferred_element_type=F32)
    y = _layer_norm(alpha * x_ref[...] + mix, g_ref[...], b_ref[...])
    o_ref[...] = y
    ob_ref[...] = y.astype(BF16)


def outproj_ln(merged, w_out, x, g, b, alpha, tm=256):
    t, d = x.shape
    row = pl.BlockSpec((tm, d), lambda i: (i, 0))
    vec = pl.BlockSpec((1, d), lambda i: (0, 0))
    return pl.pallas_call(
        functools.partial(_outproj_ln_kernel, alpha=alpha),
        grid=(t // tm,),
        in_specs=[row, pl.BlockSpec((d, d), lambda i: (0, 0)), row, vec, vec],
        out_specs=[row, row],
        out_shape=[jax.ShapeDtypeStruct((t, d), F32), jax.ShapeDtypeStruct((t, d), BF16)],
        compiler_params=_cparams(("parallel",)),
        name="outproj_ln",
    )(merged, w_out, x, g.reshape(1, d), b.reshape(1, d))


def _ple_kernel(xb_ref, wg_ref, p_ref, wp_ref, o_ref):
    ple = jax.nn.sigmoid(jnp.dot(xb_ref[...], wg_ref[...], preferred_element_type=F32)) * jnp.dot(
        p_ref[...], wp_ref[...], preferred_element_type=F32)
    o_ref[...] = ple.astype(o_ref.dtype)


def gated_embedding(xb, w_ple_gate, p, w_ple, tm=512):
    t, d = xb.shape
    pd = p.shape[1]
    return pl.pallas_call(
        _ple_kernel,
        grid=(t // tm,),
        in_specs=[pl.BlockSpec((tm, d), lambda i: (i, 0)), pl.BlockSpec((d, d), lambda i: (0, 0)),
                  pl.BlockSpec((tm, pd), lambda i: (i, 0)), pl.BlockSpec((pd, d), lambda i: (0, 0))],
        out_specs=pl.BlockSpec((tm, d), lambda i: (i, 0)),
        out_shape=jax.ShapeDtypeStruct((t, d), BF16),
        compiler_params=_cparams(("parallel",)),
        name="gated_embedding",
    )(xb, w_ple_gate, p, w_ple)


def _combine_ln_kernel(x_ref, ple_ref, y0_ref, y1_ref, gate_ref, g_ref, b_ref, o_ref, ob_ref, *, alpha):
    gate = gate_ref[...]
    ffn = gate[:, 0:1] * y0_ref[...].astype(F32) + gate[:, 1:2] * y1_ref[...].astype(F32)
    y = _layer_norm(alpha * x_ref[...] + ffn + ple_ref[...].astype(F32), g_ref[...], b_ref[...])
    o_ref[...] = y
    ob_ref[...] = y.astype(BF16)


def combine_ln(x, ple, y0, y1, gate, g, b, alpha, tm=512):
    t, d = x.shape
    row = pl.BlockSpec((tm, d), lambda i: (i, 0))
    vec = pl.BlockSpec((1, d), lambda i: (0, 0))
    return pl.pallas_call(
        functools.partial(_combine_ln_kernel, alpha=alpha),
        grid=(t // tm,),
        in_specs=[row, row, row, row, pl.BlockSpec((tm, TOP_K), lambda i: (i, 0)), vec, vec],
        out_specs=[row, row],
        out_shape=[jax.ShapeDtypeStruct((t, d), F32), jax.ShapeDtypeStruct((t, d), BF16)],
        compiler_params=_cparams(("parallel",)),
        name="combine_ln",
    )(x, ple, y0, y1, gate, g.reshape(1, d), b.reshape(1, d))


def _router_kernel(x_ref, wr_ref, br_ref, idx_ref, gate_ref):
    tm = x_ref.shape[0]
    logits = lax.dot_general(wr_ref[...], x_ref[...], (((1,), (1,)), ((), ())),
                             precision=lax.Precision.HIGHEST, preferred_element_type=F32)
    aff = jax.nn.sigmoid(logits)
    sel = aff + br_ref[...]
    scores = []
    for g in range(N_GROUPS):
        v = [sel[g * EXPERTS_PER_GROUP + e:g * EXPERTS_PER_GROUP + e + 1, :] for e in range(EXPERTS_PER_GROUP)]
        best = None
        for a in range(EXPERTS_PER_GROUP):
            for c in range(a + 1, EXPERTS_PER_GROUP):
                pair = v[a] + v[c]
                best = pair if best is None else jnp.maximum(best, pair)
        scores.append(best)
    best_score, best_group = scores[0], jnp.zeros((1, tm), jnp.int32)
    for g in range(1, N_GROUPS):
        better = scores[g] > best_score
        best_score = jnp.where(better, scores[g], best_score)
        best_group = jnp.where(better, g, best_group)
    e_row = lax.broadcasted_iota(jnp.int32, (N_EXPERTS, tm), 0)
    cand = jnp.where(e_row // EXPERTS_PER_GROUP == best_group, sel, NEG)
    picks, gates = [], []
    for _ in range(TOP_K):
        mx = jnp.max(cand, axis=0, keepdims=True)
        pick = jnp.min(jnp.where(cand == mx, e_row, N_EXPERTS), axis=0, keepdims=True)
        hit = e_row == pick
        picks.append(pick)
        gates.append(jnp.sum(jnp.where(hit, aff, 0.0), axis=0, keepdims=True))
        cand = jnp.where(hit, -jnp.inf, cand)
    total = gates[0] + gates[1]
    idx_ref[...] = jnp.concatenate(picks, axis=0)
    gate_ref[...] = jnp.concatenate([gates[0] / total, gates[1] / total], axis=0)


def router(x, w_router, b_router, tm=1024):
    t, d = x.shape
    return pl.pallas_call(
        _router_kernel,
        grid=(t // tm,),
        in_specs=[pl.BlockSpec((tm, d), lambda i: (i, 0)),
                  pl.BlockSpec((N_EXPERTS, d), lambda i: (0, 0)),
                  pl.BlockSpec((N_EXPERTS, 1), lambda i: (0, 0))],
        out_specs=[pl.BlockSpec((TOP_K, tm), lambda i: (0, i)), pl.BlockSpec((TOP_K, tm), lambda i: (0, i))],
        out_shape=[jax.ShapeDtypeStruct((TOP_K, t), jnp.int32), jax.ShapeDtypeStruct((TOP_K, t), F32)],
        compiler_params=_cparams(("parallel",)),
        name="router",
    )(x, w_router.T, b_router.reshape(N_EXPERTS, 1))


def _expert_kernel(be_ref, nu_ref, x_ref, wg_ref, wu_ref, wd_ref, *rest):
    prev_ref = rest[0] if len(rest) == 5 else None
    o_ref, wgb_ref, wub_ref, wdb_ref = rest[-4:]
    i = pl.program_id(0)

    @pl.when((i == 0) | (be_ref[i] != be_ref[jnp.maximum(i - 1, 0)]))
    def _():
        wgb_ref[...] = wg_ref[0, 0].astype(BF16)
        wub_ref[...] = wu_ref[0, 0].astype(BF16)
        wdb_ref[...] = wd_ref[0, 0].astype(BF16)

    @pl.when(i < nu_ref[0])
    def _():
        x = x_ref[...]
        g = jnp.dot(x, wgb_ref[...], preferred_element_type=F32)
        u = jnp.dot(x, wub_ref[...], preferred_element_type=F32)
        out = jnp.dot((jax.nn.silu(g) * u).astype(BF16), wdb_ref[...], preferred_element_type=F32)
        if prev_ref is not None:
            out = out + prev_ref[...]
        o_ref[...] = out.astype(o_ref.dtype)

    @pl.when(i >= nu_ref[0])
    def _():
        o_ref[...] = jnp.zeros_like(o_ref)


def expert_ffn(block_expert, n_used, x_sorted, w_gate_up, w_down, layer):
    n_rows, d = x_sorted.shape
    de = w_down.shape[2]
    ce = de // EXPERT_CHUNKS
    out = None
    for c in range(EXPERT_CHUNKS):
        last = c == EXPERT_CHUNKS - 1
        row = pl.BlockSpec((MOE_BLK, d), lambda i, be, nu: (i, 0))
        in_specs = [row,
                    pl.BlockSpec((1, 1, d, ce), lambda i, be, nu, c=c: (layer, be[i], 0, c)),
                    pl.BlockSpec((1, 1, d, ce), lambda i, be, nu, c=c: (layer, be[i], 0, EXPERT_CHUNKS + c)),
                    pl.BlockSpec((1, 1, ce, d), lambda i, be, nu, c=c: (layer, be[i], c, 0))]
        args = [x_sorted, w_gate_up, w_gate_up, w_down]
        if out is not None:
            in_specs.append(row)
            args.append(out)
        out = pl.pallas_call(
            _expert_kernel,
            grid_spec=pltpu.PrefetchScalarGridSpec(
                num_scalar_prefetch=2,
                grid=(n_rows // MOE_BLK,),
                in_specs=in_specs,
                out_specs=row,
                scratch_shapes=[pltpu.VMEM((d, ce), BF16), pltpu.VMEM((d, ce), BF16), pltpu.VMEM((ce, d), BF16)]),
            out_shape=jax.ShapeDtypeStruct((n_rows, d), BF16 if last else F32),
            compiler_params=_cparams(("arbitrary",)),
            name="expert_ffn",
        )(block_expert, n_used, *args)
    return out


def _dispatch_tables(expert_idx):
    n_tok = expert_idx.shape[0]
    n_assign = n_tok * TOP_K
    e_flat = expert_idx.reshape(n_assign)
    onehot = (e_flat[:, None] == jnp.arange(N_EXPERTS)[None, :]).astype(jnp.int32)
    csum = jnp.cumsum(onehot, axis=0)
    rank = jnp.take_along_axis(csum, e_flat[:, None], axis=1)[:, 0] - 1
    counts = csum[-1]
    padded = (counts + MOE_BLK - 1) // MOE_BLK * MOE_BLK
    pad_end = jnp.cumsum(padded)
    dest = (pad_end - padded)[e_flat] + rank
    n_rows = n_assign + N_EXPERTS * MOE_BLK
    row_tok = jnp.zeros((n_rows,), jnp.int32).at[dest].set(jnp.arange(n_assign, dtype=jnp.int32) // TOP_K)
    block_expert = jnp.minimum(
        jnp.searchsorted(pad_end, jnp.arange(n_rows // MOE_BLK) * MOE_BLK, side="right"), N_EXPERTS - 1)
    n_used = (pad_end[-1:] // MOE_BLK).astype(jnp.int32)
    return row_tok, block_expert.astype(jnp.int32), n_used, dest.reshape(n_tok, TOP_K)


def _token_mixer(xb, bsz, seq, w_in, layer, conv_a_w, conv_a_b, lru_conv_w, lru_conv_b, lru_wa, lru_ba, lru_wx, lru_bx,
                 lru_lam, cmp_pos, phi_w1, phi_b1, phi_w2, w_branch):
    d = xb.shape[1]
    n_tok = bsz * seq
    main_w = 3 * BRANCH_W + 2 * BRANCH_W + NSA_W + 6 * KV_W
    ng = 3 * NSA_HEADS
    kv0 = main_w - 6 * KV_W
    zmain = matmul_wres(xb, w_in, layer, 0, main_w, BF16, 1024, 768)
    zgate = matmul_wres(xb, w_in, layer, main_w, LANES, F32, 1024, LANES)
    zmerge = matmul_wres(xb, w_in, layer, main_w + ng, w_in.shape[2] - main_w - ng, BF16, 1024, 512)
    z3 = zmain.reshape(bsz, seq, main_w)

    y_a = conv_mixer(z3, conv_a_w, conv_a_b)
    y_b = lru_mixer(z3, 3 * BRANCH_W, lru_conv_w, lru_conv_b, lru_wa, lru_ba, lru_wx, lru_bx, lru_lam)

    kv = z3[:, :, kv0:].reshape(bsz, seq, 6, NSA_KV_HEADS, NSA_HD).transpose(2, 0, 3, 1, 4)
    nslot = seq // CMP_STRIDE
    slots = kv[0:2].reshape(2, bsz, NSA_KV_HEADS, nslot, CMP_STRIDE * NSA_HD)
    cmp = nsa_compress(slots, cmp_pos, phi_w1, phi_b1, phi_w2)
    gate_logits = zgate[:, :ng].reshape(bsz, seq, NSA_KV_HEADS, NSA_GRP * 3).transpose(0, 2, 3, 1)
    y_c = nsa_attention(z3, 5 * BRANCH_W, cmp[0], cmp[1], kv[2], kv[3], kv[4], kv[5], gate_logits)

    flat = lambda y: y.reshape(n_tok, BRANCH_W)
    return branch_merge(flat(y_a), flat(y_b), flat(y_c), w_branch.astype(BF16), zmerge)


def _moe(x, xb, w_router, b_router, w_gate_up, w_down, layer):
    idx_t, gate_t = router(x, w_router, b_router)
    row_tok, block_expert, n_used, dest = _dispatch_tables(idx_t.T)
    rows = expert_ffn(block_expert, n_used, xb[row_tok], w_gate_up, w_down, layer)
    return rows[dest[:, 0]], rows[dest[:, 1]], gate_t.T


def kernel(x, p, w_in, conv_a_w, conv_a_b, lru_conv_w, lru_conv_b, lru_wa, lru_ba, lru_wx, lru_bx, lru_lam, cmp_pos, phi_w1, phi_b1, phi_w2, w_branch, w_out, ln_g, ln_b, w_router, b_router, w_gate_up, w_down, w_ple, w_ple_gate):
    bsz, seq, d = x.shape
    depth = w_in.shape[0]
    n_tok = bsz * seq
    alpha = (2 * depth) ** 0.25
    xf = x.reshape(n_tok, d)
    xb = xf.astype(BF16)
    for i in range(depth):
        merged = _token_mixer(xb, bsz, seq, w_in, i, conv_a_w[i], conv_a_b[i], lru_conv_w[i], lru_conv_b[i],
                              lru_wa[i], lru_ba[i], lru_wx[i], lru_bx[i], lru_lam[i], cmp_pos[i], phi_w1[i],
                              phi_b1[i], phi_w2[i], w_branch[i])
        xf, xb = outproj_ln(merged, w_out[i].astype(BF16), xf, ln_g[i, 0], ln_b[i, 0], alpha)
        ple = gated_embedding(xb, w_ple_gate[i].astype(BF16), p[i].reshape(n_tok, -1).astype(BF16),
                              w_ple[i].astype(BF16))
        y0, y1, gate = _moe(xf, xb, w_router, b_router, w_gate_up, w_down, i)
        xf, xb = combine_ln(xf, ple, y0, y1, gate, ln_g[i, 1], ln_b[i, 1], alpha)
    return xf.reshape(bsz, seq, d)
```

```python
import functools

import ml_dtypes
import numpy as np
import jax
import jax.numpy as jnp
from jax import lax
from jax.experimental import pallas as pl
from jax.experimental.pallas import tpu as pltpu

F32 = jnp.float32
BF16 = jnp.bfloat16

BRANCH_W = 1024
CONV_K = 3
LRU_HEADS = 8
LRU_HD = BRANCH_W // LRU_HEADS
LRU_CONV_K = 4
LRU_C = 8.0
NSA_HEADS = 16
NSA_KV_HEADS = 4
NSA_GRP = NSA_HEADS // NSA_KV_HEADS
NSA_HD = 64
NSA_W = NSA_HEADS * NSA_HD
KV_W = NSA_KV_HEADS * NSA_HD
CMP_BLK = 32
CMP_STRIDE = 16
CMP_HIDDEN = 128
SLC_BLK = 64
SLC_TOPN = 16
WINDOW = 512
N_EXPERTS = 16
N_GROUPS = 4
EXPERTS_PER_GROUP = N_EXPERTS // N_GROUPS
TOP_K = 2
LN_EPS = 1e-5
NEG = -1e30
FORCE = 1e9

LANES = 128
SUBLANES = 8
VMEM_LIMIT = 56 * 1024 * 1024

FEAT = 256
MASK_OFF = NSA_HD
ALIBI_OFF = 2 * NSA_HD
PAD_OFF = ALIBI_OFF + 6
V_ROWS = NSA_HD + 16
LOG2E = 1.4426950408889634

ATT_TQ = 256
ATT_TK = 512
EXP_CHUNK = 64
MOE_BLK = 256
EXPERT_CHUNKS = 2


def _cparams(sem):
    return pltpu.CompilerParams(dimension_semantics=sem, vmem_limit_bytes=VMEM_LIMIT)


def _mm_wres_kernel(x_ref, w_ref, *rest, shift):
    o_ref, wb_ref = rest[-2:]

    @pl.when(pl.program_id(1) == 0)
    def _():
        w = w_ref[0]
        if shift:
            tn = w.shape[0]
            w = jnp.concatenate([w, rest[0][0]], axis=0)[shift:shift + tn]
        wb_ref[...] = w.astype(BF16)

    o_ref[...] = lax.dot_general(x_ref[...], wb_ref[...], (((1,), (1,)), ((), ())),
                                 preferred_element_type=F32).astype(o_ref.dtype)


def matmul_wres(x, wt3, layer, row0, n, out_dtype, tm, tn):
    m, k = x.shape
    shift = row0 % tn
    base = row0 - shift
    assert m % tm == 0 and n % tn == 0 and shift % SUBLANES == 0 and shift <= LANES and tn % LANES == 0
    in_specs = [pl.BlockSpec((tm, k), lambda j, i: (i, 0)),
                pl.BlockSpec((1, tn, k), lambda j, i: (layer, base // tn + j, 0))]
    args = [x, wt3]
    if shift:
        in_specs.append(pl.BlockSpec((1, LANES, k), lambda j, i: (layer, (base + (j + 1) * tn) // LANES, 0)))
        args.append(wt3)
    return pl.pallas_call(
        functools.partial(_mm_wres_kernel, shift=shift),
        grid=(n // tn, m // tm),
        in_specs=in_specs,
        out_specs=pl.BlockSpec((tm, tn), lambda j, i: (i, j)),
        out_shape=jax.ShapeDtypeStruct((m, n), out_dtype),
        scratch_shapes=[pltpu.VMEM((tn, k), BF16)],
        compiler_params=_cparams(("parallel", "arbitrary")),
        name="matmul_wres",
    )(*args)


def _causal_conv(ext, cur, w, k):
    acc = cur * w[k - 1:k, :]
    for j in range(k - 1):
        shift = k - 1 - j
        acc = acc + pltpu.roll(ext, shift, 0)[SUBLANES:, :] * w[j:j + 1, :]
    return acc


def _conv_mixer_kernel(ain_ref, ab_ref, ac_ref, w_ref, b_ref, o_ref, halo_ref):
    @pl.when(pl.program_id(2) == 0)
    def _():
        halo_ref[...] = jnp.zeros_like(halo_ref)

    v = ac_ref[0].astype(F32) * ain_ref[0].astype(F32)
    ext = jnp.concatenate([halo_ref[...], v], axis=0)
    y = _causal_conv(ext, v, w_ref[...], CONV_K) + b_ref[...]
    o_ref[0] = (ab_ref[0].astype(F32) * y).astype(o_ref.dtype)
    halo_ref[...] = v[v.shape[0] - SUBLANES:, :]


def conv_mixer(z3, conv_w, conv_b, ts=512, cw=512):
    b, s, _ = z3.shape
    nc = BRANCH_W // cw
    return pl.pallas_call(
        _conv_mixer_kernel,
        grid=(b, nc, s // ts),
        in_specs=[pl.BlockSpec((1, ts, cw), lambda bi, c, si: (bi, si, c)),
                  pl.BlockSpec((1, ts, cw), lambda bi, c, si: (bi, si, c + nc)),
                  pl.BlockSpec((1, ts, cw), lambda bi, c, si: (bi, si, c + 2 * nc)),
                  pl.BlockSpec((CONV_K, cw), lambda bi, c, si: (0, c)),
                  pl.BlockSpec((1, cw), lambda bi, c, si: (0, c))],
        out_specs=pl.BlockSpec((1, ts, cw), lambda bi, c, si: (bi, si, c)),
        out_shape=jax.ShapeDtypeStruct((b, s, BRANCH_W), BF16),
        scratch_shapes=[pltpu.VMEM((SUBLANES, cw), F32)],
        compiler_params=_cparams(("parallel", "parallel", "arbitrary")),
        name="conv_mixer",
    )(z3, z3, z3, conv_w, conv_b.reshape(1, BRANCH_W))


def _lru_kernel(gate_ref, rin_ref, cw_ref, cb_ref, wa_ref, ba_ref, wx_ref, bx_ref, lam_ref, o_ref,
                halo_ref, carry_ref, a_ref, b_ref, h_ref):
    ts, lw = a_ref.shape

    @pl.when(pl.program_id(2) == 0)
    def _():
        halo_ref[...] = jnp.zeros_like(halo_ref)
        carry_ref[...] = jnp.zeros_like(carry_ref)

    u = rin_ref[0].astype(F32)
    ext = jnp.concatenate([halo_ref[...], u], axis=0)
    xc = _causal_conv(ext, u, cw_ref[...], LRU_CONV_K) + cb_ref[...]
    halo_ref[...] = u[ts - SUBLANES:, :]

    xcb = xc.astype(BF16)
    ra, ix = [], []
    for hh in range(lw // LRU_HD):
        xh = xcb[:, hh * LRU_HD:(hh + 1) * LRU_HD]
        ra.append(jnp.dot(xh, wa_ref[hh].astype(BF16), preferred_element_type=F32))
        ix.append(jnp.dot(xh, wx_ref[hh].astype(BF16), preferred_element_type=F32))
    r = jax.nn.sigmoid(jnp.concatenate(ra, axis=1) + ba_ref[...])
    i = jax.nn.sigmoid(jnp.concatenate(ix, axis=1) + bx_ref[...])
    neg_lam = -lam_ref[...]
    softplus = jnp.maximum(neg_lam, 0.0) + jnp.log1p(jnp.exp(-jnp.abs(neg_lam)))
    log_a = -LRU_C * r * softplus
    a_ref[...] = jnp.exp(log_a)
    th = jnp.tanh(log_a)
    b_ref[...] = jnp.sqrt(-2.0 * th / (1.0 - th)) * (i * xc)

    row = lax.broadcasted_iota(jnp.int32, (SUBLANES, lw), 0)

    def group(g, hc):
        sl = pl.ds(pl.multiple_of(g * SUBLANES, SUBLANES), SUBLANES)
        a = a_ref[sl, :]
        bv = b_ref[sl, :]
        for d in (1, 2, 4):
            keep = row >= d
            bv = jnp.where(keep, a * pltpu.roll(bv, d, 0) + bv, bv)
            a = jnp.where(keep, a * pltpu.roll(a, d, 0), a)
        h = a * hc + bv
        h_ref[sl, :] = h
        return jnp.broadcast_to(h[SUBLANES - 1:SUBLANES, :], (SUBLANES, lw))

    carry_ref[...] = lax.fori_loop(0, ts // SUBLANES, group, carry_ref[...])
    o_ref[0] = (h_ref[...] * jax.nn.gelu(gate_ref[0].astype(F32))).astype(o_ref.dtype)


def lru_mixer(z3, col0, conv_w, conv_b, wa, ba, wx, bx, lam, ts=512, lw=256):
    b, s, _ = z3.shape
    nc = BRANCH_W // lw
    c0 = col0 // lw
    hpb = lw // LRU_HD
    row = lambda v: v.reshape(1, BRANCH_W)
    vec = pl.BlockSpec((1, lw), lambda bi, c, si: (0, c))
    mat = pl.BlockSpec((hpb, LRU_HD, LRU_HD), lambda bi, c, si: (c, 0, 0))
    return pl.pallas_call(
        _lru_kernel,
        grid=(b, nc, s // ts),
        in_specs=[pl.BlockSpec((1, ts, lw), lambda bi, c, si: (bi, si, c0 + c)),
                  pl.BlockSpec((1, ts, lw), lambda bi, c, si: (bi, si, c0 + nc + c)),
                  pl.BlockSpec((LRU_CONV_K, lw), lambda bi, c, si: (0, c)),
                  vec, mat, vec, mat, vec, vec],
        out_specs=pl.BlockSpec((1, ts, lw), lambda bi, c, si: (bi, si, c)),
        out_shape=jax.ShapeDtypeStruct((b, s, BRANCH_W), BF16),
        scratch_shapes=[pltpu.VMEM((SUBLANES, lw), F32), pltpu.VMEM((SUBLANES, lw), F32),
                        pltpu.VMEM((ts, lw), F32), pltpu.VMEM((ts, lw), F32), pltpu.VMEM((ts, lw), F32)],
        compiler_params=_cparams(("parallel", "parallel", "arbitrary")),
        name="rg_lru",
    )(z3, z3, conv_w, row(conv_b), wa, row(ba), wx, row(bx), row(lam))


def _compress_kernel(slots_ref, pos_ref, w1_ref, b1_ref, w2_ref, o_ref):
    g, nslot, half = slots_ref.shape[2:]
    rows = g * nslot
    x = slots_ref[0, 0].reshape(rows, half)
    w1 = w1_ref[0].astype(BF16)
    first = jnp.dot(x, w1[:half], preferred_element_type=F32)
    second = jnp.dot(x, w1[half:], preferred_element_type=F32)
    pos = jnp.broadcast_to(pos_ref[0], (SUBLANES, 2 * half)).astype(BF16)
    posb = jnp.dot(pos, w1, preferred_element_type=F32)[0:1, :]
    hidden = first + pltpu.roll(second, rows - 1, 0) + posb + b1_ref[0]
    out = jnp.dot(jax.nn.gelu(hidden).astype(BF16), w2_ref[0].astype(BF16), preferred_element_type=F32)
    o_ref[0, 0] = out.reshape(g, nslot, NSA_HD).astype(o_ref.dtype)


def nsa_compress(slots, cmp_pos, w1, b1, w2):
    _, b, g, nslot, half = slots.shape
    pos = cmp_pos.reshape(2, 1, CMP_BLK * NSA_HD)
    return pl.pallas_call(
        _compress_kernel,
        grid=(2, b),
        in_specs=[pl.BlockSpec((1, 1, g, nslot, half), lambda kv, bi: (kv, bi, 0, 0, 0)),
                  pl.BlockSpec((1, 1, 2 * half), lambda kv, bi: (kv, 0, 0)),
                  pl.BlockSpec((1, 2 * half, CMP_HIDDEN), lambda kv, bi: (kv, 0, 0)),
                  pl.BlockSpec((1, 1, CMP_HIDDEN), lambda kv, bi: (kv, 0, 0)),
                  pl.BlockSpec((1, CMP_HIDDEN, NSA_HD), lambda kv, bi: (kv, 0, 0))],
        out_specs=pl.BlockSpec((1, 1, g, nslot, NSA_HD), lambda kv, bi: (kv, bi, 0, 0, 0)),
        out_shape=jax.ShapeDtypeStruct((2, b, g, nslot, NSA_HD), BF16),
        compiler_params=_cparams(("parallel", "parallel")),
        name="nsa_compress",
    )(slots, pos, w1, b1.reshape(2, 1, CMP_HIDDEN), w2)


def _exp2_bf16(s, m):
    rows = s.shape[0]
    chunk = min(EXP_CHUNK, rows)
    return jnp.concatenate([jnp.exp2((s[r:r + chunk, :] - m).astype(BF16)) for r in range(0, rows, chunk)], axis=0)


def _nsa_kernel(q_ref, qf_ref, kc_ref, vc_ref, ks_ref, vs_ref, kw_ref, vw_ref, gl_ref, c2s_ref, bias_ref, o_ref,
                qp_ref, imp_ref, s0_ref, s1_ref, mx_ref, m_ref, acc_ref, *, n_cmp):
    tq = q_ref.shape[1]
    lanes = NSA_GRP * tq
    seq = ks_ref.shape[2]
    n_slc = seq // SLC_BLK
    t0 = pl.multiple_of(pl.program_id(2) * tq, tq)
    head = lambda r: slice(r * tq, (r + 1) * tq)

    def scores(k_tile, bias=None):
        s = jnp.dot(k_tile, qp_ref[...], preferred_element_type=F32)
        return s if bias is None else s + bias

    qt = q_ref[0].astype(F32).T * (NSA_HD ** -0.5 * LOG2E)
    qp_ref[...] = qf_ref[0]
    for r in range(NSA_GRP):
        qp_ref[0:NSA_HD, head(r)] = qt[r * NSA_HD:(r + 1) * NSA_HD, :].astype(BF16)

    s_cmp = scores(kc_ref[0, 0])
    win = []
    for off, size, bias in ((0, tq, bias_ref[1]), (tq, WINDOW - tq, None), (WINDOW, tq, bias_ref[0])):
        k0 = pl.multiple_of(t0 + off, tq)
        win.append((scores(kw_ref[0, 0, pl.ds(k0, size), :], bias), vw_ref[0, 0, :, pl.ds(k0, size)]))

    ncp = kc_ref.shape[2]
    n_row = lax.broadcasted_iota(jnp.int32, (ncp, lanes), 0)
    t_lane = t0 + lax.broadcasted_iota(jnp.int32, (ncp, lanes), 1) % tq
    ok = (t_lane >= n_row * CMP_STRIDE + (CMP_BLK - 1)) & (n_row < n_cmp)
    s = jnp.where(ok, s_cmp, NEG)
    p = jnp.where(ok, jnp.exp2(s - jnp.max(s, axis=0, keepdims=True)), 0.0)
    l = jnp.sum(p, axis=0, keepdims=True)
    p_cmp = p * (1.0 / jnp.where(l > 0.0, l, 1.0))
    o_cmp = jnp.dot(vc_ref[0, 0], p_cmp.astype(BF16), preferred_element_type=F32)

    rest = p_cmp[:, head(0)]
    for r in range(1, NSA_GRP):
        rest = rest + p_cmp[:, head(r)]
    imp = jnp.zeros((n_slc, tq), F32)
    for _ in range(3):
        part = rest.astype(BF16)
        rest = rest - part.astype(F32)
        imp = imp + jnp.dot(c2s_ref[...], part, preferred_element_type=F32)
    j_row = lax.broadcasted_iota(jnp.int32, (n_slc, tq), 0)
    t_col = t0 + lax.broadcasted_iota(jnp.int32, (n_slc, tq), 1)
    forced = (j_row == 0) | (j_row == t_col // SLC_BLK)
    causal = j_row * SLC_BLK <= t_col
    imp = jnp.where(forced, FORCE, jnp.where(causal, imp, -FORCE))
    imp_ref[...] = imp

    n_grp = n_slc // SUBLANES
    vals = [imp[v * SUBLANES:(v + 1) * SUBLANES, :] for v in range(n_grp)]
    sub = lax.broadcasted_iota(jnp.int32, (SUBLANES, tq), 0)
    rank = [jnp.zeros((SUBLANES, tq), jnp.int32) for _ in range(n_grp)]
    for i in range(n_slc):
        bi = jnp.broadcast_to(imp_ref[i:i + 1, :], (SUBLANES, tq))
        for v in range(n_grp):
            if v < i // SUBLANES:
                ahead = bi > vals[v]
            elif v > i // SUBLANES:
                ahead = bi >= vals[v]
            else:
                ahead = (bi > vals[v]) | ((bi == vals[v]) & (sub > i % SUBLANES))
            rank[v] = rank[v] + jnp.where(ahead, 1, 0)
    chosen = (jnp.concatenate(rank, axis=0) < min(SLC_TOPN, n_slc)) & causal
    mask = jnp.where(chosen, 0.0, NEG).astype(BF16)
    for r in range(NSA_GRP):
        qp_ref[MASK_OFF:MASK_OFF + n_slc, head(r)] = mask

    tk = min(ATT_TK, seq)
    n_full = t0 // tk
    last = n_full // 2

    s_bufs = (s0_ref, s1_ref)

    def pair_scores(step, buf):
        for par in range(2):
            k0 = pl.multiple_of((2 * step + par) * tk, tk)
            s = scores(ks_ref[0, 0, pl.ds(k0, tk), :])
            s_bufs[buf][par] = s
            mx_ref[buf, par] = jnp.max(s, axis=0, keepdims=True)

    def pair_update(step, buf, prefetch):
        m_new = [jnp.maximum(m_ref[par], mx_ref[buf, par]) for par in range(2)]
        probs = [_exp2_bf16(s_bufs[buf].at[par], m_new[par]) for par in range(2)]
        if prefetch:
            pair_scores(step + 1, 1 - buf)
        for par in range(2):
            k0 = pl.multiple_of((2 * step + par) * tk, tk)
            pv = jnp.dot(vs_ref[0, 0, :, pl.ds(k0, tk)], probs[par], preferred_element_type=F32)
            acc_ref[par] = jnp.exp2(m_ref[par] - m_new[par]) * acc_ref[par] + pv
            m_ref[par] = m_new[par]

    def mask_diagonal(buf):
        off = pl.multiple_of(t0 - n_full * tk, tq)
        for par in range(2):
            @pl.when(n_full % 2 == par)
            def _():
                tile = s_bufs[buf].at[par]
                tile[pl.ds(off, tq), :] = tile[pl.ds(off, tq), :] + bias_ref[0]
                mx_ref[buf, par] = jnp.max(tile[...], axis=0, keepdims=True)

    def half_trip(step, buf):
        @pl.when(step == last)
        def _():
            mask_diagonal(buf)

        @pl.when(step < last)
        def _():
            pair_update(step, buf, prefetch=True)

        @pl.when(step == last)
        def _():
            pair_update(step, buf, prefetch=False)

    def trip(i, carry):
        half_trip(2 * i, 0)
        half_trip(2 * i + 1, 1)
        return carry

    m_ref[...] = jnp.full(m_ref.shape, NEG, F32)
    acc_ref[...] = jnp.zeros(acc_ref.shape, F32)
    pair_scores(0, 0)

    mx = functools.reduce(jnp.maximum, [jnp.max(s, axis=0, keepdims=True) for s, _ in win])
    o_win = jnp.zeros((V_ROWS, lanes), F32)
    for s, v_tile in win:
        o_win = o_win + jnp.dot(v_tile, _exp2_bf16(s, mx), preferred_element_type=F32)
    o_win = o_win[0:NSA_HD] * (1.0 / o_win[NSA_HD:NSA_HD + 1])

    lax.fori_loop(0, last // 2 + 1, trip, 0)
    m_all = jnp.maximum(m_ref[0], m_ref[1])
    o_slc = jnp.exp2(m_ref[0] - m_all) * acc_ref[0] + jnp.exp2(m_ref[1] - m_all) * acc_ref[1]
    o_slc = o_slc[0:NSA_HD] * (1.0 / o_slc[NSA_HD:NSA_HD + 1])

    gates = jax.nn.sigmoid(gl_ref[0, 0].astype(F32))
    outs = []
    for r in range(NSA_GRP):
        outs.append(gates[3 * r:3 * r + 1] * o_cmp[:, head(r)] + gates[3 * r + 1:3 * r + 2] * o_slc[:, head(r)]
                    + gates[3 * r + 2:3 * r + 3] * o_win[:, head(r)])
    o_ref[0] = jnp.concatenate(outs, axis=0).T.astype(o_ref.dtype)


def _alibi_slopes():
    return (2.0 ** (-8.0 * np.arange(1, NSA_HEADS + 1) / NSA_HEADS)).astype(np.float32)


def _bf16_split3(v):
    parts, rest = [], np.asarray(v, np.float32)
    for _ in range(3):
        part = rest.astype(ml_dtypes.bfloat16).astype(np.float32)
        parts.append(part)
        rest = (rest.astype(np.float64) - part.astype(np.float64)).astype(np.float32)
    return parts


def _query_features(tq):
    feat = np.zeros((NSA_HEADS, FEAT), np.float32)
    hi, mid, lo = _bf16_split3(_alibi_slopes() * np.float32(LOG2E))
    for c, part in enumerate((hi, mid, lo, hi, mid, lo)):
        feat[:, ALIBI_OFF + c] = part
    feat[:, PAD_OFF] = NEG
    feat = np.repeat(feat.reshape(NSA_KV_HEADS, NSA_GRP, FEAT), tq, axis=1)
    return jnp.asarray(feat.transpose(0, 2, 1), BF16)


def _key_features(pos, n_slc):
    pos = np.asarray(pos)
    real = pos >= 0
    feat = np.zeros((pos.shape[0], FEAT - NSA_HD), np.float32)
    if n_slc:
        feat[real, pos[real] // SLC_BLK] = 1.0
    a0 = ALIBI_OFF - NSA_HD
    feat[real, a0:a0 + 3] = ((pos[real] // SLC_BLK) * SLC_BLK)[:, None]
    feat[real, a0 + 3:a0 + 6] = (pos[real] % SLC_BLK)[:, None]
    feat[~real, PAD_OFF - NSA_HD] = 1.0
    return feat


def _values_t(v):
    b, g, n, _ = v.shape
    extra = jnp.zeros((b, g, V_ROWS - NSA_HD, n), v.dtype).at[:, :, 0].set(1.0)
    return jnp.concatenate([jnp.swapaxes(v, 2, 3), extra], axis=2)


def _with_features(k, feat):
    b, g, _, _ = k.shape
    f = jnp.broadcast_to(jnp.asarray(feat, BF16), (b, g) + feat.shape)
    return jnp.concatenate([k.astype(BF16), f], axis=3)


def nsa_attention(z3, qcol0, kc, vc, ks, vs, kw, vw, gate_logits):
    b, s, _ = z3.shape
    tq = ATT_TQ
    tk = min(ATT_TK, s)
    assert s % (2 * tk) == 0 and tk % tq == 0 and tq % SLC_BLK == 0 and WINDOW % tq == 0 and WINDOW > tq
    n_slc = s // SLC_BLK
    assert n_slc <= ALIBI_OFF - MASK_OFF and n_slc % SUBLANES == 0
    n_cmp = (s - CMP_BLK) // CMP_STRIDE + 1
    ncp = kc.shape[2]
    gw = NSA_GRP * NSA_HD
    lanes = NSA_GRP * tq

    kc_f = _with_features(kc, _key_features(np.arange(ncp) * CMP_STRIDE + CMP_BLK - 1, 0))
    ks_f = _with_features(ks, _key_features(np.arange(s), n_slc))
    front = ((0, 0), (0, 0), (WINDOW, 0), (0, 0))
    kw_f = _with_features(jnp.pad(kw, front), _key_features(np.arange(-WINDOW, s), 0))
    t = lambda v: jnp.swapaxes(v, 2, 3)
    vw_t = _values_t(jnp.pad(vw, front))

    c_start = np.arange(ncp) * CMP_STRIDE
    s_start = np.arange(n_slc) * SLC_BLK
    overlap = np.clip(np.minimum(c_start[:, None] + CMP_BLK, s_start[None, :] + SLC_BLK)
                      - np.maximum(c_start[:, None], s_start[None, :]), 0, None) / CMP_BLK
    overlap[n_cmp:] = 0.0
    c2s_t = jnp.asarray(overlap.T, BF16)

    key = np.arange(tq)[:, None]
    qry = np.tile(np.arange(tq), NSA_GRP)[None, :]
    bias = jnp.asarray(np.stack([np.where(key <= qry, 0.0, NEG), np.where(key > qry, 0.0, NEG)]), F32)

    blk = lambda *shape: pl.BlockSpec((1, 1) + shape, lambda bi, g, qi: (bi, g, 0, 0))
    const = lambda *shape: pl.BlockSpec(shape, lambda bi, g, qi: (0,) * len(shape))
    return pl.pallas_call(
        functools.partial(_nsa_kernel, n_cmp=n_cmp),
        grid=(b, NSA_KV_HEADS, s // tq),
        in_specs=[pl.BlockSpec((1, tq, gw), lambda bi, g, qi: (bi, qi, qcol0 // gw + g)),
                  pl.BlockSpec((1, FEAT, lanes), lambda bi, g, qi: (g, 0, 0)),
                  blk(ncp, FEAT), blk(NSA_HD, ncp),
                  blk(s, FEAT), blk(V_ROWS, s),
                  blk(s + WINDOW, FEAT), blk(V_ROWS, s + WINDOW),
                  pl.BlockSpec((1, 1, NSA_GRP * 3, tq), lambda bi, g, qi: (bi, g, 0, qi)),
                  const(n_slc, ncp), const(2, tq, lanes)],
        out_specs=pl.BlockSpec((1, tq, gw), lambda bi, g, qi: (bi, qi, g)),
        out_shape=jax.ShapeDtypeStruct((b, s, NSA_W), BF16),
        scratch_shapes=[pltpu.VMEM((FEAT, lanes), BF16), pltpu.VMEM((n_slc, tq), F32),
                        pltpu.VMEM((2, tk, lanes), F32), pltpu.VMEM((2, tk, lanes), F32),
                        pltpu.VMEM((2, 2, 1, lanes), F32),
                        pltpu.VMEM((2, 1, lanes), F32), pltpu.VMEM((2, V_ROWS, lanes), F32)],
        compiler_params=_cparams(("parallel", "parallel", "arbitrary")),
        name="nsa_attention",
    )(z3, _query_features(tq), kc_f, t(vc), ks_f, _values_t(vs), kw_f, vw_t, gate_logits, c2s_t, bias)


def _merge_kernel(ya_ref, yb_ref, yc_ref, wb_ref, ga_ref, gb_ref, gc_ref, o_ref):
    acc = None
    for i, (y_ref, g_ref) in enumerate(((ya_ref, ga_ref), (yb_ref, gb_ref), (yc_ref, gc_ref))):
        term = jax.nn.sigmoid(g_ref[...].astype(F32)) * jnp.dot(y_ref[...], wb_ref[i], preferred_element_type=F32)
        acc = term if acc is None else acc + term
    o_ref[...] = acc.astype(o_ref.dtype)


def branch_merge(ya, yb, yc, wb, zmerge, tm=1024, tn=512):
    t = ya.shape[0]
    d = wb.shape[2]
    nj = d // tn
    y_spec = pl.BlockSpec((tm, BRANCH_W), lambda i, j: (i, 0))
    gate = lambda br: pl.BlockSpec((tm, tn), lambda i, j: (i, j + br * nj))
    return pl.pallas_call(
        _merge_kernel,
        grid=(t // tm, nj),
        in_specs=[y_spec, y_spec, y_spec,
                  pl.BlockSpec((3, BRANCH_W, tn), lambda i, j: (0, 0, j)),
                  gate(0), gate(1), gate(2)],
        out_specs=pl.BlockSpec((tm, tn), lambda i, j: (i, j)),
        out_shape=jax.ShapeDtypeStruct((t, d), BF16),
        compiler_params=_cparams(("parallel", "parallel")),
        name="branch_merge",
    )(ya, yb, yc, wb, zmerge, zmerge, zmerge)


def _layer_norm(y, g, b):
    mu = jnp.mean(y, axis=-1, keepdims=True)
    yc = y - mu
    var = jnp.mean(yc * yc, axis=-1, keepdims=True)
    return yc * lax.rsqrt(var + LN_EPS) * g + b


def _outproj_ln_kernel(m_ref, w_ref, x_ref, g_ref, b_ref, o_ref, ob_ref, *, alpha):
    mix = jnp.dot(m_ref[...], w_ref[...], preferred_element_type=F32)
    y = _layer_norm(alpha * x_ref[...] + mix, g_ref[...], b_ref[...])
    o_ref[...] = y
    ob_ref[...] = y.astype(BF16)


def outproj_ln(merged, w_out, x, g, b, alpha, tm=256):
    t, d = x.shape
    row = pl.BlockSpec((tm, d), lambda i: (i, 0))
    vec = pl.BlockSpec((1, d), lambda i: (0, 0))
    return pl.pallas_call(
        functools.partial(_outproj_ln_kernel, alpha=alpha),
        grid=(t // tm,),
        in_specs=[row, pl.BlockSpec((d, d), lambda i: (0, 0)), row, vec, vec],
        out_specs=[row, row],
        out_shape=[jax.ShapeDtypeStruct((t, d), F32), jax.ShapeDtypeStruct((t, d), BF16)],
        compiler_params=_cparams(("parallel",)),
        name="outproj_ln",
    )(merged, w_out, x, g.reshape(1, d), b.reshape(1, d))


def _final_kernel(x_ref, xb_ref, wg_ref, p_ref, wp_ref, y0_ref, y1_ref, gate_ref, g_ref, b_ref, o_ref, ob_ref,
                  *, alpha):
    ple = jax.nn.sigmoid(jnp.dot(xb_ref[...], wg_ref[...], preferred_element_type=F32)) * jnp.dot(
        p_ref[...], wp_ref[...], preferred_element_type=F32)
    gate = gate_ref[...]
    ffn = gate[:, 0:1] * y0_ref[...].astype(F32) + gate[:, 1:2] * y1_ref[...].astype(F32)
    y = _layer_norm(alpha * x_ref[...] + ffn + ple, g_ref[...], b_ref[...])
    o_ref[...] = y
    ob_ref[...] = y.astype(BF16)


def ple_combine_ln(x, xb, w_ple_gate, p, w_ple, y0, y1, gate, g, b, alpha, tm=256):
    t, d = x.shape
    pd = p.shape[1]
    row = pl.BlockSpec((tm, d), lambda i: (i, 0))
    vec = pl.BlockSpec((1, d), lambda i: (0, 0))
    return pl.pallas_call(
        functools.partial(_final_kernel, alpha=alpha),
        grid=(t // tm,),
        in_specs=[row, row, pl.BlockSpec((d, d), lambda i: (0, 0)),
                  pl.BlockSpec((tm, pd), lambda i: (i, 0)), pl.BlockSpec((pd, d), lambda i: (0, 0)),
                  row, row, pl.BlockSpec((tm, TOP_K), lambda i: (i, 0)), vec, vec],
        out_specs=[row, row],
        out_shape=[jax.ShapeDtypeStruct((t, d), F32), jax.ShapeDtypeStruct((t, d), BF16)],
        compiler_params=_cparams(("parallel",)),
        name="ple_combine_ln",
    )(x, xb, w_ple_gate, p, w_ple, y0, y1, gate, g.reshape(1, d), b.reshape(1, d))


def _router_kernel(x_ref, wr_ref, br_ref, idx_ref, gate_ref):
    tm = x_ref.shape[0]
    logits = lax.dot_general(wr_ref[...], x_ref[...], (((1,), (1,)), ((), ())),
                             precision=lax.Precision.HIGHEST, preferred_element_type=F32)
    aff = jax.nn.sigmoid(logits)
    sel = aff + br_ref[...]
    scores = []
    for g in range(N_GROUPS):
        v = [sel[g * EXPERTS_PER_GROUP + e:g * EXPERTS_PER_GROUP + e + 1, :] for e in range(EXPERTS_PER_GROUP)]
        best = None
        for a in range(EXPERTS_PER_GROUP):
            for c in range(a + 1, EXPERTS_PER_GROUP):
                pair = v[a] + v[c]
                best = pair if best is None else jnp.maximum(best, pair)
        scores.append(best)
    best_score, best_group = scores[0], jnp.zeros((1, tm), jnp.int32)
    for g in range(1, N_GROUPS):
        better = scores[g] > best_score
        best_score = jnp.where(better, scores[g], best_score)
        best_group = jnp.where(better, g, best_group)
    e_row = lax.broadcasted_iota(jnp.int32, (N_EXPERTS, tm), 0)
    cand = jnp.where(e_row // EXPERTS_PER_GROUP == best_group, sel, NEG)
    picks, gates = [], []
    for _ in range(TOP_K):
        mx = jnp.max(cand, axis=0, keepdims=True)
        pick = jnp.min(jnp.where(cand == mx, e_row, N_EXPERTS), axis=0, keepdims=True)
        hit = e_row == pick
        picks.append(pick)
        gates.append(jnp.sum(jnp.where(hit, aff, 0.0), axis=0, keepdims=True))
        cand = jnp.where(hit, -jnp.inf, cand)
    total = gates[0] + gates[1]
    idx_ref[...] = jnp.concatenate(picks, axis=0)
    gate_ref[...] = jnp.concatenate([gates[0] / total, gates[1] / total], axis=0)


def router(x, w_router, b_router, tm=1024):
    t, d = x.shape
    return pl.pallas_call(
        _router_kernel,
        grid=(t // tm,),
        in_specs=[pl.BlockSpec((tm, d), lambda i: (i, 0)),
                  pl.BlockSpec((N_EXPERTS, d), lambda i: (0, 0)),
                  pl.BlockSpec((N_EXPERTS, 1), lambda i: (0, 0))],
        out_specs=[pl.BlockSpec((TOP_K, tm), lambda i: (0, i)), pl.BlockSpec((TOP_K, tm), lambda i: (0, i))],
        out_shape=[jax.ShapeDtypeStruct((TOP_K, t), jnp.int32), jax.ShapeDtypeStruct((TOP_K, t), F32)],
        compiler_params=_cparams(("parallel",)),
        name="router",
    )(x, w_router.T, b_router.reshape(N_EXPERTS, 1))


def _expert_kernel(be_ref, nu_ref, x_ref, wg_ref, wu_ref, wd_ref, *rest):
    prev_ref = rest[0] if len(rest) == 5 else None
    o_ref, wgb_ref, wub_ref, wdb_ref = rest[-4:]
    i = pl.program_id(0)

    @pl.when((i == 0) | (be_ref[i] != be_ref[jnp.maximum(i - 1, 0)]))
    def _():
        wgb_ref[...] = wg_ref[0, 0].astype(BF16)
        wub_ref[...] = wu_ref[0, 0].astype(BF16)
        wdb_ref[...] = wd_ref[0, 0].astype(BF16)

    @pl.when(i < nu_ref[0])
    def _():
        x = x_ref[...]
        g = jnp.dot(x, wgb_ref[...], preferred_element_type=F32)
        u = jnp.dot(x, wub_ref[...], preferred_element_type=F32)
        out = jnp.dot((jax.nn.silu(g) * u).astype(BF16), wdb_ref[...], preferred_element_type=F32)
        if prev_ref is not None:
            out = out + prev_ref[...]
        o_ref[...] = out.astype(o_ref.dtype)

    @pl.when(i >= nu_ref[0])
    def _():
        o_ref[...] = jnp.zeros_like(o_ref)


def expert_ffn(block_expert, n_used, x_sorted, w_gate_up, w_down, layer):
    n_rows, d = x_sorted.shape
    de = w_down.shape[2]
    ce = de // EXPERT_CHUNKS
    out = None
    for c in range(EXPERT_CHUNKS):
        last = c == EXPERT_CHUNKS - 1
        row = pl.BlockSpec((MOE_BLK, d), lambda i, be, nu: (i, 0))
        in_specs = [row,
                    pl.BlockSpec((1, 1, d, ce), lambda i, be, nu, c=c: (layer, be[i], 0, c)),
                    pl.BlockSpec((1, 1, d, ce), lambda i, be, nu, c=c: (layer, be[i], 0, EXPERT_CHUNKS + c)),
                    pl.BlockSpec((1, 1, ce, d), lambda i, be, nu, c=c: (layer, be[i], c, 0))]
        args = [x_sorted, w_gate_up, w_gate_up, w_down]
        if out is not None:
            in_specs.append(row)
            args.append(out)
        out = pl.pallas_call(
            _expert_kernel,
            grid_spec=pltpu.PrefetchScalarGridSpec(
                num_scalar_prefetch=2,
                grid=(n_rows // MOE_BLK,),
                in_specs=in_specs,
                out_specs=row,
                scratch_shapes=[pltpu.VMEM((d, ce), BF16), pltpu.VMEM((d, ce), BF16), pltpu.VMEM((ce, d), BF16)]),
            out_shape=jax.ShapeDtypeStruct((n_rows, d), BF16 if last else F32),
            compiler_params=_cparams(("arbitrary",)),
            name="expert_ffn",
        )(block_expert, n_used, *args)
    return out


def _dispatch_tables(expert_idx):
    n_tok = expert_idx.shape[0]
    n_assign = n_tok * TOP_K
    e_flat = expert_idx.reshape(n_assign)
    onehot = (e_flat[:, None] == jnp.arange(N_EXPERTS)[None, :]).astype(jnp.int32)
    csum = jnp.cumsum(onehot, axis=0)
    rank = jnp.take_along_axis(csum, e_flat[:, None], axis=1)[:, 0] - 1
    counts = csum[-1]
    padded = (counts + MOE_BLK - 1) // MOE_BLK * MOE_BLK
    pad_end = jnp.cumsum(padded)
    dest = (pad_end - padded)[e_flat] + rank
    n_rows = n_assign + N_EXPERTS * MOE_BLK
    row_tok = jnp.zeros((n_rows,), jnp.int32).at[dest].set(jnp.arange(n_assign, dtype=jnp.int32) // TOP_K)
    block_expert = jnp.minimum(
        jnp.searchsorted(pad_end, jnp.arange(n_rows // MOE_BLK) * MOE_BLK, side="right"), N_EXPERTS - 1)
    n_used = (pad_end[-1:] // MOE_BLK).astype(jnp.int32)
    return row_tok, block_expert.astype(jnp.int32), n_used, dest.reshape(n_tok, TOP_K)


def _token_mixer(xb, bsz, seq, w_in_t, layer, conv_a_w, conv_a_b, lru_conv_w, lru_conv_b, lru_wa, lru_ba, lru_wx, lru_bx,
                 lru_lam, cmp_pos, phi_w1, phi_b1, phi_w2, w_branch):
    d = xb.shape[1]
    n_tok = bsz * seq
    main_w = 3 * BRANCH_W + 2 * BRANCH_W + NSA_W + 6 * KV_W
    ng = 3 * NSA_HEADS
    kv0 = main_w - 6 * KV_W
    zmain = matmul_wres(xb, w_in_t, layer, 0, main_w, BF16, 1024, 768)
    zgate = matmul_wres(xb, w_in_t, layer, main_w, LANES, F32, 1024, LANES)
    zmerge = matmul_wres(xb, w_in_t, layer, main_w + ng, w_in_t.shape[1] - main_w - ng, BF16, 1024, 512)
    z3 = zmain.reshape(bsz, seq, main_w)

    y_a = conv_mixer(z3, conv_a_w, conv_a_b)
    y_b = lru_mixer(z3, 3 * BRANCH_W, lru_conv_w, lru_conv_b, lru_wa, lru_ba, lru_wx, lru_bx, lru_lam)

    kv = z3[:, :, kv0:].reshape(bsz, seq, 6, NSA_KV_HEADS, NSA_HD).transpose(2, 0, 3, 1, 4)
    nslot = seq // CMP_STRIDE
    slots = kv[0:2].reshape(2, bsz, NSA_KV_HEADS, nslot, CMP_STRIDE * NSA_HD)
    cmp = nsa_compress(slots, cmp_pos, phi_w1, phi_b1, phi_w2)
    gate_logits = zgate[:, :ng].reshape(bsz, seq, NSA_KV_HEADS, NSA_GRP * 3).transpose(0, 2, 3, 1)
    y_c = nsa_attention(z3, 5 * BRANCH_W, cmp[0], cmp[1], kv[2], kv[3], kv[4], kv[5], gate_logits)

    flat = lambda y: y.reshape(n_tok, BRANCH_W)
    return branch_merge(flat(y_a), flat(y_b), flat(y_c), w_branch.astype(BF16), zmerge)


def _moe(x, xb, w_router, b_router, w_gate_up, w_down, layer):
    idx_t, gate_t = router(x, w_router, b_router)
    row_tok, block_expert, n_used, dest = _dispatch_tables(idx_t.T)
    rows = expert_ffn(block_expert, n_used, xb[row_tok], w_gate_up, w_down, layer)
    return rows[dest[:, 0]], rows[dest[:, 1]], gate_t.T


def kernel(x, p, w_in, conv_a_w, conv_a_b, lru_conv_w, lru_conv_b, lru_wa, lru_ba, lru_wx, lru_bx, lru_lam, cmp_pos, phi_w1, phi_b1, phi_w2, w_branch, w_out, ln_g, ln_b, w_router, b_router, w_gate_up, w_down, w_ple, w_ple_gate):
    bsz, seq, d = x.shape
    depth = w_in.shape[0]
    n_tok = bsz * seq
    alpha = (2 * depth) ** 0.25
    xf = x.reshape(n_tok, d)
    xb = xf.astype(BF16)
    w_in_t = jnp.swapaxes(w_in, 1, 2)
    for i in range(depth):
        merged = _token_mixer(xb, bsz, seq, w_in_t, i, conv_a_w[i], conv_a_b[i], lru_conv_w[i], lru_conv_b[i],
                              lru_wa[i], lru_ba[i], lru_wx[i], lru_bx[i], lru_lam[i], cmp_pos[i], phi_w1[i],
                              phi_b1[i], phi_w2[i], w_branch[i])
        xf, xb = outproj_ln(merged, w_out[i].astype(BF16), xf, ln_g[i, 0], ln_b[i, 0], alpha)
        y0, y1, gate = _moe(xf, xb, w_router, b_router, w_gate_up, w_down, i)
        xf, xb = ple_combine_ln(xf, xb, w_ple_gate[i].astype(BF16), p[i].reshape(n_tok, -1).astype(BF16),
                                w_ple[i].astype(BF16), y0, y1, gate, ln_g[i, 1], ln_b[i, 1], alpha)
    return xf.reshape(bsz, seq, d)
```

```python
import functools

import ml_dtypes
import numpy as np
import jax
import jax.numpy as jnp
from jax import lax
from jax.experimental import pallas as pl
from jax.experimental.pallas import tpu as pltpu

F32 = jnp.float32
BF16 = jnp.bfloat16

BRANCH_W = 1024
CONV_K = 3
LRU_HEADS = 8
LRU_HD = BRANCH_W // LRU_HEADS
LRU_CONV_K = 4
LRU_C = 8.0
NSA_HEADS = 16
NSA_KV_HEADS = 4
NSA_GRP = NSA_HEADS // NSA_KV_HEADS
NSA_HD = 64
NSA_W = NSA_HEADS * NSA_HD
KV_W = NSA_KV_HEADS * NSA_HD
CMP_BLK = 32
CMP_STRIDE = 16
CMP_HIDDEN = 128
SLC_BLK = 64
SLC_TOPN = 16
WINDOW = 512
N_EXPERTS = 16
N_GROUPS = 4
EXPERTS_PER_GROUP = N_EXPERTS // N_GROUPS
TOP_K = 2
LN_EPS = 1e-5
NEG = -1e30
FORCE = 1e9

LANES = 128
SUBLANES = 8
VMEM_LIMIT = 56 * 1024 * 1024

FEAT = 256
MASK_OFF = NSA_HD
ALIBI_OFF = 2 * NSA_HD
PAD_OFF = ALIBI_OFF + 6
V_ROWS = NSA_HD + 16
LOG2E = 1.4426950408889634

ATT_TQ = 256
ATT_TK = 512
EXP_CHUNK = 64
MOE_BLK = 512
EXPERT_CHUNKS = 2


def _cparams(sem):
    return pltpu.CompilerParams(dimension_semantics=sem, vmem_limit_bytes=VMEM_LIMIT)


def _mm_wres_kernel(x_ref, w_ref, *rest, shift):
    o_ref, wb_ref = rest[-2:]

    @pl.when(pl.program_id(1) == 0)
    def _():
        w = w_ref[0]
        if shift:
            tn = w.shape[0]
            w = jnp.concatenate([w, rest[0][0]], axis=0)[shift:shift + tn]
        wb_ref[...] = w.astype(BF16)

    o_ref[...] = lax.dot_general(x_ref[...], wb_ref[...], (((1,), (1,)), ((), ())),
                                 preferred_element_type=F32).astype(o_ref.dtype)


def matmul_wres(x, wt3, layer, row0, n, out_dtype, tm, tn):
    m, k = x.shape
    shift = row0 % tn
    base = row0 - shift
    assert m % tm == 0 and n % tn == 0 and shift % SUBLANES == 0 and shift <= LANES and tn % LANES == 0
    in_specs = [pl.BlockSpec((tm, k), lambda j, i: (i, 0)),
                pl.BlockSpec((1, tn, k), lambda j, i: (layer, base // tn + j, 0))]
    args = [x, wt3]
    if shift:
        in_specs.append(pl.BlockSpec((1, LANES, k), lambda j, i: (layer, (base + (j + 1) * tn) // LANES, 0)))
        args.append(wt3)
    return pl.pallas_call(
        functools.partial(_mm_wres_kernel, shift=shift),
        grid=(n // tn, m // tm),
        in_specs=in_specs,
        out_specs=pl.BlockSpec((tm, tn), lambda j, i: (i, j)),
        out_shape=jax.ShapeDtypeStruct((m, n), out_dtype),
        scratch_shapes=[pltpu.VMEM((tn, k), BF16)],
        compiler_params=_cparams(("parallel", "arbitrary")),
        name="matmul_wres",
    )(*args)


def _causal_conv(ext, cur, w, k):
    acc = cur * w[k - 1:k, :]
    for j in range(k - 1):
        shift = k - 1 - j
        acc = acc + pltpu.roll(ext, shift, 0)[SUBLANES:, :] * w[j:j + 1, :]
    return acc


def _conv_mixer_kernel(ain_ref, ab_ref, ac_ref, w_ref, b_ref, o_ref, halo_ref):
    @pl.when(pl.program_id(2) == 0)
    def _():
        halo_ref[...] = jnp.zeros_like(halo_ref)

    v = ac_ref[0].astype(F32) * ain_ref[0].astype(F32)
    ext = jnp.concatenate([halo_ref[...], v], axis=0)
    y = _causal_conv(ext, v, w_ref[...], CONV_K) + b_ref[...]
    o_ref[0] = (ab_ref[0].astype(F32) * y).astype(o_ref.dtype)
    halo_ref[...] = v[v.shape[0] - SUBLANES:, :]


def conv_mixer(z3, conv_w, conv_b, ts=512, cw=512):
    b, s, _ = z3.shape
    nc = BRANCH_W // cw
    return pl.pallas_call(
        _conv_mixer_kernel,
        grid=(b, nc, s // ts),
        in_specs=[pl.BlockSpec((1, ts, cw), lambda bi, c, si: (bi, si, c)),
                  pl.BlockSpec((1, ts, cw), lambda bi, c, si: (bi, si, c + nc)),
                  pl.BlockSpec((1, ts, cw), lambda bi, c, si: (bi, si, c + 2 * nc)),
                  pl.BlockSpec((CONV_K, cw), lambda bi, c, si: (0, c)),
                  pl.BlockSpec((1, cw), lambda bi, c, si: (0, c))],
        out_specs=pl.BlockSpec((1, ts, cw), lambda bi, c, si: (bi, si, c)),
        out_shape=jax.ShapeDtypeStruct((b, s, BRANCH_W), BF16),
        scratch_shapes=[pltpu.VMEM((SUBLANES, cw), F32)],
        compiler_params=_cparams(("parallel", "parallel", "arbitrary")),
        name="conv_mixer",
    )(z3, z3, z3, conv_w, conv_b.reshape(1, BRANCH_W))


def _lru_kernel(gate_ref, rin_ref, cw_ref, cb_ref, wa_ref, ba_ref, wx_ref, bx_ref, lam_ref, o_ref,
                halo_ref, carry_ref, a_ref, b_ref, h_ref):
    ts, lw = a_ref.shape

    @pl.when(pl.program_id(2) == 0)
    def _():
        halo_ref[...] = jnp.zeros_like(halo_ref)
        carry_ref[...] = jnp.zeros_like(carry_ref)

    u = rin_ref[0].astype(F32)
    ext = jnp.concatenate([halo_ref[...], u], axis=0)
    xc = _causal_conv(ext, u, cw_ref[...], LRU_CONV_K) + cb_ref[...]
    halo_ref[...] = u[ts - SUBLANES:, :]

    xcb = xc.astype(BF16)
    ra, ix = [], []
    for hh in range(lw // LRU_HD):
        xh = xcb[:, hh * LRU_HD:(hh + 1) * LRU_HD]
        ra.append(jnp.dot(xh, wa_ref[hh].astype(BF16), preferred_element_type=F32))
        ix.append(jnp.dot(xh, wx_ref[hh].astype(BF16), preferred_element_type=F32))
    r = jax.nn.sigmoid(jnp.concatenate(ra, axis=1) + ba_ref[...])
    i = jax.nn.sigmoid(jnp.concatenate(ix, axis=1) + bx_ref[...])
    neg_lam = -lam_ref[...]
    softplus = jnp.maximum(neg_lam, 0.0) + jnp.log1p(jnp.exp(-jnp.abs(neg_lam)))
    log_a = -LRU_C * r * softplus
    a_ref[...] = jnp.exp(log_a)
    th = jnp.tanh(log_a)
    b_ref[...] = jnp.sqrt(-2.0 * th / (1.0 - th)) * (i * xc)

    row = lax.broadcasted_iota(jnp.int32, (SUBLANES, lw), 0)

    def group(g, hc):
        sl = pl.ds(pl.multiple_of(g * SUBLANES, SUBLANES), SUBLANES)
        a = a_ref[sl, :]
        bv = b_ref[sl, :]
        for d in (1, 2, 4):
            keep = row >= d
            bv = jnp.where(keep, a * pltpu.roll(bv, d, 0) + bv, bv)
            a = jnp.where(keep, a * pltpu.roll(a, d, 0), a)
        h = a * hc + bv
        h_ref[sl, :] = h
        return jnp.broadcast_to(h[SUBLANES - 1:SUBLANES, :], (SUBLANES, lw))

    carry_ref[...] = lax.fori_loop(0, ts // SUBLANES, group, carry_ref[...])
    o_ref[0] = (h_ref[...] * jax.nn.gelu(gate_ref[0].astype(F32))).astype(o_ref.dtype)


def lru_mixer(z3, col0, conv_w, conv_b, wa, ba, wx, bx, lam, ts=512, lw=256):
    b, s, _ = z3.shape
    nc = BRANCH_W // lw
    c0 = col0 // lw
    hpb = lw // LRU_HD
    row = lambda v: v.reshape(1, BRANCH_W)
    vec = pl.BlockSpec((1, lw), lambda bi, c, si: (0, c))
    mat = pl.BlockSpec((hpb, LRU_HD, LRU_HD), lambda bi, c, si: (c, 0, 0))
    return pl.pallas_call(
        _lru_kernel,
        grid=(b, nc, s // ts),
        in_specs=[pl.BlockSpec((1, ts, lw), lambda bi, c, si: (bi, si, c0 + c)),
                  pl.BlockSpec((1, ts, lw), lambda bi, c, si: (bi, si, c0 + nc + c)),
                  pl.BlockSpec((LRU_CONV_K, lw), lambda bi, c, si: (0, c)),
                  vec, mat, vec, mat, vec, vec],
        out_specs=pl.BlockSpec((1, ts, lw), lambda bi, c, si: (bi, si, c)),
        out_shape=jax.ShapeDtypeStruct((b, s, BRANCH_W), BF16),
        scratch_shapes=[pltpu.VMEM((SUBLANES, lw), F32), pltpu.VMEM((SUBLANES, lw), F32),
                        pltpu.VMEM((ts, lw), F32), pltpu.VMEM((ts, lw), F32), pltpu.VMEM((ts, lw), F32)],
        compiler_params=_cparams(("parallel", "parallel", "arbitrary")),
        name="rg_lru",
    )(z3, z3, conv_w, row(conv_b), wa, row(ba), wx, row(bx), row(lam))


def _compress_kernel(slots_ref, pos_ref, w1_ref, b1_ref, w2_ref, o_ref):
    g, nslot, half = slots_ref.shape[2:]
    rows = g * nslot
    x = slots_ref[0, 0].reshape(rows, half)
    w1 = w1_ref[0].astype(BF16)
    first = jnp.dot(x, w1[:half], preferred_element_type=F32)
    second = jnp.dot(x, w1[half:], preferred_element_type=F32)
    pos = jnp.broadcast_to(pos_ref[0], (SUBLANES, 2 * half)).astype(BF16)
    posb = jnp.dot(pos, w1, preferred_element_type=F32)[0:1, :]
    hidden = first + pltpu.roll(second, rows - 1, 0) + posb + b1_ref[0]
    out = jnp.dot(jax.nn.gelu(hidden).astype(BF16), w2_ref[0].astype(BF16), preferred_element_type=F32)
    o_ref[0, 0] = out.reshape(g, nslot, NSA_HD).astype(o_ref.dtype)


def nsa_compress(slots, cmp_pos, w1, b1, w2):
    _, b, g, nslot, half = slots.shape
    pos = cmp_pos.reshape(2, 1, CMP_BLK * NSA_HD)
    return pl.pallas_call(
        _compress_kernel,
        grid=(2, b),
        in_specs=[pl.BlockSpec((1, 1, g, nslot, half), lambda kv, bi: (kv, bi, 0, 0, 0)),
                  pl.BlockSpec((1, 1, 2 * half), lambda kv, bi: (kv, 0, 0)),
                  pl.BlockSpec((1, 2 * half, CMP_HIDDEN), lambda kv, bi: (kv, 0, 0)),
                  pl.BlockSpec((1, 1, CMP_HIDDEN), lambda kv, bi: (kv, 0, 0)),
                  pl.BlockSpec((1, CMP_HIDDEN, NSA_HD), lambda kv, bi: (kv, 0, 0))],
        out_specs=pl.BlockSpec((1, 1, g, nslot, NSA_HD), lambda kv, bi: (kv, bi, 0, 0, 0)),
        out_shape=jax.ShapeDtypeStruct((2, b, g, nslot, NSA_HD), BF16),
        compiler_params=_cparams(("parallel", "parallel")),
        name="nsa_compress",
    )(slots, pos, w1, b1.reshape(2, 1, CMP_HIDDEN), w2)


def _exp2_bf16(s, m):
    rows = s.shape[0]
    chunk = min(EXP_CHUNK, rows)
    return jnp.concatenate([jnp.exp2((s[r:r + chunk, :] - m).astype(BF16)) for r in range(0, rows, chunk)], axis=0)


def _nsa_kernel(q_ref, qf_ref, kc_ref, vc_ref, ks_ref, vs_ref, kw_ref, vw_ref, gl_ref, c2s_ref, bias_ref, o_ref,
                qp_ref, imp_ref, s0_ref, s1_ref, mx_ref, m_ref, acc_ref, *, n_cmp):
    tq = q_ref.shape[1]
    lanes = NSA_GRP * tq
    seq = ks_ref.shape[2]
    n_slc = seq // SLC_BLK
    t0 = pl.multiple_of(pl.program_id(2) * tq, tq)
    head = lambda r: slice(r * tq, (r + 1) * tq)

    def scores(k_tile, bias=None):
        s = jnp.dot(k_tile, qp_ref[...], preferred_element_type=F32)
        return s if bias is None else s + bias

    qt = q_ref[0].astype(F32).T * (NSA_HD ** -0.5 * LOG2E)
    qp_ref[...] = qf_ref[0]
    for r in range(NSA_GRP):
        qp_ref[0:NSA_HD, head(r)] = qt[r * NSA_HD:(r + 1) * NSA_HD, :].astype(BF16)

    s_cmp = scores(kc_ref[0, 0])
    win = []
    for off, size, bias in ((0, tq, bias_ref[1]), (tq, WINDOW - tq, None), (WINDOW, tq, bias_ref[0])):
        k0 = pl.multiple_of(t0 + off, tq)
        win.append((scores(kw_ref[0, 0, pl.ds(k0, size), :], bias), vw_ref[0, 0, :, pl.ds(k0, size)]))

    ncp = kc_ref.shape[2]
    n_row = lax.broadcasted_iota(jnp.int32, (ncp, lanes), 0)
    t_lane = t0 + lax.broadcasted_iota(jnp.int32, (ncp, lanes), 1) % tq
    ok = (t_lane >= n_row * CMP_STRIDE + (CMP_BLK - 1)) & (n_row < n_cmp)
    s = jnp.where(ok, s_cmp, NEG)
    p = jnp.where(ok, jnp.exp2(s - jnp.max(s, axis=0, keepdims=True)), 0.0)
    l = jnp.sum(p, axis=0, keepdims=True)
    p_cmp = p * (1.0 / jnp.where(l > 0.0, l, 1.0))
    o_cmp = jnp.dot(vc_ref[0, 0], p_cmp.astype(BF16), preferred_element_type=F32)

    rest = p_cmp[:, head(0)]
    for r in range(1, NSA_GRP):
        rest = rest + p_cmp[:, head(r)]
    imp = jnp.zeros((n_slc, tq), F32)
    for _ in range(3):
        part = rest.astype(BF16)
        rest = rest - part.astype(F32)
        imp = imp + jnp.dot(c2s_ref[...], part, preferred_element_type=F32)
    j_row = lax.broadcasted_iota(jnp.int32, (n_slc, tq), 0)
    t_col = t0 + lax.broadcasted_iota(jnp.int32, (n_slc, tq), 1)
    forced = (j_row == 0) | (j_row == t_col // SLC_BLK)
    causal = j_row * SLC_BLK <= t_col
    imp = jnp.where(forced, FORCE, jnp.where(causal, imp, -FORCE))
    imp_ref[...] = imp

    n_grp = n_slc // SUBLANES
    vals = [imp[v * SUBLANES:(v + 1) * SUBLANES, :] for v in range(n_grp)]
    sub = lax.broadcasted_iota(jnp.int32, (SUBLANES, tq), 0)
    rank = [jnp.zeros((SUBLANES, tq), jnp.int32) for _ in range(n_grp)]
    for i in range(n_slc):
        bi = jnp.broadcast_to(imp_ref[i:i + 1, :], (SUBLANES, tq))
        for v in range(n_grp):
            if v < i // SUBLANES:
                ahead = bi > vals[v]
            elif v > i // SUBLANES:
                ahead = bi >= vals[v]
            else:
                ahead = (bi > vals[v]) | ((bi == vals[v]) & (sub > i % SUBLANES))
            rank[v] = rank[v] + jnp.where(ahead, 1, 0)
    chosen = (jnp.concatenate(rank, axis=0) < min(SLC_TOPN, n_slc)) & causal
    mask = jnp.where(chosen, 0.0, NEG).astype(BF16)
    for r in range(NSA_GRP):
        qp_ref[MASK_OFF:MASK_OFF + n_slc, head(r)] = mask

    tk = min(ATT_TK, seq)
    n_full = t0 // tk
    last = n_full // 2

    s_bufs = (s0_ref, s1_ref)

    def pair_scores(step, buf):
        for par in range(2):
            k0 = pl.multiple_of((2 * step + par) * tk, tk)
            s = scores(ks_ref[0, 0, pl.ds(k0, tk), :])
            s_bufs[buf][par] = s
            mx_ref[buf, par] = jnp.max(s, axis=0, keepdims=True)

    def pair_update(step, buf, prefetch):
        m_new = [jnp.maximum(m_ref[par], mx_ref[buf, par]) for par in range(2)]
        probs = [_exp2_bf16(s_bufs[buf].at[par], m_new[par]) for par in range(2)]
        if prefetch:
            pair_scores(step + 1, 1 - buf)
        for par in range(2):
            k0 = pl.multiple_of((2 * step + par) * tk, tk)
            pv = jnp.dot(vs_ref[0, 0, :, pl.ds(k0, tk)], probs[par], preferred_element_type=F32)
            acc_ref[par] = jnp.exp2(m_ref[par] - m_new[par]) * acc_ref[par] + pv
            m_ref[par] = m_new[par]

    def mask_diagonal(buf):
        off = pl.multiple_of(t0 - n_full * tk, tq)
        for par in range(2):
            @pl.when(n_full % 2 == par)
            def _():
                tile = s_bufs[buf].at[par]
                tile[pl.ds(off, tq), :] = tile[pl.ds(off, tq), :] + bias_ref[0]
                mx_ref[buf, par] = jnp.max(tile[...], axis=0, keepdims=True)

    def half_trip(step, buf):
        @pl.when(step == last)
        def _():
            mask_diagonal(buf)

        @pl.when(step < last)
        def _():
            pair_update(step, buf, prefetch=True)

        @pl.when(step == last)
        def _():
            pair_update(step, buf, prefetch=False)

    def trip(i, carry):
        half_trip(2 * i, 0)
        half_trip(2 * i + 1, 1)
        return carry

    m_ref[...] = jnp.full(m_ref.shape, NEG, F32)
    acc_ref[...] = jnp.zeros(acc_ref.shape, F32)
    pair_scores(0, 0)

    mx = functools.reduce(jnp.maximum, [jnp.max(s, axis=0, keepdims=True) for s, _ in win])
    o_win = jnp.zeros((V_ROWS, lanes), F32)
    for s, v_tile in win:
        o_win = o_win + jnp.dot(v_tile, _exp2_bf16(s, mx), preferred_element_type=F32)
    o_win = o_win[0:NSA_HD] * (1.0 / o_win[NSA_HD:NSA_HD + 1])

    lax.fori_loop(0, last // 2 + 1, trip, 0)
    m_all = jnp.maximum(m_ref[0], m_ref[1])
    o_slc = jnp.exp2(m_ref[0] - m_all) * acc_ref[0] + jnp.exp2(m_ref[1] - m_all) * acc_ref[1]
    o_slc = o_slc[0:NSA_HD] * (1.0 / o_slc[NSA_HD:NSA_HD + 1])

    gates = jax.nn.sigmoid(gl_ref[0, 0].astype(F32))
    outs = []
    for r in range(NSA_GRP):
        outs.append(gates[3 * r:3 * r + 1] * o_cmp[:, head(r)] + gates[3 * r + 1:3 * r + 2] * o_slc[:, head(r)]
                    + gates[3 * r + 2:3 * r + 3] * o_win[:, head(r)])
    o_ref[0] = jnp.concatenate(outs, axis=0).T.astype(o_ref.dtype)


def _alibi_slopes():
    return (2.0 ** (-8.0 * np.arange(1, NSA_HEADS + 1) / NSA_HEADS)).astype(np.float32)


def _bf16_split3(v):
    parts, rest = [], np.asarray(v, np.float32)
    for _ in range(3):
        part = rest.astype(ml_dtypes.bfloat16).astype(np.float32)
        parts.append(part)
        rest = (rest.astype(np.float64) - part.astype(np.float64)).astype(np.float32)
    return parts


def _query_features(tq):
    feat = np.zeros((NSA_HEADS, FEAT), np.float32)
    hi, mid, lo = _bf16_split3(_alibi_slopes() * np.float32(LOG2E))
    for c, part in enumerate((hi, mid, lo, hi, mid, lo)):
        feat[:, ALIBI_OFF + c] = part
    feat[:, PAD_OFF] = NEG
    feat = np.repeat(feat.reshape(NSA_KV_HEADS, NSA_GRP, FEAT), tq, axis=1)
    return jnp.asarray(feat.transpose(0, 2, 1), BF16)


def _key_features(pos, n_slc):
    pos = np.asarray(pos)
    real = pos >= 0
    feat = np.zeros((pos.shape[0], FEAT - NSA_HD), np.float32)
    if n_slc:
        feat[real, pos[real] // SLC_BLK] = 1.0
    a0 = ALIBI_OFF - NSA_HD
    feat[real, a0:a0 + 3] = ((pos[real] // SLC_BLK) * SLC_BLK)[:, None]
    feat[real, a0 + 3:a0 + 6] = (pos[real] % SLC_BLK)[:, None]
    feat[~real, PAD_OFF - NSA_HD] = 1.0
    return feat


def _values_t(v):
    b, g, n, _ = v.shape
    extra = jnp.zeros((b, g, V_ROWS - NSA_HD, n), v.dtype).at[:, :, 0].set(1.0)
    return jnp.concatenate([jnp.swapaxes(v, 2, 3), extra], axis=2)


def _with_features(k, feat):
    b, g, _, _ = k.shape
    f = jnp.broadcast_to(jnp.asarray(feat, BF16), (b, g) + feat.shape)
    return jnp.concatenate([k.astype(BF16), f], axis=3)


def nsa_attention(z3, qcol0, kc, vc, ks, vs, kw, vw, gate_logits):
    b, s, _ = z3.shape
    tq = ATT_TQ
    tk = min(ATT_TK, s)
    assert s % (2 * tk) == 0 and tk % tq == 0 and tq % SLC_BLK == 0 and WINDOW % tq == 0 and WINDOW > tq
    n_slc = s // SLC_BLK
    assert n_slc <= ALIBI_OFF - MASK_OFF and n_slc % SUBLANES == 0
    n_cmp = (s - CMP_BLK) // CMP_STRIDE + 1
    ncp = kc.shape[2]
    gw = NSA_GRP * NSA_HD
    lanes = NSA_GRP * tq

    kc_f = _with_features(kc, _key_features(np.arange(ncp) * CMP_STRIDE + CMP_BLK - 1, 0))
    ks_f = _with_features(ks, _key_features(np.arange(s), n_slc))
    front = ((0, 0), (0, 0), (WINDOW, 0), (0, 0))
    kw_f = _with_features(jnp.pad(kw, front), _key_features(np.arange(-WINDOW, s), 0))
    t = lambda v: jnp.swapaxes(v, 2, 3)
    vw_t = _values_t(jnp.pad(vw, front))

    c_start = np.arange(ncp) * CMP_STRIDE
    s_start = np.arange(n_slc) * SLC_BLK
    overlap = np.clip(np.minimum(c_start[:, None] + CMP_BLK, s_start[None, :] + SLC_BLK)
                      - np.maximum(c_start[:, None], s_start[None, :]), 0, None) / CMP_BLK
    overlap[n_cmp:] = 0.0
    c2s_t = jnp.asarray(overlap.T, BF16)

    key = np.arange(tq)[:, None]
    qry = np.tile(np.arange(tq), NSA_GRP)[None, :]
    bias = jnp.asarray(np.stack([np.where(key <= qry, 0.0, NEG), np.where(key > qry, 0.0, NEG)]), F32)

    blk = lambda *shape: pl.BlockSpec((1, 1) + shape, lambda bi, g, qi: (bi, g, 0, 0))
    const = lambda *shape: pl.BlockSpec(shape, lambda bi, g, qi: (0,) * len(shape))
    return pl.pallas_call(
        functools.partial(_nsa_kernel, n_cmp=n_cmp),
        grid=(b, NSA_KV_HEADS, s // tq),
        in_specs=[pl.BlockSpec((1, tq, gw), lambda bi, g, qi: (bi, qi, qcol0 // gw + g)),
                  pl.BlockSpec((1, FEAT, lanes), lambda bi, g, qi: (g, 0, 0)),
                  blk(ncp, FEAT), blk(NSA_HD, ncp),
                  blk(s, FEAT), blk(V_ROWS, s),
                  blk(s + WINDOW, FEAT), blk(V_ROWS, s + WINDOW),
                  pl.BlockSpec((1, 1, NSA_GRP * 3, tq), lambda bi, g, qi: (bi, g, 0, qi)),
                  const(n_slc, ncp), const(2, tq, lanes)],
        out_specs=pl.BlockSpec((1, tq, gw), lambda bi, g, qi: (bi, qi, g)),
        out_shape=jax.ShapeDtypeStruct((b, s, NSA_W), BF16),
        scratch_shapes=[pltpu.VMEM((FEAT, lanes), BF16), pltpu.VMEM((n_slc, tq), F32),
                        pltpu.VMEM((2, tk, lanes), F32), pltpu.VMEM((2, tk, lanes), F32),
                        pltpu.VMEM((2, 2, 1, lanes), F32),
                        pltpu.VMEM((2, 1, lanes), F32), pltpu.VMEM((2, V_ROWS, lanes), F32)],
        compiler_params=_cparams(("parallel", "parallel", "arbitrary")),
        name="nsa_attention",
    )(z3, _query_features(tq), kc_f, t(vc), ks_f, _values_t(vs), kw_f, vw_t, gate_logits, c2s_t, bias)


def _merge_kernel(ya_ref, yb_ref, yc_ref, wb_ref, ga_ref, gb_ref, gc_ref, o_ref):
    acc = None
    for i, (y_ref, g_ref) in enumerate(((ya_ref, ga_ref), (yb_ref, gb_ref), (yc_ref, gc_ref))):
        term = jax.nn.sigmoid(g_ref[...].astype(F32)) * jnp.dot(y_ref[...], wb_ref[i], preferred_element_type=F32)
        acc = term if acc is None else acc + term
    o_ref[...] = acc.astype(o_ref.dtype)


def branch_merge(ya, yb, yc, wb, zmerge, tm=1024, tn=512):
    t = ya.shape[0]
    d = wb.shape[2]
    nj = d // tn
    y_spec = pl.BlockSpec((tm, BRANCH_W), lambda i, j: (i, 0))
    gate = lambda br: pl.BlockSpec((tm, tn), lambda i, j: (i, j + br * nj))
    return pl.pallas_call(
        _merge_kernel,
        grid=(t // tm, nj),
        in_specs=[y_spec, y_spec, y_spec,
                  pl.BlockSpec((3, BRANCH_W, tn), lambda i, j: (0, 0, j)),
                  gate(0), gate(1), gate(2)],
        out_specs=pl.BlockSpec((tm, tn), lambda i, j: (i, j)),
        out_shape=jax.ShapeDtypeStruct((t, d), BF16),
        compiler_params=_cparams(("parallel", "parallel")),
        name="branch_merge",
    )(ya, yb, yc, wb, zmerge, zmerge, zmerge)


def _layer_norm(y, g, b):
    mu = jnp.mean(y, axis=-1, keepdims=True)
    yc = y - mu
    var = jnp.mean(yc * yc, axis=-1, keepdims=True)
    return yc * lax.rsqrt(var + LN_EPS) * g + b


def _outproj_ln_kernel(m_ref, w_ref, x_ref, g_ref, b_ref, o_ref, ob_ref, *, alpha):
    mix = jnp.dot(m_ref[...], w_ref[...], preferred_element_type=F32)
    y = _layer_norm(alpha * x_ref[...] + mix, g_ref[...], b_ref[...])
    o_ref[...] = y
    ob_ref[...] = y.astype(BF16)


def outproj_ln(merged, w_out, x, g, b, alpha, tm=256):
    t, d = x.shape
    row = pl.BlockSpec((tm, d), lambda i: (i, 0))
    vec = pl.BlockSpec((1, d), lambda i: (0, 0))
    return pl.pallas_call(
        functools.partial(_outproj_ln_kernel, alpha=alpha),
        grid=(t // tm,),
        in_specs=[row, pl.BlockSpec((d, d), lambda i: (0, 0)), row, vec, vec],
        out_specs=[row, row],
        out_shape=[jax.ShapeDtypeStruct((t, d), F32), jax.ShapeDtypeStruct((t, d), BF16)],
        compiler_params=_cparams(("parallel",)),
        name="outproj_ln",
    )(merged, w_out, x, g.reshape(1, d), b.reshape(1, d))


def _final_kernel(x_ref, xb_ref, wg_ref, p_ref, wp_ref, y0_ref, y1_ref, gate_ref, g_ref, b_ref, o_ref, ob_ref,
                  *, alpha):
    ple = jax.nn.sigmoid(jnp.dot(xb_ref[...], wg_ref[...], preferred_element_type=F32)) * jnp.dot(
        p_ref[...], wp_ref[...], preferred_element_type=F32)
    gate = gate_ref[...]
    ffn = gate[:, 0:1] * y0_ref[...].astype(F32) + gate[:, 1:2] * y1_ref[...].astype(F32)
    y = _layer_norm(alpha * x_ref[...] + ffn + ple, g_ref[...], b_ref[...])
    o_ref[...] = y
    ob_ref[...] = y.astype(BF16)


def ple_combine_ln(x, xb, w_ple_gate, p, w_ple, y0, y1, gate, g, b, alpha, tm=256):
    t, d = x.shape
    pd = p.shape[1]
    row = pl.BlockSpec((tm, d), lambda i: (i, 0))
    vec = pl.BlockSpec((1, d), lambda i: (0, 0))
    return pl.pallas_call(
        functools.partial(_final_kernel, alpha=alpha),
        grid=(t // tm,),
        in_specs=[row, row, pl.BlockSpec((d, d), lambda i: (0, 0)),
                  pl.BlockSpec((tm, pd), lambda i: (i, 0)), pl.BlockSpec((pd, d), lambda i: (0, 0)),
                  row, row, pl.BlockSpec((tm, TOP_K), lambda i: (i, 0)), vec, vec],
        out_specs=[row, row],
        out_shape=[jax.ShapeDtypeStruct((t, d), F32), jax.ShapeDtypeStruct((t, d), BF16)],
        compiler_params=_cparams(("parallel",)),
        name="ple_combine_ln",
    )(x, xb, w_ple_gate, p, w_ple, y0, y1, gate, g.reshape(1, d), b.reshape(1, d))


def _router_kernel(x_ref, wr_ref, br_ref, idx_ref, gate_ref):
    tm = x_ref.shape[0]
    logits = lax.dot_general(wr_ref[...], x_ref[...], (((1,), (1,)), ((), ())),
                             precision=lax.Precision.HIGHEST, preferred_element_type=F32)
    aff = jax.nn.sigmoid(logits)
    sel = aff + br_ref[...]
    scores = []
    for g in range(N_GROUPS):
        v = [sel[g * EXPERTS_PER_GROUP + e:g * EXPERTS_PER_GROUP + e + 1, :] for e in range(EXPERTS_PER_GROUP)]
        best = None
        for a in range(EXPERTS_PER_GROUP):
            for c in range(a + 1, EXPERTS_PER_GROUP):
                pair = v[a] + v[c]
                best = pair if best is None else jnp.maximum(best, pair)
        scores.append(best)
    best_score, best_group = scores[0], jnp.zeros((1, tm), jnp.int32)
    for g in range(1, N_GROUPS):
        better = scores[g] > best_score
        best_score = jnp.where(better, scores[g], best_score)
        best_group = jnp.where(better, g, best_group)
    e_row = lax.broadcasted_iota(jnp.int32, (N_EXPERTS, tm), 0)
    cand = jnp.where(e_row // EXPERTS_PER_GROUP == best_group, sel, NEG)
    picks, gates = [], []
    for _ in range(TOP_K):
        mx = jnp.max(cand, axis=0, keepdims=True)
        pick = jnp.min(jnp.where(cand == mx, e_row, N_EXPERTS), axis=0, keepdims=True)
        hit = e_row == pick
        picks.append(pick)
        gates.append(jnp.sum(jnp.where(hit, aff, 0.0), axis=0, keepdims=True))
        cand = jnp.where(hit, -jnp.inf, cand)
    total = gates[0] + gates[1]
    idx_ref[...] = jnp.concatenate(picks, axis=0)
    gate_ref[...] = jnp.concatenate([gates[0] / total, gates[1] / total], axis=0)


def router(x, w_router, b_router, tm=1024):
    t, d = x.shape
    return pl.pallas_call(
        _router_kernel,
        grid=(t // tm,),
        in_specs=[pl.BlockSpec((tm, d), lambda i: (i, 0)),
                  pl.BlockSpec((N_EXPERTS, d), lambda i: (0, 0)),
                  pl.BlockSpec((N_EXPERTS, 1), lambda i: (0, 0))],
        out_specs=[pl.BlockSpec((TOP_K, tm), lambda i: (0, i)), pl.BlockSpec((TOP_K, tm), lambda i: (0, i))],
        out_shape=[jax.ShapeDtypeStruct((TOP_K, t), jnp.int32), jax.ShapeDtypeStruct((TOP_K, t), F32)],
        compiler_params=_cparams(("parallel",)),
        name="router",
    )(x, w_router.T, b_router.reshape(N_EXPERTS, 1))


def _expert_kernel(be_ref, nu_ref, x_ref, wg_ref, wu_ref, wd_ref, *rest):
    prev_ref = rest[0] if len(rest) == 5 else None
    o_ref, wgb_ref, wub_ref, wdb_ref = rest[-4:]
    i = pl.program_id(0)

    @pl.when((i == 0) | (be_ref[i] != be_ref[jnp.maximum(i - 1, 0)]))
    def _():
        wgb_ref[...] = wg_ref[0, 0].astype(BF16)
        wub_ref[...] = wu_ref[0, 0].astype(BF16)
        wdb_ref[...] = wd_ref[0, 0].astype(BF16)

    @pl.when(i < nu_ref[0])
    def _():
        x = x_ref[...]
        g = jnp.dot(x, wgb_ref[...], preferred_element_type=F32)
        u = jnp.dot(x, wub_ref[...], preferred_element_type=F32)
        out = jnp.dot((jax.nn.silu(g) * u).astype(BF16), wdb_ref[...], preferred_element_type=F32)
        if prev_ref is not None:
            out = out + prev_ref[...].astype(F32)
        o_ref[...] = out.astype(o_ref.dtype)

    @pl.when(i >= nu_ref[0])
    def _():
        o_ref[...] = jnp.zeros_like(o_ref)


def expert_ffn(block_expert, n_used, x_sorted, w_gate_up, w_down, layer):
    n_rows, d = x_sorted.shape
    de = w_down.shape[2]
    ce = de // EXPERT_CHUNKS
    out = None
    for c in range(EXPERT_CHUNKS):
        last = c == EXPERT_CHUNKS - 1
        row = pl.BlockSpec((MOE_BLK, d), lambda i, be, nu: (i, 0))
        in_specs = [row,
                    pl.BlockSpec((1, 1, d, ce), lambda i, be, nu, c=c: (layer, be[i], 0, c)),
                    pl.BlockSpec((1, 1, d, ce), lambda i, be, nu, c=c: (layer, be[i], 0, EXPERT_CHUNKS + c)),
                    pl.BlockSpec((1, 1, ce, d), lambda i, be, nu, c=c: (layer, be[i], c, 0))]
        args = [x_sorted, w_gate_up, w_gate_up, w_down]
        if out is not None:
            in_specs.append(row)
            args.append(out)
        out = pl.pallas_call(
            _expert_kernel,
            grid_spec=pltpu.PrefetchScalarGridSpec(
                num_scalar_prefetch=2,
                grid=(n_rows // MOE_BLK,),
                in_specs=in_specs,
                out_specs=row,
                scratch_shapes=[pltpu.VMEM((d, ce), BF16), pltpu.VMEM((d, ce), BF16), pltpu.VMEM((ce, d), BF16)]),
            out_shape=jax.ShapeDtypeStruct((n_rows, d), BF16),
            compiler_params=_cparams(("arbitrary",)),
            name="expert_ffn",
        )(block_expert, n_used, *args)
    return out


def _dispatch_tables(expert_idx):
    n_tok = expert_idx.shape[0]
    n_assign = n_tok * TOP_K
    e_flat = expert_idx.reshape(n_assign)
    onehot = (e_flat[:, None] == jnp.arange(N_EXPERTS)[None, :]).astype(jnp.int32)
    csum = jnp.cumsum(onehot, axis=0)
    rank = jnp.take_along_axis(csum, e_flat[:, None], axis=1)[:, 0] - 1
    counts = csum[-1]
    padded = (counts + MOE_BLK - 1) // MOE_BLK * MOE_BLK
    pad_end = jnp.cumsum(padded)
    dest = (pad_end - padded)[e_flat] + rank
    n_rows = n_assign + N_EXPERTS * MOE_BLK
    row_tok = jnp.zeros((n_rows,), jnp.int32).at[dest].set(jnp.arange(n_assign, dtype=jnp.int32) // TOP_K)
    block_expert = jnp.minimum(
        jnp.searchsorted(pad_end, jnp.arange(n_rows // MOE_BLK) * MOE_BLK, side="right"), N_EXPERTS - 1)
    n_used = (pad_end[-1:] // MOE_BLK).astype(jnp.int32)
    return row_tok, block_expert.astype(jnp.int32), n_used, dest.reshape(n_tok, TOP_K)


def _token_mixer(xb, bsz, seq, w_in_t, layer, conv_a_w, conv_a_b, lru_conv_w, lru_conv_b, lru_wa, lru_ba, lru_wx, lru_bx,
                 lru_lam, cmp_pos, phi_w1, phi_b1, phi_w2, w_branch):
    d = xb.shape[1]
    n_tok = bsz * seq
    main_w = 3 * BRANCH_W + 2 * BRANCH_W + NSA_W + 6 * KV_W
    ng = 3 * NSA_HEADS
    kv0 = main_w - 6 * KV_W
    zmain = matmul_wres(xb, w_in_t, layer, 0, main_w, BF16, 2048, 768)
    zgate = matmul_wres(xb, w_in_t, layer, main_w, LANES, F32, 2048, LANES)
    zmerge = matmul_wres(xb, w_in_t, layer, main_w + ng, w_in_t.shape[1] - main_w - ng, BF16, 2048, 512)
    z3 = zmain.reshape(bsz, seq, main_w)

    y_a = conv_mixer(z3, conv_a_w, conv_a_b)
    y_b = lru_mixer(z3, 3 * BRANCH_W, lru_conv_w, lru_conv_b, lru_wa, lru_ba, lru_wx, lru_bx, lru_lam)

    kv = z3[:, :, kv0:].reshape(bsz, seq, 6, NSA_KV_HEADS, NSA_HD).transpose(2, 0, 3, 1, 4)
    nslot = seq // CMP_STRIDE
    slots = kv[0:2].reshape(2, bsz, NSA_KV_HEADS, nslot, CMP_STRIDE * NSA_HD)
    cmp = nsa_compress(slots, cmp_pos, phi_w1, phi_b1, phi_w2)
    gate_logits = zgate[:, :ng].reshape(bsz, seq, NSA_KV_HEADS, NSA_GRP * 3).transpose(0, 2, 3, 1)
    y_c = nsa_attention(z3, 5 * BRANCH_W, cmp[0], cmp[1], kv[2], kv[3], kv[4], kv[5], gate_logits)

    flat = lambda y: y.reshape(n_tok, BRANCH_W)
    return branch_merge(flat(y_a), flat(y_b), flat(y_c), w_branch.astype(BF16), zmerge)


def _moe(x, xb, w_router, b_router, w_gate_up, w_down, layer):
    idx_t, gate_t = router(x, w_router, b_router)
    row_tok, block_expert, n_used, dest = _dispatch_tables(idx_t.T)
    rows = expert_ffn(block_expert, n_used, xb[row_tok], w_gate_up, w_down, layer)
    return rows[dest[:, 0]], rows[dest[:, 1]], gate_t.T


def kernel(x, p, w_in, conv_a_w, conv_a_b, lru_conv_w, lru_conv_b, lru_wa, lru_ba, lru_wx, lru_bx, lru_lam, cmp_pos, phi_w1, phi_b1, phi_w2, w_branch, w_out, ln_g, ln_b, w_router, b_router, w_gate_up, w_down, w_ple, w_ple_gate):
    bsz, seq, d = x.shape
    depth = w_in.shape[0]
    n_tok = bsz * seq
    alpha = (2 * depth) ** 0.25
    xf = x.reshape(n_tok, d)
    xb = xf.astype(BF16)
    w_in_t = jnp.swapaxes(w_in, 1, 2)
    for i in range(depth):
        merged = _token_mixer(xb, bsz, seq, w_in_t, i, conv_a_w[i], conv_a_b[i], lru_conv_w[i], lru_conv_b[i],
                              lru_wa[i], lru_ba[i], lru_wx[i], lru_bx[i], lru_lam[i], cmp_pos[i], phi_w1[i],
                              phi_b1[i], phi_w2[i], w_branch[i])
        xf, xb = outproj_ln(merged, w_out[i].astype(BF16), xf, ln_g[i, 0], ln_b[i, 0], alpha)
        y0, y1, gate = _moe(xf, xb, w_router, b_router, w_gate_up, w_down, i)
        xf, xb = ple_combine_ln(xf, xb, w_ple_gate[i].astype(BF16), p[i].reshape(n_tok, -1).astype(BF16),
                                w_ple[i].astype(BF16), y0, y1, gate, ln_g[i, 1], ln_b[i, 1], alpha)
    return xf.reshape(bsz, seq, d)
```

```python
import functools

import ml_dtypes
import numpy as np
import jax
import jax.numpy as jnp
from jax import lax
from jax.experimental import pallas as pl
from jax.experimental.pallas import tpu as pltpu

F32 = jnp.float32
BF16 = jnp.bfloat16

BRANCH_W = 1024
CONV_K = 3
LRU_HEADS = 8
LRU_HD = BRANCH_W // LRU_HEADS
LRU_CONV_K = 4
LRU_C = 8.0
NSA_HEADS = 16
NSA_KV_HEADS = 4
NSA_GRP = NSA_HEADS // NSA_KV_HEADS
NSA_HD = 64
NSA_W = NSA_HEADS * NSA_HD
KV_W = NSA_KV_HEADS * NSA_HD
CMP_BLK = 32
CMP_STRIDE = 16
CMP_HIDDEN = 128
SLC_BLK = 64
SLC_TOPN = 16
WINDOW = 512
N_EXPERTS = 16
N_GROUPS = 4
EXPERTS_PER_GROUP = N_EXPERTS // N_GROUPS
TOP_K = 2
LN_EPS = 1e-5
NEG = -1e30
FORCE = 1e9

LANES = 128
SUBLANES = 8
VMEM_LIMIT = 56 * 1024 * 1024

FEAT = 256
MASK_OFF = NSA_HD
ALIBI_OFF = 2 * NSA_HD
PAD_OFF = ALIBI_OFF + 6
V_ROWS = NSA_HD + 16
LOG2E = 1.4426950408889634

ATT_TQ = 256
ATT_TK = 512
EXP_CHUNK = 64
MOE_BLK = 256
GATHER_BLK = 2048
EXPERT_CHUNKS = 2


def _cparams(sem):
    return pltpu.CompilerParams(dimension_semantics=sem, vmem_limit_bytes=VMEM_LIMIT)


def _mm_wres_kernel(x_ref, w_ref, *rest, shift):
    o_ref, wb_ref = rest[-2:]

    @pl.when(pl.program_id(1) == 0)
    def _():
        w = w_ref[0]
        if shift:
            tn = w.shape[0]
            w = jnp.concatenate([w, rest[0][0]], axis=0)[shift:shift + tn]
        wb_ref[...] = w.astype(BF16)

    o_ref[...] = lax.dot_general(x_ref[...], wb_ref[...], (((1,), (1,)), ((), ())),
                                 preferred_element_type=F32).astype(o_ref.dtype)


def matmul_wres(x, wt3, layer, row0, n, out_dtype, tm, tn):
    m, k = x.shape
    shift = row0 % tn
    base = row0 - shift
    assert m % tm == 0 and n % tn == 0 and shift % SUBLANES == 0 and shift <= LANES and tn % LANES == 0
    in_specs = [pl.BlockSpec((tm, k), lambda j, i: (i, 0)),
                pl.BlockSpec((1, tn, k), lambda j, i: (layer, base // tn + j, 0))]
    args = [x, wt3]
    if shift:
        in_specs.append(pl.BlockSpec((1, LANES, k), lambda j, i: (layer, (base + (j + 1) * tn) // LANES, 0)))
        args.append(wt3)
    return pl.pallas_call(
        functools.partial(_mm_wres_kernel, shift=shift),
        grid=(n // tn, m // tm),
        in_specs=in_specs,
        out_specs=pl.BlockSpec((tm, tn), lambda j, i: (i, j)),
        out_shape=jax.ShapeDtypeStruct((m, n), out_dtype),
        scratch_shapes=[pltpu.VMEM((tn, k), BF16)],
        compiler_params=_cparams(("parallel", "arbitrary")),
        name="matmul_wres",
    )(*args)


def _causal_conv(ext, cur, w, k):
    acc = cur * w[k - 1:k, :]
    for j in range(k - 1):
        shift = k - 1 - j
        acc = acc + pltpu.roll(ext, shift, 0)[SUBLANES:, :] * w[j:j + 1, :]
    return acc


def _conv_mixer_kernel(ain_ref, ab_ref, ac_ref, w_ref, b_ref, o_ref, halo_ref):
    @pl.when(pl.program_id(2) == 0)
    def _():
        halo_ref[...] = jnp.zeros_like(halo_ref)

    v = ac_ref[0].astype(F32) * ain_ref[0].astype(F32)
    ext = jnp.concatenate([halo_ref[...], v], axis=0)
    y = _causal_conv(ext, v, w_ref[...], CONV_K) + b_ref[...]
    o_ref[0] = (ab_ref[0].astype(F32) * y).astype(o_ref.dtype)
    halo_ref[...] = v[v.shape[0] - SUBLANES:, :]


def conv_mixer(z3, conv_w, conv_b, ts=512, cw=512):
    b, s, _ = z3.shape
    nc = BRANCH_W // cw
    return pl.pallas_call(
        _conv_mixer_kernel,
        grid=(b, nc, s // ts),
        in_specs=[pl.BlockSpec((1, ts, cw), lambda bi, c, si: (bi, si, c)),
                  pl.BlockSpec((1, ts, cw), lambda bi, c, si: (bi, si, c + nc)),
                  pl.BlockSpec((1, ts, cw), lambda bi, c, si: (bi, si, c + 2 * nc)),
                  pl.BlockSpec((CONV_K, cw), lambda bi, c, si: (0, c)),
                  pl.BlockSpec((1, cw), lambda bi, c, si: (0, c))],
        out_specs=pl.BlockSpec((1, ts, cw), lambda bi, c, si: (bi, si, c)),
        out_shape=jax.ShapeDtypeStruct((b, s, BRANCH_W), BF16),
        scratch_shapes=[pltpu.VMEM((SUBLANES, cw), F32)],
        compiler_params=_cparams(("parallel", "parallel", "arbitrary")),
        name="conv_mixer",
    )(z3, z3, z3, conv_w, conv_b.reshape(1, BRANCH_W))


def _lru_kernel(gate_ref, rin_ref, cw_ref, cb_ref, wa_ref, ba_ref, wx_ref, bx_ref, lam_ref, o_ref,
                halo_ref, carry_ref, a_ref, b_ref, h_ref):
    ts, lw = a_ref.shape

    @pl.when(pl.program_id(2) == 0)
    def _():
        halo_ref[...] = jnp.zeros_like(halo_ref)
        carry_ref[...] = jnp.zeros_like(carry_ref)

    u = rin_ref[0].astype(F32)
    ext = jnp.concatenate([halo_ref[...], u], axis=0)
    xc = _causal_conv(ext, u, cw_ref[...], LRU_CONV_K) + cb_ref[...]
    halo_ref[...] = u[ts - SUBLANES:, :]

    xcb = xc.astype(BF16)
    ra, ix = [], []
    for hh in range(lw // LRU_HD):
        xh = xcb[:, hh * LRU_HD:(hh + 1) * LRU_HD]
        ra.append(jnp.dot(xh, wa_ref[hh].astype(BF16), preferred_element_type=F32))
        ix.append(jnp.dot(xh, wx_ref[hh].astype(BF16), preferred_element_type=F32))
    r = jax.nn.sigmoid(jnp.concatenate(ra, axis=1) + ba_ref[...])
    i = jax.nn.sigmoid(jnp.concatenate(ix, axis=1) + bx_ref[...])
    neg_lam = -lam_ref[...]
    softplus = jnp.maximum(neg_lam, 0.0) + jnp.log1p(jnp.exp(-jnp.abs(neg_lam)))
    log_a = -LRU_C * r * softplus
    a_ref[...] = jnp.exp(log_a)
    th = jnp.tanh(log_a)
    b_ref[...] = jnp.sqrt(-2.0 * th / (1.0 - th)) * (i * xc)

    row = lax.broadcasted_iota(jnp.int32, (SUBLANES, lw), 0)

    def group(g, hc):
        sl = pl.ds(pl.multiple_of(g * SUBLANES, SUBLANES), SUBLANES)
        a = a_ref[sl, :]
        bv = b_ref[sl, :]
        for d in (1, 2, 4):
            keep = row >= d
            bv = jnp.where(keep, a * pltpu.roll(bv, d, 0) + bv, bv)
            a = jnp.where(keep, a * pltpu.roll(a, d, 0), a)
        h = a * hc + bv
        h_ref[sl, :] = h
        return jnp.broadcast_to(h[SUBLANES - 1:SUBLANES, :], (SUBLANES, lw))

    carry_ref[...] = lax.fori_loop(0, ts // SUBLANES, group, carry_ref[...])
    o_ref[0] = (h_ref[...] * jax.nn.gelu(gate_ref[0].astype(F32))).astype(o_ref.dtype)


def lru_mixer(z3, col0, conv_w, conv_b, wa, ba, wx, bx, lam, ts=512, lw=256):
    b, s, _ = z3.shape
    nc = BRANCH_W // lw
    c0 = col0 // lw
    hpb = lw // LRU_HD
    row = lambda v: v.reshape(1, BRANCH_W)
    vec = pl.BlockSpec((1, lw), lambda bi, c, si: (0, c))
    mat = pl.BlockSpec((hpb, LRU_HD, LRU_HD), lambda bi, c, si: (c, 0, 0))
    return pl.pallas_call(
        _lru_kernel,
        grid=(b, nc, s // ts),
        in_specs=[pl.BlockSpec((1, ts, lw), lambda bi, c, si: (bi, si, c0 + c)),
                  pl.BlockSpec((1, ts, lw), lambda bi, c, si: (bi, si, c0 + nc + c)),
                  pl.BlockSpec((LRU_CONV_K, lw), lambda bi, c, si: (0, c)),
                  vec, mat, vec, mat, vec, vec],
        out_specs=pl.BlockSpec((1, ts, lw), lambda bi, c, si: (bi, si, c)),
        out_shape=jax.ShapeDtypeStruct((b, s, BRANCH_W), BF16),
        scratch_shapes=[pltpu.VMEM((SUBLANES, lw), F32), pltpu.VMEM((SUBLANES, lw), F32),
                        pltpu.VMEM((ts, lw), F32), pltpu.VMEM((ts, lw), F32), pltpu.VMEM((ts, lw), F32)],
        compiler_params=_cparams(("parallel", "parallel", "arbitrary")),
        name="rg_lru",
    )(z3, z3, conv_w, row(conv_b), wa, row(ba), wx, row(bx), row(lam))


def _compress_kernel(slots_ref, pos_ref, w1_ref, b1_ref, w2_ref, o_ref):
    g, nslot, half = slots_ref.shape[2:]
    rows = g * nslot
    x = slots_ref[0, 0].reshape(rows, half)
    w1 = w1_ref[0].astype(BF16)
    first = jnp.dot(x, w1[:half], preferred_element_type=F32)
    second = jnp.dot(x, w1[half:], preferred_element_type=F32)
    pos = jnp.broadcast_to(pos_ref[0], (SUBLANES, 2 * half)).astype(BF16)
    posb = jnp.dot(pos, w1, preferred_element_type=F32)[0:1, :]
    hidden = first + pltpu.roll(second, rows - 1, 0) + posb + b1_ref[0]
    out = jnp.dot(jax.nn.gelu(hidden).astype(BF16), w2_ref[0].astype(BF16), preferred_element_type=F32)
    o_ref[0, 0] = out.reshape(g, nslot, NSA_HD).astype(o_ref.dtype)


def nsa_compress(slots, cmp_pos, w1, b1, w2):
    _, b, g, nslot, half = slots.shape
    pos = cmp_pos.reshape(2, 1, CMP_BLK * NSA_HD)
    return pl.pallas_call(
        _compress_kernel,
        grid=(2, b),
        in_specs=[pl.BlockSpec((1, 1, g, nslot, half), lambda kv, bi: (kv, bi, 0, 0, 0)),
                  pl.BlockSpec((1, 1, 2 * half), lambda kv, bi: (kv, 0, 0)),
                  pl.BlockSpec((1, 2 * half, CMP_HIDDEN), lambda kv, bi: (kv, 0, 0)),
                  pl.BlockSpec((1, 1, CMP_HIDDEN), lambda kv, bi: (kv, 0, 0)),
                  pl.BlockSpec((1, CMP_HIDDEN, NSA_HD), lambda kv, bi: (kv, 0, 0))],
        out_specs=pl.BlockSpec((1, 1, g, nslot, NSA_HD), lambda kv, bi: (kv, bi, 0, 0, 0)),
        out_shape=jax.ShapeDtypeStruct((2, b, g, nslot, NSA_HD), BF16),
        compiler_params=_cparams(("parallel", "parallel")),
        name="nsa_compress",
    )(slots, pos, w1, b1.reshape(2, 1, CMP_HIDDEN), w2)


def _exp2_bf16(s, m):
    rows = s.shape[0]
    chunk = min(EXP_CHUNK, rows)
    return jnp.concatenate([jnp.exp2((s[r:r + chunk, :] - m).astype(BF16)) for r in range(0, rows, chunk)], axis=0)


def _nsa_kernel(q_ref, qf_ref, kc_ref, vc_ref, ks_ref, vs_ref, kw_ref, vw_ref, gl_ref, c2s_ref, bias_ref, o_ref,
                qp_ref, imp_ref, s0_ref, s1_ref, mx_ref, m_ref, acc_ref, *, n_cmp):
    tq = q_ref.shape[1]
    lanes = NSA_GRP * tq
    seq = ks_ref.shape[2]
    n_slc = seq // SLC_BLK
    t0 = pl.multiple_of(pl.program_id(2) * tq, tq)
    head = lambda r: slice(r * tq, (r + 1) * tq)

    def scores(k_tile, bias=None):
        s = jnp.dot(k_tile, qp_ref[...], preferred_element_type=F32)
        return s if bias is None else s + bias

    qt = q_ref[0].astype(F32).T * (NSA_HD ** -0.5 * LOG2E)
    qp_ref[...] = qf_ref[0]
    for r in range(NSA_GRP):
        qp_ref[0:NSA_HD, head(r)] = qt[r * NSA_HD:(r + 1) * NSA_HD, :].astype(BF16)

    s_cmp = scores(kc_ref[0, 0])
    win = []
    for off, size, bias in ((0, tq, bias_ref[1]), (tq, WINDOW - tq, None), (WINDOW, tq, bias_ref[0])):
        k0 = pl.multiple_of(t0 + off, tq)
        win.append((scores(kw_ref[0, 0, pl.ds(k0, size), :], bias), vw_ref[0, 0, :, pl.ds(k0, size)]))

    ncp = kc_ref.shape[2]
    n_row = lax.broadcasted_iota(jnp.int32, (ncp, lanes), 0)
    t_lane = t0 + lax.broadcasted_iota(jnp.int32, (ncp, lanes), 1) % tq
    ok = (t_lane >= n_row * CMP_STRIDE + (CMP_BLK - 1)) & (n_row < n_cmp)
    s = jnp.where(ok, s_cmp, NEG)
    p = jnp.where(ok, jnp.exp2(s - jnp.max(s, axis=0, keepdims=True)), 0.0)
    l = jnp.sum(p, axis=0, keepdims=True)
    p_cmp = p * (1.0 / jnp.where(l > 0.0, l, 1.0))
    o_cmp = jnp.dot(vc_ref[0, 0], p_cmp.astype(BF16), preferred_element_type=F32)

    rest = p_cmp[:, head(0)]
    for r in range(1, NSA_GRP):
        rest = rest + p_cmp[:, head(r)]
    imp = jnp.zeros((n_slc, tq), F32)
    for _ in range(3):
        part = rest.astype(BF16)
        rest = rest - part.astype(F32)
        imp = imp + jnp.dot(c2s_ref[...], part, preferred_element_type=F32)
    j_row = lax.broadcasted_iota(jnp.int32, (n_slc, tq), 0)
    t_col = t0 + lax.broadcasted_iota(jnp.int32, (n_slc, tq), 1)
    forced = (j_row == 0) | (j_row == t_col // SLC_BLK)
    causal = j_row * SLC_BLK <= t_col
    imp = jnp.where(forced, FORCE, jnp.where(causal, imp, -FORCE))
    imp_ref[...] = imp

    n_grp = n_slc // SUBLANES
    vals = [imp[v * SUBLANES:(v + 1) * SUBLANES, :] for v in range(n_grp)]
    sub = lax.broadcasted_iota(jnp.int32, (SUBLANES, tq), 0)
    rank = [jnp.zeros((SUBLANES, tq), jnp.int32) for _ in range(n_grp)]
    for i in range(n_slc):
        bi = jnp.broadcast_to(imp_ref[i:i + 1, :], (SUBLANES, tq))
        for v in range(n_grp):
            if v < i // SUBLANES:
                ahead = bi > vals[v]
            elif v > i // SUBLANES:
                ahead = bi >= vals[v]
            else:
                ahead = (bi > vals[v]) | ((bi == vals[v]) & (sub > i % SUBLANES))
            rank[v] = rank[v] + jnp.where(ahead, 1, 0)
    chosen = (jnp.concatenate(rank, axis=0) < min(SLC_TOPN, n_slc)) & causal
    mask = jnp.where(chosen, 0.0, NEG).astype(BF16)
    for r in range(NSA_GRP):
        qp_ref[MASK_OFF:MASK_OFF + n_slc, head(r)] = mask

    tk = min(ATT_TK, seq)
    n_full = t0 // tk
    last = n_full // 2

    s_bufs = (s0_ref, s1_ref)

    def pair_scores(step, buf):
        for par in range(2):
            k0 = pl.multiple_of((2 * step + par) * tk, tk)
            s = scores(ks_ref[0, 0, pl.ds(k0, tk), :])
            s_bufs[buf][par] = s
            mx_ref[buf, par] = jnp.max(s, axis=0, keepdims=True)

    def pair_update(step, buf, prefetch):
        m_new = [jnp.maximum(m_ref[par], mx_ref[buf, par]) for par in range(2)]
        probs = [_exp2_bf16(s_bufs[buf].at[par], m_new[par]) for par in range(2)]
        if prefetch:
            pair_scores(step + 1, 1 - buf)
        for par in range(2):
            k0 = pl.multiple_of((2 * step + par) * tk, tk)
            pv = jnp.dot(vs_ref[0, 0, :, pl.ds(k0, tk)], probs[par], preferred_element_type=F32)
            acc_ref[par] = jnp.exp2(m_ref[par] - m_new[par]) * acc_ref[par] + pv
            m_ref[par] = m_new[par]

    def mask_diagonal(buf):
        off = pl.multiple_of(t0 - n_full * tk, tq)
        for par in range(2):
            @pl.when(n_full % 2 == par)
            def _():
                tile = s_bufs[buf].at[par]
                tile[pl.ds(off, tq), :] = tile[pl.ds(off, tq), :] + bias_ref[0]
                mx_ref[buf, par] = jnp.max(tile[...], axis=0, keepdims=True)

    def half_trip(step, buf):
        @pl.when(step == last)
        def _():
            mask_diagonal(buf)

        @pl.when(step < last)
        def _():
            pair_update(step, buf, prefetch=True)

        @pl.when(step == last)
        def _():
            pair_update(step, buf, prefetch=False)

    def trip(i, carry):
        half_trip(2 * i, 0)
        half_trip(2 * i + 1, 1)
        return carry

    m_ref[...] = jnp.full(m_ref.shape, NEG, F32)
    acc_ref[...] = jnp.zeros(acc_ref.shape, F32)
    pair_scores(0, 0)

    mx = functools.reduce(jnp.maximum, [jnp.max(s, axis=0, keepdims=True) for s, _ in win])
    o_win = jnp.zeros((V_ROWS, lanes), F32)
    for s, v_tile in win:
        o_win = o_win + jnp.dot(v_tile, _exp2_bf16(s, mx), preferred_element_type=F32)
    o_win = o_win[0:NSA_HD] * (1.0 / o_win[NSA_HD:NSA_HD + 1])

    lax.fori_loop(0, last // 2 + 1, trip, 0)
    m_all = jnp.maximum(m_ref[0], m_ref[1])
    o_slc = jnp.exp2(m_ref[0] - m_all) * acc_ref[0] + jnp.exp2(m_ref[1] - m_all) * acc_ref[1]
    o_slc = o_slc[0:NSA_HD] * (1.0 / o_slc[NSA_HD:NSA_HD + 1])

    gates = jax.nn.sigmoid(gl_ref[0, 0].astype(F32))
    outs = []
    for r in range(NSA_GRP):
        outs.append(gates[3 * r:3 * r + 1] * o_cmp[:, head(r)] + gates[3 * r + 1:3 * r + 2] * o_slc[:, head(r)]
                    + gates[3 * r + 2:3 * r + 3] * o_win[:, head(r)])
    o_ref[0] = jnp.concatenate(outs, axis=0).T.astype(o_ref.dtype)


def _alibi_slopes():
    return (2.0 ** (-8.0 * np.arange(1, NSA_HEADS + 1) / NSA_HEADS)).astype(np.float32)


def _bf16_split3(v):
    parts, rest = [], np.asarray(v, np.float32)
    for _ in range(3):
        part = rest.astype(ml_dtypes.bfloat16).astype(np.float32)
        parts.append(part)
        rest = (rest.astype(np.float64) - part.astype(np.float64)).astype(np.float32)
    return parts


def _query_features(tq):
    feat = np.zeros((NSA_HEADS, FEAT), np.float32)
    hi, mid, lo = _bf16_split3(_alibi_slopes() * np.float32(LOG2E))
    for c, part in enumerate((hi, mid, lo, hi, mid, lo)):
        feat[:, ALIBI_OFF + c] = part
    feat[:, PAD_OFF] = NEG
    feat = np.repeat(feat.reshape(NSA_KV_HEADS, NSA_GRP, FEAT), tq, axis=1)
    return jnp.asarray(feat.transpose(0, 2, 1), BF16)


def _key_features(pos, n_slc):
    pos = np.asarray(pos)
    real = pos >= 0
    feat = np.zeros((pos.shape[0], FEAT - NSA_HD), np.float32)
    if n_slc:
        feat[real, pos[real] // SLC_BLK] = 1.0
    a0 = ALIBI_OFF - NSA_HD
    feat[real, a0:a0 + 3] = ((pos[real] // SLC_BLK) * SLC_BLK)[:, None]
    feat[real, a0 + 3:a0 + 6] = (pos[real] % SLC_BLK)[:, None]
    feat[~real, PAD_OFF - NSA_HD] = 1.0
    return feat


def _values_t(v):
    b, g, n, _ = v.shape
    extra = jnp.zeros((b, g, V_ROWS - NSA_HD, n), v.dtype).at[:, :, 0].set(1.0)
    return jnp.concatenate([jnp.swapaxes(v, 2, 3), extra], axis=2)


def _with_features(k, feat):
    b, g, _, _ = k.shape
    f = jnp.broadcast_to(jnp.asarray(feat, BF16), (b, g) + feat.shape)
    return jnp.concatenate([k.astype(BF16), f], axis=3)


def nsa_attention(z3, qcol0, kc, vc, ks, vs, kw, vw, gate_logits):
    b, s, _ = z3.shape
    tq = ATT_TQ
    tk = min(ATT_TK, s)
    assert s % (2 * tk) == 0 and tk % tq == 0 and tq % SLC_BLK == 0 and WINDOW % tq == 0 and WINDOW > tq
    n_slc = s // SLC_BLK
    assert n_slc <= ALIBI_OFF - MASK_OFF and n_slc % SUBLANES == 0
    n_cmp = (s - CMP_BLK) // CMP_STRIDE + 1
    ncp = kc.shape[2]
    gw = NSA_GRP * NSA_HD
    lanes = NSA_GRP * tq

    kc_f = _with_features(kc, _key_features(np.arange(ncp) * CMP_STRIDE + CMP_BLK - 1, 0))
    ks_f = _with_features(ks, _key_features(np.arange(s), n_slc))
    front = ((0, 0), (0, 0), (WINDOW, 0), (0, 0))
    kw_f = _with_features(jnp.pad(kw, front), _key_features(np.arange(-WINDOW, s), 0))
    t = lambda v: jnp.swapaxes(v, 2, 3)
    vw_t = _values_t(jnp.pad(vw, front))

    c_start = np.arange(ncp) * CMP_STRIDE
    s_start = np.arange(n_slc) * SLC_BLK
    overlap = np.clip(np.minimum(c_start[:, None] + CMP_BLK, s_start[None, :] + SLC_BLK)
                      - np.maximum(c_start[:, None], s_start[None, :]), 0, None) / CMP_BLK
    overlap[n_cmp:] = 0.0
    c2s_t = jnp.asarray(overlap.T, BF16)

    key = np.arange(tq)[:, None]
    qry = np.tile(np.arange(tq), NSA_GRP)[None, :]
    bias = jnp.asarray(np.stack([np.where(key <= qry, 0.0, NEG), np.where(key > qry, 0.0, NEG)]), F32)

    blk = lambda *shape: pl.BlockSpec((1, 1) + shape, lambda bi, g, qi: (bi, g, 0, 0))
    const = lambda *shape: pl.BlockSpec(shape, lambda bi, g, qi: (0,) * len(shape))
    return pl.pallas_call(
        functools.partial(_nsa_kernel, n_cmp=n_cmp),
        grid=(b, NSA_KV_HEADS, s // tq),
        in_specs=[pl.BlockSpec((1, tq, gw), lambda bi, g, qi: (bi, qi, qcol0 // gw + g)),
                  pl.BlockSpec((1, FEAT, lanes), lambda bi, g, qi: (g, 0, 0)),
                  blk(ncp, FEAT), blk(NSA_HD, ncp),
                  blk(s, FEAT), blk(V_ROWS, s),
                  blk(s + WINDOW, FEAT), blk(V_ROWS, s + WINDOW),
                  pl.BlockSpec((1, 1, NSA_GRP * 3, tq), lambda bi, g, qi: (bi, g, 0, qi)),
                  const(n_slc, ncp), const(2, tq, lanes)],
        out_specs=pl.BlockSpec((1, tq, gw), lambda bi, g, qi: (bi, qi, g)),
        out_shape=jax.ShapeDtypeStruct((b, s, NSA_W), BF16),
        scratch_shapes=[pltpu.VMEM((FEAT, lanes), BF16), pltpu.VMEM((n_slc, tq), F32),
                        pltpu.VMEM((2, tk, lanes), F32), pltpu.VMEM((2, tk, lanes), F32),
                        pltpu.VMEM((2, 2, 1, lanes), F32),
                        pltpu.VMEM((2, 1, lanes), F32), pltpu.VMEM((2, V_ROWS, lanes), F32)],
        compiler_params=_cparams(("parallel", "parallel", "arbitrary")),
        name="nsa_attention",
    )(z3, _query_features(tq), kc_f, t(vc), ks_f, _values_t(vs), kw_f, vw_t, gate_logits, c2s_t, bias)


def _merge_kernel(ya_ref, yb_ref, yc_ref, wb_ref, ga_ref, gb_ref, gc_ref, o_ref):
    acc = None
    for i, (y_ref, g_ref) in enumerate(((ya_ref, ga_ref), (yb_ref, gb_ref), (yc_ref, gc_ref))):
        term = jax.nn.sigmoid(g_ref[...].astype(F32)) * jnp.dot(y_ref[...], wb_ref[i], preferred_element_type=F32)
        acc = term if acc is None else acc + term
    o_ref[...] = acc.astype(o_ref.dtype)


def branch_merge(ya, yb, yc, wb, zmerge, tm=1024, tn=512):
    t = ya.shape[0]
    d = wb.shape[2]
    nj = d // tn
    y_spec = pl.BlockSpec((tm, BRANCH_W), lambda i, j: (i, 0))
    gate = lambda br: pl.BlockSpec((tm, tn), lambda i, j: (i, j + br * nj))
    return pl.pallas_call(
        _merge_kernel,
        grid=(t // tm, nj),
        in_specs=[y_spec, y_spec, y_spec,
                  pl.BlockSpec((3, BRANCH_W, tn), lambda i, j: (0, 0, j)),
                  gate(0), gate(1), gate(2)],
        out_specs=pl.BlockSpec((tm, tn), lambda i, j: (i, j)),
        out_shape=jax.ShapeDtypeStruct((t, d), BF16),
        compiler_params=_cparams(("parallel", "parallel")),
        name="branch_merge",
    )(ya, yb, yc, wb, zmerge, zmerge, zmerge)


def _layer_norm(y, g, b):
    mu = jnp.mean(y, axis=-1, keepdims=True)
    yc = y - mu
    var = jnp.mean(yc * yc, axis=-1, keepdims=True)
    return yc * lax.rsqrt(var + LN_EPS) * g + b


def _outproj_ln_kernel(m_ref, w_ref, x_ref, g_ref, b_ref, o_ref, ob_ref, *, alpha):
    mix = jnp.dot(m_ref[...], w_ref[...], preferred_element_type=F32)
    y = _layer_norm(alpha * x_ref[...] + mix, g_ref[...], b_ref[...])
    o_ref[...] = y
    ob_ref[...] = y.astype(BF16)


def outproj_ln(merged, w_out, x, g, b, alpha, tm=256):
    t, d = x.shape
    row = pl.BlockSpec((tm, d), lambda i: (i, 0))
    vec = pl.BlockSpec((1, d), lambda i: (0, 0))
    return pl.pallas_call(
        functools.partial(_outproj_ln_kernel, alpha=alpha),
        grid=(t // tm,),
        in_specs=[row, pl.BlockSpec((d, d), lambda i: (0, 0)), row, vec, vec],
        out_specs=[row, row],
        out_shape=[jax.ShapeDtypeStruct((t, d), F32), jax.ShapeDtypeStruct((t, d), BF16)],
        compiler_params=_cparams(("parallel",)),
        name="outproj_ln",
    )(merged, w_out, x, g.reshape(1, d), b.reshape(1, d))


def _final_kernel(x_ref, xb_ref, wg_ref, p_ref, wp_ref, y0_ref, y1_ref, gate_ref, g_ref, b_ref, o_ref, ob_ref,
                  *, alpha):
    ple = jax.nn.sigmoid(jnp.dot(xb_ref[...], wg_ref[...], preferred_element_type=F32)) * jnp.dot(
        p_ref[...], wp_ref[...], preferred_element_type=F32)
    gate = gate_ref[...]
    ffn = gate[:, 0:1] * y0_ref[...].astype(F32) + gate[:, 1:2] * y1_ref[...].astype(F32)
    y = _layer_norm(alpha * x_ref[...] + ffn + ple, g_ref[...], b_ref[...])
    o_ref[...] = y
    ob_ref[...] = y.astype(BF16)


def ple_combine_ln(x, xb, w_ple_gate, p, w_ple, y0, y1, gate, g, b, alpha, tm=256):
    t, d = x.shape
    pd = p.shape[1]
    row = pl.BlockSpec((tm, d), lambda i: (i, 0))
    vec = pl.BlockSpec((1, d), lambda i: (0, 0))
    return pl.pallas_call(
        functools.partial(_final_kernel, alpha=alpha),
        grid=(t // tm,),
        in_specs=[row, row, pl.BlockSpec((d, d), lambda i: (0, 0)),
                  pl.BlockSpec((tm, pd), lambda i: (i, 0)), pl.BlockSpec((pd, d), lambda i: (0, 0)),
                  row, row, pl.BlockSpec((tm, TOP_K), lambda i: (i, 0)), vec, vec],
        out_specs=[row, row],
        out_shape=[jax.ShapeDtypeStruct((t, d), F32), jax.ShapeDtypeStruct((t, d), BF16)],
        compiler_params=_cparams(("parallel",)),
        name="ple_combine_ln",
    )(x, xb, w_ple_gate, p, w_ple, y0, y1, gate, g.reshape(1, d), b.reshape(1, d))


def _router_kernel(x_ref, wr_ref, br_ref, idx_ref, gate_ref):
    tm = x_ref.shape[0]
    logits = lax.dot_general(wr_ref[...], x_ref[...], (((1,), (1,)), ((), ())),
                             precision=lax.Precision.HIGHEST, preferred_element_type=F32)
    aff = jax.nn.sigmoid(logits)
    sel = aff + br_ref[...]
    scores = []
    for g in range(N_GROUPS):
        v = [sel[g * EXPERTS_PER_GROUP + e:g * EXPERTS_PER_GROUP + e + 1, :] for e in range(EXPERTS_PER_GROUP)]
        best = None
        for a in range(EXPERTS_PER_GROUP):
            for c in range(a + 1, EXPERTS_PER_GROUP):
                pair = v[a] + v[c]
                best = pair if best is None else jnp.maximum(best, pair)
        scores.append(best)
    best_score, best_group = scores[0], jnp.zeros((1, tm), jnp.int32)
    for g in range(1, N_GROUPS):
        better = scores[g] > best_score
        best_score = jnp.where(better, scores[g], best_score)
        best_group = jnp.where(better, g, best_group)
    e_row = lax.broadcasted_iota(jnp.int32, (N_EXPERTS, tm), 0)
    cand = jnp.where(e_row // EXPERTS_PER_GROUP == best_group, sel, NEG)
    picks, gates = [], []
    for _ in range(TOP_K):
        mx = jnp.max(cand, axis=0, keepdims=True)
        pick = jnp.min(jnp.where(cand == mx, e_row, N_EXPERTS), axis=0, keepdims=True)
        hit = e_row == pick
        picks.append(pick)
        gates.append(jnp.sum(jnp.where(hit, aff, 0.0), axis=0, keepdims=True))
        cand = jnp.where(hit, -jnp.inf, cand)
    total = gates[0] + gates[1]
    idx_ref[...] = jnp.concatenate(picks, axis=0)
    gate_ref[...] = jnp.concatenate([gates[0] / total, gates[1] / total], axis=0)


def router(x, w_router, b_router, tm=1024):
    t, d = x.shape
    return pl.pallas_call(
        _router_kernel,
        grid=(t // tm,),
        in_specs=[pl.BlockSpec((tm, d), lambda i: (i, 0)),
                  pl.BlockSpec((N_EXPERTS, d), lambda i: (0, 0)),
                  pl.BlockSpec((N_EXPERTS, 1), lambda i: (0, 0))],
        out_specs=[pl.BlockSpec((TOP_K, tm), lambda i: (0, i)), pl.BlockSpec((TOP_K, tm), lambda i: (0, i))],
        out_shape=[jax.ShapeDtypeStruct((TOP_K, t), jnp.int32), jax.ShapeDtypeStruct((TOP_K, t), F32)],
        compiler_params=_cparams(("parallel",)),
        name="router",
    )(x, w_router.T, b_router.reshape(N_EXPERTS, 1))


def _expert_kernel(be_ref, nu_ref, x_ref, wg_ref, wu_ref, wd_ref, *rest):
    prev_ref = rest[0] if len(rest) == 5 else None
    o_ref, wgb_ref, wub_ref, wdb_ref = rest[-4:]
    i = pl.program_id(0)

    @pl.when((i == 0) | (be_ref[i] != be_ref[jnp.maximum(i - 1, 0)]))
    def _():
        wgb_ref[...] = wg_ref[0, 0].astype(BF16)
        wub_ref[...] = wu_ref[0, 0].astype(BF16)
        wdb_ref[...] = wd_ref[0, 0].astype(BF16)

    @pl.when(i < nu_ref[0])
    def _():
        x = x_ref[...]
        g = jnp.dot(x, wgb_ref[...], preferred_element_type=F32)
        u = jnp.dot(x, wub_ref[...], preferred_element_type=F32)
        out = jnp.dot((jax.nn.silu(g) * u).astype(BF16), wdb_ref[...], preferred_element_type=F32)
        if prev_ref is not None:
            out = out + prev_ref[...].astype(F32)
        o_ref[...] = out.astype(o_ref.dtype)

    @pl.when(i >= nu_ref[0])
    def _():
        o_ref[...] = jnp.zeros_like(o_ref)


def expert_ffn(block_expert, n_used, x_sorted, w_gate_up, w_down, layer):
    n_rows, d = x_sorted.shape
    de = w_down.shape[2]
    ce = de // EXPERT_CHUNKS
    out = None
    for c in range(EXPERT_CHUNKS):
        last = c == EXPERT_CHUNKS - 1
        row = pl.BlockSpec((MOE_BLK, d), lambda i, be, nu: (i, 0))
        in_specs = [row,
                    pl.BlockSpec((1, 1, d, ce), lambda i, be, nu, c=c: (layer, be[i], 0, c)),
                    pl.BlockSpec((1, 1, d, ce), lambda i, be, nu, c=c: (layer, be[i], 0, EXPERT_CHUNKS + c)),
                    pl.BlockSpec((1, 1, ce, d), lambda i, be, nu, c=c: (layer, be[i], c, 0))]
        args = [x_sorted, w_gate_up, w_gate_up, w_down]
        if out is not None:
            in_specs.append(row)
            args.append(out)
        out = pl.pallas_call(
            _expert_kernel,
            grid_spec=pltpu.PrefetchScalarGridSpec(
                num_scalar_prefetch=2,
                grid=(n_rows // MOE_BLK,),
                in_specs=in_specs,
                out_specs=row,
                scratch_shapes=[pltpu.VMEM((d, ce), BF16), pltpu.VMEM((d, ce), BF16), pltpu.VMEM((ce, d), BF16)]),
            out_shape=jax.ShapeDtypeStruct((n_rows, d), BF16),
            compiler_params=_cparams(("arbitrary",)),
            name="expert_ffn",
        )(block_expert, n_used, *args)
    return out


def _row_gather_kernel(idx_ref, src_ref, dst_ref, sem):
    base = pl.program_id(0) * GATHER_BLK

    def issue(r, carry):
        pltpu.make_async_copy(src_ref.at[idx_ref[base + r]], dst_ref.at[base + r], sem).start()
        return carry

    lax.fori_loop(0, GATHER_BLK, issue, 0, unroll=8)
    block = dst_ref.at[pl.ds(base, GATHER_BLK)]
    pltpu.make_async_copy(block, block, sem).wait()


def gather_rows(x, idx):
    t, d = x.shape
    n = idx.shape[0]
    assert n % GATHER_BLK == 0 and d % LANES == 0
    tile = (d // LANES, LANES)
    out = pl.pallas_call(
        _row_gather_kernel,
        grid_spec=pltpu.PrefetchScalarGridSpec(
            num_scalar_prefetch=1,
            grid=(n // GATHER_BLK,),
            in_specs=[pl.BlockSpec(memory_space=pl.ANY)],
            out_specs=pl.BlockSpec(memory_space=pl.ANY),
            scratch_shapes=[pltpu.SemaphoreType.DMA(())]),
        out_shape=jax.ShapeDtypeStruct((n,) + tile, x.dtype),
        compiler_params=_cparams(("arbitrary",)),
        name="gather_rows",
    )(idx, x.reshape((t,) + tile))
    return out.reshape(n, d)


def _dispatch_tables(expert_idx):
    n_tok = expert_idx.shape[0]
    n_assign = n_tok * TOP_K
    e_flat = expert_idx.reshape(n_assign)
    onehot = (e_flat[:, None] == jnp.arange(N_EXPERTS)[None, :]).astype(jnp.int32)
    csum = jnp.cumsum(onehot, axis=0)
    rank = jnp.take_along_axis(csum, e_flat[:, None], axis=1)[:, 0] - 1
    counts = csum[-1]
    padded = (counts + MOE_BLK - 1) // MOE_BLK * MOE_BLK
    pad_end = jnp.cumsum(padded)
    dest = (pad_end - padded)[e_flat] + rank
    n_rows = n_assign + N_EXPERTS * MOE_BLK
    row_tok = jnp.zeros((n_rows,), jnp.int32).at[dest].set(jnp.arange(n_assign, dtype=jnp.int32) // TOP_K)
    block_expert = jnp.minimum(
        jnp.searchsorted(pad_end, jnp.arange(n_rows // MOE_BLK) * MOE_BLK, side="right"), N_EXPERTS - 1)
    n_used = (pad_end[-1:] // MOE_BLK).astype(jnp.int32)
    return row_tok, block_expert.astype(jnp.int32), n_used, dest.reshape(n_tok, TOP_K)


def _token_mixer(xb, bsz, seq, w_in_t, layer, conv_a_w, conv_a_b, lru_conv_w, lru_conv_b, lru_wa, lru_ba, lru_wx, lru_bx,
                 lru_lam, cmp_pos, phi_w1, phi_b1, phi_w2, w_branch):
    d = xb.shape[1]
    n_tok = bsz * seq
    main_w = 3 * BRANCH_W + 2 * BRANCH_W + NSA_W + 6 * KV_W
    ng = 3 * NSA_HEADS
    kv0 = main_w - 6 * KV_W
    zmain = matmul_wres(xb, w_in_t, layer, 0, main_w, BF16, 2048, 768)
    zgate = matmul_wres(xb, w_in_t, layer, main_w, LANES, F32, 2048, LANES)
    zmerge = matmul_wres(xb, w_in_t, layer, main_w + ng, w_in_t.shape[1] - main_w - ng, BF16, 2048, 512)
    z3 = zmain.reshape(bsz, seq, main_w)

    y_a = conv_mixer(z3, conv_a_w, conv_a_b)
    y_b = lru_mixer(z3, 3 * BRANCH_W, lru_conv_w, lru_conv_b, lru_wa, lru_ba, lru_wx, lru_bx, lru_lam)

    kv = z3[:, :, kv0:].reshape(bsz, seq, 6, NSA_KV_HEADS, NSA_HD).transpose(2, 0, 3, 1, 4)
    nslot = seq // CMP_STRIDE
    slots = kv[0:2].reshape(2, bsz, NSA_KV_HEADS, nslot, CMP_STRIDE * NSA_HD)
    cmp = nsa_compress(slots, cmp_pos, phi_w1, phi_b1, phi_w2)
    gate_logits = zgate[:, :ng].reshape(bsz, seq, NSA_KV_HEADS, NSA_GRP * 3).transpose(0, 2, 3, 1)
    y_c = nsa_attention(z3, 5 * BRANCH_W, cmp[0], cmp[1], kv[2], kv[3], kv[4], kv[5], gate_logits)

    flat = lambda y: y.reshape(n_tok, BRANCH_W)
    return branch_merge(flat(y_a), flat(y_b), flat(y_c), w_branch.astype(BF16), zmerge)


def _moe(x, xb, w_router, b_router, w_gate_up, w_down, layer):
    idx_t, gate_t = router(x, w_router, b_router)
    row_tok, block_expert, n_used, dest = _dispatch_tables(idx_t.T)
    rows = expert_ffn(block_expert, n_used, gather_rows(xb, row_tok), w_gate_up, w_down, layer)
    return rows[dest[:, 0]], rows[dest[:, 1]], gate_t.T


def kernel(x, p, w_in, conv_a_w, conv_a_b, lru_conv_w, lru_conv_b, lru_wa, lru_ba, lru_wx, lru_bx, lru_lam, cmp_pos, phi_w1, phi_b1, phi_w2, w_branch, w_out, ln_g, ln_b, w_router, b_router, w_gate_up, w_down, w_ple, w_ple_gate):
    bsz, seq, d = x.shape
    depth = w_in.shape[0]
    n_tok = bsz * seq
    alpha = (2 * depth) ** 0.25
    xf = x.reshape(n_tok, d)
    xb = xf.astype(BF16)
    w_in_t = jnp.swapaxes(w_in, 1, 2)
    for i in range(depth):
        merged = _token_mixer(xb, bsz, seq, w_in_t, i, conv_a_w[i], conv_a_b[i], lru_conv_w[i], lru_conv_b[i],
                              lru_wa[i], lru_ba[i], lru_wx[i], lru_bx[i], lru_lam[i], cmp_pos[i], phi_w1[i],
                              phi_b1[i], phi_w2[i], w_branch[i])
        xf, xb = outproj_ln(merged, w_out[i].astype(BF16), xf, ln_g[i, 0], ln_b[i, 0], alpha)
        y0, y1, gate = _moe(xf, xb, w_router, b_router, w_gate_up, w_down, i)
        xf, xb = ple_combine_ln(xf, xb, w_ple_gate[i].astype(BF16), p[i].reshape(n_tok, -1).astype(BF16),
                                w_ple[i].astype(BF16), y0, y1, gate, ln_g[i, 1], ln_b[i, 1], alpha)
    return xf.reshape(bsz, seq, d)
```

```python
import functools

import ml_dtypes
import numpy as np
import jax
import jax.numpy as jnp
from jax import lax
from jax.experimental import pallas as pl
from jax.experimental.pallas import tpu as pltpu

F32 = jnp.float32
BF16 = jnp.bfloat16

BRANCH_W = 1024
CONV_K = 3
LRU_HEADS = 8
LRU_HD = BRANCH_W // LRU_HEADS
LRU_CONV_K = 4
LRU_C = 8.0
NSA_HEADS = 16
NSA_KV_HEADS = 4
NSA_GRP = NSA_HEADS // NSA_KV_HEADS
NSA_HD = 64
NSA_W = NSA_HEADS * NSA_HD
KV_W = NSA_KV_HEADS * NSA_HD
CMP_BLK = 32
CMP_STRIDE = 16
CMP_HIDDEN = 128
SLC_BLK = 64
SLC_TOPN = 16
WINDOW = 512
N_EXPERTS = 16
N_GROUPS = 4
EXPERTS_PER_GROUP = N_EXPERTS // N_GROUPS
TOP_K = 2
LN_EPS = 1e-5
NEG = -1e30
FORCE = 1e9

LANES = 128
SUBLANES = 8
VMEM_LIMIT = 56 * 1024 * 1024

FEAT = 256
MASK_OFF = NSA_HD
ALIBI_OFF = 2 * NSA_HD
PAD_OFF = ALIBI_OFF + 6
V_ROWS = NSA_HD + 16
LOG2E = 1.4426950408889634

ATT_TQ = 256
ATT_TK = 512
EXP_CHUNK = 64
MOE_BLK = 256
EXPERT_CHUNKS = 2


def _cparams(sem):
    return pltpu.CompilerParams(dimension_semantics=sem, vmem_limit_bytes=VMEM_LIMIT)


def _mm_wres_kernel(x_ref, w_ref, *rest, shift):
    o_ref, wb_ref = rest[-2:]

    @pl.when(pl.program_id(1) == 0)
    def _():
        w = w_ref[0]
        if shift:
            tn = w.shape[0]
            w = jnp.concatenate([w, rest[0][0]], axis=0)[shift:shift + tn]
        wb_ref[...] = w.astype(BF16)

    o_ref[...] = lax.dot_general(x_ref[...], wb_ref[...], (((1,), (1,)), ((), ())),
                                 preferred_element_type=F32).astype(o_ref.dtype)


def matmul_wres(x, wt3, layer, row0, n, out_dtype, tm, tn):
    m, k = x.shape
    shift = row0 % tn
    base = row0 - shift
    assert m % tm == 0 and n % tn == 0 and shift % SUBLANES == 0 and shift <= LANES and tn % LANES == 0
    in_specs = [pl.BlockSpec((tm, k), lambda j, i: (i, 0)),
                pl.BlockSpec((1, tn, k), lambda j, i: (layer, base // tn + j, 0))]
    args = [x, wt3]
    if shift:
        in_specs.append(pl.BlockSpec((1, LANES, k), lambda j, i: (layer, (base + (j + 1) * tn) // LANES, 0)))
        args.append(wt3)
    return pl.pallas_call(
        functools.partial(_mm_wres_kernel, shift=shift),
        grid=(n // tn, m // tm),
        in_specs=in_specs,
        out_specs=pl.BlockSpec((tm, tn), lambda j, i: (i, j)),
        out_shape=jax.ShapeDtypeStruct((m, n), out_dtype),
        scratch_shapes=[pltpu.VMEM((tn, k), BF16)],
        compiler_params=_cparams(("parallel", "arbitrary")),
        name="matmul_wres",
    )(*args)


def _causal_conv(ext, cur, w, k):
    acc = cur * w[k - 1:k, :]
    for j in range(k - 1):
        shift = k - 1 - j
        acc = acc + pltpu.roll(ext, shift, 0)[SUBLANES:, :] * w[j:j + 1, :]
    return acc


def _conv_mixer_kernel(ain_ref, ab_ref, ac_ref, w_ref, b_ref, o_ref, halo_ref):
    @pl.when(pl.program_id(2) == 0)
    def _():
        halo_ref[...] = jnp.zeros_like(halo_ref)

    v = ac_ref[0].astype(F32) * ain_ref[0].astype(F32)
    ext = jnp.concatenate([halo_ref[...], v], axis=0)
    y = _causal_conv(ext, v, w_ref[...], CONV_K) + b_ref[...]
    o_ref[0] = (ab_ref[0].astype(F32) * y).astype(o_ref.dtype)
    halo_ref[...] = v[v.shape[0] - SUBLANES:, :]


def conv_mixer(z3, conv_w, conv_b, ts=512, cw=512):
    b, s, _ = z3.shape
    nc = BRANCH_W // cw
    return pl.pallas_call(
        _conv_mixer_kernel,
        grid=(b, nc, s // ts),
        in_specs=[pl.BlockSpec((1, ts, cw), lambda bi, c, si: (bi, si, c)),
                  pl.BlockSpec((1, ts, cw), lambda bi, c, si: (bi, si, c + nc)),
                  pl.BlockSpec((1, ts, cw), lambda bi, c, si: (bi, si, c + 2 * nc)),
                  pl.BlockSpec((CONV_K, cw), lambda bi, c, si: (0, c)),
                  pl.BlockSpec((1, cw), lambda bi, c, si: (0, c))],
        out_specs=pl.BlockSpec((1, ts, cw), lambda bi, c, si: (bi, si, c)),
        out_shape=jax.ShapeDtypeStruct((b, s, BRANCH_W), BF16),
        scratch_shapes=[pltpu.VMEM((SUBLANES, cw), F32)],
        compiler_params=_cparams(("parallel", "parallel", "arbitrary")),
        name="conv_mixer",
    )(z3, z3, z3, conv_w, conv_b.reshape(1, BRANCH_W))


def _lru_kernel(gate_ref, rin_ref, cw_ref, cb_ref, wa_ref, ba_ref, wx_ref, bx_ref, lam_ref, o_ref,
                halo_ref, carry_ref, a_ref, b_ref, h_ref):
    ts, lw = a_ref.shape

    @pl.when(pl.program_id(2) == 0)
    def _():
        halo_ref[...] = jnp.zeros_like(halo_ref)
        carry_ref[...] = jnp.zeros_like(carry_ref)

    u = rin_ref[0].astype(F32)
    ext = jnp.concatenate([halo_ref[...], u], axis=0)
    xc = _causal_conv(ext, u, cw_ref[...], LRU_CONV_K) + cb_ref[...]
    halo_ref[...] = u[ts - SUBLANES:, :]

    xcb = xc.astype(BF16)
    ra, ix = [], []
    for hh in range(lw // LRU_HD):
        xh = xcb[:, hh * LRU_HD:(hh + 1) * LRU_HD]
        ra.append(jnp.dot(xh, wa_ref[hh].astype(BF16), preferred_element_type=F32))
        ix.append(jnp.dot(xh, wx_ref[hh].astype(BF16), preferred_element_type=F32))
    r = jax.nn.sigmoid(jnp.concatenate(ra, axis=1) + ba_ref[...])
    i = jax.nn.sigmoid(jnp.concatenate(ix, axis=1) + bx_ref[...])
    neg_lam = -lam_ref[...]
    softplus = jnp.maximum(neg_lam, 0.0) + jnp.log1p(jnp.exp(-jnp.abs(neg_lam)))
    log_a = -LRU_C * r * softplus
    a_ref[...] = jnp.exp(log_a)
    th = jnp.tanh(log_a)
    b_ref[...] = jnp.sqrt(-2.0 * th / (1.0 - th)) * (i * xc)

    row = lax.broadcasted_iota(jnp.int32, (SUBLANES, lw), 0)

    def group(g, hc):
        sl = pl.ds(pl.multiple_of(g * SUBLANES, SUBLANES), SUBLANES)
        a = a_ref[sl, :]
        bv = b_ref[sl, :]
        for d in (1, 2, 4):
            keep = row >= d
            bv = jnp.where(keep, a * pltpu.roll(bv, d, 0) + bv, bv)
            a = jnp.where(keep, a * pltpu.roll(a, d, 0), a)
        h = a * hc + bv
        h_ref[sl, :] = h
        return jnp.broadcast_to(h[SUBLANES - 1:SUBLANES, :], (SUBLANES, lw))

    carry_ref[...] = lax.fori_loop(0, ts // SUBLANES, group, carry_ref[...])
    o_ref[0] = (h_ref[...] * jax.nn.gelu(gate_ref[0].astype(F32))).astype(o_ref.dtype)


def lru_mixer(z3, col0, conv_w, conv_b, wa, ba, wx, bx, lam, ts=512, lw=256):
    b, s, _ = z3.shape
    nc = BRANCH_W // lw
    c0 = col0 // lw
    hpb = lw // LRU_HD
    row = lambda v: v.reshape(1, BRANCH_W)
    vec = pl.BlockSpec((1, lw), lambda bi, c, si: (0, c))
    mat = pl.BlockSpec((hpb, LRU_HD, LRU_HD), lambda bi, c, si: (c, 0, 0))
    return pl.pallas_call(
        _lru_kernel,
        grid=(b, nc, s // ts),
        in_specs=[pl.BlockSpec((1, ts, lw), lambda bi, c, si: (bi, si, c0 + c)),
                  pl.BlockSpec((1, ts, lw), lambda bi, c, si: (bi, si, c0 + nc + c)),
                  pl.BlockSpec((LRU_CONV_K, lw), lambda bi, c, si: (0, c)),
                  vec, mat, vec, mat, vec, vec],
        out_specs=pl.BlockSpec((1, ts, lw), lambda bi, c, si: (bi, si, c)),
        out_shape=jax.ShapeDtypeStruct((b, s, BRANCH_W), BF16),
        scratch_shapes=[pltpu.VMEM((SUBLANES, lw), F32), pltpu.VMEM((SUBLANES, lw), F32),
                        pltpu.VMEM((ts, lw), F32), pltpu.VMEM((ts, lw), F32), pltpu.VMEM((ts, lw), F32)],
        compiler_params=_cparams(("parallel", "parallel", "arbitrary")),
        name="rg_lru",
    )(z3, z3, conv_w, row(conv_b), wa, row(ba), wx, row(bx), row(lam))


def _compress_kernel(slots_ref, pos_ref, w1_ref, b1_ref, w2_ref, o_ref):
    g, nslot, half = slots_ref.shape[2:]
    rows = g * nslot
    x = slots_ref[0, 0].reshape(rows, half)
    w1 = w1_ref[0].astype(BF16)
    first = jnp.dot(x, w1[:half], preferred_element_type=F32)
    second = jnp.dot(x, w1[half:], preferred_element_type=F32)
    pos = jnp.broadcast_to(pos_ref[0], (SUBLANES, 2 * half)).astype(BF16)
    posb = jnp.dot(pos, w1, preferred_element_type=F32)[0:1, :]
    hidden = first + pltpu.roll(second, rows - 1, 0) + posb + b1_ref[0]
    out = jnp.dot(jax.nn.gelu(hidden).astype(BF16), w2_ref[0].astype(BF16), preferred_element_type=F32)
    o_ref[0, 0] = out.reshape(g, nslot, NSA_HD).astype(o_ref.dtype)


def nsa_compress(slots, cmp_pos, w1, b1, w2):
    _, b, g, nslot, half = slots.shape
    pos = cmp_pos.reshape(2, 1, CMP_BLK * NSA_HD)
    return pl.pallas_call(
        _compress_kernel,
        grid=(2, b),
        in_specs=[pl.BlockSpec((1, 1, g, nslot, half), lambda kv, bi: (kv, bi, 0, 0, 0)),
                  pl.BlockSpec((1, 1, 2 * half), lambda kv, bi: (kv, 0, 0)),
                  pl.BlockSpec((1, 2 * half, CMP_HIDDEN), lambda kv, bi: (kv, 0, 0)),
                  pl.BlockSpec((1, 1, CMP_HIDDEN), lambda kv, bi: (kv, 0, 0)),
                  pl.BlockSpec((1, CMP_HIDDEN, NSA_HD), lambda kv, bi: (kv, 0, 0))],
        out_specs=pl.BlockSpec((1, 1, g, nslot, NSA_HD), lambda kv, bi: (kv, bi, 0, 0, 0)),
        out_shape=jax.ShapeDtypeStruct((2, b, g, nslot, NSA_HD), BF16),
        compiler_params=_cparams(("parallel", "parallel")),
        name="nsa_compress",
    )(slots, pos, w1, b1.reshape(2, 1, CMP_HIDDEN), w2)


def _exp2_bf16(s, m):
    rows = s.shape[0]
    chunk = min(EXP_CHUNK, rows)
    return jnp.concatenate([jnp.exp2((s[r:r + chunk, :] - m).astype(BF16)) for r in range(0, rows, chunk)], axis=0)


def _nsa_kernel(q_ref, qf_ref, kc_ref, vc_ref, ks_ref, vs_ref, kw_ref, vw_ref, gl_ref, c2s_ref, bias_ref, o_ref,
                qp_ref, imp_ref, s0_ref, s1_ref, mx_ref, m_ref, acc_ref, *, n_cmp):
    tq = q_ref.shape[1]
    lanes = NSA_GRP * tq
    seq = ks_ref.shape[2]
    n_slc = seq // SLC_BLK
    t0 = pl.multiple_of(pl.program_id(2) * tq, tq)
    head = lambda r: slice(r * tq, (r + 1) * tq)

    def scores(k_tile, bias=None):
        s = jnp.dot(k_tile, qp_ref[...], preferred_element_type=F32)
        return s if bias is None else s + bias

    qt = q_ref[0].astype(F32).T * (NSA_HD ** -0.5 * LOG2E)
    qp_ref[...] = qf_ref[0]
    for r in range(NSA_GRP):
        qp_ref[0:NSA_HD, head(r)] = qt[r * NSA_HD:(r + 1) * NSA_HD, :].astype(BF16)

    s_cmp = scores(kc_ref[0, 0])
    win = []
    for off, size, bias in ((0, tq, bias_ref[1]), (tq, WINDOW - tq, None), (WINDOW, tq, bias_ref[0])):
        k0 = pl.multiple_of(t0 + off, tq)
        win.append((scores(kw_ref[0, 0, pl.ds(k0, size), :], bias), vw_ref[0, 0, :, pl.ds(k0, size)]))

    ncp = kc_ref.shape[2]
    n_row = lax.broadcasted_iota(jnp.int32, (ncp, lanes), 0)
    t_lane = t0 + lax.broadcasted_iota(jnp.int32, (ncp, lanes), 1) % tq
    ok = (t_lane >= n_row * CMP_STRIDE + (CMP_BLK - 1)) & (n_row < n_cmp)
    s = jnp.where(ok, s_cmp, NEG)
    p = jnp.where(ok, jnp.exp2(s - jnp.max(s, axis=0, keepdims=True)), 0.0)
    l = jnp.sum(p, axis=0, keepdims=True)
    p_cmp = p * (1.0 / jnp.where(l > 0.0, l, 1.0))
    o_cmp = jnp.dot(vc_ref[0, 0], p_cmp.astype(BF16), preferred_element_type=F32)

    rest = p_cmp[:, head(0)]
    for r in range(1, NSA_GRP):
        rest = rest + p_cmp[:, head(r)]
    imp = jnp.zeros((n_slc, tq), F32)
    for _ in range(3):
        part = rest.astype(BF16)
        rest = rest - part.astype(F32)
        imp = imp + jnp.dot(c2s_ref[...], part, preferred_element_type=F32)
    j_row = lax.broadcasted_iota(jnp.int32, (n_slc, tq), 0)
    t_col = t0 + lax.broadcasted_iota(jnp.int32, (n_slc, tq), 1)
    forced = (j_row == 0) | (j_row == t_col // SLC_BLK)
    causal = j_row * SLC_BLK <= t_col
    imp = jnp.where(forced, FORCE, jnp.where(causal, imp, -FORCE))
    imp_ref[...] = imp

    n_grp = n_slc // SUBLANES
    vals = [imp[v * SUBLANES:(v + 1) * SUBLANES, :] for v in range(n_grp)]
    sub = lax.broadcasted_iota(jnp.int32, (SUBLANES, tq), 0)
    rank = [jnp.zeros((SUBLANES, tq), jnp.int32) for _ in range(n_grp)]
    for i in range(n_slc):
        bi = jnp.broadcast_to(imp_ref[i:i + 1, :], (SUBLANES, tq))
        for v in range(n_grp):
            if v < i // SUBLANES:
                ahead = bi > vals[v]
            elif v > i // SUBLANES:
                ahead = bi >= vals[v]
            else:
                ahead = (bi > vals[v]) | ((bi == vals[v]) & (sub > i % SUBLANES))
            rank[v] = rank[v] + jnp.where(ahead, 1, 0)
    chosen = (jnp.concatenate(rank, axis=0) < min(SLC_TOPN, n_slc)) & causal
    mask = jnp.where(chosen, 0.0, NEG).astype(BF16)
    for r in range(NSA_GRP):
        qp_ref[MASK_OFF:MASK_OFF + n_slc, head(r)] = mask

    tk = min(ATT_TK, seq)
    n_full = t0 // tk
    last = n_full // 2

    s_bufs = (s0_ref, s1_ref)

    def pair_scores(step, buf):
        for par in range(2):
            k0 = pl.multiple_of((2 * step + par) * tk, tk)
            s = scores(ks_ref[0, 0, pl.ds(k0, tk), :])
            s_bufs[buf][par] = s
            mx_ref[buf, par] = jnp.max(s, axis=0, keepdims=True)

    def pair_update(step, buf, prefetch):
        m_new = [jnp.maximum(m_ref[par], mx_ref[buf, par]) for par in range(2)]
        probs = [_exp2_bf16(s_bufs[buf].at[par], m_new[par]) for par in range(2)]
        if prefetch:
            pair_scores(step + 1, 1 - buf)
        for par in range(2):
            k0 = pl.multiple_of((2 * step + par) * tk, tk)
            pv = jnp.dot(vs_ref[0, 0, :, pl.ds(k0, tk)], probs[par], preferred_element_type=F32)
            acc_ref[par] = jnp.exp2(m_ref[par] - m_new[par]) * acc_ref[par] + pv
            m_ref[par] = m_new[par]

    def mask_diagonal(buf):
        off = pl.multiple_of(t0 - n_full * tk, tq)
        for par in range(2):
            @pl.when(n_full % 2 == par)
            def _():
                tile = s_bufs[buf].at[par]
                tile[pl.ds(off, tq), :] = tile[pl.ds(off, tq), :] + bias_ref[0]
                mx_ref[buf, par] = jnp.max(tile[...], axis=0, keepdims=True)

    def half_trip(step, buf):
        @pl.when(step == last)
        def _():
            mask_diagonal(buf)

        @pl.when(step < last)
        def _():
            pair_update(step, buf, prefetch=True)

        @pl.when(step == last)
        def _():
            pair_update(step, buf, prefetch=False)

    def trip(i, carry):
        half_trip(2 * i, 0)
        half_trip(2 * i + 1, 1)
        return carry

    m_ref[...] = jnp.full(m_ref.shape, NEG, F32)
    acc_ref[...] = jnp.zeros(acc_ref.shape, F32)
    pair_scores(0, 0)

    mx = functools.reduce(jnp.maximum, [jnp.max(s, axis=0, keepdims=True) for s, _ in win])
    o_win = jnp.zeros((V_ROWS, lanes), F32)
    for s, v_tile in win:
        o_win = o_win + jnp.dot(v_tile, _exp2_bf16(s, mx), preferred_element_type=F32)
    o_win = o_win[0:NSA_HD] * (1.0 / o_win[NSA_HD:NSA_HD + 1])

    lax.fori_loop(0, last // 2 + 1, trip, 0)
    m_all = jnp.maximum(m_ref[0], m_ref[1])
    o_slc = jnp.exp2(m_ref[0] - m_all) * acc_ref[0] + jnp.exp2(m_ref[1] - m_all) * acc_ref[1]
    o_slc = o_slc[0:NSA_HD] * (1.0 / o_slc[NSA_HD:NSA_HD + 1])

    gates = jax.nn.sigmoid(gl_ref[0, 0].astype(F32))
    outs = []
    for r in range(NSA_GRP):
        outs.append(gates[3 * r:3 * r + 1] * o_cmp[:, head(r)] + gates[3 * r + 1:3 * r + 2] * o_slc[:, head(r)]
                    + gates[3 * r + 2:3 * r + 3] * o_win[:, head(r)])
    o_ref[0] = jnp.concatenate(outs, axis=0).T.astype(o_ref.dtype)


def _alibi_slopes():
    return (2.0 ** (-8.0 * np.arange(1, NSA_HEADS + 1) / NSA_HEADS)).astype(np.float32)


def _bf16_split3(v):
    parts, rest = [], np.asarray(v, np.float32)
    for _ in range(3):
        part = rest.astype(ml_dtypes.bfloat16).astype(np.float32)
        parts.append(part)
        rest = (rest.astype(np.float64) - part.astype(np.float64)).astype(np.float32)
    return parts


def _query_features(tq):
    feat = np.zeros((NSA_HEADS, FEAT), np.float32)
    hi, mid, lo = _bf16_split3(_alibi_slopes() * np.float32(LOG2E))
    for c, part in enumerate((hi, mid, lo, hi, mid, lo)):
        feat[:, ALIBI_OFF + c] = part
    feat[:, PAD_OFF] = NEG
    feat = np.repeat(feat.reshape(NSA_KV_HEADS, NSA_GRP, FEAT), tq, axis=1)
    return jnp.asarray(feat.transpose(0, 2, 1), BF16)


def _key_features(pos, n_slc):
    pos = np.asarray(pos)
    real = pos >= 0
    feat = np.zeros((pos.shape[0], FEAT - NSA_HD), np.float32)
    if n_slc:
        feat[real, pos[real] // SLC_BLK] = 1.0
    a0 = ALIBI_OFF - NSA_HD
    feat[real, a0:a0 + 3] = ((pos[real] // SLC_BLK) * SLC_BLK)[:, None]
    feat[real, a0 + 3:a0 + 6] = (pos[real] % SLC_BLK)[:, None]
    feat[~real, PAD_OFF - NSA_HD] = 1.0
    return feat


def _values_t(v):
    b, g, n, _ = v.shape
    extra = jnp.zeros((b, g, V_ROWS - NSA_HD, n), v.dtype).at[:, :, 0].set(1.0)
    return jnp.concatenate([jnp.swapaxes(v, 2, 3), extra], axis=2)


def _with_features(k, feat):
    b, g, _, _ = k.shape
    f = jnp.broadcast_to(jnp.asarray(feat, BF16), (b, g) + feat.shape)
    return jnp.concatenate([k.astype(BF16), f], axis=3)


def nsa_attention(z3, qcol0, kc, vc, ks, vs, kw, vw, gate_logits):
    b, s, _ = z3.shape
    tq = ATT_TQ
    tk = min(ATT_TK, s)
    assert s % (2 * tk) == 0 and tk % tq == 0 and tq % SLC_BLK == 0 and WINDOW % tq == 0 and WINDOW > tq
    n_slc = s // SLC_BLK
    assert n_slc <= ALIBI_OFF - MASK_OFF and n_slc % SUBLANES == 0
    n_cmp = (s - CMP_BLK) // CMP_STRIDE + 1
    ncp = kc.shape[2]
    gw = NSA_GRP * NSA_HD
    lanes = NSA_GRP * tq

    kc_f = _with_features(kc, _key_features(np.arange(ncp) * CMP_STRIDE + CMP_BLK - 1, 0))
    ks_f = _with_features(ks, _key_features(np.arange(s), n_slc))
    front = ((0, 0), (0, 0), (WINDOW, 0), (0, 0))
    kw_f = _with_features(jnp.pad(kw, front), _key_features(np.arange(-WINDOW, s), 0))
    t = lambda v: jnp.swapaxes(v, 2, 3)
    vw_t = _values_t(jnp.pad(vw, front))

    c_start = np.arange(ncp) * CMP_STRIDE
    s_start = np.arange(n_slc) * SLC_BLK
    overlap = np.clip(np.minimum(c_start[:, None] + CMP_BLK, s_start[None, :] + SLC_BLK)
                      - np.maximum(c_start[:, None], s_start[None, :]), 0, None) / CMP_BLK
    overlap[n_cmp:] = 0.0
    c2s_t = jnp.asarray(overlap.T, BF16)

    key = np.arange(tq)[:, None]
    qry = np.tile(np.arange(tq), NSA_GRP)[None, :]
    bias = jnp.asarray(np.stack([np.where(key <= qry, 0.0, NEG), np.where(key > qry, 0.0, NEG)]), F32)

    blk = lambda *shape: pl.BlockSpec((1, 1) + shape, lambda bi, g, qi: (bi, g, 0, 0))
    const = lambda *shape: pl.BlockSpec(shape, lambda bi, g, qi: (0,) * len(shape))
    return pl.pallas_call(
        functools.partial(_nsa_kernel, n_cmp=n_cmp),
        grid=(b, NSA_KV_HEADS, s // tq),
        in_specs=[pl.BlockSpec((1, tq, gw), lambda bi, g, qi: (bi, qi, qcol0 // gw + g)),
                  pl.BlockSpec((1, FEAT, lanes), lambda bi, g, qi: (g, 0, 0)),
                  blk(ncp, FEAT), blk(NSA_HD, ncp),
                  blk(s, FEAT), blk(V_ROWS, s),
                  blk(s + WINDOW, FEAT), blk(V_ROWS, s + WINDOW),
                  pl.BlockSpec((1, 1, NSA_GRP * 3, tq), lambda bi, g, qi: (bi, g, 0, qi)),
                  const(n_slc, ncp), const(2, tq, lanes)],
        out_specs=pl.BlockSpec((1, tq, gw), lambda bi, g, qi: (bi, qi, g)),
        out_shape=jax.ShapeDtypeStruct((b, s, NSA_W), BF16),
        scratch_shapes=[pltpu.VMEM((FEAT, lanes), BF16), pltpu.VMEM((n_slc, tq), F32),
                        pltpu.VMEM((2, tk, lanes), F32), pltpu.VMEM((2, tk, lanes), F32),
                        pltpu.VMEM((2, 2, 1, lanes), F32),
                        pltpu.VMEM((2, 1, lanes), F32), pltpu.VMEM((2, V_ROWS, lanes), F32)],
        compiler_params=_cparams(("parallel", "parallel", "arbitrary")),
        name="nsa_attention",
    )(z3, _query_features(tq), kc_f, t(vc), ks_f, _values_t(vs), kw_f, vw_t, gate_logits, c2s_t, bias)


def _merge_kernel(ya_ref, yb_ref, yc_ref, wb_ref, ga_ref, gb_ref, gc_ref, o_ref):
    acc = None
    for i, (y_ref, g_ref) in enumerate(((ya_ref, ga_ref), (yb_ref, gb_ref), (yc_ref, gc_ref))):
        term = jax.nn.sigmoid(g_ref[...].astype(F32)) * jnp.dot(y_ref[...], wb_ref[i], preferred_element_type=F32)
        acc = term if acc is None else acc + term
    o_ref[...] = acc.astype(o_ref.dtype)


def branch_merge(ya, yb, yc, wb, zmerge, tm=1024, tn=512):
    t = ya.shape[0]
    d = wb.shape[2]
    nj = d // tn
    y_spec = pl.BlockSpec((tm, BRANCH_W), lambda i, j: (i, 0))
    gate = lambda br: pl.BlockSpec((tm, tn), lambda i, j: (i, j + br * nj))
    return pl.pallas_call(
        _merge_kernel,
        grid=(t // tm, nj),
        in_specs=[y_spec, y_spec, y_spec,
                  pl.BlockSpec((3, BRANCH_W, tn), lambda i, j: (0, 0, j)),
                  gate(0), gate(1), gate(2)],
        out_specs=pl.BlockSpec((tm, tn), lambda i, j: (i, j)),
        out_shape=jax.ShapeDtypeStruct((t, d), BF16),
        compiler_params=_cparams(("parallel", "parallel")),
        name="branch_merge",
    )(ya, yb, yc, wb, zmerge, zmerge, zmerge)


def _layer_norm(y, g, b):
    mu = jnp.mean(y, axis=-1, keepdims=True)
    yc = y - mu
    var = jnp.mean(yc * yc, axis=-1, keepdims=True)
    return yc * lax.rsqrt(var + LN_EPS) * g + b


def _outproj_ln_kernel(m_ref, w_ref, x_ref, g_ref, b_ref, o_ref, ob_ref, *, alpha):
    mix = jnp.dot(m_ref[...], w_ref[...], preferred_element_type=F32)
    y = _layer_norm(alpha * x_ref[...] + mix, g_ref[...], b_ref[...])
    o_ref[...] = y
    ob_ref[...] = y.astype(BF16)


def outproj_ln(merged, w_out, x, g, b, alpha, tm=256):
    t, d = x.shape
    row = pl.BlockSpec((tm, d), lambda i: (i, 0))
    vec = pl.BlockSpec((1, d), lambda i: (0, 0))
    return pl.pallas_call(
        functools.partial(_outproj_ln_kernel, alpha=alpha),
        grid=(t // tm,),
        in_specs=[row, pl.BlockSpec((d, d), lambda i: (0, 0)), row, vec, vec],
        out_specs=[row, row],
        out_shape=[jax.ShapeDtypeStruct((t, d), F32), jax.ShapeDtypeStruct((t, d), BF16)],
        compiler_params=_cparams(("parallel",)),
        name="outproj_ln",
    )(merged, w_out, x, g.reshape(1, d), b.reshape(1, d))


def _final_kernel(x_ref, xb_ref, wg_ref, p_ref, wp_ref, y0_ref, y1_ref, gate_ref, g_ref, b_ref, o_ref, ob_ref,
                  *, alpha):
    ple = jax.nn.sigmoid(jnp.dot(xb_ref[...], wg_ref[...], preferred_element_type=F32)) * jnp.dot(
        p_ref[...], wp_ref[...], preferred_element_type=F32)
    gate = gate_ref[...]
    ffn = gate[:, 0:1] * y0_ref[...].astype(F32) + gate[:, 1:2] * y1_ref[...].astype(F32)
    y = _layer_norm(alpha * x_ref[...] + ffn + ple, g_ref[...], b_ref[...])
    o_ref[...] = y
    ob_ref[...] = y.astype(BF16)


def ple_combine_ln(x, xb, w_ple_gate, p, w_ple, y0, y1, gate, g, b, alpha, tm=256):
    t, d = x.shape
    pd = p.shape[1]
    row = pl.BlockSpec((tm, d), lambda i: (i, 0))
    vec = pl.BlockSpec((1, d), lambda i: (0, 0))
    return pl.pallas_call(
        functools.partial(_final_kernel, alpha=alpha),
        grid=(t // tm,),
        in_specs=[row, row, pl.BlockSpec((d, d), lambda i: (0, 0)),
                  pl.BlockSpec((tm, pd), lambda i: (i, 0)), pl.BlockSpec((pd, d), lambda i: (0, 0)),
                  row, row, pl.BlockSpec((tm, TOP_K), lambda i: (i, 0)), vec, vec],
        out_specs=[row, row],
        out_shape=[jax.ShapeDtypeStruct((t, d), F32), jax.ShapeDtypeStruct((t, d), BF16)],
        compiler_params=_cparams(("parallel",)),
        name="ple_combine_ln",
    )(x, xb, w_ple_gate, p, w_ple, y0, y1, gate, g.reshape(1, d), b.reshape(1, d))


def _router_kernel(x_ref, wr_ref, br_ref, idx_ref, gate_ref):
    tm = x_ref.shape[0]
    logits = lax.dot_general(wr_ref[...], x_ref[...], (((1,), (1,)), ((), ())),
                             precision=lax.Precision.HIGHEST, preferred_element_type=F32)
    aff = jax.nn.sigmoid(logits)
    sel = aff + br_ref[...]
    scores = []
    for g in range(N_GROUPS):
        v = [sel[g * EXPERTS_PER_GROUP + e:g * EXPERTS_PER_GROUP + e + 1, :] for e in range(EXPERTS_PER_GROUP)]
        best = None
        for a in range(EXPERTS_PER_GROUP):
            for c in range(a + 1, EXPERTS_PER_GROUP):
                pair = v[a] + v[c]
                best = pair if best is None else jnp.maximum(best, pair)
        scores.append(best)
    best_score, best_group = scores[0], jnp.zeros((1, tm), jnp.int32)
    for g in range(1, N_GROUPS):
        better = scores[g] > best_score
        best_score = jnp.where(better, scores[g], best_score)
        best_group = jnp.where(better, g, best_group)
    e_row = lax.broadcasted_iota(jnp.int32, (N_EXPERTS, tm), 0)
    cand = jnp.where(e_row // EXPERTS_PER_GROUP == best_group, sel, NEG)
    picks, gates = [], []
    for _ in range(TOP_K):
        mx = jnp.max(cand, axis=0, keepdims=True)
        pick = jnp.min(jnp.where(cand == mx, e_row, N_EXPERTS), axis=0, keepdims=True)
        hit = e_row == pick
        picks.append(pick)
        gates.append(jnp.sum(jnp.where(hit, aff, 0.0), axis=0, keepdims=True))
        cand = jnp.where(hit, -jnp.inf, cand)
    total = gates[0] + gates[1]
    idx_ref[...] = jnp.concatenate(picks, axis=0)
    gate_ref[...] = jnp.concatenate([gates[0] / total, gates[1] / total], axis=0)


def router(x, w_router, b_router, tm=1024):
    t, d = x.shape
    return pl.pallas_call(
        _router_kernel,
        grid=(t // tm,),
        in_specs=[pl.BlockSpec((tm, d), lambda i: (i, 0)),
                  pl.BlockSpec((N_EXPERTS, d), lambda i: (0, 0)),
                  pl.BlockSpec((N_EXPERTS, 1), lambda i: (0, 0))],
        out_specs=[pl.BlockSpec((TOP_K, tm), lambda i: (0, i)), pl.BlockSpec((TOP_K, tm), lambda i: (0, i))],
        out_shape=[jax.ShapeDtypeStruct((TOP_K, t), jnp.int32), jax.ShapeDtypeStruct((TOP_K, t), F32)],
        compiler_params=_cparams(("parallel",)),
        name="router",
    )(x, w_router.T, b_router.reshape(N_EXPERTS, 1))


def _expert_kernel(be_ref, nu_ref, x_ref, wg_ref, wu_ref, wd_ref, *rest):
    prev_ref = rest[0] if len(rest) == 5 else None
    o_ref, wgb_ref, wub_ref, wdb_ref = rest[-4:]
    i = pl.program_id(0)

    @pl.when((i == 0) | (be_ref[i] != be_ref[jnp.maximum(i - 1, 0)]))
    def _():
        wgb_ref[...] = wg_ref[0, 0].astype(BF16)
        wub_ref[...] = wu_ref[0, 0].astype(BF16)
        wdb_ref[...] = wd_ref[0, 0].astype(BF16)

    @pl.when(i < nu_ref[0])
    def _():
        x = x_ref[...]
        g = jnp.dot(x, wgb_ref[...], preferred_element_type=F32)
        u = jnp.dot(x, wub_ref[...], preferred_element_type=F32)
        out = jnp.dot((jax.nn.silu(g) * u).astype(BF16), wdb_ref[...], preferred_element_type=F32)
        if prev_ref is not None:
            out = out + prev_ref[...].astype(F32)
        o_ref[...] = out.astype(o_ref.dtype)

    @pl.when(i >= nu_ref[0])
    def _():
        o_ref[...] = jnp.zeros_like(o_ref)


def expert_ffn(block_expert, n_used, x_sorted, w_gate_up, w_down, layer):
    n_rows, d = x_sorted.shape
    de = w_down.shape[2]
    ce = de // EXPERT_CHUNKS
    out = None
    for c in range(EXPERT_CHUNKS):
        last = c == EXPERT_CHUNKS - 1
        row = pl.BlockSpec((MOE_BLK, d), lambda i, be, nu: (i, 0))
        in_specs = [row,
                    pl.BlockSpec((1, 1, d, ce), lambda i, be, nu, c=c: (layer, be[i], 0, c)),
                    pl.BlockSpec((1, 1, d, ce), lambda i, be, nu, c=c: (layer, be[i], 0, EXPERT_CHUNKS + c)),
                    pl.BlockSpec((1, 1, ce, d), lambda i, be, nu, c=c: (layer, be[i], c, 0))]
        args = [x_sorted, w_gate_up, w_gate_up, w_down]
        if out is not None:
            in_specs.append(row)
            args.append(out)
        out = pl.pallas_call(
            _expert_kernel,
            grid_spec=pltpu.PrefetchScalarGridSpec(
                num_scalar_prefetch=2,
                grid=(n_rows // MOE_BLK,),
                in_specs=in_specs,
                out_specs=row,
                scratch_shapes=[pltpu.VMEM((d, ce), BF16), pltpu.VMEM((d, ce), BF16), pltpu.VMEM((ce, d), BF16)]),
            out_shape=jax.ShapeDtypeStruct((n_rows, d), BF16),
            compiler_params=_cparams(("arbitrary",)),
            name="expert_ffn",
        )(block_expert, n_used, *args)
    return out


def _dispatch_tables(expert_idx):
    n_tok = expert_idx.shape[0]
    n_assign = n_tok * TOP_K
    e_flat = expert_idx.reshape(n_assign)
    onehot = (e_flat[:, None] == jnp.arange(N_EXPERTS)[None, :]).astype(jnp.int32)
    csum = jnp.cumsum(onehot, axis=0)
    rank = jnp.take_along_axis(csum, e_flat[:, None], axis=1)[:, 0] - 1
    counts = csum[-1]
    padded = (counts + MOE_BLK - 1) // MOE_BLK * MOE_BLK
    pad_end = jnp.cumsum(padded)
    dest = (pad_end - padded)[e_flat] + rank
    n_rows = n_assign + N_EXPERTS * MOE_BLK
    row_tok = jnp.zeros((n_rows,), jnp.int32).at[dest].set(jnp.arange(n_assign, dtype=jnp.int32) // TOP_K)
    block_expert = jnp.minimum(
        jnp.searchsorted(pad_end, jnp.arange(n_rows // MOE_BLK) * MOE_BLK, side="right"), N_EXPERTS - 1)
    n_used = (pad_end[-1:] // MOE_BLK).astype(jnp.int32)
    return row_tok, block_expert.astype(jnp.int32), n_used, dest.reshape(n_tok, TOP_K)


def _token_mixer(xb, bsz, seq, w_in_t, layer, conv_a_w, conv_a_b, lru_conv_w, lru_conv_b, lru_wa, lru_ba, lru_wx, lru_bx,
                 lru_lam, cmp_pos, phi_w1, phi_b1, phi_w2, w_branch):
    d = xb.shape[1]
    n_tok = bsz * seq
    main_w = 3 * BRANCH_W + 2 * BRANCH_W + NSA_W + 6 * KV_W
    ng = 3 * NSA_HEADS
    kv0 = main_w - 6 * KV_W
    zmain = matmul_wres(xb, w_in_t, layer, 0, main_w, BF16, 2048, 768)
    zgate = matmul_wres(xb, w_in_t, layer, main_w, LANES, F32, 2048, LANES)
    zmerge = matmul_wres(xb, w_in_t, layer, main_w + ng, w_in_t.shape[1] - main_w - ng, BF16, 2048, 512)
    z3 = zmain.reshape(bsz, seq, main_w)

    y_a = conv_mixer(z3, conv_a_w, conv_a_b)
    y_b = lru_mixer(z3, 3 * BRANCH_W, lru_conv_w, lru_conv_b, lru_wa, lru_ba, lru_wx, lru_bx, lru_lam)

    kv = z3[:, :, kv0:].reshape(bsz, seq, 6, NSA_KV_HEADS, NSA_HD).transpose(2, 0, 3, 1, 4)
    nslot = seq // CMP_STRIDE
    slots = kv[0:2].reshape(2, bsz, NSA_KV_HEADS, nslot, CMP_STRIDE * NSA_HD)
    cmp = nsa_compress(slots, cmp_pos, phi_w1, phi_b1, phi_w2)
    gate_logits = zgate[:, :ng].reshape(bsz, seq, NSA_KV_HEADS, NSA_GRP * 3).transpose(0, 2, 3, 1)
    y_c = nsa_attention(z3, 5 * BRANCH_W, cmp[0], cmp[1], kv[2], kv[3], kv[4], kv[5], gate_logits)

    flat = lambda y: y.reshape(n_tok, BRANCH_W)
    return branch_merge(flat(y_a), flat(y_b), flat(y_c), w_branch.astype(BF16), zmerge)


def _moe(x, xb, w_router, b_router, w_gate_up, w_down, layer):
    idx_t, gate_t = router(x, w_router, b_router)
    row_tok, block_expert, n_used, dest = _dispatch_tables(idx_t.T)
    n_tok = xb.shape[0]
    x_sorted = jnp.concatenate([xb[row_tok[a:a + n_tok]] for a in range(0, row_tok.shape[0], n_tok)], axis=0)
    rows = expert_ffn(block_expert, n_used, x_sorted, w_gate_up, w_down, layer)
    return rows[dest[:, 0]], rows[dest[:, 1]], gate_t.T


def kernel(x, p, w_in, conv_a_w, conv_a_b, lru_conv_w, lru_conv_b, lru_wa, lru_ba, lru_wx, lru_bx, lru_lam, cmp_pos, phi_w1, phi_b1, phi_w2, w_branch, w_out, ln_g, ln_b, w_router, b_router, w_gate_up, w_down, w_ple, w_ple_gate):
    bsz, seq, d = x.shape
    depth = w_in.shape[0]
    n_tok = bsz * seq
    alpha = (2 * depth) ** 0.25
    xf = x.reshape(n_tok, d)
    xb = xf.astype(BF16)
    w_in_t = jnp.swapaxes(w_in, 1, 2)
    for i in range(depth):
        merged = _token_mixer(xb, bsz, seq, w_in_t, i, conv_a_w[i], conv_a_b[i], lru_conv_w[i], lru_conv_b[i],
                              lru_wa[i], lru_ba[i], lru_wx[i], lru_bx[i], lru_lam[i], cmp_pos[i], phi_w1[i],
                              phi_b1[i], phi_w2[i], w_branch[i])
        xf, xb = outproj_ln(merged, w_out[i].astype(BF16), xf, ln_g[i, 0], ln_b[i, 0], alpha)
        y0, y1, gate = _moe(xf, xb, w_router, b_router, w_gate_up, w_down, i)
        xf, xb = ple_combine_ln(xf, xb, w_ple_gate[i].astype(BF16), p[i].reshape(n_tok, -1).astype(BF16),
                                w_ple[i].astype(BF16), y0, y1, gate, ln_g[i, 1], ln_b[i, 1], alpha)
    return xf.reshape(bsz, seq, d)
```

```python
import functools

import ml_dtypes
import numpy as np
import jax
import jax.numpy as jnp
from jax import lax
from jax.experimental import pallas as pl
from jax.experimental.pallas import tpu as pltpu

F32 = jnp.float32
BF16 = jnp.bfloat16

BRANCH_W = 1024
CONV_K = 3
LRU_HEADS = 8
LRU_HD = BRANCH_W // LRU_HEADS
LRU_CONV_K = 4
LRU_C = 8.0
NSA_HEADS = 16
NSA_KV_HEADS = 4
NSA_GRP = NSA_HEADS // NSA_KV_HEADS
NSA_HD = 64
NSA_W = NSA_HEADS * NSA_HD
KV_W = NSA_KV_HEADS * NSA_HD
CMP_BLK = 32
CMP_STRIDE = 16
CMP_HIDDEN = 128
SLC_BLK = 64
SLC_TOPN = 16
WINDOW = 512
N_EXPERTS = 16
N_GROUPS = 4
EXPERTS_PER_GROUP = N_EXPERTS // N_GROUPS
TOP_K = 2
LN_EPS = 1e-5
NEG = -1e30
FORCE = 1e9

LANES = 128
SUBLANES = 8
VMEM_LIMIT = 56 * 1024 * 1024

FEAT = 256
MASK_OFF = NSA_HD
ALIBI_OFF = 2 * NSA_HD
PAD_OFF = ALIBI_OFF + 6
V_ROWS = NSA_HD + 16
LOG2E = 1.4426950408889634

ATT_TQ = 256
ATT_TK = 512
EXP_CHUNK = 64
MOE_BLK = 256
EXPERT_CHUNKS = 2


def _cparams(sem):
    return pltpu.CompilerParams(dimension_semantics=sem, vmem_limit_bytes=VMEM_LIMIT)


def _mm_wres_kernel(x_ref, w_ref, *rest, shift):
    o_ref, wb_ref = rest[-2:]

    @pl.when(pl.program_id(1) == 0)
    def _():
        w = w_ref[0]
        if shift:
            tn = w.shape[0]
            w = jnp.concatenate([w, rest[0][0]], axis=0)[shift:shift + tn]
        wb_ref[...] = w.astype(BF16)

    o_ref[...] = lax.dot_general(x_ref[...], wb_ref[...], (((1,), (1,)), ((), ())),
                                 preferred_element_type=F32).astype(o_ref.dtype)


def matmul_wres(x, wt3, layer, row0, n, out_dtype, tm, tn):
    m, k = x.shape
    shift = row0 % tn
    base = row0 - shift
    assert m % tm == 0 and n % tn == 0 and shift % SUBLANES == 0 and shift <= LANES and tn % LANES == 0
    in_specs = [pl.BlockSpec((tm, k), lambda j, i: (i, 0)),
                pl.BlockSpec((1, tn, k), lambda j, i: (layer, base // tn + j, 0))]
    args = [x, wt3]
    if shift:
        in_specs.append(pl.BlockSpec((1, LANES, k), lambda j, i: (layer, (base + (j + 1) * tn) // LANES, 0)))
        args.append(wt3)
    return pl.pallas_call(
        functools.partial(_mm_wres_kernel, shift=shift),
        grid=(n // tn, m // tm),
        in_specs=in_specs,
        out_specs=pl.BlockSpec((tm, tn), lambda j, i: (i, j)),
        out_shape=jax.ShapeDtypeStruct((m, n), out_dtype),
        scratch_shapes=[pltpu.VMEM((tn, k), BF16)],
        compiler_params=_cparams(("parallel", "arbitrary")),
        name="matmul_wres",
    )(*args)


def _causal_conv(ext, cur, w, k):
    acc = cur * w[k - 1:k, :]
    for j in range(k - 1):
        shift = k - 1 - j
        acc = acc + pltpu.roll(ext, shift, 0)[SUBLANES:, :] * w[j:j + 1, :]
    return acc


def _conv_mixer_kernel(ain_ref, ab_ref, ac_ref, w_ref, b_ref, o_ref, halo_ref):
    @pl.when(pl.program_id(2) == 0)
    def _():
        halo_ref[...] = jnp.zeros_like(halo_ref)

    v = ac_ref[0].astype(F32) * ain_ref[0].astype(F32)
    ext = jnp.concatenate([halo_ref[...], v], axis=0)
    y = _causal_conv(ext, v, w_ref[...], CONV_K) + b_ref[...]
    o_ref[0] = (ab_ref[0].astype(F32) * y).astype(o_ref.dtype)
    halo_ref[...] = v[v.shape[0] - SUBLANES:, :]


def conv_mixer(z3, conv_w, conv_b, ts=512, cw=512):
    b, s, _ = z3.shape
    nc = BRANCH_W // cw
    return pl.pallas_call(
        _conv_mixer_kernel,
        grid=(b, nc, s // ts),
        in_specs=[pl.BlockSpec((1, ts, cw), lambda bi, c, si: (bi, si, c)),
                  pl.BlockSpec((1, ts, cw), lambda bi, c, si: (bi, si, c + nc)),
                  pl.BlockSpec((1, ts, cw), lambda bi, c, si: (bi, si, c + 2 * nc)),
                  pl.BlockSpec((CONV_K, cw), lambda bi, c, si: (0, c)),
                  pl.BlockSpec((1, cw), lambda bi, c, si: (0, c))],
        out_specs=pl.BlockSpec((1, ts, cw), lambda bi, c, si: (bi, si, c)),
        out_shape=jax.ShapeDtypeStruct((b, s, BRANCH_W), BF16),
        scratch_shapes=[pltpu.VMEM((SUBLANES, cw), F32)],
        compiler_params=_cparams(("parallel", "parallel", "arbitrary")),
        name="conv_mixer",
    )(z3, z3, z3, conv_w, conv_b.reshape(1, BRANCH_W))


def _lru_kernel(gate_ref, rin_ref, cw_ref, cb_ref, wa_ref, ba_ref, wx_ref, bx_ref, lam_ref, o_ref,
                halo_ref, carry_ref, a_ref, b_ref, h_ref):
    ts, lw = a_ref.shape

    @pl.when(pl.program_id(2) == 0)
    def _():
        halo_ref[...] = jnp.zeros_like(halo_ref)
        carry_ref[...] = jnp.zeros_like(carry_ref)

    u = rin_ref[0].astype(F32)
    ext = jnp.concatenate([halo_ref[...], u], axis=0)
    xc = _causal_conv(ext, u, cw_ref[...], LRU_CONV_K) + cb_ref[...]
    halo_ref[...] = u[ts - SUBLANES:, :]

    xcb = xc.astype(BF16)
    ra, ix = [], []
    for hh in range(lw // LRU_HD):
        xh = xcb[:, hh * LRU_HD:(hh + 1) * LRU_HD]
        ra.append(jnp.dot(xh, wa_ref[hh].astype(BF16), preferred_element_type=F32))
        ix.append(jnp.dot(xh, wx_ref[hh].astype(BF16), preferred_element_type=F32))
    r = jax.nn.sigmoid(jnp.concatenate(ra, axis=1) + ba_ref[...])
    i = jax.nn.sigmoid(jnp.concatenate(ix, axis=1) + bx_ref[...])
    neg_lam = -lam_ref[...]
    softplus = jnp.maximum(neg_lam, 0.0) + jnp.log1p(jnp.exp(-jnp.abs(neg_lam)))
    log_a = -LRU_C * r * softplus
    a_ref[...] = jnp.exp(log_a)
    th = jnp.tanh(log_a)
    b_ref[...] = jnp.sqrt(-2.0 * th / (1.0 - th)) * (i * xc)

    row = lax.broadcasted_iota(jnp.int32, (SUBLANES, lw), 0)

    def group(g, hc):
        sl = pl.ds(pl.multiple_of(g * SUBLANES, SUBLANES), SUBLANES)
        a = a_ref[sl, :]
        bv = b_ref[sl, :]
        for d in (1, 2, 4):
            keep = row >= d
            bv = jnp.where(keep, a * pltpu.roll(bv, d, 0) + bv, bv)
            a = jnp.where(keep, a * pltpu.roll(a, d, 0), a)
        h = a * hc + bv
        h_ref[sl, :] = h
        return jnp.broadcast_to(h[SUBLANES - 1:SUBLANES, :], (SUBLANES, lw))

    carry_ref[...] = lax.fori_loop(0, ts // SUBLANES, group, carry_ref[...])
    o_ref[0] = (h_ref[...] * jax.nn.gelu(gate_ref[0].astype(F32))).astype(o_ref.dtype)


def lru_mixer(z3, col0, conv_w, conv_b, wa, ba, wx, bx, lam, ts=512, lw=256):
    b, s, _ = z3.shape
    nc = BRANCH_W // lw
    c0 = col0 // lw
    hpb = lw // LRU_HD
    row = lambda v: v.reshape(1, BRANCH_W)
    vec = pl.BlockSpec((1, lw), lambda bi, c, si: (0, c))
    mat = pl.BlockSpec((hpb, LRU_HD, LRU_HD), lambda bi, c, si: (c, 0, 0))
    return pl.pallas_call(
        _lru_kernel,
        grid=(b, nc, s // ts),
        in_specs=[pl.BlockSpec((1, ts, lw), lambda bi, c, si: (bi, si, c0 + c)),
                  pl.BlockSpec((1, ts, lw), lambda bi, c, si: (bi, si, c0 + nc + c)),
                  pl.BlockSpec((LRU_CONV_K, lw), lambda bi, c, si: (0, c)),
                  vec, mat, vec, mat, vec, vec],
        out_specs=pl.BlockSpec((1, ts, lw), lambda bi, c, si: (bi, si, c)),
        out_shape=jax.ShapeDtypeStruct((b, s, BRANCH_W), BF16),
        scratch_shapes=[pltpu.VMEM((SUBLANES, lw), F32), pltpu.VMEM((SUBLANES, lw), F32),
                        pltpu.VMEM((ts, lw), F32), pltpu.VMEM((ts, lw), F32), pltpu.VMEM((ts, lw), F32)],
        compiler_params=_cparams(("parallel", "parallel", "arbitrary")),
        name="rg_lru",
    )(z3, z3, conv_w, row(conv_b), wa, row(ba), wx, row(bx), row(lam))


def _compress_kernel(slots_ref, pos_ref, w1_ref, b1_ref, w2_ref, o_ref):
    g, nslot, half = slots_ref.shape[2:]
    rows = g * nslot
    x = slots_ref[0, 0].reshape(rows, half)
    w1 = w1_ref[0].astype(BF16)
    first = jnp.dot(x, w1[:half], preferred_element_type=F32)
    second = jnp.dot(x, w1[half:], preferred_element_type=F32)
    pos = jnp.broadcast_to(pos_ref[0], (SUBLANES, 2 * half)).astype(BF16)
    posb = jnp.dot(pos, w1, preferred_element_type=F32)[0:1, :]
    hidden = first + pltpu.roll(second, rows - 1, 0) + posb + b1_ref[0]
    out = jnp.dot(jax.nn.gelu(hidden).astype(BF16), w2_ref[0].astype(BF16), preferred_element_type=F32)
    o_ref[0, 0] = out.reshape(g, nslot, NSA_HD).astype(o_ref.dtype)


def nsa_compress(slots, cmp_pos, w1, b1, w2):
    _, b, g, nslot, half = slots.shape
    pos = cmp_pos.reshape(2, 1, CMP_BLK * NSA_HD)
    return pl.pallas_call(
        _compress_kernel,
        grid=(2, b),
        in_specs=[pl.BlockSpec((1, 1, g, nslot, half), lambda kv, bi: (kv, bi, 0, 0, 0)),
                  pl.BlockSpec((1, 1, 2 * half), lambda kv, bi: (kv, 0, 0)),
                  pl.BlockSpec((1, 2 * half, CMP_HIDDEN), lambda kv, bi: (kv, 0, 0)),
                  pl.BlockSpec((1, 1, CMP_HIDDEN), lambda kv, bi: (kv, 0, 0)),
                  pl.BlockSpec((1, CMP_HIDDEN, NSA_HD), lambda kv, bi: (kv, 0, 0))],
        out_specs=pl.BlockSpec((1, 1, g, nslot, NSA_HD), lambda kv, bi: (kv, bi, 0, 0, 0)),
        out_shape=jax.ShapeDtypeStruct((2, b, g, nslot, NSA_HD), BF16),
        compiler_params=_cparams(("parallel", "parallel")),
        name="nsa_compress",
    )(slots, pos, w1, b1.reshape(2, 1, CMP_HIDDEN), w2)


def _exp2_bf16(s, m):
    rows = s.shape[0]
    chunk = min(EXP_CHUNK, rows)
    return jnp.concatenate([jnp.exp2((s[r:r + chunk, :] - m).astype(BF16)) for r in range(0, rows, chunk)], axis=0)


def _nsa_kernel(q_ref, qf_ref, kc_ref, vc_ref, ks_ref, vs_ref, kw_ref, vw_ref, gl_ref, c2s_ref, bias_ref, o_ref,
                qp_ref, imp_ref, s0_ref, s1_ref, mx_ref, m_ref, acc_ref, *, n_cmp):
    tq = q_ref.shape[1]
    lanes = NSA_GRP * tq
    seq = ks_ref.shape[2]
    n_slc = seq // SLC_BLK
    t0 = pl.multiple_of(pl.program_id(2) * tq, tq)
    head = lambda r: slice(r * tq, (r + 1) * tq)

    def scores(k_tile, bias=None):
        s = jnp.dot(k_tile, qp_ref[...], preferred_element_type=F32)
        return s if bias is None else s + bias

    qt = q_ref[0].astype(F32).T * (NSA_HD ** -0.5 * LOG2E)
    qp_ref[...] = qf_ref[0]
    for r in range(NSA_GRP):
        qp_ref[0:NSA_HD, head(r)] = qt[r * NSA_HD:(r + 1) * NSA_HD, :].astype(BF16)

    s_cmp = scores(kc_ref[0, 0])
    win = []
    for off, size, bias in ((0, tq, bias_ref[1]), (tq, WINDOW - tq, None), (WINDOW, tq, bias_ref[0])):
        k0 = pl.multiple_of(t0 + off, tq)
        win.append((scores(kw_ref[0, 0, pl.ds(k0, size), :], bias), vw_ref[0, 0, :, pl.ds(k0, size)]))

    ncp = kc_ref.shape[2]
    n_row = lax.broadcasted_iota(jnp.int32, (ncp, lanes), 0)
    t_lane = t0 + lax.broadcasted_iota(jnp.int32, (ncp, lanes), 1) % tq
    ok = (t_lane >= n_row * CMP_STRIDE + (CMP_BLK - 1)) & (n_row < n_cmp)
    s = jnp.where(ok, s_cmp, NEG)
    p = jnp.where(ok, jnp.exp2(s - jnp.max(s, axis=0, keepdims=True)), 0.0)
    l = jnp.sum(p, axis=0, keepdims=True)
    p_cmp = p * (1.0 / jnp.where(l > 0.0, l, 1.0))
    o_cmp = jnp.dot(vc_ref[0, 0], p_cmp.astype(BF16), preferred_element_type=F32)

    rest = p_cmp[:, head(0)]
    for r in range(1, NSA_GRP):
        rest = rest + p_cmp[:, head(r)]
    imp = jnp.zeros((n_slc, tq), F32)
    for _ in range(3):
        part = rest.astype(BF16)
        rest = rest - part.astype(F32)
        imp = imp + jnp.dot(c2s_ref[...], part, preferred_element_type=F32)
    j_row = lax.broadcasted_iota(jnp.int32, (n_slc, tq), 0)
    t_col = t0 + lax.broadcasted_iota(jnp.int32, (n_slc, tq), 1)
    forced = (j_row == 0) | (j_row == t_col // SLC_BLK)
    causal = j_row * SLC_BLK <= t_col
    imp = jnp.where(forced, FORCE, jnp.where(causal, imp, -FORCE))
    imp_ref[...] = imp

    n_grp = n_slc // SUBLANES
    vals = [imp[v * SUBLANES:(v + 1) * SUBLANES, :] for v in range(n_grp)]
    sub = lax.broadcasted_iota(jnp.int32, (SUBLANES, tq), 0)
    rank = [jnp.zeros((SUBLANES, tq), jnp.int32) for _ in range(n_grp)]
    for i in range(n_slc):
        bi = jnp.broadcast_to(imp_ref[i:i + 1, :], (SUBLANES, tq))
        for v in range(n_grp):
            if v < i // SUBLANES:
                ahead = bi > vals[v]
            elif v > i // SUBLANES:
                ahead = bi >= vals[v]
            else:
                ahead = (bi > vals[v]) | ((bi == vals[v]) & (sub > i % SUBLANES))
            rank[v] = rank[v] + jnp.where(ahead, 1, 0)
    chosen = (jnp.concatenate(rank, axis=0) < min(SLC_TOPN, n_slc)) & causal
    mask = jnp.where(chosen, 0.0, NEG).astype(BF16)
    for r in range(NSA_GRP):
        qp_ref[MASK_OFF:MASK_OFF + n_slc, head(r)] = mask

    tk = min(ATT_TK, seq)
    n_full = t0 // tk
    last = n_full // 2

    s_bufs = (s0_ref, s1_ref)

    def pair_scores(step, buf):
        for par in range(2):
            k0 = pl.multiple_of((2 * step + par) * tk, tk)
            s = scores(ks_ref[0, 0, pl.ds(k0, tk), :])
            s_bufs[buf][par] = s
            mx_ref[buf, par] = jnp.max(s, axis=0, keepdims=True)

    def pair_update(step, buf, prefetch):
        m_new = [jnp.maximum(m_ref[par], mx_ref[buf, par]) for par in range(2)]
        probs = [_exp2_bf16(s_bufs[buf].at[par], m_new[par]) for par in range(2)]
        if prefetch:
            pair_scores(step + 1, 1 - buf)
        for par in range(2):
            k0 = pl.multiple_of((2 * step + par) * tk, tk)
            pv = jnp.dot(vs_ref[0, 0, :, pl.ds(k0, tk)], probs[par], preferred_element_type=F32)
            acc_ref[par] = jnp.exp2(m_ref[par] - m_new[par]) * acc_ref[par] + pv
            m_ref[par] = m_new[par]

    def mask_diagonal(buf):
        off = pl.multiple_of(t0 - n_full * tk, tq)
        for par in range(2):
            @pl.when(n_full % 2 == par)
            def _():
                tile = s_bufs[buf].at[par]
                tile[pl.ds(off, tq), :] = tile[pl.ds(off, tq), :] + bias_ref[0]
                mx_ref[buf, par] = jnp.max(tile[...], axis=0, keepdims=True)

    def half_trip(step, buf):
        @pl.when(step == last)
        def _():
            mask_diagonal(buf)

        @pl.when(step < last)
        def _():
            pair_update(step, buf, prefetch=True)

        @pl.when(step == last)
        def _():
            pair_update(step, buf, prefetch=False)

    def trip(i, carry):
        half_trip(2 * i, 0)
        half_trip(2 * i + 1, 1)
        return carry

    m_ref[...] = jnp.full(m_ref.shape, NEG, F32)
    acc_ref[...] = jnp.zeros(acc_ref.shape, F32)
    pair_scores(0, 0)

    mx = functools.reduce(jnp.maximum, [jnp.max(s, axis=0, keepdims=True) for s, _ in win])
    o_win = jnp.zeros((V_ROWS, lanes), F32)
    for s, v_tile in win:
        o_win = o_win + jnp.dot(v_tile, _exp2_bf16(s, mx), preferred_element_type=F32)
    o_win = o_win[0:NSA_HD] * (1.0 / o_win[NSA_HD:NSA_HD + 1])

    lax.fori_loop(0, last // 2 + 1, trip, 0)
    m_all = jnp.maximum(m_ref[0], m_ref[1])
    o_slc = jnp.exp2(m_ref[0] - m_all) * acc_ref[0] + jnp.exp2(m_ref[1] - m_all) * acc_ref[1]
    o_slc = o_slc[0:NSA_HD] * (1.0 / o_slc[NSA_HD:NSA_HD + 1])

    gates = jax.nn.sigmoid(gl_ref[0, 0].astype(F32))
    outs = []
    for r in range(NSA_GRP):
        outs.append(gates[3 * r:3 * r + 1] * o_cmp[:, head(r)] + gates[3 * r + 1:3 * r + 2] * o_slc[:, head(r)]
                    + gates[3 * r + 2:3 * r + 3] * o_win[:, head(r)])
    o_ref[0] = jnp.concatenate(outs, axis=0).T.astype(o_ref.dtype)


def _alibi_slopes():
    return (2.0 ** (-8.0 * np.arange(1, NSA_HEADS + 1) / NSA_HEADS)).astype(np.float32)


def _bf16_split3(v):
    parts, rest = [], np.asarray(v, np.float32)
    for _ in range(3):
        part = rest.astype(ml_dtypes.bfloat16).astype(np.float32)
        parts.append(part)
        rest = (rest.astype(np.float64) - part.astype(np.float64)).astype(np.float32)
    return parts


def _query_features(tq):
    feat = np.zeros((NSA_HEADS, FEAT), np.float32)
    hi, mid, lo = _bf16_split3(_alibi_slopes() * np.float32(LOG2E))
    for c, part in enumerate((hi, mid, lo, hi, mid, lo)):
        feat[:, ALIBI_OFF + c] = part
    feat[:, PAD_OFF] = NEG
    feat = np.repeat(feat.reshape(NSA_KV_HEADS, NSA_GRP, FEAT), tq, axis=1)
    return jnp.asarray(feat.transpose(0, 2, 1), BF16)


def _key_features(pos, n_slc):
    pos = np.asarray(pos)
    real = pos >= 0
    feat = np.zeros((pos.shape[0], FEAT - NSA_HD), np.float32)
    if n_slc:
        feat[real, pos[real] // SLC_BLK] = 1.0
    a0 = ALIBI_OFF - NSA_HD
    feat[real, a0:a0 + 3] = ((pos[real] // SLC_BLK) * SLC_BLK)[:, None]
    feat[real, a0 + 3:a0 + 6] = (pos[real] % SLC_BLK)[:, None]
    feat[~real, PAD_OFF - NSA_HD] = 1.0
    return feat


def _values_t(v):
    b, g, n, _ = v.shape
    extra = jnp.zeros((b, g, V_ROWS - NSA_HD, n), v.dtype).at[:, :, 0].set(1.0)
    return jnp.concatenate([jnp.swapaxes(v, 2, 3), extra], axis=2)


def _with_features(k, feat):
    b, g, _, _ = k.shape
    f = jnp.broadcast_to(jnp.asarray(feat, BF16), (b, g) + feat.shape)
    return jnp.concatenate([k.astype(BF16), f], axis=3)


def nsa_attention(z3, qcol0, kc, vc, ks, vs, kw, vw, gate_logits):
    b, s, _ = z3.shape
    tq = ATT_TQ
    tk = min(ATT_TK, s)
    assert s % (2 * tk) == 0 and tk % tq == 0 and tq % SLC_BLK == 0 and WINDOW % tq == 0 and WINDOW > tq
    n_slc = s // SLC_BLK
    assert n_slc <= ALIBI_OFF - MASK_OFF and n_slc % SUBLANES == 0
    n_cmp = (s - CMP_BLK) // CMP_STRIDE + 1
    ncp = kc.shape[2]
    gw = NSA_GRP * NSA_HD
    lanes = NSA_GRP * tq

    kc_f = _with_features(kc, _key_features(np.arange(ncp) * CMP_STRIDE + CMP_BLK - 1, 0))
    ks_f = _with_features(ks, _key_features(np.arange(s), n_slc))
    front = ((0, 0), (0, 0), (WINDOW, 0), (0, 0))
    kw_f = _with_features(jnp.pad(kw, front), _key_features(np.arange(-WINDOW, s), 0))
    t = lambda v: jnp.swapaxes(v, 2, 3)
    vw_t = _values_t(jnp.pad(vw, front))

    c_start = np.arange(ncp) * CMP_STRIDE
    s_start = np.arange(n_slc) * SLC_BLK
    overlap = np.clip(np.minimum(c_start[:, None] + CMP_BLK, s_start[None, :] + SLC_BLK)
                      - np.maximum(c_start[:, None], s_start[None, :]), 0, None) / CMP_BLK
    overlap[n_cmp:] = 0.0
    c2s_t = jnp.asarray(overlap.T, BF16)

    key = np.arange(tq)[:, None]
    qry = np.tile(np.arange(tq), NSA_GRP)[None, :]
    bias = jnp.asarray(np.stack([np.where(key <= qry, 0.0, NEG), np.where(key > qry, 0.0, NEG)]), F32)

    blk = lambda *shape: pl.BlockSpec((1, 1) + shape, lambda bi, g, qi: (bi, g, 0, 0))
    const = lambda *shape: pl.BlockSpec(shape, lambda bi, g, qi: (0,) * len(shape))
    return pl.pallas_call(
        functools.partial(_nsa_kernel, n_cmp=n_cmp),
        grid=(b, NSA_KV_HEADS, s // tq),
        in_specs=[pl.BlockSpec((1, tq, gw), lambda bi, g, qi: (bi, qi, qcol0 // gw + g)),
                  pl.BlockSpec((1, FEAT, lanes), lambda bi, g, qi: (g, 0, 0)),
                  blk(ncp, FEAT), blk(NSA_HD, ncp),
                  blk(s, FEAT), blk(V_ROWS, s),
                  blk(s + WINDOW, FEAT), blk(V_ROWS, s + WINDOW),
                  pl.BlockSpec((1, 1, NSA_GRP * 3, tq), lambda bi, g, qi: (bi, g, 0, qi)),
                  const(n_slc, ncp), const(2, tq, lanes)],
        out_specs=pl.BlockSpec((1, tq, gw), lambda bi, g, qi: (bi, qi, g)),
        out_shape=jax.ShapeDtypeStruct((b, s, NSA_W), BF16),
        scratch_shapes=[pltpu.VMEM((FEAT, lanes), BF16), pltpu.VMEM((n_slc, tq), F32),
                        pltpu.VMEM((2, tk, lanes), F32), pltpu.VMEM((2, tk, lanes), F32),
                        pltpu.VMEM((2, 2, 1, lanes), F32),
                        pltpu.VMEM((2, 1, lanes), F32), pltpu.VMEM((2, V_ROWS, lanes), F32)],
        compiler_params=_cparams(("parallel", "parallel", "arbitrary")),
        name="nsa_attention",
    )(z3, _query_features(tq), kc_f, t(vc), ks_f, _values_t(vs), kw_f, vw_t, gate_logits, c2s_t, bias)


def _merge_kernel(ya_ref, yb_ref, yc_ref, wb_ref, ga_ref, gb_ref, gc_ref, o_ref):
    acc = None
    for i, (y_ref, g_ref) in enumerate(((ya_ref, ga_ref), (yb_ref, gb_ref), (yc_ref, gc_ref))):
        term = jax.nn.sigmoid(g_ref[...].astype(F32)) * jnp.dot(y_ref[...], wb_ref[i], preferred_element_type=F32)
        acc = term if acc is None else acc + term
    o_ref[...] = acc.astype(o_ref.dtype)


def branch_merge(ya, yb, yc, wb, zmerge, tm=1024, tn=512):
    t = ya.shape[0]
    d = wb.shape[2]
    nj = d // tn
    y_spec = pl.BlockSpec((tm, BRANCH_W), lambda i, j: (i, 0))
    gate = lambda br: pl.BlockSpec((tm, tn), lambda i, j: (i, j + br * nj))
    return pl.pallas_call(
        _merge_kernel,
        grid=(t // tm, nj),
        in_specs=[y_spec, y_spec, y_spec,
                  pl.BlockSpec((3, BRANCH_W, tn), lambda i, j: (0, 0, j)),
                  gate(0), gate(1), gate(2)],
        out_specs=pl.BlockSpec((tm, tn), lambda i, j: (i, j)),
        out_shape=jax.ShapeDtypeStruct((t, d), BF16),
        compiler_params=_cparams(("parallel", "parallel")),
        name="branch_merge",
    )(ya, yb, yc, wb, zmerge, zmerge, zmerge)


def _layer_norm(y, g, b):
    mu = jnp.mean(y, axis=-1, keepdims=True)
    yc = y - mu
    var = jnp.mean(yc * yc, axis=-1, keepdims=True)
    return yc * lax.rsqrt(var + LN_EPS) * g + b


def _outproj_ln_kernel(m_ref, w_ref, x_ref, g_ref, b_ref, o_ref, ob_ref, *, alpha):
    mix = jnp.dot(m_ref[...], w_ref[...], preferred_element_type=F32)
    y = _layer_norm(alpha * x_ref[...] + mix, g_ref[...], b_ref[...])
    o_ref[...] = y
    ob_ref[...] = y.astype(BF16)


def outproj_ln(merged, w_out, x, g, b, alpha, tm=256):
    t, d = x.shape
    row = pl.BlockSpec((tm, d), lambda i: (i, 0))
    vec = pl.BlockSpec((1, d), lambda i: (0, 0))
    return pl.pallas_call(
        functools.partial(_outproj_ln_kernel, alpha=alpha),
        grid=(t // tm,),
        in_specs=[row, pl.BlockSpec((d, d), lambda i: (0, 0)), row, vec, vec],
        out_specs=[row, row],
        out_shape=[jax.ShapeDtypeStruct((t, d), F32), jax.ShapeDtypeStruct((t, d), BF16)],
        compiler_params=_cparams(("parallel",)),
        name="outproj_ln",
    )(merged, w_out, x, g.reshape(1, d), b.reshape(1, d))


def _final_kernel(x_ref, xb_ref, wg_ref, p_ref, wp_ref, y0_ref, y1_ref, gate_ref, g_ref, b_ref, o_ref, ob_ref,
                  *, alpha):
    ple = jax.nn.sigmoid(jnp.dot(xb_ref[...], wg_ref[...], preferred_element_type=F32)) * jnp.dot(
        p_ref[...], wp_ref[...], preferred_element_type=F32)
    gate = gate_ref[...]
    ffn = gate[:, 0:1] * y0_ref[...].astype(F32) + gate[:, 1:2] * y1_ref[...].astype(F32)
    y = _layer_norm(alpha * x_ref[...] + ffn + ple, g_ref[...], b_ref[...])
    o_ref[...] = y
    ob_ref[...] = y.astype(BF16)


def ple_combine_ln(x, xb, w_ple_gate, p, w_ple, y0, y1, gate, g, b, alpha, tm=256):
    t, d = x.shape
    pd = p.shape[1]
    row = pl.BlockSpec((tm, d), lambda i: (i, 0))
    vec = pl.BlockSpec((1, d), lambda i: (0, 0))
    return pl.pallas_call(
        functools.partial(_final_kernel, alpha=alpha),
        grid=(t // tm,),
        in_specs=[row, row, pl.BlockSpec((d, d), lambda i: (0, 0)),
                  pl.BlockSpec((tm, pd), lambda i: (i, 0)), pl.BlockSpec((pd, d), lambda i: (0, 0)),
                  row, row, pl.BlockSpec((tm, TOP_K), lambda i: (i, 0)), vec, vec],
        out_specs=[row, row],
        out_shape=[jax.ShapeDtypeStruct((t, d), F32), jax.ShapeDtypeStruct((t, d), BF16)],
        compiler_params=_cparams(("parallel",)),
        name="ple_combine_ln",
    )(x, xb, w_ple_gate, p, w_ple, y0, y1, gate, g.reshape(1, d), b.reshape(1, d))


def _router_kernel(x_ref, wr_ref, br_ref, idx_ref, gate_ref):
    tm = x_ref.shape[0]
    logits = lax.dot_general(wr_ref[...], x_ref[...], (((1,), (1,)), ((), ())),
                             precision=lax.Precision.HIGHEST, preferred_element_type=F32)
    aff = jax.nn.sigmoid(logits)
    sel = aff + br_ref[...]
    scores = []
    for g in range(N_GROUPS):
        v = [sel[g * EXPERTS_PER_GROUP + e:g * EXPERTS_PER_GROUP + e + 1, :] for e in range(EXPERTS_PER_GROUP)]
        best = None
        for a in range(EXPERTS_PER_GROUP):
            for c in range(a + 1, EXPERTS_PER_GROUP):
                pair = v[a] + v[c]
                best = pair if best is None else jnp.maximum(best, pair)
        scores.append(best)
    best_score, best_group = scores[0], jnp.zeros((1, tm), jnp.int32)
    for g in range(1, N_GROUPS):
        better = scores[g] > best_score
        best_score = jnp.where(better, scores[g], best_score)
        best_group = jnp.where(better, g, best_group)
    e_row = lax.broadcasted_iota(jnp.int32, (N_EXPERTS, tm), 0)
    cand = jnp.where(e_row // EXPERTS_PER_GROUP == best_group, sel, NEG)
    picks, gates = [], []
    for _ in range(TOP_K):
        mx = jnp.max(cand, axis=0, keepdims=True)
        pick = jnp.min(jnp.where(cand == mx, e_row, N_EXPERTS), axis=0, keepdims=True)
        hit = e_row == pick
        picks.append(pick)
        gates.append(jnp.sum(jnp.where(hit, aff, 0.0), axis=0, keepdims=True))
        cand = jnp.where(hit, -jnp.inf, cand)
    total = gates[0] + gates[1]
    idx_ref[...] = jnp.concatenate(picks, axis=0)
    gate_ref[...] = jnp.concatenate([gates[0] / total, gates[1] / total], axis=0)


def router(x, w_router, b_router, tm=1024):
    t, d = x.shape
    return pl.pallas_call(
        _router_kernel,
        grid=(t // tm,),
        in_specs=[pl.BlockSpec((tm, d), lambda i: (i, 0)),
                  pl.BlockSpec((N_EXPERTS, d), lambda i: (0, 0)),
                  pl.BlockSpec((N_EXPERTS, 1), lambda i: (0, 0))],
        out_specs=[pl.BlockSpec((TOP_K, tm), lambda i: (0, i)), pl.BlockSpec((TOP_K, tm), lambda i: (0, i))],
        out_shape=[jax.ShapeDtypeStruct((TOP_K, t), jnp.int32), jax.ShapeDtypeStruct((TOP_K, t), F32)],
        compiler_params=_cparams(("parallel",)),
        name="router",
    )(x, w_router.T, b_router.reshape(N_EXPERTS, 1))


def _expert_kernel(be_ref, nu_ref, *refs, part_starts):
    n_parts = len(part_starts)
    x_refs = refs[:n_parts]
    wg_ref, wu_ref, wd_ref = refs[n_parts:n_parts + 3]
    prev_ref = refs[n_parts + 3] if len(refs) == n_parts + 8 else None
    o_ref, wgb_ref, wub_ref, wdb_ref = refs[-4:]
    i = pl.program_id(0)

    @pl.when((i == 0) | (be_ref[i] != be_ref[jnp.maximum(i - 1, 0)]))
    def _():
        wgb_ref[...] = wg_ref[0, 0].astype(BF16)
        wub_ref[...] = wu_ref[0, 0].astype(BF16)
        wdb_ref[...] = wd_ref[0, 0].astype(BF16)

    @pl.when(i < nu_ref[0])
    def _():
        x = x_refs[0][...]
        for start, x_ref in zip(part_starts[1:], x_refs[1:]):
            x = jnp.where(i >= start, x_ref[...], x)
        g = jnp.dot(x, wgb_ref[...], preferred_element_type=F32)
        u = jnp.dot(x, wub_ref[...], preferred_element_type=F32)
        out = jnp.dot((jax.nn.silu(g) * u).astype(BF16), wdb_ref[...], preferred_element_type=F32)
        if prev_ref is not None:
            out = out + prev_ref[...].astype(F32)
        o_ref[...] = out.astype(o_ref.dtype)

    @pl.when(i >= nu_ref[0])
    def _():
        o_ref[...] = jnp.zeros_like(o_ref)


def expert_ffn(block_expert, n_used, x_parts, w_gate_up, w_down, layer):
    d = x_parts[0].shape[1]
    n_rows = sum(x.shape[0] for x in x_parts)
    de = w_down.shape[2]
    ce = de // EXPERT_CHUNKS
    part_blocks = [x.shape[0] // MOE_BLK for x in x_parts]
    part_starts = tuple(int(v) for v in np.cumsum([0] + part_blocks[:-1]))
    part_specs = [pl.BlockSpec((MOE_BLK, d), lambda i, be, nu, o=o, n=n: (jnp.clip(i - o, 0, n - 1), 0))
                  for o, n in zip(part_starts, part_blocks)]
    out = None
    for c in range(EXPERT_CHUNKS):
        row = pl.BlockSpec((MOE_BLK, d), lambda i, be, nu: (i, 0))
        in_specs = part_specs + [
            pl.BlockSpec((1, 1, d, ce), lambda i, be, nu, c=c: (layer, be[i], 0, c)),
            pl.BlockSpec((1, 1, d, ce), lambda i, be, nu, c=c: (layer, be[i], 0, EXPERT_CHUNKS + c)),
            pl.BlockSpec((1, 1, ce, d), lambda i, be, nu, c=c: (layer, be[i], c, 0))]
        args = list(x_parts) + [w_gate_up, w_gate_up, w_down]
        if out is not None:
            in_specs.append(row)
            args.append(out)
        out = pl.pallas_call(
            functools.partial(_expert_kernel, part_starts=part_starts),
            grid_spec=pltpu.PrefetchScalarGridSpec(
                num_scalar_prefetch=2,
                grid=(n_rows // MOE_BLK,),
                in_specs=in_specs,
                out_specs=row,
                scratch_shapes=[pltpu.VMEM((d, ce), BF16), pltpu.VMEM((d, ce), BF16), pltpu.VMEM((ce, d), BF16)]),
            out_shape=jax.ShapeDtypeStruct((n_rows, d), BF16),
            compiler_params=_cparams(("arbitrary",)),
            name="expert_ffn",
        )(block_expert, n_used, *args)
    return out


def _dispatch_tables(expert_idx):
    n_tok = expert_idx.shape[0]
    n_assign = n_tok * TOP_K
    e_flat = expert_idx.reshape(n_assign)
    onehot = e_flat[:, None] == jnp.arange(N_EXPERTS)[None, :]
    chunk = min(1024, n_assign)
    oh = onehot.astype(F32).reshape(n_assign // chunk, chunk, N_EXPERTS)
    within = jnp.einsum("ij,bjk->bik", jnp.tril(jnp.ones((chunk, chunk), F32)), oh)
    totals = within[:, -1, :]
    csum = (within + (jnp.cumsum(totals, axis=0) - totals)[:, None, :]).reshape(n_assign, N_EXPERTS)
    rank = jnp.sum(jnp.where(onehot, csum, 0.0), axis=1).astype(jnp.int32) - 1
    counts = csum[-1].astype(jnp.int32)
    padded = (counts + MOE_BLK - 1) // MOE_BLK * MOE_BLK
    pad_end = jnp.cumsum(padded)
    dest = (pad_end - padded)[e_flat] + rank
    n_rows = n_assign + N_EXPERTS * MOE_BLK
    row_tok = jnp.zeros((n_rows,), jnp.int32).at[dest].set(jnp.arange(n_assign, dtype=jnp.int32) // TOP_K)
    block_expert = jnp.minimum(
        jnp.searchsorted(pad_end, jnp.arange(n_rows // MOE_BLK) * MOE_BLK, side="right"), N_EXPERTS - 1)
    n_used = (pad_end[-1:] // MOE_BLK).astype(jnp.int32)
    return row_tok, block_expert.astype(jnp.int32), n_used, dest.reshape(n_tok, TOP_K)


def _token_mixer(xb, bsz, seq, w_in_t, layer, conv_a_w, conv_a_b, lru_conv_w, lru_conv_b, lru_wa, lru_ba, lru_wx, lru_bx,
                 lru_lam, cmp_pos, phi_w1, phi_b1, phi_w2, w_branch):
    d = xb.shape[1]
    n_tok = bsz * seq
    main_w = 3 * BRANCH_W + 2 * BRANCH_W + NSA_W + 6 * KV_W
    ng = 3 * NSA_HEADS
    kv0 = main_w - 6 * KV_W
    zmain = matmul_wres(xb, w_in_t, layer, 0, main_w, BF16, 2048, 768)
    zgate = matmul_wres(xb, w_in_t, layer, main_w, LANES, F32, 2048, LANES)
    zmerge = matmul_wres(xb, w_in_t, layer, main_w + ng, w_in_t.shape[1] - main_w - ng, BF16, 2048, 512)
    z3 = zmain.reshape(bsz, seq, main_w)

    y_a = conv_mixer(z3, conv_a_w, conv_a_b)
    y_b = lru_mixer(z3, 3 * BRANCH_W, lru_conv_w, lru_conv_b, lru_wa, lru_ba, lru_wx, lru_bx, lru_lam)

    kv = z3[:, :, kv0:].reshape(bsz, seq, 6, NSA_KV_HEADS, NSA_HD).transpose(2, 0, 3, 1, 4)
    nslot = seq // CMP_STRIDE
    slots = kv[0:2].reshape(2, bsz, NSA_KV_HEADS, nslot, CMP_STRIDE * NSA_HD)
    cmp = nsa_compress(slots, cmp_pos, phi_w1, phi_b1, phi_w2)
    gate_logits = zgate[:, :ng].reshape(bsz, seq, NSA_KV_HEADS, NSA_GRP * 3).transpose(0, 2, 3, 1)
    y_c = nsa_attention(z3, 5 * BRANCH_W, cmp[0], cmp[1], kv[2], kv[3], kv[4], kv[5], gate_logits)

    flat = lambda y: y.reshape(n_tok, BRANCH_W)
    return branch_merge(flat(y_a), flat(y_b), flat(y_c), w_branch.astype(BF16), zmerge)


def _moe(x, xb, w_router, b_router, w_gate_up, w_down, layer):
    idx_t, gate_t = router(x, w_router, b_router)
    row_tok, block_expert, n_used, dest = _dispatch_tables(idx_t.T)
    n_tok = xb.shape[0]
    x_parts = [xb[row_tok[a:a + n_tok]] for a in range(0, row_tok.shape[0], n_tok)]
    rows = expert_ffn(block_expert, n_used, x_parts, w_gate_up, w_down, layer)
    return rows[dest[:, 0]], rows[dest[:, 1]], gate_t.T


def kernel(x, p, w_in, conv_a_w, conv_a_b, lru_conv_w, lru_conv_b, lru_wa, lru_ba, lru_wx, lru_bx, lru_lam, cmp_pos, phi_w1, phi_b1, phi_w2, w_branch, w_out, ln_g, ln_b, w_router, b_router, w_gate_up, w_down, w_ple, w_ple_gate):
    bsz, seq, d = x.shape
    depth = w_in.shape[0]
    n_tok = bsz * seq
    alpha = (2 * depth) ** 0.25
    xf = x.reshape(n_tok, d)
    xb = xf.astype(BF16)
    w_in_t = jnp.swapaxes(w_in, 1, 2)
    for i in range(depth):
        merged = _token_mixer(xb, bsz, seq, w_in_t, i, conv_a_w[i], conv_a_b[i], lru_conv_w[i], lru_conv_b[i],
                              lru_wa[i], lru_ba[i], lru_wx[i], lru_bx[i], lru_lam[i], cmp_pos[i], phi_w1[i],
                              phi_b1[i], phi_w2[i], w_branch[i])
        xf, xb = outproj_ln(merged, w_out[i].astype(BF16), xf, ln_g[i, 0], ln_b[i, 0], alpha)
        y0, y1, gate = _moe(xf, xb, w_router, b_router, w_gate_up, w_down, i)
        xf, xb = ple_combine_ln(xf, xb, w_ple_gate[i].astype(BF16), p[i].reshape(n_tok, -1).astype(BF16),
                                w_ple[i].astype(BF16), y0, y1, gate, ln_g[i, 1], ln_b[i, 1], alpha)
    return xf.reshape(bsz, seq, d)
```

```python
import functools

import ml_dtypes
import numpy as np
import jax
import jax.numpy as jnp
from jax import lax
from jax.experimental import pallas as pl
from jax.experimental.pallas import tpu as pltpu

F32 = jnp.float32
BF16 = jnp.bfloat16

BRANCH_W = 1024
CONV_K = 3
LRU_HEADS = 8
LRU_HD = BRANCH_W // LRU_HEADS
LRU_CONV_K = 4
LRU_C = 8.0
NSA_HEADS = 16
NSA_KV_HEADS = 4
NSA_GRP = NSA_HEADS // NSA_KV_HEADS
NSA_HD = 64
NSA_W = NSA_HEADS * NSA_HD
KV_W = NSA_KV_HEADS * NSA_HD
CMP_BLK = 32
CMP_STRIDE = 16
CMP_HIDDEN = 128
SLC_BLK = 64
SLC_TOPN = 16
WINDOW = 512
N_EXPERTS = 16
N_GROUPS = 4
EXPERTS_PER_GROUP = N_EXPERTS // N_GROUPS
TOP_K = 2
LN_EPS = 1e-5
NEG = -1e30
FORCE = 1e9

LANES = 128
SUBLANES = 8
VMEM_LIMIT = 56 * 1024 * 1024

FEAT = 256
MASK_OFF = NSA_HD
ALIBI_OFF = 2 * NSA_HD
PAD_OFF = ALIBI_OFF + 6
V_ROWS = NSA_HD + 16
LOG2E = 1.4426950408889634

ATT_TQ = 256
ATT_TK = 512
EXP_CHUNK = 64
MOE_BLK = 512
EXPERT_CHUNKS = 2


def _cparams(sem):
    return pltpu.CompilerParams(dimension_semantics=sem, vmem_limit_bytes=VMEM_LIMIT)


def _mm_wres_kernel(x_ref, w_ref, *rest, shift):
    o_ref, wb_ref = rest[-2:]

    @pl.when(pl.program_id(1) == 0)
    def _():
        w = w_ref[0]
        if shift:
            tn = w.shape[0]
            w = jnp.concatenate([w, rest[0][0]], axis=0)[shift:shift + tn]
        wb_ref[...] = w.astype(BF16)

    o_ref[...] = lax.dot_general(x_ref[...], wb_ref[...], (((1,), (1,)), ((), ())),
                                 preferred_element_type=F32).astype(o_ref.dtype)


def matmul_wres(x, wt3, layer, row0, n, out_dtype, tm, tn):
    m, k = x.shape
    shift = row0 % tn
    base = row0 - shift
    assert m % tm == 0 and n % tn == 0 and shift % SUBLANES == 0 and shift <= LANES and tn % LANES == 0
    in_specs = [pl.BlockSpec((tm, k), lambda j, i: (i, 0)),
                pl.BlockSpec((1, tn, k), lambda j, i: (layer, base // tn + j, 0))]
    args = [x, wt3]
    if shift:
        in_specs.append(pl.BlockSpec((1, LANES, k), lambda j, i: (layer, (base + (j + 1) * tn) // LANES, 0)))
        args.append(wt3)
    return pl.pallas_call(
        functools.partial(_mm_wres_kernel, shift=shift),
        grid=(n // tn, m // tm),
        in_specs=in_specs,
        out_specs=pl.BlockSpec((tm, tn), lambda j, i: (i, j)),
        out_shape=jax.ShapeDtypeStruct((m, n), out_dtype),
        scratch_shapes=[pltpu.VMEM((tn, k), BF16)],
        compiler_params=_cparams(("parallel", "arbitrary")),
        name="matmul_wres",
    )(*args)


def _causal_conv(ext, cur, w, k):
    acc = cur * w[k - 1:k, :]
    for j in range(k - 1):
        shift = k - 1 - j
        acc = acc + pltpu.roll(ext, shift, 0)[SUBLANES:, :] * w[j:j + 1, :]
    return acc


def _conv_mixer_kernel(ain_ref, ab_ref, ac_ref, w_ref, b_ref, o_ref, halo_ref):
    @pl.when(pl.program_id(2) == 0)
    def _():
        halo_ref[...] = jnp.zeros_like(halo_ref)

    v = ac_ref[0].astype(F32) * ain_ref[0].astype(F32)
    ext = jnp.concatenate([halo_ref[...], v], axis=0)
    y = _causal_conv(ext, v, w_ref[...], CONV_K) + b_ref[...]
    o_ref[0] = (ab_ref[0].astype(F32) * y).astype(o_ref.dtype)
    halo_ref[...] = v[v.shape[0] - SUBLANES:, :]


def conv_mixer(z3, conv_w, conv_b, ts=512, cw=512):
    b, s, _ = z3.shape
    nc = BRANCH_W // cw
    return pl.pallas_call(
        _conv_mixer_kernel,
        grid=(b, nc, s // ts),
        in_specs=[pl.BlockSpec((1, ts, cw), lambda bi, c, si: (bi, si, c)),
                  pl.BlockSpec((1, ts, cw), lambda bi, c, si: (bi, si, c + nc)),
                  pl.BlockSpec((1, ts, cw), lambda bi, c, si: (bi, si, c + 2 * nc)),
                  pl.BlockSpec((CONV_K, cw), lambda bi, c, si: (0, c)),
                  pl.BlockSpec((1, cw), lambda bi, c, si: (0, c))],
        out_specs=pl.BlockSpec((1, ts, cw), lambda bi, c, si: (bi, si, c)),
        out_shape=jax.ShapeDtypeStruct((b, s, BRANCH_W), BF16),
        scratch_shapes=[pltpu.VMEM((SUBLANES, cw), F32)],
        compiler_params=_cparams(("parallel", "parallel", "arbitrary")),
        name="conv_mixer",
    )(z3, z3, z3, conv_w, conv_b.reshape(1, BRANCH_W))


def _lru_kernel(gate_ref, rin_ref, cw_ref, cb_ref, wa_ref, ba_ref, wx_ref, bx_ref, lam_ref, o_ref,
                halo_ref, carry_ref, a_ref, b_ref, h_ref):
    ts, lw = a_ref.shape

    @pl.when(pl.program_id(2) == 0)
    def _():
        halo_ref[...] = jnp.zeros_like(halo_ref)
        carry_ref[...] = jnp.zeros_like(carry_ref)

    u = rin_ref[0].astype(F32)
    ext = jnp.concatenate([halo_ref[...], u], axis=0)
    xc = _causal_conv(ext, u, cw_ref[...], LRU_CONV_K) + cb_ref[...]
    halo_ref[...] = u[ts - SUBLANES:, :]

    xcb = xc.astype(BF16)
    ra, ix = [], []
    for hh in range(lw // LRU_HD):
        xh = xcb[:, hh * LRU_HD:(hh + 1) * LRU_HD]
        ra.append(jnp.dot(xh, wa_ref[hh].astype(BF16), preferred_element_type=F32))
        ix.append(jnp.dot(xh, wx_ref[hh].astype(BF16), preferred_element_type=F32))
    r = jax.nn.sigmoid(jnp.concatenate(ra, axis=1) + ba_ref[...])
    i = jax.nn.sigmoid(jnp.concatenate(ix, axis=1) + bx_ref[...])
    neg_lam = -lam_ref[...]
    softplus = jnp.maximum(neg_lam, 0.0) + jnp.log1p(jnp.exp(-jnp.abs(neg_lam)))
    log_a = -LRU_C * r * softplus
    a_ref[...] = jnp.exp(log_a)
    th = jnp.tanh(log_a)
    b_ref[...] = jnp.sqrt(-2.0 * th / (1.0 - th)) * (i * xc)

    row = lax.broadcasted_iota(jnp.int32, (SUBLANES, lw), 0)

    def group(g, hc):
        sl = pl.ds(pl.multiple_of(g * SUBLANES, SUBLANES), SUBLANES)
        a = a_ref[sl, :]
        bv = b_ref[sl, :]
        for d in (1, 2, 4):
            keep = row >= d
            bv = jnp.where(keep, a * pltpu.roll(bv, d, 0) + bv, bv)
            a = jnp.where(keep, a * pltpu.roll(a, d, 0), a)
        h = a * hc + bv
        h_ref[sl, :] = h
        return jnp.broadcast_to(h[SUBLANES - 1:SUBLANES, :], (SUBLANES, lw))

    carry_ref[...] = lax.fori_loop(0, ts // SUBLANES, group, carry_ref[...])
    o_ref[0] = (h_ref[...] * jax.nn.gelu(gate_ref[0].astype(F32))).astype(o_ref.dtype)


def lru_mixer(z3, col0, conv_w, conv_b, wa, ba, wx, bx, lam, ts=512, lw=256):
    b, s, _ = z3.shape
    nc = BRANCH_W // lw
    c0 = col0 // lw
    hpb = lw // LRU_HD
    row = lambda v: v.reshape(1, BRANCH_W)
    vec = pl.BlockSpec((1, lw), lambda bi, c, si: (0, c))
    mat = pl.BlockSpec((hpb, LRU_HD, LRU_HD), lambda bi, c, si: (c, 0, 0))
    return pl.pallas_call(
        _lru_kernel,
        grid=(b, nc, s // ts),
        in_specs=[pl.BlockSpec((1, ts, lw), lambda bi, c, si: (bi, si, c0 + c)),
                  pl.BlockSpec((1, ts, lw), lambda bi, c, si: (bi, si, c0 + nc + c)),
                  pl.BlockSpec((LRU_CONV_K, lw), lambda bi, c, si: (0, c)),
                  vec, mat, vec, mat, vec, vec],
        out_specs=pl.BlockSpec((1, ts, lw), lambda bi, c, si: (bi, si, c)),
        out_shape=jax.ShapeDtypeStruct((b, s, BRANCH_W), BF16),
        scratch_shapes=[pltpu.VMEM((SUBLANES, lw), F32), pltpu.VMEM((SUBLANES, lw), F32),
                        pltpu.VMEM((ts, lw), F32), pltpu.VMEM((ts, lw), F32), pltpu.VMEM((ts, lw), F32)],
        compiler_params=_cparams(("parallel", "parallel", "arbitrary")),
        name="rg_lru",
    )(z3, z3, conv_w, row(conv_b), wa, row(ba), wx, row(bx), row(lam))


def _compress_kernel(slots_ref, pos_ref, w1_ref, b1_ref, w2_ref, o_ref):
    g, nslot, half = slots_ref.shape[2:]
    rows = g * nslot
    x = slots_ref[0, 0].reshape(rows, half)
    w1 = w1_ref[0].astype(BF16)
    first = jnp.dot(x, w1[:half], preferred_element_type=F32)
    second = jnp.dot(x, w1[half:], preferred_element_type=F32)
    pos = jnp.broadcast_to(pos_ref[0], (SUBLANES, 2 * half)).astype(BF16)
    posb = jnp.dot(pos, w1, preferred_element_type=F32)[0:1, :]
    hidden = first + pltpu.roll(second, rows - 1, 0) + posb + b1_ref[0]
    out = jnp.dot(jax.nn.gelu(hidden).astype(BF16), w2_ref[0].astype(BF16), preferred_element_type=F32)
    o_ref[0, 0] = out.reshape(g, nslot, NSA_HD).astype(o_ref.dtype)


def nsa_compress(slots, cmp_pos, w1, b1, w2):
    _, b, g, nslot, half = slots.shape
    pos = cmp_pos.reshape(2, 1, CMP_BLK * NSA_HD)
    return pl.pallas_call(
        _compress_kernel,
        grid=(2, b),
        in_specs=[pl.BlockSpec((1, 1, g, nslot, half), lambda kv, bi: (kv, bi, 0, 0, 0)),
                  pl.BlockSpec((1, 1, 2 * half), lambda kv, bi: (kv, 0, 0)),
                  pl.BlockSpec((1, 2 * half, CMP_HIDDEN), lambda kv, bi: (kv, 0, 0)),
                  pl.BlockSpec((1, 1, CMP_HIDDEN), lambda kv, bi: (kv, 0, 0)),
                  pl.BlockSpec((1, CMP_HIDDEN, NSA_HD), lambda kv, bi: (kv, 0, 0))],
        out_specs=pl.BlockSpec((1, 1, g, nslot, NSA_HD), lambda kv, bi: (kv, bi, 0, 0, 0)),
        out_shape=jax.ShapeDtypeStruct((2, b, g, nslot, NSA_HD), BF16),
        compiler_params=_cparams(("parallel", "parallel")),
        name="nsa_compress",
    )(slots, pos, w1, b1.reshape(2, 1, CMP_HIDDEN), w2)


def _exp2_bf16(s, m):
    rows = s.shape[0]
    chunk = min(EXP_CHUNK, rows)
    return jnp.concatenate([jnp.exp2((s[r:r + chunk, :] - m).astype(BF16)) for r in range(0, rows, chunk)], axis=0)


def _nsa_kernel(q_ref, qf_ref, kc_ref, vc_ref, ksb_ref, vsb_ref, kwb_ref, vwb_ref, fs_ref, fw_ref, gl_ref, c2s_ref,
                bias_ref, o_ref, ks_ref, vs_ref, kw_ref, vw_ref, qp_ref, imp_ref, s0_ref, s1_ref, mx_ref, m_ref, acc_ref,
                *, n_cmp):
    tq = q_ref.shape[1]
    lanes = NSA_GRP * tq
    seq = ks_ref.shape[0]
    n_slc = seq // SLC_BLK
    t0 = pl.multiple_of(pl.program_id(2) * tq, tq)
    head = lambda r: slice(r * tq, (r + 1) * tq)

    @pl.when(pl.program_id(2) == 0)
    def _():
        odd = pl.program_id(1) % 2 == 1

        def keys(blk_ref):
            blk = blk_ref[0].astype(F32)
            return jnp.where(odd, pltpu.roll(blk, NSA_HD, 1), blk)[:, :NSA_HD].astype(BF16)

        def values_t(blk_ref):
            vt = blk_ref[0].astype(F32).T
            return jnp.where(odd, vt[NSA_HD:], vt[:NSA_HD]).astype(BF16)

        def ones_row(n):
            first = lax.broadcasted_iota(jnp.int32, (V_ROWS - NSA_HD, n), 0) == 0
            return jnp.where(first, 1.0, 0.0).astype(BF16)

        ks_ref[:, 0:NSA_HD] = keys(ksb_ref)
        ks_ref[:, NSA_HD:] = fs_ref[...]
        vs_ref[0:NSA_HD, :] = values_t(vsb_ref)
        vs_ref[NSA_HD:, :] = ones_row(seq)
        kw_ref[0:WINDOW, 0:NSA_HD] = jnp.zeros((WINDOW, NSA_HD), BF16)
        kw_ref[WINDOW:, 0:NSA_HD] = keys(kwb_ref)
        kw_ref[:, NSA_HD:] = fw_ref[...]
        vw_ref[0:NSA_HD, 0:WINDOW] = jnp.zeros((NSA_HD, WINDOW), BF16)
        vw_ref[0:NSA_HD, WINDOW:] = values_t(vwb_ref)
        vw_ref[NSA_HD:, :] = ones_row(seq + WINDOW)

    def scores(k_tile, bias=None):
        s = jnp.dot(k_tile, qp_ref[...], preferred_element_type=F32)
        return s if bias is None else s + bias

    qt = q_ref[0].astype(F32).T * (NSA_HD ** -0.5 * LOG2E)
    qp_ref[...] = qf_ref[0]
    for r in range(NSA_GRP):
        qp_ref[0:NSA_HD, head(r)] = qt[r * NSA_HD:(r + 1) * NSA_HD, :].astype(BF16)

    s_cmp = scores(kc_ref[0, 0])
    win = []
    for off, size, bias in ((0, tq, bias_ref[1]), (tq, WINDOW - tq, None), (WINDOW, tq, bias_ref[0])):
        k0 = pl.multiple_of(t0 + off, tq)
        win.append((scores(kw_ref[pl.ds(k0, size), :], bias), vw_ref[:, pl.ds(k0, size)]))

    ncp = kc_ref.shape[2]
    n_row = lax.broadcasted_iota(jnp.int32, (ncp, lanes), 0)
    t_lane = t0 + lax.broadcasted_iota(jnp.int32, (ncp, lanes), 1) % tq
    ok = (t_lane >= n_row * CMP_STRIDE + (CMP_BLK - 1)) & (n_row < n_cmp)
    s = jnp.where(ok, s_cmp, NEG)
    p = jnp.where(ok, jnp.exp2(s - jnp.max(s, axis=0, keepdims=True)), 0.0)
    l = jnp.sum(p, axis=0, keepdims=True)
    p_cmp = p * (1.0 / jnp.where(l > 0.0, l, 1.0))
    o_cmp = jnp.dot(vc_ref[0, 0], p_cmp.astype(BF16), preferred_element_type=F32)

    rest = p_cmp[:, head(0)]
    for r in range(1, NSA_GRP):
        rest = rest + p_cmp[:, head(r)]
    imp = jnp.zeros((n_slc, tq), F32)
    for _ in range(3):
        part = rest.astype(BF16)
        rest = rest - part.astype(F32)
        imp = imp + jnp.dot(c2s_ref[...], part, preferred_element_type=F32)
    j_row = lax.broadcasted_iota(jnp.int32, (n_slc, tq), 0)
    t_col = t0 + lax.broadcasted_iota(jnp.int32, (n_slc, tq), 1)
    forced = (j_row == 0) | (j_row == t_col // SLC_BLK)
    causal = j_row * SLC_BLK <= t_col
    imp = jnp.where(forced, FORCE, jnp.where(causal, imp, -FORCE))
    imp_ref[...] = imp

    n_grp = n_slc // SUBLANES
    vals = [imp[v * SUBLANES:(v + 1) * SUBLANES, :] for v in range(n_grp)]
    sub = lax.broadcasted_iota(jnp.int32, (SUBLANES, tq), 0)
    rank = [jnp.zeros((SUBLANES, tq), jnp.int32) for _ in range(n_grp)]
    for i in range(n_slc):
        bi = jnp.broadcast_to(imp_ref[i:i + 1, :], (SUBLANES, tq))
        for v in range(n_grp):
            if v < i // SUBLANES:
                ahead = bi > vals[v]
            elif v > i // SUBLANES:
                ahead = bi >= vals[v]
            else:
                ahead = (bi > vals[v]) | ((bi == vals[v]) & (sub > i % SUBLANES))
            rank[v] = rank[v] + jnp.where(ahead, 1, 0)
    chosen = (jnp.concatenate(rank, axis=0) < min(SLC_TOPN, n_slc)) & causal
    mask = jnp.where(chosen, 0.0, NEG).astype(BF16)
    for r in range(NSA_GRP):
        qp_ref[MASK_OFF:MASK_OFF + n_slc, head(r)] = mask

    tk = min(ATT_TK, seq)
    n_full = t0 // tk
    last = n_full // 2

    s_bufs = (s0_ref, s1_ref)

    def pair_scores(step, buf):
        for par in range(2):
            k0 = pl.multiple_of((2 * step + par) * tk, tk)
            s = scores(ks_ref[pl.ds(k0, tk), :])
            s_bufs[buf][par] = s
            mx_ref[buf, par] = jnp.max(s, axis=0, keepdims=True)

    def pair_update(step, buf, prefetch):
        m_new = [jnp.maximum(m_ref[par], mx_ref[buf, par]) for par in range(2)]
        probs = [_exp2_bf16(s_bufs[buf].at[par], m_new[par]) for par in range(2)]
        if prefetch:
            pair_scores(step + 1, 1 - buf)
        for par in range(2):
            k0 = pl.multiple_of((2 * step + par) * tk, tk)
            pv = jnp.dot(vs_ref[:, pl.ds(k0, tk)], probs[par], preferred_element_type=F32)
            acc_ref[par] = jnp.exp2(m_ref[par] - m_new[par]) * acc_ref[par] + pv
            m_ref[par] = m_new[par]

    def mask_diagonal(buf):
        off = pl.multiple_of(t0 - n_full * tk, tq)
        for par in range(2):
            @pl.when(n_full % 2 == par)
            def _():
                tile = s_bufs[buf].at[par]
                tile[pl.ds(off, tq), :] = tile[pl.ds(off, tq), :] + bias_ref[0]
                mx_ref[buf, par] = jnp.max(tile[...], axis=0, keepdims=True)

    def half_trip(step, buf):
        @pl.when(step == last)
        def _():
            mask_diagonal(buf)

        @pl.when(step < last)
        def _():
            pair_update(step, buf, prefetch=True)

        @pl.when(step == last)
        def _():
            pair_update(step, buf, prefetch=False)

    def trip(i, carry):
        half_trip(2 * i, 0)
        half_trip(2 * i + 1, 1)
        return carry

    m_ref[...] = jnp.full(m_ref.shape, NEG, F32)
    acc_ref[...] = jnp.zeros(acc_ref.shape, F32)
    pair_scores(0, 0)

    mx = functools.reduce(jnp.maximum, [jnp.max(s, axis=0, keepdims=True) for s, _ in win])
    o_win = jnp.zeros((V_ROWS, lanes), F32)
    for s, v_tile in win:
        o_win = o_win + jnp.dot(v_tile, _exp2_bf16(s, mx), preferred_element_type=F32)
    o_win = o_win[0:NSA_HD] * (1.0 / o_win[NSA_HD:NSA_HD + 1])

    lax.fori_loop(0, last // 2 + 1, trip, 0)
    m_all = jnp.maximum(m_ref[0], m_ref[1])
    o_slc = jnp.exp2(m_ref[0] - m_all) * acc_ref[0] + jnp.exp2(m_ref[1] - m_all) * acc_ref[1]
    o_slc = o_slc[0:NSA_HD] * (1.0 / o_slc[NSA_HD:NSA_HD + 1])

    gates = jax.nn.sigmoid(gl_ref[0, 0].astype(F32))
    outs = []
    for r in range(NSA_GRP):
        outs.append(gates[3 * r:3 * r + 1] * o_cmp[:, head(r)] + gates[3 * r + 1:3 * r + 2] * o_slc[:, head(r)]
                    + gates[3 * r + 2:3 * r + 3] * o_win[:, head(r)])
    o_ref[0] = jnp.concatenate(outs, axis=0).T.astype(o_ref.dtype)


def _alibi_slopes():
    return (2.0 ** (-8.0 * np.arange(1, NSA_HEADS + 1) / NSA_HEADS)).astype(np.float32)


def _bf16_split3(v):
    parts, rest = [], np.asarray(v, np.float32)
    for _ in range(3):
        part = rest.astype(ml_dtypes.bfloat16).astype(np.float32)
        parts.append(part)
        rest = (rest.astype(np.float64) - part.astype(np.float64)).astype(np.float32)
    return parts


def _query_features(tq):
    feat = np.zeros((NSA_HEADS, FEAT), np.float32)
    hi, mid, lo = _bf16_split3(_alibi_slopes() * np.float32(LOG2E))
    for c, part in enumerate((hi, mid, lo, hi, mid, lo)):
        feat[:, ALIBI_OFF + c] = part
    feat[:, PAD_OFF] = NEG
    feat = np.repeat(feat.reshape(NSA_KV_HEADS, NSA_GRP, FEAT), tq, axis=1)
    return jnp.asarray(feat.transpose(0, 2, 1), BF16)


def _key_features(pos, n_slc):
    pos = np.asarray(pos)
    real = pos >= 0
    feat = np.zeros((pos.shape[0], FEAT - NSA_HD), np.float32)
    if n_slc:
        feat[real, pos[real] // SLC_BLK] = 1.0
    a0 = ALIBI_OFF - NSA_HD
    feat[real, a0:a0 + 3] = ((pos[real] // SLC_BLK) * SLC_BLK)[:, None]
    feat[real, a0 + 3:a0 + 6] = (pos[real] % SLC_BLK)[:, None]
    feat[~real, PAD_OFF - NSA_HD] = 1.0
    return feat


def _with_features(k, feat):
    b, g, _, _ = k.shape
    f = jnp.broadcast_to(jnp.asarray(feat, BF16), (b, g) + feat.shape)
    return jnp.concatenate([k.astype(BF16), f], axis=3)


def nsa_attention(z3, qcol0, kvcol0, kc, vc, gate_logits):
    b, s, _ = z3.shape
    tq = ATT_TQ
    tk = min(ATT_TK, s)
    assert s % (2 * tk) == 0 and tk % tq == 0 and tq % SLC_BLK == 0 and WINDOW % tq == 0 and WINDOW > tq
    n_slc = s // SLC_BLK
    assert n_slc <= ALIBI_OFF - MASK_OFF and n_slc % SUBLANES == 0
    assert 2 * NSA_HD == LANES and NSA_KV_HEADS % 2 == 0 and kvcol0 % LANES == 0
    n_cmp = (s - CMP_BLK) // CMP_STRIDE + 1
    ncp = kc.shape[2]
    gw = NSA_GRP * NSA_HD
    lanes = NSA_GRP * tq

    kc_f = _with_features(kc, _key_features(np.arange(ncp) * CMP_STRIDE + CMP_BLK - 1, 0))
    feat_s = jnp.asarray(_key_features(np.arange(s), n_slc), BF16)
    feat_w = jnp.asarray(_key_features(np.arange(-WINDOW, s), 0), BF16)

    c_start = np.arange(ncp) * CMP_STRIDE
    s_start = np.arange(n_slc) * SLC_BLK
    overlap = np.clip(np.minimum(c_start[:, None] + CMP_BLK, s_start[None, :] + SLC_BLK)
                      - np.maximum(c_start[:, None], s_start[None, :]), 0, None) / CMP_BLK
    overlap[n_cmp:] = 0.0
    c2s_t = jnp.asarray(overlap.T, BF16)

    key = np.arange(tq)[:, None]
    qry = np.tile(np.arange(tq), NSA_GRP)[None, :]
    bias = jnp.asarray(np.stack([np.where(key <= qry, 0.0, NEG), np.where(key > qry, 0.0, NEG)]), F32)

    blk = lambda *shape: pl.BlockSpec((1, 1) + shape, lambda bi, g, qi: (bi, g, 0, 0))
    const = lambda *shape: pl.BlockSpec(shape, lambda bi, g, qi: (0,) * len(shape))
    pair = lambda j: pl.BlockSpec((1, s, LANES), lambda bi, g, qi: (bi, 0, (kvcol0 + j * KV_W) // LANES + g // 2))
    return pl.pallas_call(
        functools.partial(_nsa_kernel, n_cmp=n_cmp),
        grid=(b, NSA_KV_HEADS, s // tq),
        in_specs=[pl.BlockSpec((1, tq, gw), lambda bi, g, qi: (bi, qi, qcol0 // gw + g)),
                  pl.BlockSpec((1, FEAT, lanes), lambda bi, g, qi: (g, 0, 0)),
                  blk(ncp, FEAT), blk(NSA_HD, ncp),
                  pair(0), pair(1), pair(2), pair(3),
                  const(s, FEAT - NSA_HD), const(s + WINDOW, FEAT - NSA_HD),
                  pl.BlockSpec((1, 1, NSA_GRP * 3, tq), lambda bi, g, qi: (bi, g, 0, qi)),
                  const(n_slc, ncp), const(2, tq, lanes)],
        out_specs=pl.BlockSpec((1, tq, gw), lambda bi, g, qi: (bi, qi, g)),
        out_shape=jax.ShapeDtypeStruct((b, s, NSA_W), BF16),
        scratch_shapes=[pltpu.VMEM((s, FEAT), BF16), pltpu.VMEM((V_ROWS, s), BF16),
                        pltpu.VMEM((s + WINDOW, FEAT), BF16), pltpu.VMEM((V_ROWS, s + WINDOW), BF16),
                        pltpu.VMEM((FEAT, lanes), BF16), pltpu.VMEM((n_slc, tq), F32),
                        pltpu.VMEM((2, tk, lanes), F32), pltpu.VMEM((2, tk, lanes), F32),
                        pltpu.VMEM((2, 2, 1, lanes), F32),
                        pltpu.VMEM((2, 1, lanes), F32), pltpu.VMEM((2, V_ROWS, lanes), F32)],
        compiler_params=_cparams(("parallel", "parallel", "arbitrary")),
        name="nsa_attention",
    )(z3, _query_features(tq), kc_f, jnp.swapaxes(vc, 2, 3), z3, z3, z3, z3, feat_s, feat_w, gate_logits, c2s_t,
      bias)


def _merge_kernel(ya_ref, yb_ref, yc_ref, wb_ref, ga_ref, gb_ref, gc_ref, o_ref):
    acc = None
    for i, (y_ref, g_ref) in enumerate(((ya_ref, ga_ref), (yb_ref, gb_ref), (yc_ref, gc_ref))):
        term = jax.nn.sigmoid(g_ref[...].astype(F32)) * jnp.dot(y_ref[...], wb_ref[i], preferred_element_type=F32)
        acc = term if acc is None else acc + term
    o_ref[...] = acc.astype(o_ref.dtype)


def branch_merge(ya, yb, yc, wb, zmerge, tm=1024, tn=512):
    t = ya.shape[0]
    d = wb.shape[2]
    nj = d // tn
    y_spec = pl.BlockSpec((tm, BRANCH_W), lambda i, j: (i, 0))
    gate = lambda br: pl.BlockSpec((tm, tn), lambda i, j: (i, j + br * nj))
    return pl.pallas_call(
        _merge_kernel,
        grid=(t // tm, nj),
        in_specs=[y_spec, y_spec, y_spec,
                  pl.BlockSpec((3, BRANCH_W, tn), lambda i, j: (0, 0, j)),
                  gate(0), gate(1), gate(2)],
        out_specs=pl.BlockSpec((tm, tn), lambda i, j: (i, j)),
        out_shape=jax.ShapeDtypeStruct((t, d), BF16),
        compiler_params=_cparams(("parallel", "parallel")),
        name="branch_merge",
    )(ya, yb, yc, wb, zmerge, zmerge, zmerge)


def _layer_norm(y, g, b):
    mu = jnp.mean(y, axis=-1, keepdims=True)
    yc = y - mu
    var = jnp.mean(yc * yc, axis=-1, keepdims=True)
    return yc * lax.rsqrt(var + LN_EPS) * g + b


def _outproj_ln_kernel(m_ref, w_ref, x_ref, g_ref, b_ref, o_ref, ob_ref, *, alpha):
    mix = jnp.dot(m_ref[...], w_ref[...], preferred_element_type=F32)
    y = _layer_norm(alpha * x_ref[...] + mix, g_ref[...], b_ref[...])
    o_ref[...] = y
    ob_ref[...] = y.astype(BF16)


def outproj_ln(merged, w_out, x, g, b, alpha, tm=256):
    t, d = x.shape
    row = pl.BlockSpec((tm, d), lambda i: (i, 0))
    vec = pl.BlockSpec((1, d), lambda i: (0, 0))
    return pl.pallas_call(
        functools.partial(_outproj_ln_kernel, alpha=alpha),
        grid=(t // tm,),
        in_specs=[row, pl.BlockSpec((d, d), lambda i: (0, 0)), row, vec, vec],
        out_specs=[row, row],
        out_shape=[jax.ShapeDtypeStruct((t, d), F32), jax.ShapeDtypeStruct((t, d), BF16)],
        compiler_params=_cparams(("parallel",)),
        name="outproj_ln",
    )(merged, w_out, x, g.reshape(1, d), b.reshape(1, d))


def _final_kernel(x_ref, xb_ref, wg_ref, p_ref, wp_ref, y0_ref, y1_ref, gate_ref, g_ref, b_ref, o_ref, ob_ref,
                  *, alpha):
    ple = jax.nn.sigmoid(jnp.dot(xb_ref[...], wg_ref[...], preferred_element_type=F32)) * jnp.dot(
        p_ref[...], wp_ref[...], preferred_element_type=F32)
    gate = gate_ref[...]
    ffn = gate[:, 0:1] * y0_ref[...].astype(F32) + gate[:, 1:2] * y1_ref[...].astype(F32)
    y = _layer_norm(alpha * x_ref[...] + ffn + ple, g_ref[...], b_ref[...])
    o_ref[...] = y
    ob_ref[...] = y.astype(BF16)


def ple_combine_ln(x, xb, w_ple_gate, p, w_ple, y0, y1, gate, g, b, alpha, tm=256):
    t, d = x.shape
    pd = p.shape[1]
    row = pl.BlockSpec((tm, d), lambda i: (i, 0))
    vec = pl.BlockSpec((1, d), lambda i: (0, 0))
    return pl.pallas_call(
        functools.partial(_final_kernel, alpha=alpha),
        grid=(t // tm,),
        in_specs=[row, row, pl.BlockSpec((d, d), lambda i: (0, 0)),
                  pl.BlockSpec((tm, pd), lambda i: (i, 0)), pl.BlockSpec((pd, d), lambda i: (0, 0)),
                  row, row, pl.BlockSpec((tm, TOP_K), lambda i: (i, 0)), vec, vec],
        out_specs=[row, row],
        out_shape=[jax.ShapeDtypeStruct((t, d), F32), jax.ShapeDtypeStruct((t, d), BF16)],
        compiler_params=_cparams(("parallel",)),
        name="ple_combine_ln",
    )(x, xb, w_ple_gate, p, w_ple, y0, y1, gate, g.reshape(1, d), b.reshape(1, d))


def _router_kernel(x_ref, wr_ref, br_ref, idx_ref, gate_ref):
    tm = x_ref.shape[0]
    logits = lax.dot_general(wr_ref[...], x_ref[...], (((1,), (1,)), ((), ())),
                             precision=lax.Precision.HIGHEST, preferred_element_type=F32)
    aff = jax.nn.sigmoid(logits)
    sel = aff + br_ref[...]
    scores = []
    for g in range(N_GROUPS):
        v = [sel[g * EXPERTS_PER_GROUP + e:g * EXPERTS_PER_GROUP + e + 1, :] for e in range(EXPERTS_PER_GROUP)]
        best = None
        for a in range(EXPERTS_PER_GROUP):
            for c in range(a + 1, EXPERTS_PER_GROUP):
                pair = v[a] + v[c]
                best = pair if best is None else jnp.maximum(best, pair)
        scores.append(best)
    best_score, best_group = scores[0], jnp.zeros((1, tm), jnp.int32)
    for g in range(1, N_GROUPS):
        better = scores[g] > best_score
        best_score = jnp.where(better, scores[g], best_score)
        best_group = jnp.where(better, g, best_group)
    e_row = lax.broadcasted_iota(jnp.int32, (N_EXPERTS, tm), 0)
    cand = jnp.where(e_row // EXPERTS_PER_GROUP == best_group, sel, NEG)
    picks, gates = [], []
    for _ in range(TOP_K):
        mx = jnp.max(cand, axis=0, keepdims=True)
        pick = jnp.min(jnp.where(cand == mx, e_row, N_EXPERTS), axis=0, keepdims=True)
        hit = e_row == pick
        picks.append(pick)
        gates.append(jnp.sum(jnp.where(hit, aff, 0.0), axis=0, keepdims=True))
        cand = jnp.where(hit, -jnp.inf, cand)
    total = gates[0] + gates[1]
    idx_ref[...] = jnp.concatenate(picks, axis=0)
    gate_ref[...] = jnp.concatenate([gates[0] / total, gates[1] / total], axis=0)


def router(x, w_router, b_router, tm=1024):
    t, d = x.shape
    return pl.pallas_call(
        _router_kernel,
        grid=(t // tm,),
        in_specs=[pl.BlockSpec((tm, d), lambda i: (i, 0)),
                  pl.BlockSpec((N_EXPERTS, d), lambda i: (0, 0)),
                  pl.BlockSpec((N_EXPERTS, 1), lambda i: (0, 0))],
        out_specs=[pl.BlockSpec((TOP_K, tm), lambda i: (0, i)), pl.BlockSpec((TOP_K, tm), lambda i: (0, i))],
        out_shape=[jax.ShapeDtypeStruct((TOP_K, t), jnp.int32), jax.ShapeDtypeStruct((TOP_K, t), F32)],
        compiler_params=_cparams(("parallel",)),
        name="router",
    )(x, w_router.T, b_router.reshape(N_EXPERTS, 1))


def _expert_kernel(be_ref, nu_ref, *refs, part_starts):
    n_parts = len(part_starts)
    x_refs = refs[:n_parts]
    wg_ref, wu_ref, wd_ref = refs[n_parts:n_parts + 3]
    prev_ref = refs[n_parts + 3] if len(refs) == n_parts + 8 else None
    o_ref, wgb_ref, wub_ref, wdb_ref = refs[-4:]
    i = pl.program_id(0)

    @pl.when((i == 0) | (be_ref[i] != be_ref[jnp.maximum(i - 1, 0)]))
    def _():
        wgb_ref[...] = wg_ref[0, 0].astype(BF16)
        wub_ref[...] = wu_ref[0, 0].astype(BF16)
        wdb_ref[...] = wd_ref[0, 0].astype(BF16)

    @pl.when(i < nu_ref[0])
    def _():
        x = x_refs[0][...]
        for start, x_ref in zip(part_starts[1:], x_refs[1:]):
            x = jnp.where(i >= start, x_ref[...], x)
        g = jnp.dot(x, wgb_ref[...], preferred_element_type=F32)
        u = jnp.dot(x, wub_ref[...], preferred_element_type=F32)
        out = jnp.dot((jax.nn.silu(g) * u).astype(BF16), wdb_ref[...], preferred_element_type=F32)
        if prev_ref is not None:
            out = out + prev_ref[...].astype(F32)
        o_ref[...] = out.astype(o_ref.dtype)

    @pl.when(i >= nu_ref[0])
    def _():
        o_ref[...] = jnp.zeros_like(o_ref)


def expert_ffn(block_expert, n_used, x_parts, w_gate_up, w_down, layer):
    d = x_parts[0].shape[1]
    n_rows = sum(x.shape[0] for x in x_parts)
    de = w_down.shape[2]
    ce = de // EXPERT_CHUNKS
    part_blocks = [x.shape[0] // MOE_BLK for x in x_parts]
    part_starts = tuple(int(v) for v in np.cumsum([0] + part_blocks[:-1]))
    part_specs = [pl.BlockSpec((MOE_BLK, d), lambda i, be, nu, o=o, n=n: (jnp.clip(i - o, 0, n - 1), 0))
                  for o, n in zip(part_starts, part_blocks)]
    out = None
    for c in range(EXPERT_CHUNKS):
        row = pl.BlockSpec((MOE_BLK, d), lambda i, be, nu: (i, 0))
        in_specs = part_specs + [
            pl.BlockSpec((1, 1, d, ce), lambda i, be, nu, c=c: (layer, be[i], 0, c)),
            pl.BlockSpec((1, 1, d, ce), lambda i, be, nu, c=c: (layer, be[i], 0, EXPERT_CHUNKS + c)),
            pl.BlockSpec((1, 1, ce, d), lambda i, be, nu, c=c: (layer, be[i], c, 0))]
        args = list(x_parts) + [w_gate_up, w_gate_up, w_down]
        if out is not None:
            in_specs.append(row)
            args.append(out)
        out = pl.pallas_call(
            functools.partial(_expert_kernel, part_starts=part_starts),
            grid_spec=pltpu.PrefetchScalarGridSpec(
                num_scalar_prefetch=2,
                grid=(n_rows // MOE_BLK,),
                in_specs=in_specs,
                out_specs=row,
                scratch_shapes=[pltpu.VMEM((d, ce), BF16), pltpu.VMEM((d, ce), BF16), pltpu.VMEM((ce, d), BF16)]),
            out_shape=jax.ShapeDtypeStruct((n_rows, d), BF16),
            compiler_params=_cparams(("arbitrary",)),
            name="expert_ffn",
        )(block_expert, n_used, *args)
    return out


def _dispatch_tables(expert_idx):
    n_tok = expert_idx.shape[0]
    n_assign = n_tok * TOP_K
    e_flat = expert_idx.reshape(n_assign)
    onehot = e_flat[:, None] == jnp.arange(N_EXPERTS)[None, :]
    chunk = min(1024, n_assign)
    oh = onehot.astype(F32).reshape(n_assign // chunk, chunk, N_EXPERTS)
    within = jnp.einsum("ij,bjk->bik", jnp.tril(jnp.ones((chunk, chunk), F32)), oh)
    totals = within[:, -1, :]
    csum = (within + (jnp.cumsum(totals, axis=0) - totals)[:, None, :]).reshape(n_assign, N_EXPERTS)
    rank = jnp.sum(jnp.where(onehot, csum, 0.0), axis=1).astype(jnp.int32) - 1
    counts = csum[-1].astype(jnp.int32)
    padded = (counts + MOE_BLK - 1) // MOE_BLK * MOE_BLK
    pad_end = jnp.cumsum(padded)
    pad_start = pad_end - padded
    dest = pad_start[e_flat] + rank
    n_rows = n_assign + N_EXPERTS * MOE_BLK
    block_start = jnp.arange(n_rows // MOE_BLK, dtype=jnp.int32) * MOE_BLK
    block_expert = jnp.minimum(jnp.sum(pad_end[None, :] <= block_start[:, None], axis=1), N_EXPERTS - 1)
    order = jnp.sort(e_flat * n_assign + jnp.arange(n_assign, dtype=jnp.int32)) % n_assign
    row_expert = jnp.repeat(block_expert, MOE_BLK)
    row_rank = jnp.arange(n_rows, dtype=jnp.int32) - pad_start[row_expert]
    source = order[jnp.clip((jnp.cumsum(counts) - counts)[row_expert] + row_rank, 0, n_assign - 1)]
    row_tok = jnp.where(row_rank < counts[row_expert], source // TOP_K, 0)
    n_used = (pad_end[-1:] // MOE_BLK).astype(jnp.int32)
    return row_tok.astype(jnp.int32), block_expert.astype(jnp.int32), n_used, dest.reshape(n_tok, TOP_K)


def _token_mixer(xb, bsz, seq, w_in_t, layer, conv_a_w, conv_a_b, lru_conv_w, lru_conv_b, lru_wa, lru_ba, lru_wx, lru_bx,
                 lru_lam, cmp_pos, phi_w1, phi_b1, phi_w2, w_branch):
    d = xb.shape[1]
    n_tok = bsz * seq
    main_w = 3 * BRANCH_W + 2 * BRANCH_W + NSA_W + 6 * KV_W
    ng = 3 * NSA_HEADS
    kv0 = main_w - 6 * KV_W
    zmain = matmul_wres(xb, w_in_t, layer, 0, main_w, BF16, 2048, 768)
    zgate = matmul_wres(xb, w_in_t, layer, main_w, LANES, F32, 2048, LANES)
    zmerge = matmul_wres(xb, w_in_t, layer, main_w + ng, w_in_t.shape[1] - main_w - ng, BF16, 2048, 512)
    z3 = zmain.reshape(bsz, seq, main_w)

    y_a = conv_mixer(z3, conv_a_w, conv_a_b)
    y_b = lru_mixer(z3, 3 * BRANCH_W, lru_conv_w, lru_conv_b, lru_wa, lru_ba, lru_wx, lru_bx, lru_lam)

    kv = z3[:, :, kv0:kv0 + 2 * KV_W].reshape(bsz, seq, 2, NSA_KV_HEADS, NSA_HD).transpose(2, 0, 3, 1, 4)
    nslot = seq // CMP_STRIDE
    slots = kv.reshape(2, bsz, NSA_KV_HEADS, nslot, CMP_STRIDE * NSA_HD)
    cmp = nsa_compress(slots, cmp_pos, phi_w1, phi_b1, phi_w2)
    gate_logits = zgate[:, :ng].reshape(bsz, seq, NSA_KV_HEADS, NSA_GRP * 3).transpose(0, 2, 3, 1)
    y_c = nsa_attention(z3, 5 * BRANCH_W, kv0 + 2 * KV_W, cmp[0], cmp[1], gate_logits)

    flat = lambda y: y.reshape(n_tok, BRANCH_W)
    return branch_merge(flat(y_a), flat(y_b), flat(y_c), w_branch.astype(BF16), zmerge)


def _moe(x, xb, w_router, b_router, w_gate_up, w_down, layer):
    idx_t, gate_t = router(x, w_router, b_router)
    row_tok, block_expert, n_used, dest = _dispatch_tables(idx_t.T)
    n_tok = xb.shape[0]
    x_parts = [xb[row_tok[a:a + n_tok]] for a in range(0, row_tok.shape[0], n_tok)]
    rows = expert_ffn(block_expert, n_used, x_parts, w_gate_up, w_down, layer)
    return rows[dest[:, 0]], rows[dest[:, 1]], gate_t.T


def kernel(x, p, w_in, conv_a_w, conv_a_b, lru_conv_w, lru_conv_b, lru_wa, lru_ba, lru_wx, lru_bx, lru_lam, cmp_pos, phi_w1, phi_b1, phi_w2, w_branch, w_out, ln_g, ln_b, w_router, b_router, w_gate_up, w_down, w_ple, w_ple_gate):
    bsz, seq, d = x.shape
    depth = w_in.shape[0]
    n_tok = bsz * seq
    alpha = (2 * depth) ** 0.25
    xf = x.reshape(n_tok, d)
    xb = xf.astype(BF16)
    w_in_t = jnp.swapaxes(w_in, 1, 2)
    for i in range(depth):
        merged = _token_mixer(xb, bsz, seq, w_in_t, i, conv_a_w[i], conv_a_b[i], lru_conv_w[i], lru_conv_b[i],
                              lru_wa[i], lru_ba[i], lru_wx[i], lru_bx[i], lru_lam[i], cmp_pos[i], phi_w1[i],
                              phi_b1[i], phi_w2[i], w_branch[i])
        xf, xb = outproj_ln(merged, w_out[i].astype(BF16), xf, ln_g[i, 0], ln_b[i, 0], alpha)
        y0, y1, gate = _moe(xf, xb, w_router, b_router, w_gate_up, w_down, i)
        xf, xb = ple_combine_ln(xf, xb, w_ple_gate[i].astype(BF16), p[i].reshape(n_tok, -1).astype(BF16),
                                w_ple[i].astype(BF16), y0, y1, gate, ln_g[i, 1], ln_b[i, 1], alpha)
    return xf.reshape(bsz, seq, d)
```

```python
import functools

import ml_dtypes
import numpy as np
import jax
import jax.numpy as jnp
from jax import lax
from jax.experimental import pallas as pl
from jax.experimental.pallas import tpu as pltpu

F32 = jnp.float32
BF16 = jnp.bfloat16

BRANCH_W = 1024
CONV_K = 3
LRU_HEADS = 8
LRU_HD = BRANCH_W // LRU_HEADS
LRU_CONV_K = 4
LRU_C = 8.0
NSA_HEADS = 16
NSA_KV_HEADS = 4
NSA_GRP = NSA_HEADS // NSA_KV_HEADS
NSA_HD = 64
NSA_W = NSA_HEADS * NSA_HD
KV_W = NSA_KV_HEADS * NSA_HD
CMP_BLK = 32
CMP_STRIDE = 16
CMP_HIDDEN = 128
SLC_BLK = 64
SLC_TOPN = 16
WINDOW = 512
N_EXPERTS = 16
N_GROUPS = 4
EXPERTS_PER_GROUP = N_EXPERTS // N_GROUPS
TOP_K = 2
LN_EPS = 1e-5
NEG = -1e30
FORCE = 1e9

LANES = 128
SUBLANES = 8
VMEM_LIMIT = 56 * 1024 * 1024

FEAT = 256
MASK_OFF = NSA_HD
ALIBI_OFF = 2 * NSA_HD
PAD_OFF = ALIBI_OFF + 6
V_ROWS = NSA_HD + 16
LOG2E = 1.4426950408889634

ATT_TQ = 256
ATT_TK = 512
EXP_CHUNK = 64
MOE_BLK = 256
EXPERT_CHUNKS = 2


def _cparams(sem):
    return pltpu.CompilerParams(dimension_semantics=sem, vmem_limit_bytes=VMEM_LIMIT)


def _mm_wres_kernel(x_ref, w_ref, *rest, shift):
    o_ref, wb_ref = rest[-2:]

    @pl.when(pl.program_id(1) == 0)
    def _():
        w = w_ref[0]
        if shift:
            tn = w.shape[0]
            w = jnp.concatenate([w, rest[0][0]], axis=0)[shift:shift + tn]
        wb_ref[...] = w.astype(BF16)

    o_ref[...] = lax.dot_general(x_ref[...], wb_ref[...], (((1,), (1,)), ((), ())),
                                 preferred_element_type=F32).astype(o_ref.dtype)


def matmul_wres(x, wt3, layer, row0, n, out_dtype, tm, tn):
    m, k = x.shape
    shift = row0 % tn
    base = row0 - shift
    assert m % tm == 0 and n % tn == 0 and shift % SUBLANES == 0 and shift <= LANES and tn % LANES == 0
    in_specs = [pl.BlockSpec((tm, k), lambda j, i: (i, 0)),
                pl.BlockSpec((1, tn, k), lambda j, i: (layer, base // tn + j, 0))]
    args = [x, wt3]
    if shift:
        in_specs.append(pl.BlockSpec((1, LANES, k), lambda j, i: (layer, (base + (j + 1) * tn) // LANES, 0)))
        args.append(wt3)
    return pl.pallas_call(
        functools.partial(_mm_wres_kernel, shift=shift),
        grid=(n // tn, m // tm),
        in_specs=in_specs,
        out_specs=pl.BlockSpec((tm, tn), lambda j, i: (i, j)),
        out_shape=jax.ShapeDtypeStruct((m, n), out_dtype),
        scratch_shapes=[pltpu.VMEM((tn, k), BF16)],
        compiler_params=_cparams(("parallel", "arbitrary")),
        name="matmul_wres",
    )(*args)


def _causal_conv(ext, cur, w, k):
    acc = cur * w[k - 1:k, :]
    for j in range(k - 1):
        shift = k - 1 - j
        acc = acc + pltpu.roll(ext, shift, 0)[SUBLANES:, :] * w[j:j + 1, :]
    return acc


def _conv_mixer_kernel(ain_ref, ab_ref, ac_ref, w_ref, b_ref, o_ref, halo_ref):
    @pl.when(pl.program_id(2) == 0)
    def _():
        halo_ref[...] = jnp.zeros_like(halo_ref)

    v = ac_ref[0].astype(F32) * ain_ref[0].astype(F32)
    ext = jnp.concatenate([halo_ref[...], v], axis=0)
    y = _causal_conv(ext, v, w_ref[...], CONV_K) + b_ref[...]
    o_ref[0] = (ab_ref[0].astype(F32) * y).astype(o_ref.dtype)
    halo_ref[...] = v[v.shape[0] - SUBLANES:, :]


def conv_mixer(z3, conv_w, conv_b, ts=512, cw=512):
    b, s, _ = z3.shape
    nc = BRANCH_W // cw
    return pl.pallas_call(
        _conv_mixer_kernel,
        grid=(b, nc, s // ts),
        in_specs=[pl.BlockSpec((1, ts, cw), lambda bi, c, si: (bi, si, c)),
                  pl.BlockSpec((1, ts, cw), lambda bi, c, si: (bi, si, c + nc)),
                  pl.BlockSpec((1, ts, cw), lambda bi, c, si: (bi, si, c + 2 * nc)),
                  pl.BlockSpec((CONV_K, cw), lambda bi, c, si: (0, c)),
                  pl.BlockSpec((1, cw), lambda bi, c, si: (0, c))],
        out_specs=pl.BlockSpec((1, ts, cw), lambda bi, c, si: (bi, si, c)),
        out_shape=jax.ShapeDtypeStruct((b, s, BRANCH_W), BF16),
        scratch_shapes=[pltpu.VMEM((SUBLANES, cw), F32)],
        compiler_params=_cparams(("parallel", "parallel", "arbitrary")),
        name="conv_mixer",
    )(z3, z3, z3, conv_w, conv_b.reshape(1, BRANCH_W))


def _lru_kernel(gate_ref, rin_ref, cw_ref, cb_ref, wa_ref, ba_ref, wx_ref, bx_ref, lam_ref, o_ref,
                halo_ref, carry_ref, a_ref, b_ref, h_ref):
    ts, lw = a_ref.shape

    @pl.when(pl.program_id(2) == 0)
    def _():
        halo_ref[...] = jnp.zeros_like(halo_ref)
        carry_ref[...] = jnp.zeros_like(carry_ref)

    u = rin_ref[0].astype(F32)
    ext = jnp.concatenate([halo_ref[...], u], axis=0)
    xc = _causal_conv(ext, u, cw_ref[...], LRU_CONV_K) + cb_ref[...]
    halo_ref[...] = u[ts - SUBLANES:, :]

    xcb = xc.astype(BF16)
    ra, ix = [], []
    for hh in range(lw // LRU_HD):
        xh = xcb[:, hh * LRU_HD:(hh + 1) * LRU_HD]
        ra.append(jnp.dot(xh, wa_ref[hh].astype(BF16), preferred_element_type=F32))
        ix.append(jnp.dot(xh, wx_ref[hh].astype(BF16), preferred_element_type=F32))
    r = jax.nn.sigmoid(jnp.concatenate(ra, axis=1) + ba_ref[...])
    i = jax.nn.sigmoid(jnp.concatenate(ix, axis=1) + bx_ref[...])
    neg_lam = -lam_ref[...]
    softplus = jnp.maximum(neg_lam, 0.0) + jnp.log1p(jnp.exp(-jnp.abs(neg_lam)))
    log_a = -LRU_C * r * softplus
    a_ref[...] = jnp.exp(log_a)
    th = jnp.tanh(log_a)
    b_ref[...] = jnp.sqrt(-2.0 * th / (1.0 - th)) * (i * xc)

    row = lax.broadcasted_iota(jnp.int32, (SUBLANES, lw), 0)

    def group(g, hc):
        sl = pl.ds(pl.multiple_of(g * SUBLANES, SUBLANES), SUBLANES)
        a = a_ref[sl, :]
        bv = b_ref[sl, :]
        for d in (1, 2, 4):
            keep = row >= d
            bv = jnp.where(keep, a * pltpu.roll(bv, d, 0) + bv, bv)
            a = jnp.where(keep, a * pltpu.roll(a, d, 0), a)
        h = a * hc + bv
        h_ref[sl, :] = h
        return jnp.broadcast_to(h[SUBLANES - 1:SUBLANES, :], (SUBLANES, lw))

    carry_ref[...] = lax.fori_loop(0, ts // SUBLANES, group, carry_ref[...])
    o_ref[0] = (h_ref[...] * jax.nn.gelu(gate_ref[0].astype(F32))).astype(o_ref.dtype)


def lru_mixer(z3, col0, conv_w, conv_b, wa, ba, wx, bx, lam, ts=512, lw=256):
    b, s, _ = z3.shape
    nc = BRANCH_W // lw
    c0 = col0 // lw
    hpb = lw // LRU_HD
    row = lambda v: v.reshape(1, BRANCH_W)
    vec = pl.BlockSpec((1, lw), lambda bi, c, si: (0, c))
    mat = pl.BlockSpec((hpb, LRU_HD, LRU_HD), lambda bi, c, si: (c, 0, 0))
    return pl.pallas_call(
        _lru_kernel,
        grid=(b, nc, s // ts),
        in_specs=[pl.BlockSpec((1, ts, lw), lambda bi, c, si: (bi, si, c0 + c)),
                  pl.BlockSpec((1, ts, lw), lambda bi, c, si: (bi, si, c0 + nc + c)),
                  pl.BlockSpec((LRU_CONV_K, lw), lambda bi, c, si: (0, c)),
                  vec, mat, vec, mat, vec, vec],
        out_specs=pl.BlockSpec((1, ts, lw), lambda bi, c, si: (bi, si, c)),
        out_shape=jax.ShapeDtypeStruct((b, s, BRANCH_W), BF16),
        scratch_shapes=[pltpu.VMEM((SUBLANES, lw), F32), pltpu.VMEM((SUBLANES, lw), F32),
                        pltpu.VMEM((ts, lw), F32), pltpu.VMEM((ts, lw), F32), pltpu.VMEM((ts, lw), F32)],
        compiler_params=_cparams(("parallel", "parallel", "arbitrary")),
        name="rg_lru",
    )(z3, z3, conv_w, row(conv_b), wa, row(ba), wx, row(bx), row(lam))


def _compress_kernel(slots_ref, pos_ref, w1_ref, b1_ref, w2_ref, o_ref):
    g, nslot, half = slots_ref.shape[2:]
    rows = g * nslot
    x = slots_ref[0, 0].reshape(rows, half)
    w1 = w1_ref[0].astype(BF16)
    first = jnp.dot(x, w1[:half], preferred_element_type=F32)
    second = jnp.dot(x, w1[half:], preferred_element_type=F32)
    pos = jnp.broadcast_to(pos_ref[0], (SUBLANES, 2 * half)).astype(BF16)
    posb = jnp.dot(pos, w1, preferred_element_type=F32)[0:1, :]
    hidden = first + pltpu.roll(second, rows - 1, 0) + posb + b1_ref[0]
    out = jnp.dot(jax.nn.gelu(hidden).astype(BF16), w2_ref[0].astype(BF16), preferred_element_type=F32)
    o_ref[0, 0] = out.reshape(g, nslot, NSA_HD).astype(o_ref.dtype)


def nsa_compress(slots, cmp_pos, w1, b1, w2):
    _, b, g, nslot, half = slots.shape
    pos = cmp_pos.reshape(2, 1, CMP_BLK * NSA_HD)
    return pl.pallas_call(
        _compress_kernel,
        grid=(2, b),
        in_specs=[pl.BlockSpec((1, 1, g, nslot, half), lambda kv, bi: (kv, bi, 0, 0, 0)),
                  pl.BlockSpec((1, 1, 2 * half), lambda kv, bi: (kv, 0, 0)),
                  pl.BlockSpec((1, 2 * half, CMP_HIDDEN), lambda kv, bi: (kv, 0, 0)),
                  pl.BlockSpec((1, 1, CMP_HIDDEN), lambda kv, bi: (kv, 0, 0)),
                  pl.BlockSpec((1, CMP_HIDDEN, NSA_HD), lambda kv, bi: (kv, 0, 0))],
        out_specs=pl.BlockSpec((1, 1, g, nslot, NSA_HD), lambda kv, bi: (kv, bi, 0, 0, 0)),
        out_shape=jax.ShapeDtypeStruct((2, b, g, nslot, NSA_HD), BF16),
        compiler_params=_cparams(("parallel", "parallel")),
        name="nsa_compress",
    )(slots, pos, w1, b1.reshape(2, 1, CMP_HIDDEN), w2)


def _exp2_bf16(s, m):
    rows = s.shape[0]
    chunk = min(EXP_CHUNK, rows)
    return jnp.concatenate([jnp.exp2((s[r:r + chunk, :] - m).astype(BF16)) for r in range(0, rows, chunk)], axis=0)


def _nsa_kernel(q_ref, qf_ref, kc_ref, vc_ref, ksb_ref, vsb_ref, kwb_ref, vwb_ref, fs_ref, fw_ref, gl_ref, c2s_ref,
                bias_ref, o_ref, ks_ref, vs_ref, kw_ref, vw_ref, qp_ref, imp_ref, s0_ref, s1_ref, mx_ref, m_ref, acc_ref,
                *, n_cmp):
    tq = q_ref.shape[1]
    lanes = NSA_GRP * tq
    seq = ks_ref.shape[0]
    n_slc = seq // SLC_BLK
    t0 = pl.multiple_of(pl.program_id(2) * tq, tq)
    head = lambda r: slice(r * tq, (r + 1) * tq)

    @pl.when(pl.program_id(2) == 0)
    def _():
        odd = pl.program_id(1) % 2 == 1

        def keys(blk_ref):
            blk = blk_ref[0].astype(F32)
            return jnp.where(odd, pltpu.roll(blk, NSA_HD, 1), blk)[:, :NSA_HD].astype(BF16)

        def values_t(blk_ref):
            vt = blk_ref[0].astype(F32).T
            return jnp.where(odd, vt[NSA_HD:], vt[:NSA_HD]).astype(BF16)

        def ones_row(n):
            first = lax.broadcasted_iota(jnp.int32, (V_ROWS - NSA_HD, n), 0) == 0
            return jnp.where(first, 1.0, 0.0).astype(BF16)

        ks_ref[:, 0:NSA_HD] = keys(ksb_ref)
        ks_ref[:, NSA_HD:] = fs_ref[...]
        vs_ref[0:NSA_HD, :] = values_t(vsb_ref)
        vs_ref[NSA_HD:, :] = ones_row(seq)
        kw_ref[0:WINDOW, 0:NSA_HD] = jnp.zeros((WINDOW, NSA_HD), BF16)
        kw_ref[WINDOW:, 0:NSA_HD] = keys(kwb_ref)
        kw_ref[:, NSA_HD:] = fw_ref[...]
        vw_ref[0:NSA_HD, 0:WINDOW] = jnp.zeros((NSA_HD, WINDOW), BF16)
        vw_ref[0:NSA_HD, WINDOW:] = values_t(vwb_ref)
        vw_ref[NSA_HD:, :] = ones_row(seq + WINDOW)

    def scores(k_tile, bias=None):
        s = jnp.dot(k_tile, qp_ref[...], preferred_element_type=F32)
        return s if bias is None else s + bias

    qt = q_ref[0].astype(F32).T * (NSA_HD ** -0.5 * LOG2E)
    qp_ref[...] = qf_ref[0]
    for r in range(NSA_GRP):
        qp_ref[0:NSA_HD, head(r)] = qt[r * NSA_HD:(r + 1) * NSA_HD, :].astype(BF16)

    s_cmp = scores(kc_ref[0, 0])
    win = []
    for off, size, bias in ((0, tq, bias_ref[1]), (tq, WINDOW - tq, None), (WINDOW, tq, bias_ref[0])):
        k0 = pl.multiple_of(t0 + off, tq)
        win.append((scores(kw_ref[pl.ds(k0, size), :], bias), vw_ref[:, pl.ds(k0, size)]))

    ncp = kc_ref.shape[2]
    n_row = lax.broadcasted_iota(jnp.int32, (ncp, lanes), 0)
    t_lane = t0 + lax.broadcasted_iota(jnp.int32, (ncp, lanes), 1) % tq
    ok = (t_lane >= n_row * CMP_STRIDE + (CMP_BLK - 1)) & (n_row < n_cmp)
    s = jnp.where(ok, s_cmp, NEG)
    p = jnp.where(ok, jnp.exp2(s - jnp.max(s, axis=0, keepdims=True)), 0.0)
    l = jnp.sum(p, axis=0, keepdims=True)
    p_cmp = p * (1.0 / jnp.where(l > 0.0, l, 1.0))
    o_cmp = jnp.dot(vc_ref[0, 0], p_cmp.astype(BF16), preferred_element_type=F32)

    rest = p_cmp[:, head(0)]
    for r in range(1, NSA_GRP):
        rest = rest + p_cmp[:, head(r)]
    imp = jnp.zeros((n_slc, tq), F32)
    for _ in range(3):
        part = rest.astype(BF16)
        rest = rest - part.astype(F32)
        imp = imp + jnp.dot(c2s_ref[...], part, preferred_element_type=F32)
    j_row = lax.broadcasted_iota(jnp.int32, (n_slc, tq), 0)
    t_col = t0 + lax.broadcasted_iota(jnp.int32, (n_slc, tq), 1)
    forced = (j_row == 0) | (j_row == t_col // SLC_BLK)
    causal = j_row * SLC_BLK <= t_col
    imp = jnp.where(forced, FORCE, jnp.where(causal, imp, -FORCE))
    imp_ref[...] = imp

    n_grp = n_slc // SUBLANES
    vals = [imp[v * SUBLANES:(v + 1) * SUBLANES, :] for v in range(n_grp)]
    sub = lax.broadcasted_iota(jnp.int32, (SUBLANES, tq), 0)
    rank = [jnp.zeros((SUBLANES, tq), jnp.int32) for _ in range(n_grp)]
    for i in range(n_slc):
        bi = jnp.broadcast_to(imp_ref[i:i + 1, :], (SUBLANES, tq))
        for v in range(n_grp):
            if v < i // SUBLANES:
                ahead = bi > vals[v]
            elif v > i // SUBLANES:
                ahead = bi >= vals[v]
            else:
                ahead = (bi > vals[v]) | ((bi == vals[v]) & (sub > i % SUBLANES))
            rank[v] = rank[v] + jnp.where(ahead, 1, 0)
    chosen = (jnp.concatenate(rank, axis=0) < min(SLC_TOPN, n_slc)) & causal
    mask = jnp.where(chosen, 0.0, NEG).astype(BF16)
    for r in range(NSA_GRP):
        qp_ref[MASK_OFF:MASK_OFF + n_slc, head(r)] = mask

    tk = min(ATT_TK, seq)
    n_full = t0 // tk
    last = n_full // 2

    s_bufs = (s0_ref, s1_ref)

    def pair_scores(step, buf):
        for par in range(2):
            k0 = pl.multiple_of((2 * step + par) * tk, tk)
            s = scores(ks_ref[pl.ds(k0, tk), :])
            s_bufs[buf][par] = s
            mx_ref[buf, par] = jnp.max(s, axis=0, keepdims=True)

    def pair_update(step, buf, prefetch):
        m_new = [jnp.maximum(m_ref[par], mx_ref[buf, par]) for par in range(2)]
        probs = [_exp2_bf16(s_bufs[buf].at[par], m_new[par]) for par in range(2)]
        if prefetch:
            pair_scores(step + 1, 1 - buf)
        for par in range(2):
            k0 = pl.multiple_of((2 * step + par) * tk, tk)
            pv = jnp.dot(vs_ref[:, pl.ds(k0, tk)], probs[par], preferred_element_type=F32)
            acc_ref[par] = jnp.exp2(m_ref[par] - m_new[par]) * acc_ref[par] + pv
            m_ref[par] = m_new[par]

    def mask_diagonal(buf):
        off = pl.multiple_of(t0 - n_full * tk, tq)
        for par in range(2):
            @pl.when(n_full % 2 == par)
            def _():
                tile = s_bufs[buf].at[par]
                tile[pl.ds(off, tq), :] = tile[pl.ds(off, tq), :] + bias_ref[0]
                mx_ref[buf, par] = jnp.max(tile[...], axis=0, keepdims=True)

    def half_trip(step, buf):
        @pl.when(step == last)
        def _():
            mask_diagonal(buf)

        @pl.when(step < last)
        def _():
            pair_update(step, buf, prefetch=True)

        @pl.when(step == last)
        def _():
            pair_update(step, buf, prefetch=False)

    def trip(i, carry):
        half_trip(2 * i, 0)
        half_trip(2 * i + 1, 1)
        return carry

    m_ref[...] = jnp.full(m_ref.shape, NEG, F32)
    acc_ref[...] = jnp.zeros(acc_ref.shape, F32)
    pair_scores(0, 0)

    mx = functools.reduce(jnp.maximum, [jnp.max(s, axis=0, keepdims=True) for s, _ in win])
    o_win = jnp.zeros((V_ROWS, lanes), F32)
    for s, v_tile in win:
        o_win = o_win + jnp.dot(v_tile, _exp2_bf16(s, mx), preferred_element_type=F32)
    o_win = o_win[0:NSA_HD] * (1.0 / o_win[NSA_HD:NSA_HD + 1])

    lax.fori_loop(0, last // 2 + 1, trip, 0)
    m_all = jnp.maximum(m_ref[0], m_ref[1])
    o_slc = jnp.exp2(m_ref[0] - m_all) * acc_ref[0] + jnp.exp2(m_ref[1] - m_all) * acc_ref[1]
    o_slc = o_slc[0:NSA_HD] * (1.0 / o_slc[NSA_HD:NSA_HD + 1])

    gates = jax.nn.sigmoid(gl_ref[0, 0].astype(F32))
    outs = []
    for r in range(NSA_GRP):
        outs.append(gates[3 * r:3 * r + 1] * o_cmp[:, head(r)] + gates[3 * r + 1:3 * r + 2] * o_slc[:, head(r)]
                    + gates[3 * r + 2:3 * r + 3] * o_win[:, head(r)])
    o_ref[0] = jnp.concatenate(outs, axis=0).T.astype(o_ref.dtype)


def _alibi_slopes():
    return (2.0 ** (-8.0 * np.arange(1, NSA_HEADS + 1) / NSA_HEADS)).astype(np.float32)


def _bf16_split3(v):
    parts, rest = [], np.asarray(v, np.float32)
    for _ in range(3):
        part = rest.astype(ml_dtypes.bfloat16).astype(np.float32)
        parts.append(part)
        rest = (rest.astype(np.float64) - part.astype(np.float64)).astype(np.float32)
    return parts


def _query_features(tq):
    feat = np.zeros((NSA_HEADS, FEAT), np.float32)
    hi, mid, lo = _bf16_split3(_alibi_slopes() * np.float32(LOG2E))
    for c, part in enumerate((hi, mid, lo, hi, mid, lo)):
        feat[:, ALIBI_OFF + c] = part
    feat[:, PAD_OFF] = NEG
    feat = np.repeat(feat.reshape(NSA_KV_HEADS, NSA_GRP, FEAT), tq, axis=1)
    return jnp.asarray(feat.transpose(0, 2, 1), BF16)


def _key_features(pos, n_slc):
    pos = np.asarray(pos)
    real = pos >= 0
    feat = np.zeros((pos.shape[0], FEAT - NSA_HD), np.float32)
    if n_slc:
        feat[real, pos[real] // SLC_BLK] = 1.0
    a0 = ALIBI_OFF - NSA_HD
    feat[real, a0:a0 + 3] = ((pos[real] // SLC_BLK) * SLC_BLK)[:, None]
    feat[real, a0 + 3:a0 + 6] = (pos[real] % SLC_BLK)[:, None]
    feat[~real, PAD_OFF - NSA_HD] = 1.0
    return feat


def _with_features(k, feat):
    b, g, _, _ = k.shape
    f = jnp.broadcast_to(jnp.asarray(feat, BF16), (b, g) + feat.shape)
    return jnp.concatenate([k.astype(BF16), f], axis=3)


def nsa_attention(z3, qcol0, kvcol0, kc, vc, gate_logits):
    b, s, _ = z3.shape
    tq = ATT_TQ
    tk = min(ATT_TK, s)
    assert s % (2 * tk) == 0 and tk % tq == 0 and tq % SLC_BLK == 0 and WINDOW % tq == 0 and WINDOW > tq
    n_slc = s // SLC_BLK
    assert n_slc <= ALIBI_OFF - MASK_OFF and n_slc % SUBLANES == 0
    assert 2 * NSA_HD == LANES and NSA_KV_HEADS % 2 == 0 and kvcol0 % LANES == 0
    n_cmp = (s - CMP_BLK) // CMP_STRIDE + 1
    ncp = kc.shape[2]
    gw = NSA_GRP * NSA_HD
    lanes = NSA_GRP * tq

    kc_f = _with_features(kc, _key_features(np.arange(ncp) * CMP_STRIDE + CMP_BLK - 1, 0))
    feat_s = jnp.asarray(_key_features(np.arange(s), n_slc), BF16)
    feat_w = jnp.asarray(_key_features(np.arange(-WINDOW, s), 0), BF16)

    c_start = np.arange(ncp) * CMP_STRIDE
    s_start = np.arange(n_slc) * SLC_BLK
    overlap = np.clip(np.minimum(c_start[:, None] + CMP_BLK, s_start[None, :] + SLC_BLK)
                      - np.maximum(c_start[:, None], s_start[None, :]), 0, None) / CMP_BLK
    overlap[n_cmp:] = 0.0
    c2s_t = jnp.asarray(overlap.T, BF16)

    key = np.arange(tq)[:, None]
    qry = np.tile(np.arange(tq), NSA_GRP)[None, :]
    bias = jnp.asarray(np.stack([np.where(key <= qry, 0.0, NEG), np.where(key > qry, 0.0, NEG)]), F32)

    blk = lambda *shape: pl.BlockSpec((1, 1) + shape, lambda bi, g, qi: (bi, g, 0, 0))
    const = lambda *shape: pl.BlockSpec(shape, lambda bi, g, qi: (0,) * len(shape))
    pair = lambda j: pl.BlockSpec((1, s, LANES), lambda bi, g, qi: (bi, 0, (kvcol0 + j * KV_W) // LANES + g // 2))
    return pl.pallas_call(
        functools.partial(_nsa_kernel, n_cmp=n_cmp),
        grid=(b, NSA_KV_HEADS, s // tq),
        in_specs=[pl.BlockSpec((1, tq, gw), lambda bi, g, qi: (bi, qi, qcol0 // gw + g)),
                  pl.BlockSpec((1, FEAT, lanes), lambda bi, g, qi: (g, 0, 0)),
                  blk(ncp, FEAT), blk(NSA_HD, ncp),
                  pair(0), pair(1), pair(2), pair(3),
                  const(s, FEAT - NSA_HD), const(s + WINDOW, FEAT - NSA_HD),
                  pl.BlockSpec((1, 1, NSA_GRP * 3, tq), lambda bi, g, qi: (bi, g, 0, qi)),
                  const(n_slc, ncp), const(2, tq, lanes)],
        out_specs=pl.BlockSpec((1, tq, gw), lambda bi, g, qi: (bi, qi, g)),
        out_shape=jax.ShapeDtypeStruct((b, s, NSA_W), BF16),
        scratch_shapes=[pltpu.VMEM((s, FEAT), BF16), pltpu.VMEM((V_ROWS, s), BF16),
                        pltpu.VMEM((s + WINDOW, FEAT), BF16), pltpu.VMEM((V_ROWS, s + WINDOW), BF16),
                        pltpu.VMEM((FEAT, lanes), BF16), pltpu.VMEM((n_slc, tq), F32),
                        pltpu.VMEM((2, tk, lanes), F32), pltpu.VMEM((2, tk, lanes), F32),
                        pltpu.VMEM((2, 2, 1, lanes), F32),
                        pltpu.VMEM((2, 1, lanes), F32), pltpu.VMEM((2, V_ROWS, lanes), F32)],
        compiler_params=_cparams(("parallel", "parallel", "arbitrary")),
        name="nsa_attention",
    )(z3, _query_features(tq), kc_f, jnp.swapaxes(vc, 2, 3), z3, z3, z3, z3, feat_s, feat_w, gate_logits, c2s_t,
      bias)


def _merge_kernel(ya_ref, yb_ref, yc_ref, wb_ref, ga_ref, gb_ref, gc_ref, o_ref):
    acc = None
    for i, (y_ref, g_ref) in enumerate(((ya_ref, ga_ref), (yb_ref, gb_ref), (yc_ref, gc_ref))):
        term = jax.nn.sigmoid(g_ref[...].astype(F32)) * jnp.dot(y_ref[...], wb_ref[i], preferred_element_type=F32)
        acc = term if acc is None else acc + term
    o_ref[...] = acc.astype(o_ref.dtype)


def branch_merge(ya, yb, yc, wb, zmerge, tm=1024, tn=512):
    t = ya.shape[0]
    d = wb.shape[2]
    nj = d // tn
    y_spec = pl.BlockSpec((tm, BRANCH_W), lambda i, j: (i, 0))
    gate = lambda br: pl.BlockSpec((tm, tn), lambda i, j: (i, j + br * nj))
    return pl.pallas_call(
        _merge_kernel,
        grid=(t // tm, nj),
        in_specs=[y_spec, y_spec, y_spec,
                  pl.BlockSpec((3, BRANCH_W, tn), lambda i, j: (0, 0, j)),
                  gate(0), gate(1), gate(2)],
        out_specs=pl.BlockSpec((tm, tn), lambda i, j: (i, j)),
        out_shape=jax.ShapeDtypeStruct((t, d), BF16),
        compiler_params=_cparams(("parallel", "parallel")),
        name="branch_merge",
    )(ya, yb, yc, wb, zmerge, zmerge, zmerge)


def _layer_norm(y, g, b):
    mu = jnp.mean(y, axis=-1, keepdims=True)
    yc = y - mu
    var = jnp.mean(yc * yc, axis=-1, keepdims=True)
    return yc * lax.rsqrt(var + LN_EPS) * g + b


def _outproj_ln_kernel(m_ref, w_ref, x_ref, g_ref, b_ref, o_ref, ob_ref, *, alpha):
    mix = jnp.dot(m_ref[...], w_ref[...], preferred_element_type=F32)
    y = _layer_norm(alpha * x_ref[...] + mix, g_ref[...], b_ref[...])
    o_ref[...] = y
    ob_ref[...] = y.astype(BF16)


def outproj_ln(merged, w_out, x, g, b, alpha, tm=256):
    t, d = x.shape
    row = pl.BlockSpec((tm, d), lambda i: (i, 0))
    vec = pl.BlockSpec((1, d), lambda i: (0, 0))
    return pl.pallas_call(
        functools.partial(_outproj_ln_kernel, alpha=alpha),
        grid=(t // tm,),
        in_specs=[row, pl.BlockSpec((d, d), lambda i: (0, 0)), row, vec, vec],
        out_specs=[row, row],
        out_shape=[jax.ShapeDtypeStruct((t, d), F32), jax.ShapeDtypeStruct((t, d), BF16)],
        compiler_params=_cparams(("parallel",)),
        name="outproj_ln",
    )(merged, w_out, x, g.reshape(1, d), b.reshape(1, d))


def _final_kernel(x_ref, xb_ref, wg_ref, p_ref, wp_ref, y0_ref, y1_ref, gate_ref, g_ref, b_ref, o_ref, ob_ref,
                  *, alpha):
    ple = jax.nn.sigmoid(jnp.dot(xb_ref[...], wg_ref[...], preferred_element_type=F32)) * jnp.dot(
        p_ref[...], wp_ref[...], preferred_element_type=F32)
    gate = gate_ref[...]
    ffn = gate[:, 0:1] * y0_ref[...].astype(F32) + gate[:, 1:2] * y1_ref[...].astype(F32)
    y = _layer_norm(alpha * x_ref[...] + ffn + ple, g_ref[...], b_ref[...])
    o_ref[...] = y
    ob_ref[...] = y.astype(BF16)


def ple_combine_ln(x, xb, w_ple_gate, p, w_ple, y0, y1, gate, g, b, alpha, tm=256):
    t, d = x.shape
    pd = p.shape[1]
    row = pl.BlockSpec((tm, d), lambda i: (i, 0))
    vec = pl.BlockSpec((1, d), lambda i: (0, 0))
    return pl.pallas_call(
        functools.partial(_final_kernel, alpha=alpha),
        grid=(t // tm,),
        in_specs=[row, row, pl.BlockSpec((d, d), lambda i: (0, 0)),
                  pl.BlockSpec((tm, pd), lambda i: (i, 0)), pl.BlockSpec((pd, d), lambda i: (0, 0)),
                  row, row, pl.BlockSpec((tm, TOP_K), lambda i: (i, 0)), vec, vec],
        out_specs=[row, row],
        out_shape=[jax.ShapeDtypeStruct((t, d), F32), jax.ShapeDtypeStruct((t, d), BF16)],
        compiler_params=_cparams(("parallel",)),
        name="ple_combine_ln",
    )(x, xb, w_ple_gate, p, w_ple, y0, y1, gate, g.reshape(1, d), b.reshape(1, d))


def _router_kernel(x_ref, wr_ref, br_ref, idx_ref, gate_ref):
    tm = x_ref.shape[0]
    logits = lax.dot_general(wr_ref[...], x_ref[...], (((1,), (1,)), ((), ())),
                             precision=lax.Precision.HIGHEST, preferred_element_type=F32)
    aff = jax.nn.sigmoid(logits)
    sel = aff + br_ref[...]
    scores = []
    for g in range(N_GROUPS):
        v = [sel[g * EXPERTS_PER_GROUP + e:g * EXPERTS_PER_GROUP + e + 1, :] for e in range(EXPERTS_PER_GROUP)]
        best = None
        for a in range(EXPERTS_PER_GROUP):
            for c in range(a + 1, EXPERTS_PER_GROUP):
                pair = v[a] + v[c]
                best = pair if best is None else jnp.maximum(best, pair)
        scores.append(best)
    best_score, best_group = scores[0], jnp.zeros((1, tm), jnp.int32)
    for g in range(1, N_GROUPS):
        better = scores[g] > best_score
        best_score = jnp.where(better, scores[g], best_score)
        best_group = jnp.where(better, g, best_group)
    e_row = lax.broadcasted_iota(jnp.int32, (N_EXPERTS, tm), 0)
    cand = jnp.where(e_row // EXPERTS_PER_GROUP == best_group, sel, NEG)
    picks, gates = [], []
    for _ in range(TOP_K):
        mx = jnp.max(cand, axis=0, keepdims=True)
        pick = jnp.min(jnp.where(cand == mx, e_row, N_EXPERTS), axis=0, keepdims=True)
        hit = e_row == pick
        picks.append(pick)
        gates.append(jnp.sum(jnp.where(hit, aff, 0.0), axis=0, keepdims=True))
        cand = jnp.where(hit, -jnp.inf, cand)
    total = gates[0] + gates[1]
    idx_ref[...] = jnp.concatenate(picks, axis=0)
    gate_ref[...] = jnp.concatenate([gates[0] / total, gates[1] / total], axis=0)


def router(x, w_router, b_router, tm=1024):
    t, d = x.shape
    return pl.pallas_call(
        _router_kernel,
        grid=(t // tm,),
        in_specs=[pl.BlockSpec((tm, d), lambda i: (i, 0)),
                  pl.BlockSpec((N_EXPERTS, d), lambda i: (0, 0)),
                  pl.BlockSpec((N_EXPERTS, 1), lambda i: (0, 0))],
        out_specs=[pl.BlockSpec((TOP_K, tm), lambda i: (0, i)), pl.BlockSpec((TOP_K, tm), lambda i: (0, i))],
        out_shape=[jax.ShapeDtypeStruct((TOP_K, t), jnp.int32), jax.ShapeDtypeStruct((TOP_K, t), F32)],
        compiler_params=_cparams(("parallel",)),
        name="router",
    )(x, w_router.T, b_router.reshape(N_EXPERTS, 1))


def _expert_kernel(be_ref, nu_ref, *refs, part_starts):
    n_parts = len(part_starts)
    x_refs = refs[:n_parts]
    wg_ref, wu_ref, wd_ref = refs[n_parts:n_parts + 3]
    prev_ref = refs[n_parts + 3] if len(refs) == n_parts + 8 else None
    o_ref, wgb_ref, wub_ref, wdb_ref = refs[-4:]
    i = pl.program_id(0)

    @pl.when((i == 0) | (be_ref[i] != be_ref[jnp.maximum(i - 1, 0)]))
    def _():
        wgb_ref[...] = wg_ref[0, 0].astype(BF16)
        wub_ref[...] = wu_ref[0, 0].astype(BF16)
        wdb_ref[...] = wd_ref[0, 0].astype(BF16)

    @pl.when(i < nu_ref[0])
    def _():
        x = x_refs[0][...]
        for start, x_ref in zip(part_starts[1:], x_refs[1:]):
            x = jnp.where(i >= start, x_ref[...], x)
        g = jnp.dot(x, wgb_ref[...], preferred_element_type=F32)
        u = jnp.dot(x, wub_ref[...], preferred_element_type=F32)
        out = jnp.dot((jax.nn.silu(g) * u).astype(BF16), wdb_ref[...], preferred_element_type=F32)
        if prev_ref is not None:
            out = out + prev_ref[...].astype(F32)
        o_ref[...] = out.astype(o_ref.dtype)

    @pl.when(i >= nu_ref[0])
    def _():
        o_ref[...] = jnp.zeros_like(o_ref)


def expert_ffn(block_expert, n_used, x_parts, w_gate_up, w_down, layer):
    d = x_parts[0].shape[1]
    n_rows = sum(x.shape[0] for x in x_parts)
    de = w_down.shape[2]
    ce = de // EXPERT_CHUNKS
    part_blocks = [x.shape[0] // MOE_BLK for x in x_parts]
    part_starts = tuple(int(v) for v in np.cumsum([0] + part_blocks[:-1]))
    part_specs = [pl.BlockSpec((MOE_BLK, d), lambda i, be, nu, o=o, n=n: (jnp.clip(i - o, 0, n - 1), 0))
                  for o, n in zip(part_starts, part_blocks)]
    out = None
    for c in range(EXPERT_CHUNKS):
        row = pl.BlockSpec((MOE_BLK, d), lambda i, be, nu: (i, 0))
        in_specs = part_specs + [
            pl.BlockSpec((1, 1, d, ce), lambda i, be, nu, c=c: (layer, be[i], 0, c)),
            pl.BlockSpec((1, 1, d, ce), lambda i, be, nu, c=c: (layer, be[i], 0, EXPERT_CHUNKS + c)),
            pl.BlockSpec((1, 1, ce, d), lambda i, be, nu, c=c: (layer, be[i], c, 0))]
        args = list(x_parts) + [w_gate_up, w_gate_up, w_down]
        if out is not None:
            in_specs.append(row)
            args.append(out)
        out = pl.pallas_call(
            functools.partial(_expert_kernel, part_starts=part_starts),
            grid_spec=pltpu.PrefetchScalarGridSpec(
                num_scalar_prefetch=2,
                grid=(n_rows // MOE_BLK,),
                in_specs=in_specs,
                out_specs=row,
                scratch_shapes=[pltpu.VMEM((d, ce), BF16), pltpu.VMEM((d, ce), BF16), pltpu.VMEM((ce, d), BF16)]),
            out_shape=jax.ShapeDtypeStruct((n_rows, d), BF16),
            compiler_params=_cparams(("arbitrary",)),
            name="expert_ffn",
        )(block_expert, n_used, *args)
    return out


def _dispatch_tables(expert_idx):
    n_tok = expert_idx.shape[0]
    n_assign = n_tok * TOP_K
    e_flat = expert_idx.reshape(n_assign)
    onehot = e_flat[:, None] == jnp.arange(N_EXPERTS)[None, :]
    chunk = min(1024, n_assign)
    oh = onehot.astype(F32).reshape(n_assign // chunk, chunk, N_EXPERTS)
    within = jnp.einsum("ij,bjk->bik", jnp.tril(jnp.ones((chunk, chunk), F32)), oh)
    totals = within[:, -1, :]
    csum = (within + (jnp.cumsum(totals, axis=0) - totals)[:, None, :]).reshape(n_assign, N_EXPERTS)
    rank = jnp.sum(jnp.where(onehot, csum, 0.0), axis=1).astype(jnp.int32) - 1
    counts = csum[-1].astype(jnp.int32)
    padded = (counts + MOE_BLK - 1) // MOE_BLK * MOE_BLK
    pad_end = jnp.cumsum(padded)
    pad_start = pad_end - padded
    dest = pad_start[e_flat] + rank
    n_rows = n_assign + N_EXPERTS * MOE_BLK
    block_start = jnp.arange(n_rows // MOE_BLK, dtype=jnp.int32) * MOE_BLK
    block_expert = jnp.minimum(jnp.sum(pad_end[None, :] <= block_start[:, None], axis=1), N_EXPERTS - 1)
    order = jnp.sort(e_flat * n_assign + jnp.arange(n_assign, dtype=jnp.int32)) % n_assign
    row_expert = jnp.repeat(block_expert, MOE_BLK)
    row_rank = jnp.arange(n_rows, dtype=jnp.int32) - pad_start[row_expert]
    source = order[jnp.clip((jnp.cumsum(counts) - counts)[row_expert] + row_rank, 0, n_assign - 1)]
    row_tok = jnp.where(row_rank < counts[row_expert], source // TOP_K, 0)
    n_used = (pad_end[-1:] // MOE_BLK).astype(jnp.int32)
    return row_tok.astype(jnp.int32), block_expert.astype(jnp.int32), n_used, dest.reshape(n_tok, TOP_K)


def _token_mixer(xb, bsz, seq, w_in_t, layer, conv_a_w, conv_a_b, lru_conv_w, lru_conv_b, lru_wa, lru_ba, lru_wx, lru_bx,
                 lru_lam, cmp_pos, phi_w1, phi_b1, phi_w2, w_branch):
    d = xb.shape[1]
    n_tok = bsz * seq
    main_w = 3 * BRANCH_W + 2 * BRANCH_W + NSA_W + 6 * KV_W
    ng = 3 * NSA_HEADS
    kv0 = main_w - 6 * KV_W
    zmain = matmul_wres(xb, w_in_t, layer, 0, main_w, BF16, 2048, 768)
    zgate = matmul_wres(xb, w_in_t, layer, main_w, LANES, F32, 2048, LANES)
    zmerge = matmul_wres(xb, w_in_t, layer, main_w + ng, w_in_t.shape[1] - main_w - ng, BF16, 2048, 512)
    z3 = zmain.reshape(bsz, seq, main_w)

    y_a = conv_mixer(z3, conv_a_w, conv_a_b)
    y_b = lru_mixer(z3, 3 * BRANCH_W, lru_conv_w, lru_conv_b, lru_wa, lru_ba, lru_wx, lru_bx, lru_lam)

    kv = z3[:, :, kv0:kv0 + 2 * KV_W].reshape(bsz, seq, 2, NSA_KV_HEADS, NSA_HD).transpose(2, 0, 3, 1, 4)
    nslot = seq // CMP_STRIDE
    slots = kv.reshape(2, bsz, NSA_KV_HEADS, nslot, CMP_STRIDE * NSA_HD)
    cmp = nsa_compress(slots, cmp_pos, phi_w1, phi_b1, phi_w2)
    gate_logits = zgate[:, :ng].reshape(bsz, seq, NSA_KV_HEADS, NSA_GRP * 3).transpose(0, 2, 3, 1)
    y_c = nsa_attention(z3, 5 * BRANCH_W, kv0 + 2 * KV_W, cmp[0], cmp[1], gate_logits)

    flat = lambda y: y.reshape(n_tok, BRANCH_W)
    return branch_merge(flat(y_a), flat(y_b), flat(y_c), w_branch.astype(BF16), zmerge)


def _moe(x, xb, w_router, b_router, w_gate_up, w_down, layer):
    idx_t, gate_t = router(x, w_router, b_router)
    row_tok, block_expert, n_used, dest = _dispatch_tables(idx_t.T)
    n_tok = xb.shape[0]
    x_parts = [xb[row_tok[a:a + n_tok]] for a in range(0, row_tok.shape[0], n_tok)]
    rows = expert_ffn(block_expert, n_used, x_parts, w_gate_up, w_down, layer)
    return rows[dest[:, 0]], rows[dest[:, 1]], gate_t.T


def kernel(x, p, w_in, conv_a_w, conv_a_b, lru_conv_w, lru_conv_b, lru_wa, lru_ba, lru_wx, lru_bx, lru_lam, cmp_pos, phi_w1, phi_b1, phi_w2, w_branch, w_out, ln_g, ln_b, w_router, b_router, w_gate_up, w_down, w_ple, w_ple_gate):
    bsz, seq, d = x.shape
    depth = w_in.shape[0]
    n_tok = bsz * seq
    alpha = (2 * depth) ** 0.25
    xf = x.reshape(n_tok, d)
    xb = xf.astype(BF16)
    w_in_t = jnp.swapaxes(w_in, 1, 2)
    for i in range(depth):
        merged = _token_mixer(xb, bsz, seq, w_in_t, i, conv_a_w[i], conv_a_b[i], lru_conv_w[i], lru_conv_b[i],
                              lru_wa[i], lru_ba[i], lru_wx[i], lru_bx[i], lru_lam[i], cmp_pos[i], phi_w1[i],
                              phi_b1[i], phi_w2[i], w_branch[i])
        xf, xb = outproj_ln(merged, w_out[i].astype(BF16), xf, ln_g[i, 0], ln_b[i, 0], alpha)
        y0, y1, gate = _moe(xf, xb, w_router, b_router, w_gate_up, w_down, i)
        xf, xb = ple_combine_ln(xf, xb, w_ple_gate[i].astype(BF16), p[i].reshape(n_tok, -1).astype(BF16),
                                w_ple[i].astype(BF16), y0, y1, gate, ln_g[i, 1], ln_b[i, 1], alpha)
    return xf.reshape(bsz, seq, d)
```

```python
import functools

import ml_dtypes
import numpy as np
import jax
import jax.numpy as jnp
from jax import lax
from jax.experimental import pallas as pl
from jax.experimental.pallas import tpu as pltpu

F32 = jnp.float32
BF16 = jnp.bfloat16

BRANCH_W = 1024
CONV_K = 3
LRU_HEADS = 8
LRU_HD = BRANCH_W // LRU_HEADS
LRU_CONV_K = 4
LRU_C = 8.0
NSA_HEADS = 16
NSA_KV_HEADS = 4
NSA_GRP = NSA_HEADS // NSA_KV_HEADS
NSA_HD = 64
NSA_W = NSA_HEADS * NSA_HD
KV_W = NSA_KV_HEADS * NSA_HD
CMP_BLK = 32
CMP_STRIDE = 16
CMP_HIDDEN = 128
SLC_BLK = 64
SLC_TOPN = 16
WINDOW = 512
N_EXPERTS = 16
N_GROUPS = 4
EXPERTS_PER_GROUP = N_EXPERTS // N_GROUPS
TOP_K = 2
LN_EPS = 1e-5
NEG = -1e30
FORCE = 1e9

LANES = 128
SUBLANES = 8
VMEM_LIMIT = 56 * 1024 * 1024

FEAT = 256
MASK_OFF = NSA_HD
ALIBI_OFF = 2 * NSA_HD
PAD_OFF = ALIBI_OFF + 6
V_ROWS = NSA_HD + 16
LOG2E = 1.4426950408889634

ATT_TQ = 256
ATT_TK = 512
EXP_CHUNK = 64
MOE_BLK = 256
EXPERT_CHUNKS = 2


def _cparams(sem):
    return pltpu.CompilerParams(dimension_semantics=sem, vmem_limit_bytes=VMEM_LIMIT)


def _mm_wres_kernel(x_ref, w_ref, *rest, shift):
    o_ref, wb_ref = rest[-2:]

    @pl.when(pl.program_id(1) == 0)
    def _():
        w = w_ref[0]
        if shift:
            tn = w.shape[0]
            w = jnp.concatenate([w, rest[0][0]], axis=0)[shift:shift + tn]
        wb_ref[...] = w.astype(BF16)

    o_ref[...] = lax.dot_general(x_ref[...], wb_ref[...], (((1,), (1,)), ((), ())),
                                 preferred_element_type=F32).astype(o_ref.dtype)


def matmul_wres(x, wt3, layer, row0, n, out_dtype, tm, tn):
    m, k = x.shape
    shift = row0 % tn
    base = row0 - shift
    assert m % tm == 0 and n % tn == 0 and shift % SUBLANES == 0 and shift <= LANES and tn % LANES == 0
    in_specs = [pl.BlockSpec((tm, k), lambda j, i: (i, 0)),
                pl.BlockSpec((1, tn, k), lambda j, i: (layer, base // tn + j, 0))]
    args = [x, wt3]
    if shift:
        in_specs.append(pl.BlockSpec((1, LANES, k), lambda j, i: (layer, (base + (j + 1) * tn) // LANES, 0)))
        args.append(wt3)
    return pl.pallas_call(
        functools.partial(_mm_wres_kernel, shift=shift),
        grid=(n // tn, m // tm),
        in_specs=in_specs,
        out_specs=pl.BlockSpec((tm, tn), lambda j, i: (i, j)),
        out_shape=jax.ShapeDtypeStruct((m, n), out_dtype),
        scratch_shapes=[pltpu.VMEM((tn, k), BF16)],
        compiler_params=_cparams(("parallel", "arbitrary")),
        name="matmul_wres",
    )(*args)


def _causal_conv(ext, cur, w, k):
    acc = cur * w[k - 1:k, :]
    for j in range(k - 1):
        shift = k - 1 - j
        acc = acc + pltpu.roll(ext, shift, 0)[SUBLANES:, :] * w[j:j + 1, :]
    return acc


def _conv_mixer_kernel(ain_ref, ab_ref, ac_ref, w_ref, b_ref, o_ref, halo_ref):
    @pl.when(pl.program_id(2) == 0)
    def _():
        halo_ref[...] = jnp.zeros_like(halo_ref)

    v = ac_ref[0].astype(F32) * ain_ref[0].astype(F32)
    ext = jnp.concatenate([halo_ref[...], v], axis=0)
    y = _causal_conv(ext, v, w_ref[...], CONV_K) + b_ref[...]
    o_ref[0] = (ab_ref[0].astype(F32) * y).astype(o_ref.dtype)
    halo_ref[...] = v[v.shape[0] - SUBLANES:, :]


def conv_mixer(z3, conv_w, conv_b, ts=512, cw=512):
    b, s, _ = z3.shape
    nc = BRANCH_W // cw
    return pl.pallas_call(
        _conv_mixer_kernel,
        grid=(b, nc, s // ts),
        in_specs=[pl.BlockSpec((1, ts, cw), lambda bi, c, si: (bi, si, c)),
                  pl.BlockSpec((1, ts, cw), lambda bi, c, si: (bi, si, c + nc)),
                  pl.BlockSpec((1, ts, cw), lambda bi, c, si: (bi, si, c + 2 * nc)),
                  pl.BlockSpec((CONV_K, cw), lambda bi, c, si: (0, c)),
                  pl.BlockSpec((1, cw), lambda bi, c, si: (0, c))],
        out_specs=pl.BlockSpec((1, ts, cw), lambda bi, c, si: (bi, si, c)),
        out_shape=jax.ShapeDtypeStruct((b, s, BRANCH_W), BF16),
        scratch_shapes=[pltpu.VMEM((SUBLANES, cw), F32)],
        compiler_params=_cparams(("parallel", "parallel", "arbitrary")),
        name="conv_mixer",
    )(z3, z3, z3, conv_w, conv_b.reshape(1, BRANCH_W))


def _lru_kernel(gate_ref, rin_ref, cw_ref, cb_ref, wa_ref, ba_ref, wx_ref, bx_ref, lam_ref, o_ref,
                halo_ref, carry_ref, a_ref, b_ref, h_ref):
    ts, lw = a_ref.shape

    @pl.when(pl.program_id(2) == 0)
    def _():
        halo_ref[...] = jnp.zeros_like(halo_ref)
        carry_ref[...] = jnp.zeros_like(carry_ref)

    u = rin_ref[0].astype(F32)
    ext = jnp.concatenate([halo_ref[...], u], axis=0)
    xc = _causal_conv(ext, u, cw_ref[...], LRU_CONV_K) + cb_ref[...]
    halo_ref[...] = u[ts - SUBLANES:, :]

    xcb = xc.astype(BF16)
    ra, ix = [], []
    for hh in range(lw // LRU_HD):
        xh = xcb[:, hh * LRU_HD:(hh + 1) * LRU_HD]
        ra.append(jnp.dot(xh, wa_ref[hh].astype(BF16), preferred_element_type=F32))
        ix.append(jnp.dot(xh, wx_ref[hh].astype(BF16), preferred_element_type=F32))
    r = jax.nn.sigmoid(jnp.concatenate(ra, axis=1) + ba_ref[...])
    i = jax.nn.sigmoid(jnp.concatenate(ix, axis=1) + bx_ref[...])
    neg_lam = -lam_ref[...]
    softplus = jnp.maximum(neg_lam, 0.0) + jnp.log1p(jnp.exp(-jnp.abs(neg_lam)))
    log_a = -LRU_C * r * softplus
    a_ref[...] = jnp.exp(log_a)
    th = jnp.tanh(log_a)
    b_ref[...] = jnp.sqrt(-2.0 * th / (1.0 - th)) * (i * xc)

    row = lax.broadcasted_iota(jnp.int32, (SUBLANES, lw), 0)

    def group(g, hc):
        sl = pl.ds(pl.multiple_of(g * SUBLANES, SUBLANES), SUBLANES)
        a = a_ref[sl, :]
        bv = b_ref[sl, :]
        for d in (1, 2, 4):
            keep = row >= d
            bv = jnp.where(keep, a * pltpu.roll(bv, d, 0) + bv, bv)
            a = jnp.where(keep, a * pltpu.roll(a, d, 0), a)
        h = a * hc + bv
        h_ref[sl, :] = h
        return jnp.broadcast_to(h[SUBLANES - 1:SUBLANES, :], (SUBLANES, lw))

    carry_ref[...] = lax.fori_loop(0, ts // SUBLANES, group, carry_ref[...])
    o_ref[0] = (h_ref[...] * jax.nn.gelu(gate_ref[0].astype(F32))).astype(o_ref.dtype)


def lru_mixer(z3, col0, conv_w, conv_b, wa, ba, wx, bx, lam, ts=512, lw=256):
    b, s, _ = z3.shape
    nc = BRANCH_W // lw
    c0 = col0 // lw
    hpb = lw // LRU_HD
    row = lambda v: v.reshape(1, BRANCH_W)
    vec = pl.BlockSpec((1, lw), lambda bi, c, si: (0, c))
    mat = pl.BlockSpec((hpb, LRU_HD, LRU_HD), lambda bi, c, si: (c, 0, 0))
    return pl.pallas_call(
        _lru_kernel,
        grid=(b, nc, s // ts),
        in_specs=[pl.BlockSpec((1, ts, lw), lambda bi, c, si: (bi, si, c0 + c)),
                  pl.BlockSpec((1, ts, lw), lambda bi, c, si: (bi, si, c0 + nc + c)),
                  pl.BlockSpec((LRU_CONV_K, lw), lambda bi, c, si: (0, c)),
                  vec, mat, vec, mat, vec, vec],
        out_specs=pl.BlockSpec((1, ts, lw), lambda bi, c, si: (bi, si, c)),
        out_shape=jax.ShapeDtypeStruct((b, s, BRANCH_W), BF16),
        scratch_shapes=[pltpu.VMEM((SUBLANES, lw), F32), pltpu.VMEM((SUBLANES, lw), F32),
                        pltpu.VMEM((ts, lw), F32), pltpu.VMEM((ts, lw), F32), pltpu.VMEM((ts, lw), F32)],
        compiler_params=_cparams(("parallel", "parallel", "arbitrary")),
        name="rg_lru",
    )(z3, z3, conv_w, row(conv_b), wa, row(ba), wx, row(bx), row(lam))


def _compress_kernel(slots_ref, pos_ref, w1_ref, b1_ref, w2_ref, o_ref):
    g, nslot, half = slots_ref.shape[2:]
    rows = g * nslot
    x = slots_ref[0, 0].reshape(rows, half)
    w1 = w1_ref[0].astype(BF16)
    first = jnp.dot(x, w1[:half], preferred_element_type=F32)
    second = jnp.dot(x, w1[half:], preferred_element_type=F32)
    pos = jnp.broadcast_to(pos_ref[0], (SUBLANES, 2 * half)).astype(BF16)
    posb = jnp.dot(pos, w1, preferred_element_type=F32)[0:1, :]
    hidden = first + pltpu.roll(second, rows - 1, 0) + posb + b1_ref[0]
    out = jnp.dot(jax.nn.gelu(hidden).astype(BF16), w2_ref[0].astype(BF16), preferred_element_type=F32)
    o_ref[0, 0] = out.reshape(g, nslot, NSA_HD).astype(o_ref.dtype)


def nsa_compress(slots, cmp_pos, w1, b1, w2):
    _, b, g, nslot, half = slots.shape
    pos = cmp_pos.reshape(2, 1, CMP_BLK * NSA_HD)
    return pl.pallas_call(
        _compress_kernel,
        grid=(2, b),
        in_specs=[pl.BlockSpec((1, 1, g, nslot, half), lambda kv, bi: (kv, bi, 0, 0, 0)),
                  pl.BlockSpec((1, 1, 2 * half), lambda kv, bi: (kv, 0, 0)),
                  pl.BlockSpec((1, 2 * half, CMP_HIDDEN), lambda kv, bi: (kv, 0, 0)),
                  pl.BlockSpec((1, 1, CMP_HIDDEN), lambda kv, bi: (kv, 0, 0)),
                  pl.BlockSpec((1, CMP_HIDDEN, NSA_HD), lambda kv, bi: (kv, 0, 0))],
        out_specs=pl.BlockSpec((1, 1, g, nslot, NSA_HD), lambda kv, bi: (kv, bi, 0, 0, 0)),
        out_shape=jax.ShapeDtypeStruct((2, b, g, nslot, NSA_HD), BF16),
        compiler_params=_cparams(("parallel", "parallel")),
        name="nsa_compress",
    )(slots, pos, w1, b1.reshape(2, 1, CMP_HIDDEN), w2)


def _exp2_bf16(s, m):
    rows = s.shape[0]
    chunk = min(EXP_CHUNK, rows)
    return jnp.concatenate([jnp.exp2((s[r:r + chunk, :] - m).astype(BF16)) for r in range(0, rows, chunk)], axis=0)


def _nsa_kernel(q_ref, qf_ref, kc_ref, vc_ref, ksb_ref, vsb_ref, kwb_ref, vwb_ref, fs_ref, fw_ref, gl_ref, c2s_ref,
                bias_ref, o_ref, ks_ref, vs_ref, kw_ref, vw_ref, qp_ref, imp_ref, s0_ref, s1_ref, mx_ref, m_ref, acc_ref,
                *, n_cmp):
    tq = q_ref.shape[1]
    lanes = NSA_GRP * tq
    seq = ks_ref.shape[0]
    n_slc = seq // SLC_BLK
    t0 = pl.multiple_of(pl.program_id(2) * tq, tq)
    head = lambda r: slice(r * tq, (r + 1) * tq)

    @pl.when(pl.program_id(2) == 0)
    def _():
        odd = pl.program_id(1) % 2 == 1

        def keys(blk_ref):
            blk = blk_ref[0].astype(F32)
            return jnp.where(odd, pltpu.roll(blk, NSA_HD, 1), blk)[:, :NSA_HD].astype(BF16)

        def values_t(blk_ref):
            vt = blk_ref[0].astype(F32).T
            return jnp.where(odd, vt[NSA_HD:], vt[:NSA_HD]).astype(BF16)

        def ones_row(n):
            first = lax.broadcasted_iota(jnp.int32, (V_ROWS - NSA_HD, n), 0) == 0
            return jnp.where(first, 1.0, 0.0).astype(BF16)

        ks_ref[:, 0:NSA_HD] = keys(ksb_ref)
        ks_ref[:, NSA_HD:] = fs_ref[...]
        vs_ref[0:NSA_HD, :] = values_t(vsb_ref)
        vs_ref[NSA_HD:, :] = ones_row(seq)
        kw_ref[0:WINDOW, 0:NSA_HD] = jnp.zeros((WINDOW, NSA_HD), BF16)
        kw_ref[WINDOW:, 0:NSA_HD] = keys(kwb_ref)
        kw_ref[:, NSA_HD:] = fw_ref[...]
        vw_ref[0:NSA_HD, 0:WINDOW] = jnp.zeros((NSA_HD, WINDOW), BF16)
        vw_ref[0:NSA_HD, WINDOW:] = values_t(vwb_ref)
        vw_ref[NSA_HD:, :] = ones_row(seq + WINDOW)

    def scores(k_tile, bias=None):
        s = jnp.dot(k_tile, qp_ref[...], preferred_element_type=F32)
        return s if bias is None else s + bias

    qt = q_ref[0].astype(F32).T * (NSA_HD ** -0.5 * LOG2E)
    qp_ref[...] = qf_ref[0]
    for r in range(NSA_GRP):
        qp_ref[0:NSA_HD, head(r)] = qt[r * NSA_HD:(r + 1) * NSA_HD, :].astype(BF16)

    s_cmp = scores(kc_ref[0, 0])
    win = []
    for off, size, bias in ((0, tq, bias_ref[1]), (tq, WINDOW - tq, None), (WINDOW, tq, bias_ref[0])):
        k0 = pl.multiple_of(t0 + off, tq)
        win.append((scores(kw_ref[pl.ds(k0, size), :], bias), vw_ref[:, pl.ds(k0, size)]))

    ncp = kc_ref.shape[2]
    n_row = lax.broadcasted_iota(jnp.int32, (ncp, lanes), 0)
    t_lane = t0 + lax.broadcasted_iota(jnp.int32, (ncp, lanes), 1) % tq
    ok = (t_lane >= n_row * CMP_STRIDE + (CMP_BLK - 1)) & (n_row < n_cmp)
    s = jnp.where(ok, s_cmp, NEG)
    p = jnp.where(ok, jnp.exp2(s - jnp.max(s, axis=0, keepdims=True)), 0.0)
    l = jnp.sum(p, axis=0, keepdims=True)
    p_cmp = p * (1.0 / jnp.where(l > 0.0, l, 1.0))
    o_cmp = jnp.dot(vc_ref[0, 0], p_cmp.astype(BF16), preferred_element_type=F32)

    rest = p_cmp[:, head(0)]
    for r in range(1, NSA_GRP):
        rest = rest + p_cmp[:, head(r)]
    imp = jnp.zeros((n_slc, tq), F32)
    for _ in range(3):
        part = rest.astype(BF16)
        rest = rest - part.astype(F32)
        imp = imp + jnp.dot(c2s_ref[...], part, preferred_element_type=F32)
    j_row = lax.broadcasted_iota(jnp.int32, (n_slc, tq), 0)
    t_col = t0 + lax.broadcasted_iota(jnp.int32, (n_slc, tq), 1)
    forced = (j_row == 0) | (j_row == t_col // SLC_BLK)
    causal = j_row * SLC_BLK <= t_col
    imp = jnp.where(forced, FORCE, jnp.where(causal, imp, -FORCE))
    imp_ref[...] = imp

    n_grp = n_slc // SUBLANES
    vals = [imp[v * SUBLANES:(v + 1) * SUBLANES, :] for v in range(n_grp)]
    sub = lax.broadcasted_iota(jnp.int32, (SUBLANES, tq), 0)
    rank = [jnp.zeros((SUBLANES, tq), jnp.int32) for _ in range(n_grp)]
    for i in range(n_slc):
        bi = jnp.broadcast_to(imp_ref[i:i + 1, :], (SUBLANES, tq))
        for v in range(n_grp):
            if v < i // SUBLANES:
                ahead = bi > vals[v]
            elif v > i // SUBLANES:
                ahead = bi >= vals[v]
            else:
                ahead = (bi > vals[v]) | ((bi == vals[v]) & (sub > i % SUBLANES))
            rank[v] = rank[v] + jnp.where(ahead, 1, 0)
    chosen = (jnp.concatenate(rank, axis=0) < min(SLC_TOPN, n_slc)) & causal
    mask = jnp.where(chosen, 0.0, NEG).astype(BF16)
    for r in range(NSA_GRP):
        qp_ref[MASK_OFF:MASK_OFF + n_slc, head(r)] = mask

    tk = min(ATT_TK, seq)
    n_full = t0 // tk
    last = n_full // 2

    s_bufs = (s0_ref, s1_ref)

    def pair_scores(step, buf):
        for par in range(2):
            k0 = pl.multiple_of((2 * step + par) * tk, tk)
            s = scores(ks_ref[pl.ds(k0, tk), :])
            s_bufs[buf][par] = s
            mx_ref[buf, par] = jnp.max(s, axis=0, keepdims=True)

    def pair_update(step, buf, prefetch):
        m_new = [jnp.maximum(m_ref[par], mx_ref[buf, par]) for par in range(2)]
        probs = [_exp2_bf16(s_bufs[buf].at[par], m_new[par]) for par in range(2)]
        if prefetch:
            pair_scores(step + 1, 1 - buf)
        for par in range(2):
            k0 = pl.multiple_of((2 * step + par) * tk, tk)
            pv = jnp.dot(vs_ref[:, pl.ds(k0, tk)], probs[par], preferred_element_type=F32)
            acc_ref[par] = jnp.exp2(m_ref[par] - m_new[par]) * acc_ref[par] + pv
            m_ref[par] = m_new[par]

    def mask_diagonal(buf):
        off = pl.multiple_of(t0 - n_full * tk, tq)
        for par in range(2):
            @pl.when(n_full % 2 == par)
            def _():
                tile = s_bufs[buf].at[par]
                tile[pl.ds(off, tq), :] = tile[pl.ds(off, tq), :] + bias_ref[0]
                mx_ref[buf, par] = jnp.max(tile[...], axis=0, keepdims=True)

    def half_trip(step, buf):
        @pl.when(step == last)
        def _():
            mask_diagonal(buf)

        @pl.when(step < last)
        def _():
            pair_update(step, buf, prefetch=True)

        @pl.when(step == last)
        def _():
            pair_update(step, buf, prefetch=False)

    def trip(i, carry):
        half_trip(2 * i, 0)
        half_trip(2 * i + 1, 1)
        return carry

    m_ref[...] = jnp.full(m_ref.shape, NEG, F32)
    acc_ref[...] = jnp.zeros(acc_ref.shape, F32)
    pair_scores(0, 0)

    mx = functools.reduce(jnp.maximum, [jnp.max(s, axis=0, keepdims=True) for s, _ in win])
    o_win = jnp.zeros((V_ROWS, lanes), F32)
    for s, v_tile in win:
        o_win = o_win + jnp.dot(v_tile, _exp2_bf16(s, mx), preferred_element_type=F32)
    o_win = o_win[0:NSA_HD] * (1.0 / o_win[NSA_HD:NSA_HD + 1])

    lax.fori_loop(0, last // 2 + 1, trip, 0)
    m_all = jnp.maximum(m_ref[0], m_ref[1])
    o_slc = jnp.exp2(m_ref[0] - m_all) * acc_ref[0] + jnp.exp2(m_ref[1] - m_all) * acc_ref[1]
    o_slc = o_slc[0:NSA_HD] * (1.0 / o_slc[NSA_HD:NSA_HD + 1])

    gates = jax.nn.sigmoid(gl_ref[0, 0].astype(F32))
    outs = []
    for r in range(NSA_GRP):
        outs.append(gates[3 * r:3 * r + 1] * o_cmp[:, head(r)] + gates[3 * r + 1:3 * r + 2] * o_slc[:, head(r)]
                    + gates[3 * r + 2:3 * r + 3] * o_win[:, head(r)])
    o_ref[0] = jnp.concatenate(outs, axis=0).T.astype(o_ref.dtype)


def _alibi_slopes():
    return (2.0 ** (-8.0 * np.arange(1, NSA_HEADS + 1) / NSA_HEADS)).astype(np.float32)


def _bf16_split3(v):
    parts, rest = [], np.asarray(v, np.float32)
    for _ in range(3):
        part = rest.astype(ml_dtypes.bfloat16).astype(np.float32)
        parts.append(part)
        rest = (rest.astype(np.float64) - part.astype(np.float64)).astype(np.float32)
    return parts


def _query_features(tq):
    feat = np.zeros((NSA_HEADS, FEAT), np.float32)
    hi, mid, lo = _bf16_split3(_alibi_slopes() * np.float32(LOG2E))
    for c, part in enumerate((hi, mid, lo, hi, mid, lo)):
        feat[:, ALIBI_OFF + c] = part
    feat[:, PAD_OFF] = NEG
    feat = np.repeat(feat.reshape(NSA_KV_HEADS, NSA_GRP, FEAT), tq, axis=1)
    return jnp.asarray(feat.transpose(0, 2, 1), BF16)


def _key_features(pos, n_slc):
    pos = np.asarray(pos)
    real = pos >= 0
    feat = np.zeros((pos.shape[0], FEAT - NSA_HD), np.float32)
    if n_slc:
        feat[real, pos[real] // SLC_BLK] = 1.0
    a0 = ALIBI_OFF - NSA_HD
    feat[real, a0:a0 + 3] = ((pos[real] // SLC_BLK) * SLC_BLK)[:, None]
    feat[real, a0 + 3:a0 + 6] = (pos[real] % SLC_BLK)[:, None]
    feat[~real, PAD_OFF - NSA_HD] = 1.0
    return feat


def _with_features(k, feat):
    b, g, _, _ = k.shape
    f = jnp.broadcast_to(jnp.asarray(feat, BF16), (b, g) + feat.shape)
    return jnp.concatenate([k.astype(BF16), f], axis=3)


def nsa_attention(z3, qcol0, kvcol0, kc, vc, gate_logits):
    b, s, _ = z3.shape
    tq = ATT_TQ
    tk = min(ATT_TK, s)
    assert s % (2 * tk) == 0 and tk % tq == 0 and tq % SLC_BLK == 0 and WINDOW % tq == 0 and WINDOW > tq
    n_slc = s // SLC_BLK
    assert n_slc <= ALIBI_OFF - MASK_OFF and n_slc % SUBLANES == 0
    assert 2 * NSA_HD == LANES and NSA_KV_HEADS % 2 == 0 and kvcol0 % LANES == 0
    n_cmp = (s - CMP_BLK) // CMP_STRIDE + 1
    ncp = kc.shape[2]
    gw = NSA_GRP * NSA_HD
    lanes = NSA_GRP * tq

    kc_f = _with_features(kc, _key_features(np.arange(ncp) * CMP_STRIDE + CMP_BLK - 1, 0))
    feat_s = jnp.asarray(_key_features(np.arange(s), n_slc), BF16)
    feat_w = jnp.asarray(_key_features(np.arange(-WINDOW, s), 0), BF16)

    c_start = np.arange(ncp) * CMP_STRIDE
    s_start = np.arange(n_slc) * SLC_BLK
    overlap = np.clip(np.minimum(c_start[:, None] + CMP_BLK, s_start[None, :] + SLC_BLK)
                      - np.maximum(c_start[:, None], s_start[None, :]), 0, None) / CMP_BLK
    overlap[n_cmp:] = 0.0
    c2s_t = jnp.asarray(overlap.T, BF16)

    key = np.arange(tq)[:, None]
    qry = np.tile(np.arange(tq), NSA_GRP)[None, :]
    bias = jnp.asarray(np.stack([np.where(key <= qry, 0.0, NEG), np.where(key > qry, 0.0, NEG)]), F32)

    blk = lambda *shape: pl.BlockSpec((1, 1) + shape, lambda bi, g, qi: (bi, g, 0, 0))
    const = lambda *shape: pl.BlockSpec(shape, lambda bi, g, qi: (0,) * len(shape))
    pair = lambda j: pl.BlockSpec((1, s, LANES), lambda bi, g, qi: (bi, 0, (kvcol0 + j * KV_W) // LANES + g // 2))
    return pl.pallas_call(
        functools.partial(_nsa_kernel, n_cmp=n_cmp),
        grid=(b, NSA_KV_HEADS, s // tq),
        in_specs=[pl.BlockSpec((1, tq, gw), lambda bi, g, qi: (bi, qi, qcol0 // gw + g)),
                  pl.BlockSpec((1, FEAT, lanes), lambda bi, g, qi: (g, 0, 0)),
                  blk(ncp, FEAT), blk(NSA_HD, ncp),
                  pair(0), pair(1), pair(2), pair(3),
                  const(s, FEAT - NSA_HD), const(s + WINDOW, FEAT - NSA_HD),
                  pl.BlockSpec((1, 1, NSA_GRP * 3, tq), lambda bi, g, qi: (bi, g, 0, qi)),
                  const(n_slc, ncp), const(2, tq, lanes)],
        out_specs=pl.BlockSpec((1, tq, gw), lambda bi, g, qi: (bi, qi, g)),
        out_shape=jax.ShapeDtypeStruct((b, s, NSA_W), BF16),
        scratch_shapes=[pltpu.VMEM((s, FEAT), BF16), pltpu.VMEM((V_ROWS, s), BF16),
                        pltpu.VMEM((s + WINDOW, FEAT), BF16), pltpu.VMEM((V_ROWS, s + WINDOW), BF16),
                        pltpu.VMEM((FEAT, lanes), BF16), pltpu.VMEM((n_slc, tq), F32),
                        pltpu.VMEM((2, tk, lanes), F32), pltpu.VMEM((2, tk, lanes), F32),
                        pltpu.VMEM((2, 2, 1, lanes), F32),
                        pltpu.VMEM((2, 1, lanes), F32), pltpu.VMEM((2, V_ROWS, lanes), F32)],
        compiler_params=_cparams(("parallel", "parallel", "arbitrary")),
        name="nsa_attention",
    )(z3, _query_features(tq), kc_f, jnp.swapaxes(vc, 2, 3), z3, z3, z3, z3, feat_s, feat_w, gate_logits, c2s_t,
      bias)


def _merge_kernel(ya_ref, yb_ref, yc_ref, wb_ref, ga_ref, gb_ref, gc_ref, o_ref):
    acc = None
    for i, (y_ref, g_ref) in enumerate(((ya_ref, ga_ref), (yb_ref, gb_ref), (yc_ref, gc_ref))):
        term = jax.nn.sigmoid(g_ref[...].astype(F32)) * jnp.dot(y_ref[...], wb_ref[i], preferred_element_type=F32)
        acc = term if acc is None else acc + term
    o_ref[...] = acc.astype(o_ref.dtype)


def branch_merge(ya, yb, yc, wb, zmerge, tm=1024, tn=1024):
    t = ya.shape[0]
    d = wb.shape[2]
    nj = d // tn
    y_spec = pl.BlockSpec((tm, BRANCH_W), lambda i, j: (i, 0))
    gate = lambda br: pl.BlockSpec((tm, tn), lambda i, j: (i, j + br * nj))
    return pl.pallas_call(
        _merge_kernel,
        grid=(t // tm, nj),
        in_specs=[y_spec, y_spec, y_spec,
                  pl.BlockSpec((3, BRANCH_W, tn), lambda i, j: (0, 0, j)),
                  gate(0), gate(1), gate(2)],
        out_specs=pl.BlockSpec((tm, tn), lambda i, j: (i, j)),
        out_shape=jax.ShapeDtypeStruct((t, d), BF16),
        compiler_params=_cparams(("parallel", "parallel")),
        name="branch_merge",
    )(ya, yb, yc, wb, zmerge, zmerge, zmerge)


def _layer_norm(y, g, b):
    mu = jnp.mean(y, axis=-1, keepdims=True)
    yc = y - mu
    var = jnp.mean(yc * yc, axis=-1, keepdims=True)
    return yc * lax.rsqrt(var + LN_EPS) * g + b


def _outproj_ln_kernel(m_ref, w_ref, x_ref, g_ref, b_ref, o_ref, ob_ref, *, alpha):
    mix = jnp.dot(m_ref[...], w_ref[...], preferred_element_type=F32)
    y = _layer_norm(alpha * x_ref[...] + mix, g_ref[...], b_ref[...])
    o_ref[...] = y
    ob_ref[...] = y.astype(BF16)


def outproj_ln(merged, w_out, x, g, b, alpha, tm=512):
    t, d = x.shape
    row = pl.BlockSpec((tm, d), lambda i: (i, 0))
    vec = pl.BlockSpec((1, d), lambda i: (0, 0))
    return pl.pallas_call(
        functools.partial(_outproj_ln_kernel, alpha=alpha),
        grid=(t // tm,),
        in_specs=[row, pl.BlockSpec((d, d), lambda i: (0, 0)), row, vec, vec],
        out_specs=[row, row],
        out_shape=[jax.ShapeDtypeStruct((t, d), F32), jax.ShapeDtypeStruct((t, d), BF16)],
        compiler_params=_cparams(("parallel",)),
        name="outproj_ln",
    )(merged, w_out, x, g.reshape(1, d), b.reshape(1, d))


def _final_kernel(x_ref, xb_ref, wg_ref, p_ref, wp_ref, y0_ref, y1_ref, gate_ref, g_ref, b_ref, o_ref, ob_ref,
                  *, alpha):
    ple = jax.nn.sigmoid(jnp.dot(xb_ref[...], wg_ref[...], preferred_element_type=F32)) * jnp.dot(
        p_ref[...], wp_ref[...], preferred_element_type=F32)
    gate = gate_ref[...]
    ffn = gate[:, 0:1] * y0_ref[...].astype(F32) + gate[:, 1:2] * y1_ref[...].astype(F32)
    y = _layer_norm(alpha * x_ref[...] + ffn + ple, g_ref[...], b_ref[...])
    o_ref[...] = y
    ob_ref[...] = y.astype(BF16)


def ple_combine_ln(x, xb, w_ple_gate, p, w_ple, y0, y1, gate, g, b, alpha, tm=512):
    t, d = x.shape
    pd = p.shape[1]
    row = pl.BlockSpec((tm, d), lambda i: (i, 0))
    vec = pl.BlockSpec((1, d), lambda i: (0, 0))
    return pl.pallas_call(
        functools.partial(_final_kernel, alpha=alpha),
        grid=(t // tm,),
        in_specs=[row, row, pl.BlockSpec((d, d), lambda i: (0, 0)),
                  pl.BlockSpec((tm, pd), lambda i: (i, 0)), pl.BlockSpec((pd, d), lambda i: (0, 0)),
                  row, row, pl.BlockSpec((tm, TOP_K), lambda i: (i, 0)), vec, vec],
        out_specs=[row, row],
        out_shape=[jax.ShapeDtypeStruct((t, d), F32), jax.ShapeDtypeStruct((t, d), BF16)],
        compiler_params=_cparams(("parallel",)),
        name="ple_combine_ln",
    )(x, xb, w_ple_gate, p, w_ple, y0, y1, gate, g.reshape(1, d), b.reshape(1, d))


def _router_kernel(x_ref, wr_ref, br_ref, idx_ref, gate_ref):
    tm = x_ref.shape[0]
    nt = (((1,), (1,)), ((), ()))
    x = x_ref[...]
    x_hi = x.astype(BF16)
    x_lo = (x - x_hi.astype(F32)).astype(BF16)
    w = wr_ref[...]
    w_hi = w.astype(BF16)
    w_lo = (w - w_hi.astype(F32)).astype(BF16)
    both = lax.dot_general(jnp.concatenate([w_hi, w_lo], axis=0), x_hi, nt, preferred_element_type=F32)
    logits = both[:N_EXPERTS] + both[N_EXPERTS:] + lax.dot_general(w_hi, x_lo, nt, preferred_element_type=F32)
    aff = jax.nn.sigmoid(logits)
    sel = aff + br_ref[...]
    scores = []
    for g in range(N_GROUPS):
        v = [sel[g * EXPERTS_PER_GROUP + e:g * EXPERTS_PER_GROUP + e + 1, :] for e in range(EXPERTS_PER_GROUP)]
        best = None
        for a in range(EXPERTS_PER_GROUP):
            for c in range(a + 1, EXPERTS_PER_GROUP):
                pair = v[a] + v[c]
                best = pair if best is None else jnp.maximum(best, pair)
        scores.append(best)
    best_score, best_group = scores[0], jnp.zeros((1, tm), jnp.int32)
    for g in range(1, N_GROUPS):
        better = scores[g] > best_score
        best_score = jnp.where(better, scores[g], best_score)
        best_group = jnp.where(better, g, best_group)
    e_row = lax.broadcasted_iota(jnp.int32, (N_EXPERTS, tm), 0)
    cand = jnp.where(e_row // EXPERTS_PER_GROUP == best_group, sel, NEG)
    picks, gates = [], []
    for _ in range(TOP_K):
        mx = jnp.max(cand, axis=0, keepdims=True)
        pick = jnp.min(jnp.where(cand == mx, e_row, N_EXPERTS), axis=0, keepdims=True)
        hit = e_row == pick
        picks.append(pick)
        gates.append(jnp.sum(jnp.where(hit, aff, 0.0), axis=0, keepdims=True))
        cand = jnp.where(hit, -jnp.inf, cand)
    total = gates[0] + gates[1]
    idx_ref[...] = jnp.concatenate(picks, axis=0)
    gate_ref[...] = jnp.concatenate([gates[0] / total, gates[1] / total], axis=0)


def router(x, w_router, b_router, tm=1024):
    t, d = x.shape
    return pl.pallas_call(
        _router_kernel,
        grid=(t // tm,),
        in_specs=[pl.BlockSpec((tm, d), lambda i: (i, 0)),
                  pl.BlockSpec((N_EXPERTS, d), lambda i: (0, 0)),
                  pl.BlockSpec((N_EXPERTS, 1), lambda i: (0, 0))],
        out_specs=[pl.BlockSpec((TOP_K, tm), lambda i: (0, i)), pl.BlockSpec((TOP_K, tm), lambda i: (0, i))],
        out_shape=[jax.ShapeDtypeStruct((TOP_K, t), jnp.int32), jax.ShapeDtypeStruct((TOP_K, t), F32)],
        compiler_params=_cparams(("parallel",)),
        name="router",
    )(x, w_router.T, b_router.reshape(N_EXPERTS, 1))


def _expert_kernel(be_ref, nu_ref, *refs, part_starts):
    n_parts = len(part_starts)
    x_refs = refs[:n_parts]
    wg_ref, wu_ref, wd_ref = refs[n_parts:n_parts + 3]
    prev_ref = refs[n_parts + 3] if len(refs) == n_parts + 8 else None
    o_ref, wgb_ref, wub_ref, wdb_ref = refs[-4:]
    i = pl.program_id(0)

    @pl.when((i == 0) | (be_ref[i] != be_ref[jnp.maximum(i - 1, 0)]))
    def _():
        wgb_ref[...] = wg_ref[0, 0].astype(BF16)
        wub_ref[...] = wu_ref[0, 0].astype(BF16)
        wdb_ref[...] = wd_ref[0, 0].astype(BF16)

    @pl.when(i < nu_ref[0])
    def _():
        x = x_refs[0][...]
        for start, x_ref in zip(part_starts[1:], x_refs[1:]):
            x = jnp.where(i >= start, x_ref[...], x)
        g = jnp.dot(x, wgb_ref[...], preferred_element_type=F32)
        u = jnp.dot(x, wub_ref[...], preferred_element_type=F32)
        out = jnp.dot((jax.nn.silu(g) * u).astype(BF16), wdb_ref[...], preferred_element_type=F32)
        if prev_ref is not None:
            out = out + prev_ref[...].astype(F32)
        o_ref[...] = out.astype(o_ref.dtype)

    @pl.when(i >= nu_ref[0])
    def _():
        o_ref[...] = jnp.zeros_like(o_ref)


def expert_ffn(block_expert, n_used, x_parts, w_gate_up, w_down, layer):
    d = x_parts[0].shape[1]
    n_rows = sum(x.shape[0] for x in x_parts)
    de = w_down.shape[2]
    ce = de // EXPERT_CHUNKS
    part_blocks = [x.shape[0] // MOE_BLK for x in x_parts]
    part_starts = tuple(int(v) for v in np.cumsum([0] + part_blocks[:-1]))
    part_specs = [pl.BlockSpec((MOE_BLK, d), lambda i, be, nu, o=o, n=n: (jnp.clip(i - o, 0, n - 1), 0))
                  for o, n in zip(part_starts, part_blocks)]
    out = None
    for c in range(EXPERT_CHUNKS):
        row = pl.BlockSpec((MOE_BLK, d), lambda i, be, nu: (i, 0))
        in_specs = part_specs + [
            pl.BlockSpec((1, 1, d, ce), lambda i, be, nu, c=c: (layer, be[i], 0, c)),
            pl.BlockSpec((1, 1, d, ce), lambda i, be, nu, c=c: (layer, be[i], 0, EXPERT_CHUNKS + c)),
            pl.BlockSpec((1, 1, ce, d), lambda i, be, nu, c=c: (layer, be[i], c, 0))]
        args = list(x_parts) + [w_gate_up, w_gate_up, w_down]
        if out is not None:
            in_specs.append(row)
            args.append(out)
        out = pl.pallas_call(
            functools.partial(_expert_kernel, part_starts=part_starts),
            grid_spec=pltpu.PrefetchScalarGridSpec(
                num_scalar_prefetch=2,
                grid=(n_rows // MOE_BLK,),
                in_specs=in_specs,
                out_specs=row,
                scratch_shapes=[pltpu.VMEM((d, ce), BF16), pltpu.VMEM((d, ce), BF16), pltpu.VMEM((ce, d), BF16)]),
            out_shape=jax.ShapeDtypeStruct((n_rows, d), BF16),
            compiler_params=_cparams(("arbitrary",)),
            name="expert_ffn",
        )(block_expert, n_used, *args)
    return out


def _dispatch_tables(expert_idx):
    n_tok = expert_idx.shape[0]
    n_assign = n_tok * TOP_K
    e_flat = expert_idx.reshape(n_assign)
    onehot = e_flat[:, None] == jnp.arange(N_EXPERTS)[None, :]
    chunk = min(1024, n_assign)
    oh = onehot.astype(F32).reshape(n_assign // chunk, chunk, N_EXPERTS)
    within = jnp.einsum("ij,bjk->bik", jnp.tril(jnp.ones((chunk, chunk), F32)), oh)
    totals = within[:, -1, :]
    csum = (within + (jnp.cumsum(totals, axis=0) - totals)[:, None, :]).reshape(n_assign, N_EXPERTS)
    rank = jnp.sum(jnp.where(onehot, csum, 0.0), axis=1).astype(jnp.int32) - 1
    counts = csum[-1].astype(jnp.int32)
    padded = (counts + MOE_BLK - 1) // MOE_BLK * MOE_BLK
    pad_end = jnp.cumsum(padded)
    pad_start = pad_end - padded
    dest = pad_start[e_flat] + rank
    n_rows = n_assign + N_EXPERTS * MOE_BLK
    block_start = jnp.arange(n_rows // MOE_BLK, dtype=jnp.int32) * MOE_BLK
    block_expert = jnp.minimum(jnp.sum(pad_end[None, :] <= block_start[:, None], axis=1), N_EXPERTS - 1)
    order = jnp.sort(e_flat * n_assign + jnp.arange(n_assign, dtype=jnp.int32)) % n_assign
    row_expert = jnp.repeat(block_expert, MOE_BLK)
    row_rank = jnp.arange(n_rows, dtype=jnp.int32) - pad_start[row_expert]
    source = order[jnp.clip((jnp.cumsum(counts) - counts)[row_expert] + row_rank, 0, n_assign - 1)]
    row_tok = jnp.where(row_rank < counts[row_expert], source // TOP_K, 0)
    n_used = (pad_end[-1:] // MOE_BLK).astype(jnp.int32)
    return row_tok.astype(jnp.int32), block_expert.astype(jnp.int32), n_used, dest.reshape(n_tok, TOP_K)


def _token_mixer(xb, bsz, seq, w_in_t, layer, conv_a_w, conv_a_b, lru_conv_w, lru_conv_b, lru_wa, lru_ba, lru_wx, lru_bx,
                 lru_lam, cmp_pos, phi_w1, phi_b1, phi_w2, w_branch):
    d = xb.shape[1]
    n_tok = bsz * seq
    main_w = 3 * BRANCH_W + 2 * BRANCH_W + NSA_W + 6 * KV_W
    ng = 3 * NSA_HEADS
    kv0 = main_w - 6 * KV_W
    zmain = matmul_wres(xb, w_in_t, layer, 0, main_w, BF16, 2048, 768)
    zgate = matmul_wres(xb, w_in_t, layer, main_w, LANES, F32, 2048, LANES)
    zmerge = matmul_wres(xb, w_in_t, layer, main_w + ng, w_in_t.shape[1] - main_w - ng, BF16, 2048, 512)
    z3 = zmain.reshape(bsz, seq, main_w)

    y_a = conv_mixer(z3, conv_a_w, conv_a_b)
    y_b = lru_mixer(z3, 3 * BRANCH_W, lru_conv_w, lru_conv_b, lru_wa, lru_ba, lru_wx, lru_bx, lru_lam)

    kv = z3[:, :, kv0:kv0 + 2 * KV_W].reshape(bsz, seq, 2, NSA_KV_HEADS, NSA_HD).transpose(2, 0, 3, 1, 4)
    nslot = seq // CMP_STRIDE
    slots = kv.reshape(2, bsz, NSA_KV_HEADS, nslot, CMP_STRIDE * NSA_HD)
    cmp = nsa_compress(slots, cmp_pos, phi_w1, phi_b1, phi_w2)
    gate_logits = zgate[:, :ng].reshape(bsz, seq, NSA_KV_HEADS, NSA_GRP * 3).transpose(0, 2, 3, 1)
    y_c = nsa_attention(z3, 5 * BRANCH_W, kv0 + 2 * KV_W, cmp[0], cmp[1], gate_logits)

    flat = lambda y: y.reshape(n_tok, BRANCH_W)
    return branch_merge(flat(y_a), flat(y_b), flat(y_c), w_branch.astype(BF16), zmerge)


def _moe(x, xb, w_router, b_router, w_gate_up, w_down, layer):
    idx_t, gate_t = router(x, w_router, b_router)
    row_tok, block_expert, n_used, dest = _dispatch_tables(idx_t.T)
    n_tok = xb.shape[0]
    x_parts = [xb[row_tok[a:a + n_tok]] for a in range(0, row_tok.shape[0], n_tok)]
    rows = expert_ffn(block_expert, n_used, x_parts, w_gate_up, w_down, layer)
    return rows[dest[:, 0]], rows[dest[:, 1]], gate_t.T


def kernel(x, p, w_in, conv_a_w, conv_a_b, lru_conv_w, lru_conv_b, lru_wa, lru_ba, lru_wx, lru_bx, lru_lam, cmp_pos, phi_w1, phi_b1, phi_w2, w_branch, w_out, ln_g, ln_b, w_router, b_router, w_gate_up, w_down, w_ple, w_ple_gate):
    bsz, seq, d = x.shape
    depth = w_in.shape[0]
    n_tok = bsz * seq
    alpha = (2 * depth) ** 0.25
    xf = x.reshape(n_tok, d)
    xb = xf.astype(BF16)
    w_in_t = jnp.swapaxes(w_in, 1, 2)
    for i in range(depth):
        merged = _token_mixer(xb, bsz, seq, w_in_t, i, conv_a_w[i], conv_a_b[i], lru_conv_w[i], lru_conv_b[i],
                              lru_wa[i], lru_ba[i], lru_wx[i], lru_bx[i], lru_lam[i], cmp_pos[i], phi_w1[i],
                              phi_b1[i], phi_w2[i], w_branch[i])
        xf, xb = outproj_ln(merged, w_out[i].astype(BF16), xf, ln_g[i, 0], ln_b[i, 0], alpha)
        y0, y1, gate = _moe(xf, xb, w_router, b_router, w_gate_up, w_down, i)
        xf, xb = ple_combine_ln(xf, xb, w_ple_gate[i].astype(BF16), p[i].reshape(n_tok, -1).astype(BF16),
                                w_ple[i].astype(BF16), y0, y1, gate, ln_g[i, 1], ln_b[i, 1], alpha)
    return xf.reshape(bsz, seq, d)
```

```python
import functools

import ml_dtypes
import numpy as np
import jax
import jax.numpy as jnp
from jax import lax
from jax.experimental import pallas as pl
from jax.experimental.pallas import tpu as pltpu

F32 = jnp.float32
BF16 = jnp.bfloat16

BRANCH_W = 1024
CONV_K = 3
LRU_HEADS = 8
LRU_HD = BRANCH_W // LRU_HEADS
LRU_CONV_K = 4
LRU_C = 8.0
NSA_HEADS = 16
NSA_KV_HEADS = 4
NSA_GRP = NSA_HEADS // NSA_KV_HEADS
NSA_HD = 64
NSA_W = NSA_HEADS * NSA_HD
KV_W = NSA_KV_HEADS * NSA_HD
CMP_BLK = 32
CMP_STRIDE = 16
CMP_HIDDEN = 128
SLC_BLK = 64
SLC_TOPN = 16
WINDOW = 512
N_EXPERTS = 16
N_GROUPS = 4
EXPERTS_PER_GROUP = N_EXPERTS // N_GROUPS
TOP_K = 2
LN_EPS = 1e-5
NEG = -1e30
FORCE = 1e9

LANES = 128
SUBLANES = 8
VMEM_LIMIT = 56 * 1024 * 1024

FEAT = 256
MASK_OFF = NSA_HD
ALIBI_OFF = 2 * NSA_HD
PAD_OFF = ALIBI_OFF + 6
V_ROWS = NSA_HD + 16
LOG2E = 1.4426950408889634

ATT_TQ = 256
ATT_TK = 512
EXP_CHUNK = 64
MOE_BLK = 256
EXPERT_CHUNKS = 2


def _cparams(sem):
    return pltpu.CompilerParams(dimension_semantics=sem, vmem_limit_bytes=VMEM_LIMIT)


def _mm_wres_kernel(x_ref, w_ref, *rest, shift):
    o_ref, wb_ref = rest[-2:]

    @pl.when(pl.program_id(1) == 0)
    def _():
        w = w_ref[0]
        if shift:
            tn = w.shape[0]
            w = jnp.concatenate([w, rest[0][0]], axis=0)[shift:shift + tn]
        wb_ref[...] = w.astype(BF16)

    o_ref[...] = lax.dot_general(x_ref[...], wb_ref[...], (((1,), (1,)), ((), ())),
                                 preferred_element_type=F32).astype(o_ref.dtype)


def matmul_wres(x, wt3, layer, row0, n, out_dtype, tm, tn):
    m, k = x.shape
    shift = row0 % tn
    base = row0 - shift
    assert m % tm == 0 and n % tn == 0 and shift % SUBLANES == 0 and shift <= LANES and tn % LANES == 0
    in_specs = [pl.BlockSpec((tm, k), lambda j, i: (i, 0)),
                pl.BlockSpec((1, tn, k), lambda j, i: (layer, base // tn + j, 0))]
    args = [x, wt3]
    if shift:
        in_specs.append(pl.BlockSpec((1, LANES, k), lambda j, i: (layer, (base + (j + 1) * tn) // LANES, 0)))
        args.append(wt3)
    return pl.pallas_call(
        functools.partial(_mm_wres_kernel, shift=shift),
        grid=(n // tn, m // tm),
        in_specs=in_specs,
        out_specs=pl.BlockSpec((tm, tn), lambda j, i: (i, j)),
        out_shape=jax.ShapeDtypeStruct((m, n), out_dtype),
        scratch_shapes=[pltpu.VMEM((tn, k), BF16)],
        compiler_params=_cparams(("parallel", "arbitrary")),
        name="matmul_wres",
    )(*args)


def _causal_conv(ext, cur, w, k):
    acc = cur * w[k - 1:k, :]
    for j in range(k - 1):
        shift = k - 1 - j
        acc = acc + pltpu.roll(ext, shift, 0)[SUBLANES:, :] * w[j:j + 1, :]
    return acc


def _conv_mixer_kernel(ain_ref, ab_ref, ac_ref, w_ref, b_ref, o_ref, halo_ref):
    @pl.when(pl.program_id(2) == 0)
    def _():
        halo_ref[...] = jnp.zeros_like(halo_ref)

    v = ac_ref[0].astype(F32) * ain_ref[0].astype(F32)
    ext = jnp.concatenate([halo_ref[...], v], axis=0)
    y = _causal_conv(ext, v, w_ref[...], CONV_K) + b_ref[...]
    o_ref[0] = (ab_ref[0].astype(F32) * y).astype(o_ref.dtype)
    halo_ref[...] = v[v.shape[0] - SUBLANES:, :]


def conv_mixer(z3, conv_w, conv_b, ts=512, cw=512):
    b, s, _ = z3.shape
    nc = BRANCH_W // cw
    return pl.pallas_call(
        _conv_mixer_kernel,
        grid=(b, nc, s // ts),
        in_specs=[pl.BlockSpec((1, ts, cw), lambda bi, c, si: (bi, si, c)),
                  pl.BlockSpec((1, ts, cw), lambda bi, c, si: (bi, si, c + nc)),
                  pl.BlockSpec((1, ts, cw), lambda bi, c, si: (bi, si, c + 2 * nc)),
                  pl.BlockSpec((CONV_K, cw), lambda bi, c, si: (0, c)),
                  pl.BlockSpec((1, cw), lambda bi, c, si: (0, c))],
        out_specs=pl.BlockSpec((1, ts, cw), lambda bi, c, si: (bi, si, c)),
        out_shape=jax.ShapeDtypeStruct((b, s, BRANCH_W), BF16),
        scratch_shapes=[pltpu.VMEM((SUBLANES, cw), F32)],
        compiler_params=_cparams(("parallel", "parallel", "arbitrary")),
        name="conv_mixer",
    )(z3, z3, z3, conv_w, conv_b.reshape(1, BRANCH_W))


def _lru_kernel(gate_ref, rin_ref, cw_ref, cb_ref, wa_ref, ba_ref, wx_ref, bx_ref, lam_ref, o_ref,
                halo_ref, carry_ref, a_ref, b_ref, h_ref):
    ts, lw = a_ref.shape

    @pl.when(pl.program_id(2) == 0)
    def _():
        halo_ref[...] = jnp.zeros_like(halo_ref)
        carry_ref[...] = jnp.zeros_like(carry_ref)

    u = rin_ref[0].astype(F32)
    ext = jnp.concatenate([halo_ref[...], u], axis=0)
    xc = _causal_conv(ext, u, cw_ref[...], LRU_CONV_K) + cb_ref[...]
    halo_ref[...] = u[ts - SUBLANES:, :]

    xcb = xc.astype(BF16)
    ra, ix = [], []
    for hh in range(lw // LRU_HD):
        xh = xcb[:, hh * LRU_HD:(hh + 1) * LRU_HD]
        ra.append(jnp.dot(xh, wa_ref[hh].astype(BF16), preferred_element_type=F32))
        ix.append(jnp.dot(xh, wx_ref[hh].astype(BF16), preferred_element_type=F32))
    r = jax.nn.sigmoid(jnp.concatenate(ra, axis=1) + ba_ref[...])
    i = jax.nn.sigmoid(jnp.concatenate(ix, axis=1) + bx_ref[...])
    neg_lam = -lam_ref[...]
    softplus = jnp.maximum(neg_lam, 0.0) + jnp.log1p(jnp.exp(-jnp.abs(neg_lam)))
    log_a = -LRU_C * r * softplus
    a_ref[...] = jnp.exp(log_a)
    th = jnp.tanh(log_a)
    b_ref[...] = jnp.sqrt(-2.0 * th / (1.0 - th)) * (i * xc)

    row = lax.broadcasted_iota(jnp.int32, (SUBLANES, lw), 0)

    def group(g, hc):
        sl = pl.ds(pl.multiple_of(g * SUBLANES, SUBLANES), SUBLANES)
        a = a_ref[sl, :]
        bv = b_ref[sl, :]
        for d in (1, 2, 4):
            keep = row >= d
            bv = jnp.where(keep, a * pltpu.roll(bv, d, 0) + bv, bv)
            a = jnp.where(keep, a * pltpu.roll(a, d, 0), a)
        h = a * hc + bv
        h_ref[sl, :] = h
        return jnp.broadcast_to(h[SUBLANES - 1:SUBLANES, :], (SUBLANES, lw))

    carry_ref[...] = lax.fori_loop(0, ts // SUBLANES, group, carry_ref[...])
    o_ref[0] = (h_ref[...] * jax.nn.gelu(gate_ref[0].astype(F32))).astype(o_ref.dtype)


def lru_mixer(z3, col0, conv_w, conv_b, wa, ba, wx, bx, lam, ts=512, lw=256):
    b, s, _ = z3.shape
    nc = BRANCH_W // lw
    c0 = col0 // lw
    hpb = lw // LRU_HD
    row = lambda v: v.reshape(1, BRANCH_W)
    vec = pl.BlockSpec((1, lw), lambda bi, c, si: (0, c))
    mat = pl.BlockSpec((hpb, LRU_HD, LRU_HD), lambda bi, c, si: (c, 0, 0))
    return pl.pallas_call(
        _lru_kernel,
        grid=(b, nc, s // ts),
        in_specs=[pl.BlockSpec((1, ts, lw), lambda bi, c, si: (bi, si, c0 + c)),
                  pl.BlockSpec((1, ts, lw), lambda bi, c, si: (bi, si, c0 + nc + c)),
                  pl.BlockSpec((LRU_CONV_K, lw), lambda bi, c, si: (0, c)),
                  vec, mat, vec, mat, vec, vec],
        out_specs=pl.BlockSpec((1, ts, lw), lambda bi, c, si: (bi, si, c)),
        out_shape=jax.ShapeDtypeStruct((b, s, BRANCH_W), BF16),
        scratch_shapes=[pltpu.VMEM((SUBLANES, lw), F32), pltpu.VMEM((SUBLANES, lw), F32),
                        pltpu.VMEM((ts, lw), F32), pltpu.VMEM((ts, lw), F32), pltpu.VMEM((ts, lw), F32)],
        compiler_params=_cparams(("parallel", "parallel", "arbitrary")),
        name="rg_lru",
    )(z3, z3, conv_w, row(conv_b), wa, row(ba), wx, row(bx), row(lam))


def _compress_kernel(x_ref, pos_ref, w1_ref, b1_ref, w2_ref, o_ref, xs_ref):
    seq = x_ref.shape[1]
    nslot = seq // CMP_STRIDE
    xs_ref[...] = x_ref[0].astype(F32)
    w1 = w1_ref[0].astype(BF16)
    zeros = jnp.zeros((NSA_HD, CMP_HIDDEN), BF16)
    heads = 2
    first = [jnp.zeros((nslot, CMP_HIDDEN), F32) for _ in range(heads)]
    second = [jnp.zeros((nslot, CMP_HIDDEN), F32) for _ in range(heads)]
    for o in range(CMP_STRIDE):
        rows = xs_ref[pl.ds(o, nslot, stride=CMP_STRIDE), :].astype(BF16)
        w_a = w1[o * NSA_HD:(o + 1) * NSA_HD]
        w_b = w1[(CMP_STRIDE + o) * NSA_HD:(CMP_STRIDE + o + 1) * NSA_HD]
        for h in range(heads):
            pad = lambda w: jnp.concatenate([w, zeros] if h == 0 else [zeros, w], axis=0)
            first[h] = first[h] + jnp.dot(rows, pad(w_a), preferred_element_type=F32)
            second[h] = second[h] + jnp.dot(rows, pad(w_b), preferred_element_type=F32)
    pos = jnp.broadcast_to(pos_ref[0], (SUBLANES, CMP_BLK * NSA_HD)).astype(BF16)
    posb = jnp.dot(pos, w1, preferred_element_type=F32)[0:1, :]
    w2 = w2_ref[0].astype(BF16)
    for h in range(heads):
        hidden = first[h] + pltpu.roll(second[h], nslot - 1, 0) + posb + b1_ref[0]
        out = jnp.dot(jax.nn.gelu(hidden).astype(BF16), w2, preferred_element_type=F32)
        o_ref[0, 0, h] = out.astype(o_ref.dtype)


def nsa_compress(z3, col0, cmp_pos, w1, b1, w2):
    b, s, _ = z3.shape
    assert CMP_BLK == 2 * CMP_STRIDE and 2 * NSA_HD == LANES and col0 % LANES == 0 and s % CMP_STRIDE == 0
    nslot = s // CMP_STRIDE
    pos = cmp_pos.reshape(2, 1, CMP_BLK * NSA_HD)
    return pl.pallas_call(
        _compress_kernel,
        grid=(2, b, NSA_KV_HEADS // 2),
        in_specs=[pl.BlockSpec((1, s, LANES), lambda kv, bi, hp: (bi, 0, (col0 + kv * KV_W) // LANES + hp)),
                  pl.BlockSpec((1, 1, CMP_BLK * NSA_HD), lambda kv, bi, hp: (kv, 0, 0)),
                  pl.BlockSpec((1, CMP_BLK * NSA_HD, CMP_HIDDEN), lambda kv, bi, hp: (kv, 0, 0)),
                  pl.BlockSpec((1, 1, CMP_HIDDEN), lambda kv, bi, hp: (kv, 0, 0)),
                  pl.BlockSpec((1, CMP_HIDDEN, NSA_HD), lambda kv, bi, hp: (kv, 0, 0))],
        out_specs=pl.BlockSpec((1, 1, 2, nslot, NSA_HD), lambda kv, bi, hp: (kv, bi, hp, 0, 0)),
        out_shape=jax.ShapeDtypeStruct((2, b, NSA_KV_HEADS, nslot, NSA_HD), BF16),
        scratch_shapes=[pltpu.VMEM((s, LANES), F32)],
        compiler_params=_cparams(("parallel", "parallel", "parallel")),
        name="nsa_compress",
    )(z3, pos, w1, b1.reshape(2, 1, CMP_HIDDEN), w2)


def _exp2_bf16(s, m):
    rows = s.shape[0]
    chunk = min(EXP_CHUNK, rows)
    return jnp.concatenate([jnp.exp2((s[r:r + chunk, :] - m).astype(BF16)) for r in range(0, rows, chunk)], axis=0)


def _nsa_kernel(q_ref, qf_ref, kc_ref, vc_ref, ksb_ref, vsb_ref, kwb_ref, vwb_ref, fs_ref, fw_ref, gl_ref, c2s_ref,
                bias_ref, o_ref, ks_ref, vs_ref, kw_ref, vw_ref, qp_ref, imp_ref, s0_ref, s1_ref, mx_ref, m_ref, acc_ref,
                *, n_cmp):
    tq = q_ref.shape[1]
    lanes = NSA_GRP * tq
    seq = ks_ref.shape[0]
    n_slc = seq // SLC_BLK
    t0 = pl.multiple_of(pl.program_id(2) * tq, tq)
    head = lambda r: slice(r * tq, (r + 1) * tq)

    @pl.when(pl.program_id(2) == 0)
    def _():
        odd = pl.program_id(1) % 2 == 1

        def keys(blk_ref):
            blk = blk_ref[0].astype(F32)
            return jnp.where(odd, pltpu.roll(blk, NSA_HD, 1), blk)[:, :NSA_HD].astype(BF16)

        def values_t(blk_ref):
            vt = blk_ref[0].astype(F32).T
            return jnp.where(odd, vt[NSA_HD:], vt[:NSA_HD]).astype(BF16)

        def ones_row(n):
            first = lax.broadcasted_iota(jnp.int32, (V_ROWS - NSA_HD, n), 0) == 0
            return jnp.where(first, 1.0, 0.0).astype(BF16)

        ks_ref[:, 0:NSA_HD] = keys(ksb_ref)
        ks_ref[:, NSA_HD:] = fs_ref[...]
        vs_ref[0:NSA_HD, :] = values_t(vsb_ref)
        vs_ref[NSA_HD:, :] = ones_row(seq)
        kw_ref[0:WINDOW, 0:NSA_HD] = jnp.zeros((WINDOW, NSA_HD), BF16)
        kw_ref[WINDOW:, 0:NSA_HD] = keys(kwb_ref)
        kw_ref[:, NSA_HD:] = fw_ref[...]
        vw_ref[0:NSA_HD, 0:WINDOW] = jnp.zeros((NSA_HD, WINDOW), BF16)
        vw_ref[0:NSA_HD, WINDOW:] = values_t(vwb_ref)
        vw_ref[NSA_HD:, :] = ones_row(seq + WINDOW)

    def scores(k_tile, bias=None):
        s = jnp.dot(k_tile, qp_ref[...], preferred_element_type=F32)
        return s if bias is None else s + bias

    qt = q_ref[0].astype(F32).T * (NSA_HD ** -0.5 * LOG2E)
    qp_ref[...] = qf_ref[0]
    for r in range(NSA_GRP):
        qp_ref[0:NSA_HD, head(r)] = qt[r * NSA_HD:(r + 1) * NSA_HD, :].astype(BF16)

    s_cmp = scores(kc_ref[0, 0])
    win = []
    for off, size, bias in ((0, tq, bias_ref[1]), (tq, WINDOW - tq, None), (WINDOW, tq, bias_ref[0])):
        k0 = pl.multiple_of(t0 + off, tq)
        win.append((scores(kw_ref[pl.ds(k0, size), :], bias), vw_ref[:, pl.ds(k0, size)]))

    ncp = kc_ref.shape[2]
    n_row = lax.broadcasted_iota(jnp.int32, (ncp, lanes), 0)
    t_lane = t0 + lax.broadcasted_iota(jnp.int32, (ncp, lanes), 1) % tq
    ok = (t_lane >= n_row * CMP_STRIDE + (CMP_BLK - 1)) & (n_row < n_cmp)
    s = jnp.where(ok, s_cmp, NEG)
    p = jnp.where(ok, jnp.exp2(s - jnp.max(s, axis=0, keepdims=True)), 0.0)
    l = jnp.sum(p, axis=0, keepdims=True)
    p_cmp = p * (1.0 / jnp.where(l > 0.0, l, 1.0))
    o_cmp = jnp.dot(vc_ref[0, 0], p_cmp.astype(BF16), preferred_element_type=F32)

    rest = p_cmp[:, head(0)]
    for r in range(1, NSA_GRP):
        rest = rest + p_cmp[:, head(r)]
    imp = jnp.zeros((n_slc, tq), F32)
    for _ in range(3):
        part = rest.astype(BF16)
        rest = rest - part.astype(F32)
        imp = imp + jnp.dot(c2s_ref[...], part, preferred_element_type=F32)
    j_row = lax.broadcasted_iota(jnp.int32, (n_slc, tq), 0)
    t_col = t0 + lax.broadcasted_iota(jnp.int32, (n_slc, tq), 1)
    forced = (j_row == 0) | (j_row == t_col // SLC_BLK)
    causal = j_row * SLC_BLK <= t_col
    imp = jnp.where(forced, FORCE, jnp.where(causal, imp, -FORCE))
    imp_ref[...] = imp

    n_grp = n_slc // SUBLANES
    vals = [imp[v * SUBLANES:(v + 1) * SUBLANES, :] for v in range(n_grp)]
    sub = lax.broadcasted_iota(jnp.int32, (SUBLANES, tq), 0)
    rank = [jnp.zeros((SUBLANES, tq), jnp.int32) for _ in range(n_grp)]
    for i in range(n_slc):
        bi = jnp.broadcast_to(imp_ref[i:i + 1, :], (SUBLANES, tq))
        for v in range(n_grp):
            if v < i // SUBLANES:
                ahead = bi > vals[v]
            elif v > i // SUBLANES:
                ahead = bi >= vals[v]
            else:
                ahead = (bi > vals[v]) | ((bi == vals[v]) & (sub > i % SUBLANES))
            rank[v] = rank[v] + jnp.where(ahead, 1, 0)
    chosen = (jnp.concatenate(rank, axis=0) < min(SLC_TOPN, n_slc)) & causal
    mask = jnp.where(chosen, 0.0, NEG).astype(BF16)
    for r in range(NSA_GRP):
        qp_ref[MASK_OFF:MASK_OFF + n_slc, head(r)] = mask

    tk = min(ATT_TK, seq)
    n_full = t0 // tk
    last = n_full // 2

    s_bufs = (s0_ref, s1_ref)

    def pair_scores(step, buf):
        for par in range(2):
            k0 = pl.multiple_of((2 * step + par) * tk, tk)
            s = scores(ks_ref[pl.ds(k0, tk), :])
            s_bufs[buf][par] = s
            mx_ref[buf, par] = jnp.max(s, axis=0, keepdims=True)

    def pair_update(step, buf, prefetch):
        m_new = [jnp.maximum(m_ref[par], mx_ref[buf, par]) for par in range(2)]
        probs = [_exp2_bf16(s_bufs[buf].at[par], m_new[par]) for par in range(2)]
        if prefetch:
            pair_scores(step + 1, 1 - buf)
        for par in range(2):
            k0 = pl.multiple_of((2 * step + par) * tk, tk)
            pv = jnp.dot(vs_ref[:, pl.ds(k0, tk)], probs[par], preferred_element_type=F32)
            acc_ref[par] = jnp.exp2(m_ref[par] - m_new[par]) * acc_ref[par] + pv
            m_ref[par] = m_new[par]

    def mask_diagonal(buf):
        off = pl.multiple_of(t0 - n_full * tk, tq)
        for par in range(2):
            @pl.when(n_full % 2 == par)
            def _():
                tile = s_bufs[buf].at[par]
                tile[pl.ds(off, tq), :] = tile[pl.ds(off, tq), :] + bias_ref[0]
                mx_ref[buf, par] = jnp.max(tile[...], axis=0, keepdims=True)

    def half_trip(step, buf):
        @pl.when(step == last)
        def _():
            mask_diagonal(buf)

        @pl.when(step < last)
        def _():
            pair_update(step, buf, prefetch=True)

        @pl.when(step == last)
        def _():
            pair_update(step, buf, prefetch=False)

    def trip(i, carry):
        half_trip(2 * i, 0)
        half_trip(2 * i + 1, 1)
        return carry

    m_ref[...] = jnp.full(m_ref.shape, NEG, F32)
    acc_ref[...] = jnp.zeros(acc_ref.shape, F32)
    pair_scores(0, 0)

    mx = functools.reduce(jnp.maximum, [jnp.max(s, axis=0, keepdims=True) for s, _ in win])
    o_win = jnp.zeros((V_ROWS, lanes), F32)
    for s, v_tile in win:
        o_win = o_win + jnp.dot(v_tile, _exp2_bf16(s, mx), preferred_element_type=F32)
    o_win = o_win[0:NSA_HD] * (1.0 / o_win[NSA_HD:NSA_HD + 1])

    lax.fori_loop(0, last // 2 + 1, trip, 0)
    m_all = jnp.maximum(m_ref[0], m_ref[1])
    o_slc = jnp.exp2(m_ref[0] - m_all) * acc_ref[0] + jnp.exp2(m_ref[1] - m_all) * acc_ref[1]
    o_slc = o_slc[0:NSA_HD] * (1.0 / o_slc[NSA_HD:NSA_HD + 1])

    gates = jax.nn.sigmoid(gl_ref[0, 0].astype(F32))
    outs = []
    for r in range(NSA_GRP):
        outs.append(gates[3 * r:3 * r + 1] * o_cmp[:, head(r)] + gates[3 * r + 1:3 * r + 2] * o_slc[:, head(r)]
                    + gates[3 * r + 2:3 * r + 3] * o_win[:, head(r)])
    o_ref[0] = jnp.concatenate(outs, axis=0).T.astype(o_ref.dtype)


def _alibi_slopes():
    return (2.0 ** (-8.0 * np.arange(1, NSA_HEADS + 1) / NSA_HEADS)).astype(np.float32)


def _bf16_split3(v):
    parts, rest = [], np.asarray(v, np.float32)
    for _ in range(3):
        part = rest.astype(ml_dtypes.bfloat16).astype(np.float32)
        parts.append(part)
        rest = (rest.astype(np.float64) - part.astype(np.float64)).astype(np.float32)
    return parts


def _query_features(tq):
    feat = np.zeros((NSA_HEADS, FEAT), np.float32)
    hi, mid, lo = _bf16_split3(_alibi_slopes() * np.float32(LOG2E))
    for c, part in enumerate((hi, mid, lo, hi, mid, lo)):
        feat[:, ALIBI_OFF + c] = part
    feat[:, PAD_OFF] = NEG
    feat = np.repeat(feat.reshape(NSA_KV_HEADS, NSA_GRP, FEAT), tq, axis=1)
    return jnp.asarray(feat.transpose(0, 2, 1), BF16)


def _key_features(pos, n_slc):
    pos = np.asarray(pos)
    real = pos >= 0
    feat = np.zeros((pos.shape[0], FEAT - NSA_HD), np.float32)
    if n_slc:
        feat[real, pos[real] // SLC_BLK] = 1.0
    a0 = ALIBI_OFF - NSA_HD
    feat[real, a0:a0 + 3] = ((pos[real] // SLC_BLK) * SLC_BLK)[:, None]
    feat[real, a0 + 3:a0 + 6] = (pos[real] % SLC_BLK)[:, None]
    feat[~real, PAD_OFF - NSA_HD] = 1.0
    return feat


def _with_features(k, feat):
    b, g, _, _ = k.shape
    f = jnp.broadcast_to(jnp.asarray(feat, BF16), (b, g) + feat.shape)
    return jnp.concatenate([k.astype(BF16), f], axis=3)


def nsa_attention(z3, qcol0, kvcol0, kc, vc, gate_logits):
    b, s, _ = z3.shape
    tq = ATT_TQ
    tk = min(ATT_TK, s)
    assert s % (2 * tk) == 0 and tk % tq == 0 and tq % SLC_BLK == 0 and WINDOW % tq == 0 and WINDOW > tq
    n_slc = s // SLC_BLK
    assert n_slc <= ALIBI_OFF - MASK_OFF and n_slc % SUBLANES == 0
    assert 2 * NSA_HD == LANES and NSA_KV_HEADS % 2 == 0 and kvcol0 % LANES == 0
    n_cmp = (s - CMP_BLK) // CMP_STRIDE + 1
    ncp = kc.shape[2]
    gw = NSA_GRP * NSA_HD
    lanes = NSA_GRP * tq

    kc_f = _with_features(kc, _key_features(np.arange(ncp) * CMP_STRIDE + CMP_BLK - 1, 0))
    feat_s = jnp.asarray(_key_features(np.arange(s), n_slc), BF16)
    feat_w = jnp.asarray(_key_features(np.arange(-WINDOW, s), 0), BF16)

    c_start = np.arange(ncp) * CMP_STRIDE
    s_start = np.arange(n_slc) * SLC_BLK
    overlap = np.clip(np.minimum(c_start[:, None] + CMP_BLK, s_start[None, :] + SLC_BLK)
                      - np.maximum(c_start[:, None], s_start[None, :]), 0, None) / CMP_BLK
    overlap[n_cmp:] = 0.0
    c2s_t = jnp.asarray(overlap.T, BF16)

    key = np.arange(tq)[:, None]
    qry = np.tile(np.arange(tq), NSA_GRP)[None, :]
    bias = jnp.asarray(np.stack([np.where(key <= qry, 0.0, NEG), np.where(key > qry, 0.0, NEG)]), F32)

    blk = lambda *shape: pl.BlockSpec((1, 1) + shape, lambda bi, g, qi: (bi, g, 0, 0))
    const = lambda *shape: pl.BlockSpec(shape, lambda bi, g, qi: (0,) * len(shape))
    pair = lambda j: pl.BlockSpec((1, s, LANES), lambda bi, g, qi: (bi, 0, (kvcol0 + j * KV_W) // LANES + g // 2))
    return pl.pallas_call(
        functools.partial(_nsa_kernel, n_cmp=n_cmp),
        grid=(b, NSA_KV_HEADS, s // tq),
        in_specs=[pl.BlockSpec((1, tq, gw), lambda bi, g, qi: (bi, qi, qcol0 // gw + g)),
                  pl.BlockSpec((1, FEAT, lanes), lambda bi, g, qi: (g, 0, 0)),
                  blk(ncp, FEAT), blk(NSA_HD, ncp),
                  pair(0), pair(1), pair(2), pair(3),
                  const(s, FEAT - NSA_HD), const(s + WINDOW, FEAT - NSA_HD),
                  pl.BlockSpec((1, 1, NSA_GRP * 3, tq), lambda bi, g, qi: (bi, g, 0, qi)),
                  const(n_slc, ncp), const(2, tq, lanes)],
        out_specs=pl.BlockSpec((1, tq, gw), lambda bi, g, qi: (bi, qi, g)),
        out_shape=jax.ShapeDtypeStruct((b, s, NSA_W), BF16),
        scratch_shapes=[pltpu.VMEM((s, FEAT), BF16), pltpu.VMEM((V_ROWS, s), BF16),
                        pltpu.VMEM((s + WINDOW, FEAT), BF16), pltpu.VMEM((V_ROWS, s + WINDOW), BF16),
                        pltpu.VMEM((FEAT, lanes), BF16), pltpu.VMEM((n_slc, tq), F32),
                        pltpu.VMEM((2, tk, lanes), F32), pltpu.VMEM((2, tk, lanes), F32),
                        pltpu.VMEM((2, 2, 1, lanes), F32),
                        pltpu.VMEM((2, 1, lanes), F32), pltpu.VMEM((2, V_ROWS, lanes), F32)],
        compiler_params=_cparams(("parallel", "parallel", "arbitrary")),
        name="nsa_attention",
    )(z3, _query_features(tq), kc_f, jnp.swapaxes(vc, 2, 3), z3, z3, z3, z3, feat_s, feat_w, gate_logits, c2s_t,
      bias)


def _merge_kernel(ya_ref, yb_ref, yc_ref, wb_ref, ga_ref, gb_ref, gc_ref, o_ref):
    acc = None
    for i, (y_ref, g_ref) in enumerate(((ya_ref, ga_ref), (yb_ref, gb_ref), (yc_ref, gc_ref))):
        term = jax.nn.sigmoid(g_ref[...].astype(F32)) * jnp.dot(y_ref[...], wb_ref[i], preferred_element_type=F32)
        acc = term if acc is None else acc + term
    o_ref[...] = acc.astype(o_ref.dtype)


def branch_merge(ya, yb, yc, wb, zmerge, tm=1024, tn=1024):
    t = ya.shape[0]
    d = wb.shape[2]
    nj = d // tn
    y_spec = pl.BlockSpec((tm, BRANCH_W), lambda i, j: (i, 0))
    gate = lambda br: pl.BlockSpec((tm, tn), lambda i, j: (i, j + br * nj))
    return pl.pallas_call(
        _merge_kernel,
        grid=(t // tm, nj),
        in_specs=[y_spec, y_spec, y_spec,
                  pl.BlockSpec((3, BRANCH_W, tn), lambda i, j: (0, 0, j)),
                  gate(0), gate(1), gate(2)],
        out_specs=pl.BlockSpec((tm, tn), lambda i, j: (i, j)),
        out_shape=jax.ShapeDtypeStruct((t, d), BF16),
        compiler_params=_cparams(("parallel", "parallel")),
        name="branch_merge",
    )(ya, yb, yc, wb, zmerge, zmerge, zmerge)


def _layer_norm(y, g, b):
    mu = jnp.mean(y, axis=-1, keepdims=True)
    yc = y - mu
    var = jnp.mean(yc * yc, axis=-1, keepdims=True)
    return yc * lax.rsqrt(var + LN_EPS) * g + b


def _outproj_ln_kernel(m_ref, w_ref, x_ref, g_ref, b_ref, o_ref, ob_ref, *, alpha):
    mix = jnp.dot(m_ref[...], w_ref[...], preferred_element_type=F32)
    y = _layer_norm(alpha * x_ref[...] + mix, g_ref[...], b_ref[...])
    o_ref[...] = y
    ob_ref[...] = y.astype(BF16)


def outproj_ln(merged, w_out, x, g, b, alpha, tm=512):
    t, d = x.shape
    row = pl.BlockSpec((tm, d), lambda i: (i, 0))
    vec = pl.BlockSpec((1, d), lambda i: (0, 0))
    return pl.pallas_call(
        functools.partial(_outproj_ln_kernel, alpha=alpha),
        grid=(t // tm,),
        in_specs=[row, pl.BlockSpec((d, d), lambda i: (0, 0)), row, vec, vec],
        out_specs=[row, row],
        out_shape=[jax.ShapeDtypeStruct((t, d), F32), jax.ShapeDtypeStruct((t, d), BF16)],
        compiler_params=_cparams(("parallel",)),
        name="outproj_ln",
    )(merged, w_out, x, g.reshape(1, d), b.reshape(1, d))


def _final_kernel(x_ref, xb_ref, wg_ref, p_ref, wp_ref, y0_ref, y1_ref, gate_ref, g_ref, b_ref, o_ref, ob_ref,
                  *, alpha):
    ple = jax.nn.sigmoid(jnp.dot(xb_ref[...], wg_ref[...], preferred_element_type=F32)) * jnp.dot(
        p_ref[...].astype(BF16), wp_ref[...], preferred_element_type=F32)
    gate = gate_ref[...]
    ffn = gate[:, 0:1] * y0_ref[...].astype(F32) + gate[:, 1:2] * y1_ref[...].astype(F32)
    y = _layer_norm(alpha * x_ref[...] + ffn + ple, g_ref[...], b_ref[...])
    o_ref[...] = y
    ob_ref[...] = y.astype(BF16)


def ple_combine_ln(x, xb, w_ple_gate, p, w_ple, y0, y1, gate, g, b, alpha, tm=512):
    t, d = x.shape
    pd = p.shape[1]
    row = pl.BlockSpec((tm, d), lambda i: (i, 0))
    vec = pl.BlockSpec((1, d), lambda i: (0, 0))
    return pl.pallas_call(
        functools.partial(_final_kernel, alpha=alpha),
        grid=(t // tm,),
        in_specs=[row, row, pl.BlockSpec((d, d), lambda i: (0, 0)),
                  pl.BlockSpec((tm, pd), lambda i: (i, 0)), pl.BlockSpec((pd, d), lambda i: (0, 0)),
                  row, row, pl.BlockSpec((tm, TOP_K), lambda i: (i, 0)), vec, vec],
        out_specs=[row, row],
        out_shape=[jax.ShapeDtypeStruct((t, d), F32), jax.ShapeDtypeStruct((t, d), BF16)],
        compiler_params=_cparams(("parallel",)),
        name="ple_combine_ln",
    )(x, xb, w_ple_gate, p, w_ple, y0, y1, gate, g.reshape(1, d), b.reshape(1, d))


def _router_kernel(x_ref, wr_ref, br_ref, idx_ref, gate_ref):
    tm = x_ref.shape[0]
    nt = (((1,), (1,)), ((), ()))
    x = x_ref[...]
    x_hi = x.astype(BF16)
    x_lo = (x - x_hi.astype(F32)).astype(BF16)
    w = wr_ref[...]
    w_hi = w.astype(BF16)
    w_lo = (w - w_hi.astype(F32)).astype(BF16)
    both = lax.dot_general(jnp.concatenate([w_hi, w_lo], axis=0), x_hi, nt, preferred_element_type=F32)
    logits = both[:N_EXPERTS] + both[N_EXPERTS:] + lax.dot_general(w_hi, x_lo, nt, preferred_element_type=F32)
    aff = jax.nn.sigmoid(logits)
    sel = aff + br_ref[...]
    scores = []
    for g in range(N_GROUPS):
        v = [sel[g * EXPERTS_PER_GROUP + e:g * EXPERTS_PER_GROUP + e + 1, :] for e in range(EXPERTS_PER_GROUP)]
        best = None
        for a in range(EXPERTS_PER_GROUP):
            for c in range(a + 1, EXPERTS_PER_GROUP):
                pair = v[a] + v[c]
                best = pair if best is None else jnp.maximum(best, pair)
        scores.append(best)
    best_score, best_group = scores[0], jnp.zeros((1, tm), jnp.int32)
    for g in range(1, N_GROUPS):
        better = scores[g] > best_score
        best_score = jnp.where(better, scores[g], best_score)
        best_group = jnp.where(better, g, best_group)
    e_row = lax.broadcasted_iota(jnp.int32, (N_EXPERTS, tm), 0)
    cand = jnp.where(e_row // EXPERTS_PER_GROUP == best_group, sel, NEG)
    picks, gates = [], []
    for _ in range(TOP_K):
        mx = jnp.max(cand, axis=0, keepdims=True)
        pick = jnp.min(jnp.where(cand == mx, e_row, N_EXPERTS), axis=0, keepdims=True)
        hit = e_row == pick
        picks.append(pick)
        gates.append(jnp.sum(jnp.where(hit, aff, 0.0), axis=0, keepdims=True))
        cand = jnp.where(hit, -jnp.inf, cand)
    total = gates[0] + gates[1]
    idx_ref[...] = jnp.concatenate(picks, axis=0)
    gate_ref[...] = jnp.concatenate([gates[0] / total, gates[1] / total], axis=0)


def router(x, w_router, b_router, tm=1024):
    t, d = x.shape
    return pl.pallas_call(
        _router_kernel,
        grid=(t // tm,),
        in_specs=[pl.BlockSpec((tm, d), lambda i: (i, 0)),
                  pl.BlockSpec((N_EXPERTS, d), lambda i: (0, 0)),
                  pl.BlockSpec((N_EXPERTS, 1), lambda i: (0, 0))],
        out_specs=[pl.BlockSpec((TOP_K, tm), lambda i: (0, i)), pl.BlockSpec((TOP_K, tm), lambda i: (0, i))],
        out_shape=[jax.ShapeDtypeStruct((TOP_K, t), jnp.int32), jax.ShapeDtypeStruct((TOP_K, t), F32)],
        compiler_params=_cparams(("parallel",)),
        name="router",
    )(x, w_router.T, b_router.reshape(N_EXPERTS, 1))


def _expert_kernel(be_ref, nu_ref, *refs, part_starts):
    n_parts = len(part_starts)
    x_refs = refs[:n_parts]
    wg_ref, wu_ref, wd_ref = refs[n_parts:n_parts + 3]
    prev_ref = refs[n_parts + 3] if len(refs) == n_parts + 8 else None
    o_ref, wgb_ref, wub_ref, wdb_ref = refs[-4:]
    i = pl.program_id(0)

    @pl.when((i == 0) | (be_ref[i] != be_ref[jnp.maximum(i - 1, 0)]))
    def _():
        wgb_ref[...] = wg_ref[0, 0].astype(BF16)
        wub_ref[...] = wu_ref[0, 0].astype(BF16)
        wdb_ref[...] = wd_ref[0, 0].astype(BF16)

    @pl.when(i < nu_ref[0])
    def _():
        x = x_refs[0][...]
        for start, x_ref in zip(part_starts[1:], x_refs[1:]):
            x = jnp.where(i >= start, x_ref[...], x)
        g = jnp.dot(x, wgb_ref[...], preferred_element_type=F32)
        u = jnp.dot(x, wub_ref[...], preferred_element_type=F32)
        out = jnp.dot((jax.nn.silu(g) * u).astype(BF16), wdb_ref[...], preferred_element_type=F32)
        if prev_ref is not None:
            out = out + prev_ref[...].astype(F32)
        o_ref[...] = out.astype(o_ref.dtype)

    @pl.when(i >= nu_ref[0])
    def _():
        o_ref[...] = jnp.zeros_like(o_ref)


def expert_ffn(block_expert, n_used, x_parts, w_gate_up, w_down, layer):
    d = x_parts[0].shape[1]
    n_rows = sum(x.shape[0] for x in x_parts)
    de = w_down.shape[2]
    ce = de // EXPERT_CHUNKS
    part_blocks = [x.shape[0] // MOE_BLK for x in x_parts]
    part_starts = tuple(int(v) for v in np.cumsum([0] + part_blocks[:-1]))
    part_specs = [pl.BlockSpec((MOE_BLK, d), lambda i, be, nu, o=o, n=n: (jnp.clip(i - o, 0, n - 1), 0))
                  for o, n in zip(part_starts, part_blocks)]
    out = None
    for c in range(EXPERT_CHUNKS):
        row = pl.BlockSpec((MOE_BLK, d), lambda i, be, nu: (i, 0))
        in_specs = part_specs + [
            pl.BlockSpec((1, 1, d, ce), lambda i, be, nu, c=c: (layer, be[i], 0, c)),
            pl.BlockSpec((1, 1, d, ce), lambda i, be, nu, c=c: (layer, be[i], 0, EXPERT_CHUNKS + c)),
            pl.BlockSpec((1, 1, ce, d), lambda i, be, nu, c=c: (layer, be[i], c, 0))]
        args = list(x_parts) + [w_gate_up, w_gate_up, w_down]
        if out is not None:
            in_specs.append(row)
            args.append(out)
        out = pl.pallas_call(
            functools.partial(_expert_kernel, part_starts=part_starts),
            grid_spec=pltpu.PrefetchScalarGridSpec(
                num_scalar_prefetch=2,
                grid=(n_rows // MOE_BLK,),
                in_specs=in_specs,
                out_specs=row,
                scratch_shapes=[pltpu.VMEM((d, ce), BF16), pltpu.VMEM((d, ce), BF16), pltpu.VMEM((ce, d), BF16)]),
            out_shape=jax.ShapeDtypeStruct((n_rows, d), BF16),
            compiler_params=_cparams(("arbitrary",)),
            name="expert_ffn",
        )(block_expert, n_used, *args)
    return out


def _dispatch_tables(expert_idx_t):
    n_tok = expert_idx_t.shape[1]
    n_assign = n_tok * TOP_K
    e_flat = expert_idx_t.reshape(n_assign)
    onehot = e_flat[:, None] == jnp.arange(N_EXPERTS)[None, :]
    chunk = min(1024, n_assign)
    oh = onehot.astype(F32).reshape(n_assign // chunk, chunk, N_EXPERTS)
    within = jnp.einsum("ij,bjk->bik", jnp.tril(jnp.ones((chunk, chunk), F32)), oh)
    totals = within[:, -1, :]
    csum = (within + (jnp.cumsum(totals, axis=0) - totals)[:, None, :]).reshape(n_assign, N_EXPERTS)
    rank = jnp.sum(jnp.where(onehot, csum, 0.0), axis=1).astype(jnp.int32) - 1
    counts = csum[-1].astype(jnp.int32)
    padded = (counts + MOE_BLK - 1) // MOE_BLK * MOE_BLK
    pad_end = jnp.cumsum(padded)
    pad_start = pad_end - padded
    dest = pad_start[e_flat] + rank
    n_rows = n_assign + N_EXPERTS * MOE_BLK
    block_start = jnp.arange(n_rows // MOE_BLK, dtype=jnp.int32) * MOE_BLK
    block_expert = jnp.minimum(jnp.sum(pad_end[None, :] <= block_start[:, None], axis=1), N_EXPERTS - 1)
    order = jnp.sort(e_flat * n_assign + jnp.arange(n_assign, dtype=jnp.int32)) % n_assign
    row_expert = jnp.repeat(block_expert, MOE_BLK)
    row_rank = jnp.arange(n_rows, dtype=jnp.int32) - pad_start[row_expert]
    source = order[jnp.clip((jnp.cumsum(counts) - counts)[row_expert] + row_rank, 0, n_assign - 1)]
    row_tok = jnp.where(row_rank < counts[row_expert], source % n_tok, 0)
    n_used = (pad_end[-1:] // MOE_BLK).astype(jnp.int32)
    return row_tok.astype(jnp.int32), block_expert.astype(jnp.int32), n_used, dest.reshape(TOP_K, n_tok)


def _token_mixer(xb, bsz, seq, w_in_t, layer, conv_a_w, conv_a_b, lru_conv_w, lru_conv_b, lru_wa, lru_ba, lru_wx, lru_bx,
                 lru_lam, cmp_pos, phi_w1, phi_b1, phi_w2, w_branch):
    d = xb.shape[1]
    n_tok = bsz * seq
    main_w = 3 * BRANCH_W + 2 * BRANCH_W + NSA_W + 6 * KV_W
    ng = 3 * NSA_HEADS
    kv0 = main_w - 6 * KV_W
    zmain = matmul_wres(xb, w_in_t, layer, 0, main_w, BF16, 2048, 768)
    zgate = matmul_wres(xb, w_in_t, layer, main_w, LANES, F32, 2048, LANES)
    zmerge = matmul_wres(xb, w_in_t, layer, main_w + ng, w_in_t.shape[1] - main_w - ng, BF16, 2048, 512)
    z3 = zmain.reshape(bsz, seq, main_w)

    y_a = conv_mixer(z3, conv_a_w, conv_a_b)
    y_b = lru_mixer(z3, 3 * BRANCH_W, lru_conv_w, lru_conv_b, lru_wa, lru_ba, lru_wx, lru_bx, lru_lam)

    cmp = nsa_compress(z3, kv0, cmp_pos, phi_w1, phi_b1, phi_w2)
    gate_logits = zgate[:, :ng].reshape(bsz, seq, NSA_KV_HEADS, NSA_GRP * 3).transpose(0, 2, 3, 1)
    y_c = nsa_attention(z3, 5 * BRANCH_W, kv0 + 2 * KV_W, cmp[0], cmp[1], gate_logits)

    flat = lambda y: y.reshape(n_tok, BRANCH_W)
    return branch_merge(flat(y_a), flat(y_b), flat(y_c), w_branch.astype(BF16), zmerge)


def _moe(x, xb, w_router, b_router, w_gate_up, w_down, layer):
    idx_t, gate_t = router(x, w_router, b_router)
    row_tok, block_expert, n_used, dest = _dispatch_tables(idx_t)
    n_tok = xb.shape[0]
    x_parts = [xb[row_tok[a:a + n_tok]] for a in range(0, row_tok.shape[0], n_tok)]
    rows = expert_ffn(block_expert, n_used, x_parts, w_gate_up, w_down, layer)
    return rows[dest[0]], rows[dest[1]], gate_t.T


def kernel(x, p, w_in, conv_a_w, conv_a_b, lru_conv_w, lru_conv_b, lru_wa, lru_ba, lru_wx, lru_bx, lru_lam, cmp_pos, phi_w1, phi_b1, phi_w2, w_branch, w_out, ln_g, ln_b, w_router, b_router, w_gate_up, w_down, w_ple, w_ple_gate):
    bsz, seq, d = x.shape
    depth = w_in.shape[0]
    n_tok = bsz * seq
    alpha = (2 * depth) ** 0.25
    xf = x.reshape(n_tok, d)
    xb = xf.astype(BF16)
    w_in_t = jnp.swapaxes(w_in, 1, 2)
    for i in range(depth):
        merged = _token_mixer(xb, bsz, seq, w_in_t, i, conv_a_w[i], conv_a_b[i], lru_conv_w[i], lru_conv_b[i],
                              lru_wa[i], lru_ba[i], lru_wx[i], lru_bx[i], lru_lam[i], cmp_pos[i], phi_w1[i],
                              phi_b1[i], phi_w2[i], w_branch[i])
        xf, xb = outproj_ln(merged, w_out[i].astype(BF16), xf, ln_g[i, 0], ln_b[i, 0], alpha)
        y0, y1, gate = _moe(xf, xb, w_router, b_router, w_gate_up, w_down, i)
        xf, xb = ple_combine_ln(xf, xb, w_ple_gate[i].astype(BF16), p[i].reshape(n_tok, -1),
                                w_ple[i].astype(BF16), y0, y1, gate, ln_g[i, 1], ln_b[i, 1], alpha)
    return xf.reshape(bsz, seq, d)
```

```python
import functools

import ml_dtypes
import numpy as np
import jax
import jax.numpy as jnp
from jax import lax
from jax.experimental import pallas as pl
from jax.experimental.pallas import tpu as pltpu

F32 = jnp.float32
BF16 = jnp.bfloat16

BRANCH_W = 1024
CONV_K = 3
LRU_HEADS = 8
LRU_HD = BRANCH_W // LRU_HEADS
LRU_CONV_K = 4
LRU_C = 8.0
NSA_HEADS = 16
NSA_KV_HEADS = 4
NSA_GRP = NSA_HEADS // NSA_KV_HEADS
NSA_HD = 64
NSA_W = NSA_HEADS * NSA_HD
KV_W = NSA_KV_HEADS * NSA_HD
CMP_BLK = 32
CMP_STRIDE = 16
CMP_HIDDEN = 128
SLC_BLK = 64
SLC_TOPN = 16
WINDOW = 512
N_EXPERTS = 16
N_GROUPS = 4
EXPERTS_PER_GROUP = N_EXPERTS // N_GROUPS
TOP_K = 2
LN_EPS = 1e-5
NEG = -1e30
FORCE = 1e9

LANES = 128
SUBLANES = 8
VMEM_LIMIT = 56 * 1024 * 1024

FEAT = 256
MASK_OFF = NSA_HD
ALIBI_OFF = 2 * NSA_HD
PAD_OFF = ALIBI_OFF + 6
V_ROWS = NSA_HD + 16
LOG2E = 1.4426950408889634

ATT_TQ = 256
ATT_TK = 512
EXP_CHUNK = 64
MOE_BLK = 256
EXPERT_CHUNKS = 2


def _cparams(sem):
    return pltpu.CompilerParams(dimension_semantics=sem, vmem_limit_bytes=VMEM_LIMIT)


def _mm_wres_kernel(x_ref, w_ref, *rest, shift):
    o_ref, wb_ref = rest[-2:]

    @pl.when(pl.program_id(1) == 0)
    def _():
        w = w_ref[0]
        if shift:
            tn = w.shape[0]
            w = jnp.concatenate([w, rest[0][0]], axis=0)[shift:shift + tn]
        wb_ref[...] = w.astype(BF16)

    o_ref[...] = lax.dot_general(x_ref[...], wb_ref[...], (((1,), (1,)), ((), ())),
                                 preferred_element_type=F32).astype(o_ref.dtype)


def matmul_wres(x, wt3, layer, row0, n, out_dtype, tm, tn):
    m, k = x.shape
    shift = row0 % tn
    base = row0 - shift
    assert m % tm == 0 and n % tn == 0 and shift % SUBLANES == 0 and shift <= LANES and tn % LANES == 0
    in_specs = [pl.BlockSpec((tm, k), lambda j, i: (i, 0)),
                pl.BlockSpec((1, tn, k), lambda j, i: (layer, base // tn + j, 0))]
    args = [x, wt3]
    if shift:
        in_specs.append(pl.BlockSpec((1, LANES, k), lambda j, i: (layer, (base + (j + 1) * tn) // LANES, 0)))
        args.append(wt3)
    return pl.pallas_call(
        functools.partial(_mm_wres_kernel, shift=shift),
        grid=(n // tn, m // tm),
        in_specs=in_specs,
        out_specs=pl.BlockSpec((tm, tn), lambda j, i: (i, j)),
        out_shape=jax.ShapeDtypeStruct((m, n), out_dtype),
        scratch_shapes=[pltpu.VMEM((tn, k), BF16)],
        compiler_params=_cparams(("parallel", "arbitrary")),
        name="matmul_wres",
    )(*args)


def _causal_conv(ext, cur, w, k):
    acc = cur * w[k - 1:k, :]
    for j in range(k - 1):
        shift = k - 1 - j
        acc = acc + pltpu.roll(ext, shift, 0)[SUBLANES:, :] * w[j:j + 1, :]
    return acc


def _conv_mixer_kernel(ain_ref, ab_ref, ac_ref, w_ref, b_ref, o_ref, halo_ref):
    @pl.when(pl.program_id(2) == 0)
    def _():
        halo_ref[...] = jnp.zeros_like(halo_ref)

    v = ac_ref[0].astype(F32) * ain_ref[0].astype(F32)
    ext = jnp.concatenate([halo_ref[...], v], axis=0)
    y = _causal_conv(ext, v, w_ref[...], CONV_K) + b_ref[...]
    o_ref[0] = (ab_ref[0].astype(F32) * y).astype(o_ref.dtype)
    halo_ref[...] = v[v.shape[0] - SUBLANES:, :]


def conv_mixer(z3, conv_w, conv_b, ts=512, cw=512):
    b, s, _ = z3.shape
    nc = BRANCH_W // cw
    return pl.pallas_call(
        _conv_mixer_kernel,
        grid=(b, nc, s // ts),
        in_specs=[pl.BlockSpec((1, ts, cw), lambda bi, c, si: (bi, si, c)),
                  pl.BlockSpec((1, ts, cw), lambda bi, c, si: (bi, si, c + nc)),
                  pl.BlockSpec((1, ts, cw), lambda bi, c, si: (bi, si, c + 2 * nc)),
                  pl.BlockSpec((CONV_K, cw), lambda bi, c, si: (0, c)),
                  pl.BlockSpec((1, cw), lambda bi, c, si: (0, c))],
        out_specs=pl.BlockSpec((1, ts, cw), lambda bi, c, si: (bi, si, c)),
        out_shape=jax.ShapeDtypeStruct((b, s, BRANCH_W), BF16),
        scratch_shapes=[pltpu.VMEM((SUBLANES, cw), F32)],
        compiler_params=_cparams(("parallel", "parallel", "arbitrary")),
        name="conv_mixer",
    )(z3, z3, z3, conv_w, conv_b.reshape(1, BRANCH_W))


def _lru_kernel(gate_ref, rin_ref, cw_ref, cb_ref, wa_ref, ba_ref, wx_ref, bx_ref, lam_ref, o_ref,
                halo_ref, carry_ref, a_ref, b_ref, h_ref):
    ts, lw = a_ref.shape

    @pl.when(pl.program_id(2) == 0)
    def _():
        halo_ref[...] = jnp.zeros_like(halo_ref)
        carry_ref[...] = jnp.zeros_like(carry_ref)

    u = rin_ref[0].astype(F32)
    ext = jnp.concatenate([halo_ref[...], u], axis=0)
    xc = _causal_conv(ext, u, cw_ref[...], LRU_CONV_K) + cb_ref[...]
    halo_ref[...] = u[ts - SUBLANES:, :]

    xcb = xc.astype(BF16)
    ra, ix = [], []
    for hh in range(lw // LRU_HD):
        xh = xcb[:, hh * LRU_HD:(hh + 1) * LRU_HD]
        ra.append(jnp.dot(xh, wa_ref[hh].astype(BF16), preferred_element_type=F32))
        ix.append(jnp.dot(xh, wx_ref[hh].astype(BF16), preferred_element_type=F32))
    r = jax.nn.sigmoid(jnp.concatenate(ra, axis=1) + ba_ref[...])
    i = jax.nn.sigmoid(jnp.concatenate(ix, axis=1) + bx_ref[...])
    neg_lam = -lam_ref[...]
    softplus = jnp.maximum(neg_lam, 0.0) + jnp.log1p(jnp.exp(-jnp.abs(neg_lam)))
    log_a = -LRU_C * r * softplus
    a_ref[...] = jnp.exp(log_a)
    th = jnp.tanh(log_a)
    b_ref[...] = jnp.sqrt(-2.0 * th / (1.0 - th)) * (i * xc)

    row = lax.broadcasted_iota(jnp.int32, (SUBLANES, lw), 0)

    def group(g, hc):
        sl = pl.ds(pl.multiple_of(g * SUBLANES, SUBLANES), SUBLANES)
        a = a_ref[sl, :]
        bv = b_ref[sl, :]
        for d in (1, 2, 4):
            keep = row >= d
            bv = jnp.where(keep, a * pltpu.roll(bv, d, 0) + bv, bv)
            a = jnp.where(keep, a * pltpu.roll(a, d, 0), a)
        h = a * hc + bv
        h_ref[sl, :] = h
        return jnp.broadcast_to(h[SUBLANES - 1:SUBLANES, :], (SUBLANES, lw))

    carry_ref[...] = lax.fori_loop(0, ts // SUBLANES, group, carry_ref[...])
    o_ref[0] = (h_ref[...] * jax.nn.gelu(gate_ref[0].astype(F32))).astype(o_ref.dtype)


def lru_mixer(z3, col0, conv_w, conv_b, wa, ba, wx, bx, lam, ts=512, lw=256):
    b, s, _ = z3.shape
    nc = BRANCH_W // lw
    c0 = col0 // lw
    hpb = lw // LRU_HD
    row = lambda v: v.reshape(1, BRANCH_W)
    vec = pl.BlockSpec((1, lw), lambda bi, c, si: (0, c))
    mat = pl.BlockSpec((hpb, LRU_HD, LRU_HD), lambda bi, c, si: (c, 0, 0))
    return pl.pallas_call(
        _lru_kernel,
        grid=(b, nc, s // ts),
        in_specs=[pl.BlockSpec((1, ts, lw), lambda bi, c, si: (bi, si, c0 + c)),
                  pl.BlockSpec((1, ts, lw), lambda bi, c, si: (bi, si, c0 + nc + c)),
                  pl.BlockSpec((LRU_CONV_K, lw), lambda bi, c, si: (0, c)),
                  vec, mat, vec, mat, vec, vec],
        out_specs=pl.BlockSpec((1, ts, lw), lambda bi, c, si: (bi, si, c)),
        out_shape=jax.ShapeDtypeStruct((b, s, BRANCH_W), BF16),
        scratch_shapes=[pltpu.VMEM((SUBLANES, lw), F32), pltpu.VMEM((SUBLANES, lw), F32),
                        pltpu.VMEM((ts, lw), F32), pltpu.VMEM((ts, lw), F32), pltpu.VMEM((ts, lw), F32)],
        compiler_params=_cparams(("parallel", "parallel", "arbitrary")),
        name="rg_lru",
    )(z3, z3, conv_w, row(conv_b), wa, row(ba), wx, row(bx), row(lam))


def _compress_kernel(x_ref, pos_ref, w1_ref, b1_ref, w2_ref, o_ref, xs_ref):
    seq = x_ref.shape[1]
    nslot = seq // CMP_STRIDE
    xs_ref[...] = x_ref[0].astype(F32)
    w1 = w1_ref[0].astype(BF16)
    zeros = jnp.zeros((NSA_HD, CMP_HIDDEN), BF16)
    heads = 2
    first = [jnp.zeros((nslot, CMP_HIDDEN), F32) for _ in range(heads)]
    second = [jnp.zeros((nslot, CMP_HIDDEN), F32) for _ in range(heads)]
    for o in range(CMP_STRIDE):
        rows = xs_ref[pl.ds(o, nslot, stride=CMP_STRIDE), :].astype(BF16)
        w_a = w1[o * NSA_HD:(o + 1) * NSA_HD]
        w_b = w1[(CMP_STRIDE + o) * NSA_HD:(CMP_STRIDE + o + 1) * NSA_HD]
        for h in range(heads):
            pad = lambda w: jnp.concatenate([w, zeros] if h == 0 else [zeros, w], axis=0)
            first[h] = first[h] + jnp.dot(rows, pad(w_a), preferred_element_type=F32)
            second[h] = second[h] + jnp.dot(rows, pad(w_b), preferred_element_type=F32)
    pos = jnp.broadcast_to(pos_ref[0], (SUBLANES, CMP_BLK * NSA_HD)).astype(BF16)
    posb = jnp.dot(pos, w1, preferred_element_type=F32)[0:1, :]
    w2 = w2_ref[0].astype(BF16)
    for h in range(heads):
        hidden = first[h] + pltpu.roll(second[h], nslot - 1, 0) + posb + b1_ref[0]
        out = jnp.dot(jax.nn.gelu(hidden).astype(BF16), w2, preferred_element_type=F32)
        o_ref[0, 0, h] = out.astype(o_ref.dtype)


def nsa_compress(z3, col0, cmp_pos, w1, b1, w2):
    b, s, _ = z3.shape
    assert CMP_BLK == 2 * CMP_STRIDE and 2 * NSA_HD == LANES and col0 % LANES == 0 and s % CMP_STRIDE == 0
    nslot = s // CMP_STRIDE
    pos = cmp_pos.reshape(2, 1, CMP_BLK * NSA_HD)
    return pl.pallas_call(
        _compress_kernel,
        grid=(2, b, NSA_KV_HEADS // 2),
        in_specs=[pl.BlockSpec((1, s, LANES), lambda kv, bi, hp: (bi, 0, (col0 + kv * KV_W) // LANES + hp)),
                  pl.BlockSpec((1, 1, CMP_BLK * NSA_HD), lambda kv, bi, hp: (kv, 0, 0)),
                  pl.BlockSpec((1, CMP_BLK * NSA_HD, CMP_HIDDEN), lambda kv, bi, hp: (kv, 0, 0)),
                  pl.BlockSpec((1, 1, CMP_HIDDEN), lambda kv, bi, hp: (kv, 0, 0)),
                  pl.BlockSpec((1, CMP_HIDDEN, NSA_HD), lambda kv, bi, hp: (kv, 0, 0))],
        out_specs=pl.BlockSpec((1, 1, 2, nslot, NSA_HD), lambda kv, bi, hp: (kv, bi, hp, 0, 0)),
        out_shape=jax.ShapeDtypeStruct((2, b, NSA_KV_HEADS, nslot, NSA_HD), BF16),
        scratch_shapes=[pltpu.VMEM((s, LANES), F32)],
        compiler_params=_cparams(("parallel", "parallel", "parallel")),
        name="nsa_compress",
    )(z3, pos, w1, b1.reshape(2, 1, CMP_HIDDEN), w2)


def _exp2_bf16(s, m):
    rows = s.shape[0]
    chunk = min(EXP_CHUNK, rows)
    return jnp.concatenate([jnp.exp2((s[r:r + chunk, :] - m).astype(BF16)) for r in range(0, rows, chunk)], axis=0)


def _nsa_kernel(q_ref, qf_ref, kc_ref, vc_ref, ksb_ref, vsb_ref, kwb_ref, vwb_ref, fs_ref, fw_ref, gl_ref, c2s_ref,
                bias_ref, o_ref, ks_ref, vs_ref, kw_ref, vw_ref, qp_ref, imp_ref, s0_ref, s1_ref, mx_ref, m_ref, acc_ref,
                *, n_cmp):
    tq = q_ref.shape[1]
    lanes = NSA_GRP * tq
    seq = ks_ref.shape[0]
    n_slc = seq // SLC_BLK
    t0 = pl.multiple_of(pl.program_id(2) * tq, tq)
    head = lambda r: slice(r * tq, (r + 1) * tq)

    @pl.when(pl.program_id(2) == 0)
    def _():
        odd = pl.program_id(1) % 2 == 1

        def keys(blk_ref):
            blk = blk_ref[0].astype(F32)
            return jnp.where(odd, pltpu.roll(blk, NSA_HD, 1), blk)[:, :NSA_HD].astype(BF16)

        def values_t(blk_ref):
            vt = blk_ref[0].astype(F32).T
            return jnp.where(odd, vt[NSA_HD:], vt[:NSA_HD]).astype(BF16)

        def ones_row(n):
            first = lax.broadcasted_iota(jnp.int32, (V_ROWS - NSA_HD, n), 0) == 0
            return jnp.where(first, 1.0, 0.0).astype(BF16)

        ks_ref[:, 0:NSA_HD] = keys(ksb_ref)
        ks_ref[:, NSA_HD:] = fs_ref[...]
        vs_ref[0:NSA_HD, :] = values_t(vsb_ref)
        vs_ref[NSA_HD:, :] = ones_row(seq)
        kw_ref[0:WINDOW, 0:NSA_HD] = jnp.zeros((WINDOW, NSA_HD), BF16)
        kw_ref[WINDOW:, 0:NSA_HD] = keys(kwb_ref)
        kw_ref[:, NSA_HD:] = fw_ref[...]
        vw_ref[0:NSA_HD, 0:WINDOW] = jnp.zeros((NSA_HD, WINDOW), BF16)
        vw_ref[0:NSA_HD, WINDOW:] = values_t(vwb_ref)
        vw_ref[NSA_HD:, :] = ones_row(seq + WINDOW)

    def scores(k_tile, bias=None):
        s = jnp.dot(k_tile, qp_ref[...], preferred_element_type=F32)
        return s if bias is None else s + bias

    qt = q_ref[0].astype(F32).T * (NSA_HD ** -0.5 * LOG2E)
    qp_ref[...] = qf_ref[0]
    for r in range(NSA_GRP):
        qp_ref[0:NSA_HD, head(r)] = qt[r * NSA_HD:(r + 1) * NSA_HD, :].astype(BF16)

    s_cmp = scores(kc_ref[0, 0])
    win = []
    for off, size, bias in ((0, tq, bias_ref[1]), (tq, WINDOW - tq, None), (WINDOW, tq, bias_ref[0])):
        k0 = pl.multiple_of(t0 + off, tq)
        win.append((scores(kw_ref[pl.ds(k0, size), :], bias), vw_ref[:, pl.ds(k0, size)]))

    ncp = kc_ref.shape[2]
    n_row = lax.broadcasted_iota(jnp.int32, (ncp, lanes), 0)
    t_lane = t0 + lax.broadcasted_iota(jnp.int32, (ncp, lanes), 1) % tq
    ok = (t_lane >= n_row * CMP_STRIDE + (CMP_BLK - 1)) & (n_row < n_cmp)
    s = jnp.where(ok, s_cmp, NEG)
    p = jnp.where(ok, jnp.exp2(s - jnp.max(s, axis=0, keepdims=True)), 0.0)
    l = jnp.sum(p, axis=0, keepdims=True)
    p_cmp = p * (1.0 / jnp.where(l > 0.0, l, 1.0))
    o_cmp = jnp.dot(vc_ref[0, 0], p_cmp.astype(BF16), preferred_element_type=F32)

    rest = p_cmp[:, head(0)]
    for r in range(1, NSA_GRP):
        rest = rest + p_cmp[:, head(r)]
    imp = jnp.zeros((n_slc, tq), F32)
    for _ in range(3):
        part = rest.astype(BF16)
        rest = rest - part.astype(F32)
        imp = imp + jnp.dot(c2s_ref[...], part, preferred_element_type=F32)
    j_row = lax.broadcasted_iota(jnp.int32, (n_slc, tq), 0)
    t_col = t0 + lax.broadcasted_iota(jnp.int32, (n_slc, tq), 1)
    forced = (j_row == 0) | (j_row == t_col // SLC_BLK)
    causal = j_row * SLC_BLK <= t_col
    imp = jnp.where(forced, FORCE, jnp.where(causal, imp, -FORCE))
    imp_ref[...] = imp

    n_grp = n_slc // SUBLANES
    vals = [imp[v * SUBLANES:(v + 1) * SUBLANES, :] for v in range(n_grp)]
    sub = lax.broadcasted_iota(jnp.int32, (SUBLANES, tq), 0)
    rank = [jnp.zeros((SUBLANES, tq), jnp.int32) for _ in range(n_grp)]
    for i in range(n_slc):
        bi = jnp.broadcast_to(imp_ref[i:i + 1, :], (SUBLANES, tq))
        for v in range(n_grp):
            if v < i // SUBLANES:
                ahead = bi > vals[v]
            elif v > i // SUBLANES:
                ahead = bi >= vals[v]
            else:
                ahead = (bi > vals[v]) | ((bi == vals[v]) & (sub > i % SUBLANES))
            rank[v] = rank[v] + jnp.where(ahead, 1, 0)
    chosen = (jnp.concatenate(rank, axis=0) < min(SLC_TOPN, n_slc)) & causal
    mask = jnp.where(chosen, 0.0, NEG).astype(BF16)
    for r in range(NSA_GRP):
        qp_ref[MASK_OFF:MASK_OFF + n_slc, head(r)] = mask

    tk = min(ATT_TK, seq)
    n_full = t0 // tk
    last = n_full // 2

    s_bufs = (s0_ref, s1_ref)

    def pair_scores(step, buf):
        for par in range(2):
            k0 = pl.multiple_of((2 * step + par) * tk, tk)
            s = scores(ks_ref[pl.ds(k0, tk), :])
            s_bufs[buf][par] = s
            mx_ref[buf, par] = jnp.max(s, axis=0, keepdims=True)

    def pair_update(step, buf, prefetch, tiles=(0, 1)):
        m_new = {par: jnp.maximum(m_ref[par], mx_ref[buf, par]) for par in tiles}
        probs = {par: _exp2_bf16(s_bufs[buf].at[par], m_new[par]) for par in tiles}
        if prefetch:
            pair_scores(step + 1, 1 - buf)
        for par in tiles:
            k0 = pl.multiple_of((2 * step + par) * tk, tk)
            pv = jnp.dot(vs_ref[:, pl.ds(k0, tk)], probs[par], preferred_element_type=F32)
            acc_ref[par] = jnp.exp2(m_ref[par] - m_new[par]) * acc_ref[par] + pv
            m_ref[par] = m_new[par]

    def mask_diagonal(buf):
        off = pl.multiple_of(t0 - n_full * tk, tq)
        for par in range(2):
            @pl.when(n_full % 2 == par)
            def _():
                tile = s_bufs[buf].at[par]
                tile[pl.ds(off, tq), :] = tile[pl.ds(off, tq), :] + bias_ref[0]
                mx_ref[buf, par] = jnp.max(tile[...], axis=0, keepdims=True)

    def half_trip(step, buf):
        @pl.when(step == last)
        def _():
            mask_diagonal(buf)

        @pl.when(step < last)
        def _():
            pair_update(step, buf, prefetch=True)

        @pl.when((step == last) & (n_full % 2 == 1))
        def _():
            pair_update(step, buf, prefetch=False)

        @pl.when((step == last) & (n_full % 2 == 0))
        def _():
            pair_update(step, buf, prefetch=False, tiles=(0,))

    def trip(i, carry):
        half_trip(2 * i, 0)
        half_trip(2 * i + 1, 1)
        return carry

    m_ref[...] = jnp.full(m_ref.shape, NEG, F32)
    acc_ref[...] = jnp.zeros(acc_ref.shape, F32)
    pair_scores(0, 0)

    mx = functools.reduce(jnp.maximum, [jnp.max(s, axis=0, keepdims=True) for s, _ in win])
    o_win = jnp.zeros((V_ROWS, lanes), F32)
    for s, v_tile in win:
        o_win = o_win + jnp.dot(v_tile, _exp2_bf16(s, mx), preferred_element_type=F32)
    o_win = o_win[0:NSA_HD] * (1.0 / o_win[NSA_HD:NSA_HD + 1])

    lax.fori_loop(0, last // 2 + 1, trip, 0)
    m_all = jnp.maximum(m_ref[0], m_ref[1])
    o_slc = jnp.exp2(m_ref[0] - m_all) * acc_ref[0] + jnp.exp2(m_ref[1] - m_all) * acc_ref[1]
    o_slc = o_slc[0:NSA_HD] * (1.0 / o_slc[NSA_HD:NSA_HD + 1])

    gates = jax.nn.sigmoid(gl_ref[0, 0].astype(F32))
    outs = []
    for r in range(NSA_GRP):
        outs.append(gates[3 * r:3 * r + 1] * o_cmp[:, head(r)] + gates[3 * r + 1:3 * r + 2] * o_slc[:, head(r)]
                    + gates[3 * r + 2:3 * r + 3] * o_win[:, head(r)])
    o_ref[0] = jnp.concatenate(outs, axis=0).T.astype(o_ref.dtype)


def _alibi_slopes():
    return (2.0 ** (-8.0 * np.arange(1, NSA_HEADS + 1) / NSA_HEADS)).astype(np.float32)


def _bf16_split3(v):
    parts, rest = [], np.asarray(v, np.float32)
    for _ in range(3):
        part = rest.astype(ml_dtypes.bfloat16).astype(np.float32)
        parts.append(part)
        rest = (rest.astype(np.float64) - part.astype(np.float64)).astype(np.float32)
    return parts


def _query_features(tq):
    feat = np.zeros((NSA_HEADS, FEAT), np.float32)
    hi, mid, lo = _bf16_split3(_alibi_slopes() * np.float32(LOG2E))
    for c, part in enumerate((hi, mid, lo, hi, mid, lo)):
        feat[:, ALIBI_OFF + c] = part
    feat[:, PAD_OFF] = NEG
    feat = np.repeat(feat.reshape(NSA_KV_HEADS, NSA_GRP, FEAT), tq, axis=1)
    return jnp.asarray(feat.transpose(0, 2, 1), BF16)


def _key_features(pos, n_slc):
    pos = np.asarray(pos)
    real = pos >= 0
    feat = np.zeros((pos.shape[0], FEAT - NSA_HD), np.float32)
    if n_slc:
        feat[real, pos[real] // SLC_BLK] = 1.0
    a0 = ALIBI_OFF - NSA_HD
    feat[real, a0:a0 + 3] = ((pos[real] // SLC_BLK) * SLC_BLK)[:, None]
    feat[real, a0 + 3:a0 + 6] = (pos[real] % SLC_BLK)[:, None]
    feat[~real, PAD_OFF - NSA_HD] = 1.0
    return feat


def _with_features(k, feat):
    b, g, _, _ = k.shape
    f = jnp.broadcast_to(jnp.asarray(feat, BF16), (b, g) + feat.shape)
    return jnp.concatenate([k.astype(BF16), f], axis=3)


def nsa_attention(z3, qcol0, kvcol0, kc, vc, gate_logits):
    b, s, _ = z3.shape
    tq = ATT_TQ
    tk = min(ATT_TK, s)
    assert s % (2 * tk) == 0 and tk % tq == 0 and tq % SLC_BLK == 0 and WINDOW % tq == 0 and WINDOW > tq
    n_slc = s // SLC_BLK
    assert n_slc <= ALIBI_OFF - MASK_OFF and n_slc % SUBLANES == 0
    assert 2 * NSA_HD == LANES and NSA_KV_HEADS % 2 == 0 and kvcol0 % LANES == 0
    n_cmp = (s - CMP_BLK) // CMP_STRIDE + 1
    ncp = kc.shape[2]
    gw = NSA_GRP * NSA_HD
    lanes = NSA_GRP * tq

    kc_f = _with_features(kc, _key_features(np.arange(ncp) * CMP_STRIDE + CMP_BLK - 1, 0))
    feat_s = jnp.asarray(_key_features(np.arange(s), n_slc), BF16)
    feat_w = jnp.asarray(_key_features(np.arange(-WINDOW, s), 0), BF16)

    c_start = np.arange(ncp) * CMP_STRIDE
    s_start = np.arange(n_slc) * SLC_BLK
    overlap = np.clip(np.minimum(c_start[:, None] + CMP_BLK, s_start[None, :] + SLC_BLK)
                      - np.maximum(c_start[:, None], s_start[None, :]), 0, None) / CMP_BLK
    overlap[n_cmp:] = 0.0
    c2s_t = jnp.asarray(overlap.T, BF16)

    key = np.arange(tq)[:, None]
    qry = np.tile(np.arange(tq), NSA_GRP)[None, :]
    bias = jnp.asarray(np.stack([np.where(key <= qry, 0.0, NEG), np.where(key > qry, 0.0, NEG)]), F32)

    blk = lambda *shape: pl.BlockSpec((1, 1) + shape, lambda bi, g, qi: (bi, g, 0, 0))
    const = lambda *shape: pl.BlockSpec(shape, lambda bi, g, qi: (0,) * len(shape))
    pair = lambda j: pl.BlockSpec((1, s, LANES), lambda bi, g, qi: (bi, 0, (kvcol0 + j * KV_W) // LANES + g // 2))
    return pl.pallas_call(
        functools.partial(_nsa_kernel, n_cmp=n_cmp),
        grid=(b, NSA_KV_HEADS, s // tq),
        in_specs=[pl.BlockSpec((1, tq, gw), lambda bi, g, qi: (bi, qi, qcol0 // gw + g)),
                  pl.BlockSpec((1, FEAT, lanes), lambda bi, g, qi: (g, 0, 0)),
                  blk(ncp, FEAT), blk(NSA_HD, ncp),
                  pair(0), pair(1), pair(2), pair(3),
                  const(s, FEAT - NSA_HD), const(s + WINDOW, FEAT - NSA_HD),
                  pl.BlockSpec((1, 1, NSA_GRP * 3, tq), lambda bi, g, qi: (bi, g, 0, qi)),
                  const(n_slc, ncp), const(2, tq, lanes)],
        out_specs=pl.BlockSpec((1, tq, gw), lambda bi, g, qi: (bi, qi, g)),
        out_shape=jax.ShapeDtypeStruct((b, s, NSA_W), BF16),
        scratch_shapes=[pltpu.VMEM((s, FEAT), BF16), pltpu.VMEM((V_ROWS, s), BF16),
                        pltpu.VMEM((s + WINDOW, FEAT), BF16), pltpu.VMEM((V_ROWS, s + WINDOW), BF16),
                        pltpu.VMEM((FEAT, lanes), BF16), pltpu.VMEM((n_slc, tq), F32),
                        pltpu.VMEM((2, tk, lanes), F32), pltpu.VMEM((2, tk, lanes), F32),
                        pltpu.VMEM((2, 2, 1, lanes), F32),
                        pltpu.VMEM((2, 1, lanes), F32), pltpu.VMEM((2, V_ROWS, lanes), F32)],
        compiler_params=_cparams(("parallel", "parallel", "arbitrary")),
        name="nsa_attention",
    )(z3, _query_features(tq), kc_f, jnp.swapaxes(vc, 2, 3), z3, z3, z3, z3, feat_s, feat_w, gate_logits, c2s_t,
      bias)


def _merge_kernel(ya_ref, yb_ref, yc_ref, wb_ref, ga_ref, gb_ref, gc_ref, o_ref):
    acc = None
    for i, (y_ref, g_ref) in enumerate(((ya_ref, ga_ref), (yb_ref, gb_ref), (yc_ref, gc_ref))):
        term = jax.nn.sigmoid(g_ref[...].astype(F32)) * jnp.dot(y_ref[...], wb_ref[i], preferred_element_type=F32)
        acc = term if acc is None else acc + term
    o_ref[...] = acc.astype(o_ref.dtype)


def branch_merge(ya, yb, yc, wb, zmerge, tm=1024, tn=1024):
    t = ya.shape[0]
    d = wb.shape[2]
    nj = d // tn
    y_spec = pl.BlockSpec((tm, BRANCH_W), lambda i, j: (i, 0))
    gate = lambda br: pl.BlockSpec((tm, tn), lambda i, j: (i, j + br * nj))
    return pl.pallas_call(
        _merge_kernel,
        grid=(t // tm, nj),
        in_specs=[y_spec, y_spec, y_spec,
                  pl.BlockSpec((3, BRANCH_W, tn), lambda i, j: (0, 0, j)),
                  gate(0), gate(1), gate(2)],
        out_specs=pl.BlockSpec((tm, tn), lambda i, j: (i, j)),
        out_shape=jax.ShapeDtypeStruct((t, d), BF16),
        compiler_params=_cparams(("parallel", "parallel")),
        name="branch_merge",
    )(ya, yb, yc, wb, zmerge, zmerge, zmerge)


def _layer_norm(y, g, b):
    mu = jnp.mean(y, axis=-1, keepdims=True)
    yc = y - mu
    var = jnp.mean(yc * yc, axis=-1, keepdims=True)
    return yc * lax.rsqrt(var + LN_EPS) * g + b


def _outproj_ln_kernel(m_ref, w_ref, x_ref, g_ref, b_ref, o_ref, ob_ref, *, alpha):
    mix = jnp.dot(m_ref[...], w_ref[...], preferred_element_type=F32)
    y = _layer_norm(alpha * x_ref[...] + mix, g_ref[...], b_ref[...])
    o_ref[...] = y
    ob_ref[...] = y.astype(BF16)


def outproj_ln(merged, w_out, x, g, b, alpha, tm=512):
    t, d = x.shape
    row = pl.BlockSpec((tm, d), lambda i: (i, 0))
    vec = pl.BlockSpec((1, d), lambda i: (0, 0))
    return pl.pallas_call(
        functools.partial(_outproj_ln_kernel, alpha=alpha),
        grid=(t // tm,),
        in_specs=[row, pl.BlockSpec((d, d), lambda i: (0, 0)), row, vec, vec],
        out_specs=[row, row],
        out_shape=[jax.ShapeDtypeStruct((t, d), F32), jax.ShapeDtypeStruct((t, d), BF16)],
        compiler_params=_cparams(("parallel",)),
        name="outproj_ln",
    )(merged, w_out, x, g.reshape(1, d), b.reshape(1, d))


def _final_kernel(x_ref, xb_ref, wg_ref, p_ref, wp_ref, y0_ref, y1_ref, gate_ref, g_ref, b_ref, o_ref, ob_ref,
                  *, alpha):
    ple = jax.nn.sigmoid(jnp.dot(xb_ref[...], wg_ref[...], preferred_element_type=F32)) * jnp.dot(
        p_ref[...].astype(BF16), wp_ref[...], preferred_element_type=F32)
    gate = gate_ref[...]
    ffn = gate[:, 0:1] * y0_ref[...].astype(F32) + gate[:, 1:2] * y1_ref[...].astype(F32)
    y = _layer_norm(alpha * x_ref[...] + ffn + ple, g_ref[...], b_ref[...])
    o_ref[...] = y
    ob_ref[...] = y.astype(BF16)


def ple_combine_ln(x, xb, w_ple_gate, p, w_ple, y0, y1, gate, g, b, alpha, tm=512):
    t, d = x.shape
    pd = p.shape[1]
    row = pl.BlockSpec((tm, d), lambda i: (i, 0))
    vec = pl.BlockSpec((1, d), lambda i: (0, 0))
    return pl.pallas_call(
        functools.partial(_final_kernel, alpha=alpha),
        grid=(t // tm,),
        in_specs=[row, row, pl.BlockSpec((d, d), lambda i: (0, 0)),
                  pl.BlockSpec((tm, pd), lambda i: (i, 0)), pl.BlockSpec((pd, d), lambda i: (0, 0)),
                  row, row, pl.BlockSpec((tm, TOP_K), lambda i: (i, 0)), vec, vec],
        out_specs=[row, row],
        out_shape=[jax.ShapeDtypeStruct((t, d), F32), jax.ShapeDtypeStruct((t, d), BF16)],
        compiler_params=_cparams(("parallel",)),
        name="ple_combine_ln",
    )(x, xb, w_ple_gate, p, w_ple, y0, y1, gate, g.reshape(1, d), b.reshape(1, d))


def _router_kernel(x_ref, wr_ref, br_ref, idx_ref, gate_ref):
    tm = x_ref.shape[0]
    nt = (((1,), (1,)), ((), ()))
    x = x_ref[...]
    x_hi = x.astype(BF16)
    x_lo = (x - x_hi.astype(F32)).astype(BF16)
    w = wr_ref[...]
    w_hi = w.astype(BF16)
    w_lo = (w - w_hi.astype(F32)).astype(BF16)
    both = lax.dot_general(jnp.concatenate([w_hi, w_lo], axis=0), x_hi, nt, preferred_element_type=F32)
    logits = both[:N_EXPERTS] + both[N_EXPERTS:] + lax.dot_general(w_hi, x_lo, nt, preferred_element_type=F32)
    aff = jax.nn.sigmoid(logits)
    sel = aff + br_ref[...]
    scores = []
    for g in range(N_GROUPS):
        v = [sel[g * EXPERTS_PER_GROUP + e:g * EXPERTS_PER_GROUP + e + 1, :] for e in range(EXPERTS_PER_GROUP)]
        best = None
        for a in range(EXPERTS_PER_GROUP):
            for c in range(a + 1, EXPERTS_PER_GROUP):
                pair = v[a] + v[c]
                best = pair if best is None else jnp.maximum(best, pair)
        scores.append(best)
    best_score, best_group = scores[0], jnp.zeros((1, tm), jnp.int32)
    for g in range(1, N_GROUPS):
        better = scores[g] > best_score
        best_score = jnp.where(better, scores[g], best_score)
        best_group = jnp.where(better, g, best_group)
    e_row = lax.broadcasted_iota(jnp.int32, (N_EXPERTS, tm), 0)
    cand = jnp.where(e_row // EXPERTS_PER_GROUP == best_group, sel, NEG)
    picks, gates = [], []
    for _ in range(TOP_K):
        mx = jnp.max(cand, axis=0, keepdims=True)
        pick = jnp.min(jnp.where(cand == mx, e_row, N_EXPERTS), axis=0, keepdims=True)
        hit = e_row == pick
        picks.append(pick)
        gates.append(jnp.sum(jnp.where(hit, aff, 0.0), axis=0, keepdims=True))
        cand = jnp.where(hit, -jnp.inf, cand)
    total = gates[0] + gates[1]
    idx_ref[...] = jnp.concatenate(picks, axis=0)
    gate_ref[...] = jnp.concatenate([gates[0] / total, gates[1] / total], axis=0)


def router(x, w_router, b_router, tm=1024):
    t, d = x.shape
    return pl.pallas_call(
        _router_kernel,
        grid=(t // tm,),
        in_specs=[pl.BlockSpec((tm, d), lambda i: (i, 0)),
                  pl.BlockSpec((N_EXPERTS, d), lambda i: (0, 0)),
                  pl.BlockSpec((N_EXPERTS, 1), lambda i: (0, 0))],
        out_specs=[pl.BlockSpec((TOP_K, tm), lambda i: (0, i)), pl.BlockSpec((TOP_K, tm), lambda i: (0, i))],
        out_shape=[jax.ShapeDtypeStruct((TOP_K, t), jnp.int32), jax.ShapeDtypeStruct((TOP_K, t), F32)],
        compiler_params=_cparams(("parallel",)),
        name="router",
    )(x, w_router.T, b_router.reshape(N_EXPERTS, 1))


def _expert_kernel(be_ref, nu_ref, *refs, part_starts):
    n_parts = len(part_starts)
    x_refs = refs[:n_parts]
    wg_ref, wu_ref, wd_ref = refs[n_parts:n_parts + 3]
    prev_ref = refs[n_parts + 3] if len(refs) == n_parts + 8 else None
    o_ref, wgb_ref, wub_ref, wdb_ref = refs[-4:]
    i = pl.program_id(0)

    @pl.when((i == 0) | (be_ref[i] != be_ref[jnp.maximum(i - 1, 0)]))
    def _():
        wgb_ref[...] = wg_ref[0, 0].astype(BF16)
        wub_ref[...] = wu_ref[0, 0].astype(BF16)
        wdb_ref[...] = wd_ref[0, 0].astype(BF16)

    @pl.when(i < nu_ref[0])
    def _():
        x = x_refs[0][...]
        for start, x_ref in zip(part_starts[1:], x_refs[1:]):
            x = jnp.where(i >= start, x_ref[...], x)
        g = jnp.dot(x, wgb_ref[...], preferred_element_type=F32)
        u = jnp.dot(x, wub_ref[...], preferred_element_type=F32)
        out = jnp.dot((jax.nn.silu(g) * u).astype(BF16), wdb_ref[...], preferred_element_type=F32)
        if prev_ref is not None:
            out = out + prev_ref[...].astype(F32)
        o_ref[...] = out.astype(o_ref.dtype)

    @pl.when(i >= nu_ref[0])
    def _():
        o_ref[...] = jnp.zeros_like(o_ref)


def expert_ffn(block_expert, n_used, x_parts, w_gate_up, w_down, layer):
    d = x_parts[0].shape[1]
    n_rows = sum(x.shape[0] for x in x_parts)
    de = w_down.shape[2]
    ce = de // EXPERT_CHUNKS
    part_blocks = [x.shape[0] // MOE_BLK for x in x_parts]
    part_starts = tuple(int(v) for v in np.cumsum([0] + part_blocks[:-1]))
    part_specs = [pl.BlockSpec((MOE_BLK, d), lambda i, be, nu, o=o, n=n: (jnp.clip(i - o, 0, n - 1), 0))
                  for o, n in zip(part_starts, part_blocks)]
    out = None
    for c in range(EXPERT_CHUNKS):
        row = pl.BlockSpec((MOE_BLK, d), lambda i, be, nu: (i, 0))
        in_specs = part_specs + [
            pl.BlockSpec((1, 1, d, ce), lambda i, be, nu, c=c: (layer, be[i], 0, c)),
            pl.BlockSpec((1, 1, d, ce), lambda i, be, nu, c=c: (layer, be[i], 0, EXPERT_CHUNKS + c)),
            pl.BlockSpec((1, 1, ce, d), lambda i, be, nu, c=c: (layer, be[i], c, 0))]
        args = list(x_parts) + [w_gate_up, w_gate_up, w_down]
        if out is not None:
            in_specs.append(row)
            args.append(out)
        out = pl.pallas_call(
            functools.partial(_expert_kernel, part_starts=part_starts),
            grid_spec=pltpu.PrefetchScalarGridSpec(
                num_scalar_prefetch=2,
                grid=(n_rows // MOE_BLK,),
                in_specs=in_specs,
                out_specs=row,
                scratch_shapes=[pltpu.VMEM((d, ce), BF16), pltpu.VMEM((d, ce), BF16), pltpu.VMEM((ce, d), BF16)]),
            out_shape=jax.ShapeDtypeStruct((n_rows, d), BF16),
            compiler_params=_cparams(("arbitrary",)),
            name="expert_ffn",
        )(block_expert, n_used, *args)
    return out


def _dispatch_tables(expert_idx_t):
    n_tok = expert_idx_t.shape[1]
    n_assign = n_tok * TOP_K
    e_flat = expert_idx_t.reshape(n_assign)
    onehot = e_flat[:, None] == jnp.arange(N_EXPERTS)[None, :]
    chunk = min(1024, n_assign)
    oh = onehot.astype(F32).reshape(n_assign // chunk, chunk, N_EXPERTS)
    within = jnp.einsum("ij,bjk->bik", jnp.tril(jnp.ones((chunk, chunk), F32)), oh)
    totals = within[:, -1, :]
    csum = (within + (jnp.cumsum(totals, axis=0) - totals)[:, None, :]).reshape(n_assign, N_EXPERTS)
    rank = jnp.sum(jnp.where(onehot, csum, 0.0), axis=1).astype(jnp.int32) - 1
    counts = csum[-1].astype(jnp.int32)
    padded = (counts + MOE_BLK - 1) // MOE_BLK * MOE_BLK
    pad_end = jnp.cumsum(padded)
    pad_start = pad_end - padded
    dest = pad_start[e_flat] + rank
    n_rows = n_assign + N_EXPERTS * MOE_BLK
    block_start = jnp.arange(n_rows // MOE_BLK, dtype=jnp.int32) * MOE_BLK
    block_expert = jnp.minimum(jnp.sum(pad_end[None, :] <= block_start[:, None], axis=1), N_EXPERTS - 1)
    order = jnp.sort(e_flat * n_assign + jnp.arange(n_assign, dtype=jnp.int32)) % n_assign
    row_expert = jnp.repeat(block_expert, MOE_BLK)
    row_rank = jnp.arange(n_rows, dtype=jnp.int32) - pad_start[row_expert]
    source = order[jnp.clip((jnp.cumsum(counts) - counts)[row_expert] + row_rank, 0, n_assign - 1)]
    row_tok = jnp.where(row_rank < counts[row_expert], source % n_tok, 0)
    n_used = (pad_end[-1:] // MOE_BLK).astype(jnp.int32)
    return row_tok.astype(jnp.int32), block_expert.astype(jnp.int32), n_used, dest.reshape(TOP_K, n_tok)


def _token_mixer(xb, bsz, seq, w_in_t, layer, conv_a_w, conv_a_b, lru_conv_w, lru_conv_b, lru_wa, lru_ba, lru_wx, lru_bx,
                 lru_lam, cmp_pos, phi_w1, phi_b1, phi_w2, w_branch):
    d = xb.shape[1]
    n_tok = bsz * seq
    main_w = 3 * BRANCH_W + 2 * BRANCH_W + NSA_W + 6 * KV_W
    ng = 3 * NSA_HEADS
    kv0 = main_w - 6 * KV_W
    zmain = matmul_wres(xb, w_in_t, layer, 0, main_w, BF16, 2048, 768)
    zgate = matmul_wres(xb, w_in_t, layer, main_w, LANES, F32, 2048, LANES)
    zmerge = matmul_wres(xb, w_in_t, layer, main_w + ng, w_in_t.shape[1] - main_w - ng, BF16, 2048, 512)
    z3 = zmain.reshape(bsz, seq, main_w)

    y_a = conv_mixer(z3, conv_a_w, conv_a_b)
    y_b = lru_mixer(z3, 3 * BRANCH_W, lru_conv_w, lru_conv_b, lru_wa, lru_ba, lru_wx, lru_bx, lru_lam)

    cmp = nsa_compress(z3, kv0, cmp_pos, phi_w1, phi_b1, phi_w2)
    gate_logits = zgate[:, :ng].reshape(bsz, seq, NSA_KV_HEADS, NSA_GRP * 3).transpose(0, 2, 3, 1)
    y_c = nsa_attention(z3, 5 * BRANCH_W, kv0 + 2 * KV_W, cmp[0], cmp[1], gate_logits)

    flat = lambda y: y.reshape(n_tok, BRANCH_W)
    return branch_merge(flat(y_a), flat(y_b), flat(y_c), w_branch.astype(BF16), zmerge)


def _moe(x, xb, w_router, b_router, w_gate_up, w_down, layer):
    idx_t, gate_t = router(x, w_router, b_router)
    row_tok, block_expert, n_used, dest = _dispatch_tables(idx_t)
    n_tok = xb.shape[0]
    x_parts = [xb[row_tok[a:a + n_tok]] for a in range(0, row_tok.shape[0], n_tok)]
    rows = expert_ffn(block_expert, n_used, x_parts, w_gate_up, w_down, layer)
    return rows[dest[0]], rows[dest[1]], gate_t.T


def kernel(x, p, w_in, conv_a_w, conv_a_b, lru_conv_w, lru_conv_b, lru_wa, lru_ba, lru_wx, lru_bx, lru_lam, cmp_pos, phi_w1, phi_b1, phi_w2, w_branch, w_out, ln_g, ln_b, w_router, b_router, w_gate_up, w_down, w_ple, w_ple_gate):
    bsz, seq, d = x.shape
    depth = w_in.shape[0]
    n_tok = bsz * seq
    alpha = (2 * depth) ** 0.25
    xf = x.reshape(n_tok, d)
    xb = xf.astype(BF16)
    w_in_t = jnp.swapaxes(w_in, 1, 2)
    for i in range(depth):
        merged = _token_mixer(xb, bsz, seq, w_in_t, i, conv_a_w[i], conv_a_b[i], lru_conv_w[i], lru_conv_b[i],
                              lru_wa[i], lru_ba[i], lru_wx[i], lru_bx[i], lru_lam[i], cmp_pos[i], phi_w1[i],
                              phi_b1[i], phi_w2[i], w_branch[i])
        xf, xb = outproj_ln(merged, w_out[i].astype(BF16), xf, ln_g[i, 0], ln_b[i, 0], alpha)
        y0, y1, gate = _moe(xf, xb, w_router, b_router, w_gate_up, w_down, i)
        xf, xb = ple_combine_ln(xf, xb, w_ple_gate[i].astype(BF16), p[i].reshape(n_tok, -1),
                                w_ple[i].astype(BF16), y0, y1, gate, ln_g[i, 1], ln_b[i, 1], alpha)
    return xf.reshape(bsz, seq, d)
```

```python
import functools

import ml_dtypes
import numpy as np
import jax
import jax.numpy as jnp
from jax import lax
from jax.experimental import pallas as pl
from jax.experimental.pallas import tpu as pltpu

F32 = jnp.float32
BF16 = jnp.bfloat16

BRANCH_W = 1024
CONV_K = 3
LRU_HEADS = 8
LRU_HD = BRANCH_W // LRU_HEADS
LRU_CONV_K = 4
LRU_C = 8.0
NSA_HEADS = 16
NSA_KV_HEADS = 4
NSA_GRP = NSA_HEADS // NSA_KV_HEADS
NSA_HD = 64
NSA_W = NSA_HEADS * NSA_HD
KV_W = NSA_KV_HEADS * NSA_HD
CMP_BLK = 32
CMP_STRIDE = 16
CMP_HIDDEN = 128
SLC_BLK = 64
SLC_TOPN = 16
WINDOW = 512
N_EXPERTS = 16
N_GROUPS = 4
EXPERTS_PER_GROUP = N_EXPERTS // N_GROUPS
TOP_K = 2
LN_EPS = 1e-5
NEG = -1e30
FORCE = 1e9

LANES = 128
SUBLANES = 8
VMEM_LIMIT = 56 * 1024 * 1024

FEAT = 256
MASK_OFF = NSA_HD
ALIBI_OFF = 2 * NSA_HD
PAD_OFF = ALIBI_OFF + 6
V_ROWS = NSA_HD + 16
LOG2E = 1.4426950408889634

ATT_TQ = 256
ATT_TK = 512
EXP_CHUNK = 64
MOE_BLK = 256
EXPERT_CHUNKS = 2


def _cparams(sem):
    return pltpu.CompilerParams(dimension_semantics=sem, vmem_limit_bytes=VMEM_LIMIT)


def _mm_wres_kernel(x_ref, w_ref, *rest, shift):
    o_ref, wb_ref = rest[-2:]

    @pl.when(pl.program_id(1) == 0)
    def _():
        w = w_ref[0]
        if shift:
            tn = w.shape[0]
            w = jnp.concatenate([w, rest[0][0]], axis=0)[shift:shift + tn]
        wb_ref[...] = w.astype(BF16)

    o_ref[...] = lax.dot_general(x_ref[...], wb_ref[...], (((1,), (1,)), ((), ())),
                                 preferred_element_type=F32).astype(o_ref.dtype)


def matmul_wres(x, wt3, layer, row0, n, out_dtype, tm, tn):
    m, k = x.shape
    shift = row0 % tn
    base = row0 - shift
    assert m % tm == 0 and n % tn == 0 and shift % SUBLANES == 0 and shift <= LANES and tn % LANES == 0
    in_specs = [pl.BlockSpec((tm, k), lambda j, i: (i, 0)),
                pl.BlockSpec((1, tn, k), lambda j, i: (layer, base // tn + j, 0))]
    args = [x, wt3]
    if shift:
        in_specs.append(pl.BlockSpec((1, LANES, k), lambda j, i: (layer, (base + (j + 1) * tn) // LANES, 0)))
        args.append(wt3)
    return pl.pallas_call(
        functools.partial(_mm_wres_kernel, shift=shift),
        grid=(n // tn, m // tm),
        in_specs=in_specs,
        out_specs=pl.BlockSpec((tm, tn), lambda j, i: (i, j)),
        out_shape=jax.ShapeDtypeStruct((m, n), out_dtype),
        scratch_shapes=[pltpu.VMEM((tn, k), BF16)],
        compiler_params=_cparams(("parallel", "arbitrary")),
        name="matmul_wres",
    )(*args)


def _causal_conv(ext, cur, w, k):
    acc = cur * w[k - 1:k, :]
    for j in range(k - 1):
        shift = k - 1 - j
        acc = acc + pltpu.roll(ext, shift, 0)[SUBLANES:, :] * w[j:j + 1, :]
    return acc


def _conv_mixer_kernel(ain_ref, ab_ref, ac_ref, w_ref, b_ref, o_ref, halo_ref):
    @pl.when(pl.program_id(2) == 0)
    def _():
        halo_ref[...] = jnp.zeros_like(halo_ref)

    v = ac_ref[0].astype(F32) * ain_ref[0].astype(F32)
    ext = jnp.concatenate([halo_ref[...], v], axis=0)
    y = _causal_conv(ext, v, w_ref[...], CONV_K) + b_ref[...]
    o_ref[0] = (ab_ref[0].astype(F32) * y).astype(o_ref.dtype)
    halo_ref[...] = v[v.shape[0] - SUBLANES:, :]


def conv_mixer(z3, conv_w, conv_b, ts=512, cw=512):
    b, s, _ = z3.shape
    nc = BRANCH_W // cw
    return pl.pallas_call(
        _conv_mixer_kernel,
        grid=(b, nc, s // ts),
        in_specs=[pl.BlockSpec((1, ts, cw), lambda bi, c, si: (bi, si, c)),
                  pl.BlockSpec((1, ts, cw), lambda bi, c, si: (bi, si, c + nc)),
                  pl.BlockSpec((1, ts, cw), lambda bi, c, si: (bi, si, c + 2 * nc)),
                  pl.BlockSpec((CONV_K, cw), lambda bi, c, si: (0, c)),
                  pl.BlockSpec((1, cw), lambda bi, c, si: (0, c))],
        out_specs=pl.BlockSpec((1, ts, cw), lambda bi, c, si: (bi, si, c)),
        out_shape=jax.ShapeDtypeStruct((b, s, BRANCH_W), BF16),
        scratch_shapes=[pltpu.VMEM((SUBLANES, cw), F32)],
        compiler_params=_cparams(("parallel", "parallel", "arbitrary")),
        name="conv_mixer",
    )(z3, z3, z3, conv_w, conv_b.reshape(1, BRANCH_W))


def _lru_kernel(gate_ref, rin_ref, cw_ref, cb_ref, wa_ref, ba_ref, wx_ref, bx_ref, lam_ref, o_ref,
                halo_ref, carry_ref, a_ref, b_ref, h_ref):
    ts, lw = a_ref.shape

    @pl.when(pl.program_id(2) == 0)
    def _():
        halo_ref[...] = jnp.zeros_like(halo_ref)
        carry_ref[...] = jnp.zeros_like(carry_ref)

    u = rin_ref[0].astype(F32)
    ext = jnp.concatenate([halo_ref[...], u], axis=0)
    xc = _causal_conv(ext, u, cw_ref[...], LRU_CONV_K) + cb_ref[...]
    halo_ref[...] = u[ts - SUBLANES:, :]

    xcb = xc.astype(BF16)
    ra, ix = [], []
    for hh in range(lw // LRU_HD):
        xh = xcb[:, hh * LRU_HD:(hh + 1) * LRU_HD]
        ra.append(jnp.dot(xh, wa_ref[hh].astype(BF16), preferred_element_type=F32))
        ix.append(jnp.dot(xh, wx_ref[hh].astype(BF16), preferred_element_type=F32))
    r = jax.nn.sigmoid(jnp.concatenate(ra, axis=1) + ba_ref[...])
    i = jax.nn.sigmoid(jnp.concatenate(ix, axis=1) + bx_ref[...])
    neg_lam = -lam_ref[...]
    softplus = jnp.maximum(neg_lam, 0.0) + jnp.log1p(jnp.exp(-jnp.abs(neg_lam)))
    log_a = -LRU_C * r * softplus
    a_ref[...] = jnp.exp(log_a)
    th = jnp.tanh(log_a)
    b_ref[...] = jnp.sqrt(-2.0 * th / (1.0 - th)) * (i * xc)

    row = lax.broadcasted_iota(jnp.int32, (SUBLANES, lw), 0)

    def group(g, hc):
        sl = pl.ds(pl.multiple_of(g * SUBLANES, SUBLANES), SUBLANES)
        a = a_ref[sl, :]
        bv = b_ref[sl, :]
        for d in (1, 2, 4):
            keep = row >= d
            bv = jnp.where(keep, a * pltpu.roll(bv, d, 0) + bv, bv)
            a = jnp.where(keep, a * pltpu.roll(a, d, 0), a)
        h = a * hc + bv
        h_ref[sl, :] = h
        return jnp.broadcast_to(h[SUBLANES - 1:SUBLANES, :], (SUBLANES, lw))

    carry_ref[...] = lax.fori_loop(0, ts // SUBLANES, group, carry_ref[...])
    o_ref[0] = (h_ref[...] * jax.nn.gelu(gate_ref[0].astype(F32))).astype(o_ref.dtype)


def lru_mixer(z3, col0, conv_w, conv_b, wa, ba, wx, bx, lam, ts=512, lw=256):
    b, s, _ = z3.shape
    nc = BRANCH_W // lw
    c0 = col0 // lw
    hpb = lw // LRU_HD
    row = lambda v: v.reshape(1, BRANCH_W)
    vec = pl.BlockSpec((1, lw), lambda bi, c, si: (0, c))
    mat = pl.BlockSpec((hpb, LRU_HD, LRU_HD), lambda bi, c, si: (c, 0, 0))
    return pl.pallas_call(
        _lru_kernel,
        grid=(b, nc, s // ts),
        in_specs=[pl.BlockSpec((1, ts, lw), lambda bi, c, si: (bi, si, c0 + c)),
                  pl.BlockSpec((1, ts, lw), lambda bi, c, si: (bi, si, c0 + nc + c)),
                  pl.BlockSpec((LRU_CONV_K, lw), lambda bi, c, si: (0, c)),
                  vec, mat, vec, mat, vec, vec],
        out_specs=pl.BlockSpec((1, ts, lw), lambda bi, c, si: (bi, si, c)),
        out_shape=jax.ShapeDtypeStruct((b, s, BRANCH_W), BF16),
        scratch_shapes=[pltpu.VMEM((SUBLANES, lw), F32), pltpu.VMEM((SUBLANES, lw), F32),
                        pltpu.VMEM((ts, lw), F32), pltpu.VMEM((ts, lw), F32), pltpu.VMEM((ts, lw), F32)],
        compiler_params=_cparams(("parallel", "parallel", "arbitrary")),
        name="rg_lru",
    )(z3, z3, conv_w, row(conv_b), wa, row(ba), wx, row(bx), row(lam))


def _compress_kernel(x_ref, pos_ref, w1_ref, b1_ref, w2_ref, o_ref, xs_ref):
    seq = x_ref.shape[1]
    nslot = seq // CMP_STRIDE
    xs_ref[...] = x_ref[0].astype(F32)
    w1 = w1_ref[0].astype(BF16)
    zeros = jnp.zeros((NSA_HD, CMP_HIDDEN), BF16)
    heads = 2
    first = [jnp.zeros((nslot, CMP_HIDDEN), F32) for _ in range(heads)]
    second = [jnp.zeros((nslot, CMP_HIDDEN), F32) for _ in range(heads)]
    for o in range(CMP_STRIDE):
        rows = xs_ref[pl.ds(o, nslot, stride=CMP_STRIDE), :].astype(BF16)
        w_a = w1[o * NSA_HD:(o + 1) * NSA_HD]
        w_b = w1[(CMP_STRIDE + o) * NSA_HD:(CMP_STRIDE + o + 1) * NSA_HD]
        for h in range(heads):
            pad = lambda w: jnp.concatenate([w, zeros] if h == 0 else [zeros, w], axis=0)
            first[h] = first[h] + jnp.dot(rows, pad(w_a), preferred_element_type=F32)
            second[h] = second[h] + jnp.dot(rows, pad(w_b), preferred_element_type=F32)
    pos = jnp.broadcast_to(pos_ref[0], (SUBLANES, CMP_BLK * NSA_HD)).astype(BF16)
    posb = jnp.dot(pos, w1, preferred_element_type=F32)[0:1, :]
    w2 = w2_ref[0].astype(BF16)
    for h in range(heads):
        hidden = first[h] + pltpu.roll(second[h], nslot - 1, 0) + posb + b1_ref[0]
        out = jnp.dot(jax.nn.gelu(hidden).astype(BF16), w2, preferred_element_type=F32)
        o_ref[0, 0, h] = out.astype(o_ref.dtype)


def nsa_compress(z3, col0, cmp_pos, w1, b1, w2):
    b, s, _ = z3.shape
    assert CMP_BLK == 2 * CMP_STRIDE and 2 * NSA_HD == LANES and col0 % LANES == 0 and s % CMP_STRIDE == 0
    nslot = s // CMP_STRIDE
    pos = cmp_pos.reshape(2, 1, CMP_BLK * NSA_HD)
    return pl.pallas_call(
        _compress_kernel,
        grid=(2, b, NSA_KV_HEADS // 2),
        in_specs=[pl.BlockSpec((1, s, LANES), lambda kv, bi, hp: (bi, 0, (col0 + kv * KV_W) // LANES + hp)),
                  pl.BlockSpec((1, 1, CMP_BLK * NSA_HD), lambda kv, bi, hp: (kv, 0, 0)),
                  pl.BlockSpec((1, CMP_BLK * NSA_HD, CMP_HIDDEN), lambda kv, bi, hp: (kv, 0, 0)),
                  pl.BlockSpec((1, 1, CMP_HIDDEN), lambda kv, bi, hp: (kv, 0, 0)),
                  pl.BlockSpec((1, CMP_HIDDEN, NSA_HD), lambda kv, bi, hp: (kv, 0, 0))],
        out_specs=pl.BlockSpec((1, 1, 2, nslot, NSA_HD), lambda kv, bi, hp: (kv, bi, hp, 0, 0)),
        out_shape=jax.ShapeDtypeStruct((2, b, NSA_KV_HEADS, nslot, NSA_HD), BF16),
        scratch_shapes=[pltpu.VMEM((s, LANES), F32)],
        compiler_params=_cparams(("parallel", "parallel", "parallel")),
        name="nsa_compress",
    )(z3, pos, w1, b1.reshape(2, 1, CMP_HIDDEN), w2)


def _exp2_bf16(s, m):
    rows = s.shape[0]
    chunk = min(EXP_CHUNK, rows)
    return jnp.concatenate([jnp.exp2((s[r:r + chunk, :] - m).astype(BF16)) for r in range(0, rows, chunk)], axis=0)


def _nsa_kernel(q_ref, qf_ref, kc_ref, vc_ref, ksb_ref, vsb_ref, kwb_ref, vwb_ref, fs_ref, fw_ref, gl_ref, c2s_ref,
                bias_ref, o_ref, ks_ref, vs_ref, kw_ref, vw_ref, qp_ref, imp_ref, s0_ref, s1_ref, mx_ref, m_ref, acc_ref,
                *, n_cmp):
    tq = q_ref.shape[1]
    lanes = NSA_GRP * tq
    seq = ks_ref.shape[0]
    n_slc = seq // SLC_BLK
    t0 = pl.multiple_of(pl.program_id(2) * tq, tq)
    head = lambda r: slice(r * tq, (r + 1) * tq)

    @pl.when(pl.program_id(2) == 0)
    def _():
        odd = pl.program_id(1) % 2 == 1

        def keys(blk_ref):
            blk = blk_ref[0].astype(F32)
            return jnp.where(odd, pltpu.roll(blk, NSA_HD, 1), blk)[:, :NSA_HD].astype(BF16)

        def values_t(blk_ref):
            vt = blk_ref[0].astype(F32).T
            return jnp.where(odd, vt[NSA_HD:], vt[:NSA_HD]).astype(BF16)

        def ones_row(n):
            first = lax.broadcasted_iota(jnp.int32, (V_ROWS - NSA_HD, n), 0) == 0
            return jnp.where(first, 1.0, 0.0).astype(BF16)

        ks_ref[:, 0:NSA_HD] = keys(ksb_ref)
        ks_ref[:, NSA_HD:] = fs_ref[...]
        vs_ref[0:NSA_HD, :] = values_t(vsb_ref)
        vs_ref[NSA_HD:, :] = ones_row(seq)
        kw_ref[0:WINDOW, 0:NSA_HD] = jnp.zeros((WINDOW, NSA_HD), BF16)
        kw_ref[WINDOW:, 0:NSA_HD] = keys(kwb_ref)
        kw_ref[:, NSA_HD:] = fw_ref[...]
        vw_ref[0:NSA_HD, 0:WINDOW] = jnp.zeros((NSA_HD, WINDOW), BF16)
        vw_ref[0:NSA_HD, WINDOW:] = values_t(vwb_ref)
        vw_ref[NSA_HD:, :] = ones_row(seq + WINDOW)

    def scores(k_tile, bias=None):
        s = jnp.dot(k_tile, qp_ref[...], preferred_element_type=F32)
        return s if bias is None else s + bias

    qt = q_ref[0].astype(F32).T * (NSA_HD ** -0.5 * LOG2E)
    qp_ref[...] = qf_ref[0]
    for r in range(NSA_GRP):
        qp_ref[0:NSA_HD, head(r)] = qt[r * NSA_HD:(r + 1) * NSA_HD, :].astype(BF16)

    s_cmp = scores(kc_ref[0, 0])
    win = []
    for off, size, bias in ((0, tq, bias_ref[1]), (tq, WINDOW - tq, None), (WINDOW, tq, bias_ref[0])):
        k0 = pl.multiple_of(t0 + off, tq)
        win.append((scores(kw_ref[pl.ds(k0, size), :], bias), vw_ref[:, pl.ds(k0, size)]))

    ncp = kc_ref.shape[2]
    n_row = lax.broadcasted_iota(jnp.int32, (ncp, lanes), 0)
    t_lane = t0 + lax.broadcasted_iota(jnp.int32, (ncp, lanes), 1) % tq
    ok = (t_lane >= n_row * CMP_STRIDE + (CMP_BLK - 1)) & (n_row < n_cmp)
    s = jnp.where(ok, s_cmp, NEG)
    p = jnp.where(ok, jnp.exp2(s - jnp.max(s, axis=0, keepdims=True)), 0.0)
    l = jnp.sum(p, axis=0, keepdims=True)
    p_cmp = p * (1.0 / jnp.where(l > 0.0, l, 1.0))
    o_cmp = jnp.dot(vc_ref[0, 0], p_cmp.astype(BF16), preferred_element_type=F32)

    rest = p_cmp[:, head(0)]
    for r in range(1, NSA_GRP):
        rest = rest + p_cmp[:, head(r)]
    imp = jnp.zeros((n_slc, tq), F32)
    for _ in range(3):
        part = rest.astype(BF16)
        rest = rest - part.astype(F32)
        imp = imp + jnp.dot(c2s_ref[...], part, preferred_element_type=F32)
    j_row = lax.broadcasted_iota(jnp.int32, (n_slc, tq), 0)
    t_col = t0 + lax.broadcasted_iota(jnp.int32, (n_slc, tq), 1)
    forced = (j_row == 0) | (j_row == t_col // SLC_BLK)
    causal = j_row * SLC_BLK <= t_col
    imp = jnp.where(forced, FORCE, jnp.where(causal, imp, -FORCE))
    imp_ref[...] = imp

    n_grp = n_slc // SUBLANES
    vals = [imp[v * SUBLANES:(v + 1) * SUBLANES, :] for v in range(n_grp)]
    sub = lax.broadcasted_iota(jnp.int32, (SUBLANES, tq), 0)
    rank = [jnp.zeros((SUBLANES, tq), jnp.int32) for _ in range(n_grp)]
    for i in range(n_slc):
        bi = jnp.broadcast_to(imp_ref[i:i + 1, :], (SUBLANES, tq))
        for v in range(n_grp):
            if v < i // SUBLANES:
                ahead = bi > vals[v]
            elif v > i // SUBLANES:
                ahead = bi >= vals[v]
            else:
                ahead = (bi > vals[v]) | ((bi == vals[v]) & (sub > i % SUBLANES))
            rank[v] = rank[v] + jnp.where(ahead, 1, 0)
    chosen = (jnp.concatenate(rank, axis=0) < min(SLC_TOPN, n_slc)) & causal
    mask = jnp.where(chosen, 0.0, NEG).astype(BF16)
    for r in range(NSA_GRP):
        qp_ref[MASK_OFF:MASK_OFF + n_slc, head(r)] = mask

    tk = min(ATT_TK, seq)
    n_full = t0 // tk
    last = n_full // 2

    s_bufs = (s0_ref, s1_ref)

    def pair_scores(step, buf):
        for par in range(2):
            k0 = pl.multiple_of((2 * step + par) * tk, tk)
            s = scores(ks_ref[pl.ds(k0, tk), :])
            s_bufs[buf][par] = s
            mx_ref[buf, par] = jnp.max(s, axis=0, keepdims=True)

    def pair_update(step, buf, prefetch, tiles=(0, 1)):
        m_new = {par: jnp.maximum(m_ref[par], mx_ref[buf, par]) for par in tiles}
        probs = {par: _exp2_bf16(s_bufs[buf].at[par], m_new[par]) for par in tiles}
        if prefetch:
            pair_scores(step + 1, 1 - buf)
        for par in tiles:
            k0 = pl.multiple_of((2 * step + par) * tk, tk)
            pv = jnp.dot(vs_ref[:, pl.ds(k0, tk)], probs[par], preferred_element_type=F32)
            acc_ref[par] = jnp.exp2(m_ref[par] - m_new[par]) * acc_ref[par] + pv
            m_ref[par] = m_new[par]

    def mask_diagonal(buf):
        off = pl.multiple_of(t0 - n_full * tk, tq)
        for par in range(2):
            @pl.when(n_full % 2 == par)
            def _():
                tile = s_bufs[buf].at[par]
                tile[pl.ds(off, tq), :] = tile[pl.ds(off, tq), :] + bias_ref[0]
                mx_ref[buf, par] = jnp.max(tile[...], axis=0, keepdims=True)

    def half_trip(step, buf):
        @pl.when(step == last)
        def _():
            mask_diagonal(buf)

        @pl.when(step < last)
        def _():
            pair_update(step, buf, prefetch=True)

        @pl.when((step == last) & (n_full % 2 == 1))
        def _():
            pair_update(step, buf, prefetch=False)

        @pl.when((step == last) & (n_full % 2 == 0))
        def _():
            pair_update(step, buf, prefetch=False, tiles=(0,))

    def trip(i, carry):
        half_trip(2 * i, 0)
        half_trip(2 * i + 1, 1)
        return carry

    m_ref[...] = jnp.full(m_ref.shape, NEG, F32)
    acc_ref[...] = jnp.zeros(acc_ref.shape, F32)
    pair_scores(0, 0)

    mx = functools.reduce(jnp.maximum, [jnp.max(s, axis=0, keepdims=True) for s, _ in win])
    o_win = jnp.zeros((V_ROWS, lanes), F32)
    for s, v_tile in win:
        o_win = o_win + jnp.dot(v_tile, _exp2_bf16(s, mx), preferred_element_type=F32)
    o_win = o_win[0:NSA_HD] * (1.0 / o_win[NSA_HD:NSA_HD + 1])

    lax.fori_loop(0, last // 2 + 1, trip, 0)
    m_all = jnp.maximum(m_ref[0], m_ref[1])
    o_slc = jnp.exp2(m_ref[0] - m_all) * acc_ref[0] + jnp.exp2(m_ref[1] - m_all) * acc_ref[1]
    o_slc = o_slc[0:NSA_HD] * (1.0 / o_slc[NSA_HD:NSA_HD + 1])

    gates = jax.nn.sigmoid(gl_ref[0, 0].astype(F32))
    outs = []
    for r in range(NSA_GRP):
        outs.append(gates[3 * r:3 * r + 1] * o_cmp[:, head(r)] + gates[3 * r + 1:3 * r + 2] * o_slc[:, head(r)]
                    + gates[3 * r + 2:3 * r + 3] * o_win[:, head(r)])
    o_ref[0] = jnp.concatenate(outs, axis=0).T.astype(o_ref.dtype)


def _alibi_slopes():
    return (2.0 ** (-8.0 * np.arange(1, NSA_HEADS + 1) / NSA_HEADS)).astype(np.float32)


def _bf16_split3(v):
    parts, rest = [], np.asarray(v, np.float32)
    for _ in range(3):
        part = rest.astype(ml_dtypes.bfloat16).astype(np.float32)
        parts.append(part)
        rest = (rest.astype(np.float64) - part.astype(np.float64)).astype(np.float32)
    return parts


def _query_features(tq):
    feat = np.zeros((NSA_HEADS, FEAT), np.float32)
    hi, mid, lo = _bf16_split3(_alibi_slopes() * np.float32(LOG2E))
    for c, part in enumerate((hi, mid, lo, hi, mid, lo)):
        feat[:, ALIBI_OFF + c] = part
    feat[:, PAD_OFF] = NEG
    feat = np.repeat(feat.reshape(NSA_KV_HEADS, NSA_GRP, FEAT), tq, axis=1)
    return jnp.asarray(feat.transpose(0, 2, 1), BF16)


def _key_features(pos, n_slc):
    pos = np.asarray(pos)
    real = pos >= 0
    feat = np.zeros((pos.shape[0], FEAT - NSA_HD), np.float32)
    if n_slc:
        feat[real, pos[real] // SLC_BLK] = 1.0
    a0 = ALIBI_OFF - NSA_HD
    feat[real, a0:a0 + 3] = ((pos[real] // SLC_BLK) * SLC_BLK)[:, None]
    feat[real, a0 + 3:a0 + 6] = (pos[real] % SLC_BLK)[:, None]
    feat[~real, PAD_OFF - NSA_HD] = 1.0
    return feat


def _with_features(k, feat):
    b, g, _, _ = k.shape
    f = jnp.broadcast_to(jnp.asarray(feat, BF16), (b, g) + feat.shape)
    return jnp.concatenate([k.astype(BF16), f], axis=3)


def nsa_attention(z3, qcol0, kvcol0, kc, vc, gate_logits):
    b, s, _ = z3.shape
    tq = ATT_TQ
    tk = min(ATT_TK, s)
    assert s % (2 * tk) == 0 and tk % tq == 0 and tq % SLC_BLK == 0 and WINDOW % tq == 0 and WINDOW > tq
    n_slc = s // SLC_BLK
    assert n_slc <= ALIBI_OFF - MASK_OFF and n_slc % SUBLANES == 0
    assert 2 * NSA_HD == LANES and NSA_KV_HEADS % 2 == 0 and kvcol0 % LANES == 0
    n_cmp = (s - CMP_BLK) // CMP_STRIDE + 1
    ncp = kc.shape[2]
    gw = NSA_GRP * NSA_HD
    lanes = NSA_GRP * tq

    kc_f = _with_features(kc, _key_features(np.arange(ncp) * CMP_STRIDE + CMP_BLK - 1, 0))
    feat_s = jnp.asarray(_key_features(np.arange(s), n_slc), BF16)
    feat_w = jnp.asarray(_key_features(np.arange(-WINDOW, s), 0), BF16)

    c_start = np.arange(ncp) * CMP_STRIDE
    s_start = np.arange(n_slc) * SLC_BLK
    overlap = np.clip(np.minimum(c_start[:, None] + CMP_BLK, s_start[None, :] + SLC_BLK)
                      - np.maximum(c_start[:, None], s_start[None, :]), 0, None) / CMP_BLK
    overlap[n_cmp:] = 0.0
    c2s_t = jnp.asarray(overlap.T, BF16)

    key = np.arange(tq)[:, None]
    qry = np.tile(np.arange(tq), NSA_GRP)[None, :]
    bias = jnp.asarray(np.stack([np.where(key <= qry, 0.0, NEG), np.where(key > qry, 0.0, NEG)]), F32)

    blk = lambda *shape: pl.BlockSpec((1, 1) + shape, lambda bi, g, qi: (bi, g, 0, 0))
    const = lambda *shape: pl.BlockSpec(shape, lambda bi, g, qi: (0,) * len(shape))
    pair = lambda j: pl.BlockSpec((1, s, LANES), lambda bi, g, qi: (bi, 0, (kvcol0 + j * KV_W) // LANES + g // 2))
    return pl.pallas_call(
        functools.partial(_nsa_kernel, n_cmp=n_cmp),
        grid=(b, NSA_KV_HEADS, s // tq),
        in_specs=[pl.BlockSpec((1, tq, gw), lambda bi, g, qi: (bi, qi, qcol0 // gw + g)),
                  pl.BlockSpec((1, FEAT, lanes), lambda bi, g, qi: (g, 0, 0)),
                  blk(ncp, FEAT), blk(NSA_HD, ncp),
                  pair(0), pair(1), pair(2), pair(3),
                  const(s, FEAT - NSA_HD), const(s + WINDOW, FEAT - NSA_HD),
                  pl.BlockSpec((1, 1, NSA_GRP * 3, tq), lambda bi, g, qi: (bi, g, 0, qi)),
                  const(n_slc, ncp), const(2, tq, lanes)],
        out_specs=pl.BlockSpec((1, tq, gw), lambda bi, g, qi: (bi, qi, g)),
        out_shape=jax.ShapeDtypeStruct((b, s, NSA_W), BF16),
        scratch_shapes=[pltpu.VMEM((s, FEAT), BF16), pltpu.VMEM((V_ROWS, s), BF16),
                        pltpu.VMEM((s + WINDOW, FEAT), BF16), pltpu.VMEM((V_ROWS, s + WINDOW), BF16),
                        pltpu.VMEM((FEAT, lanes), BF16), pltpu.VMEM((n_slc, tq), F32),
                        pltpu.VMEM((2, tk, lanes), F32), pltpu.VMEM((2, tk, lanes), F32),
                        pltpu.VMEM((2, 2, 1, lanes), F32),
                        pltpu.VMEM((2, 1, lanes), F32), pltpu.VMEM((2, V_ROWS, lanes), F32)],
        compiler_params=_cparams(("parallel", "parallel", "arbitrary")),
        name="nsa_attention",
    )(z3, _query_features(tq), kc_f, jnp.swapaxes(vc, 2, 3), z3, z3, z3, z3, feat_s, feat_w, gate_logits, c2s_t,
      bias)


def _merge_kernel(ya_ref, yb_ref, yc_ref, wb_ref, ga_ref, gb_ref, gc_ref, o_ref):
    acc = None
    for i, (y_ref, g_ref) in enumerate(((ya_ref, ga_ref), (yb_ref, gb_ref), (yc_ref, gc_ref))):
        term = jax.nn.sigmoid(g_ref[...].astype(F32)) * jnp.dot(y_ref[...], wb_ref[i], preferred_element_type=F32)
        acc = term if acc is None else acc + term
    o_ref[...] = acc.astype(o_ref.dtype)


def branch_merge(ya, yb, yc, wb, zmerge, tm=1024, tn=1024):
    t = ya.shape[0]
    d = wb.shape[2]
    nj = d // tn
    y_spec = pl.BlockSpec((tm, BRANCH_W), lambda i, j: (i, 0))
    gate = lambda br: pl.BlockSpec((tm, tn), lambda i, j: (i, j + br * nj))
    return pl.pallas_call(
        _merge_kernel,
        grid=(t // tm, nj),
        in_specs=[y_spec, y_spec, y_spec,
                  pl.BlockSpec((3, BRANCH_W, tn), lambda i, j: (0, 0, j)),
                  gate(0), gate(1), gate(2)],
        out_specs=pl.BlockSpec((tm, tn), lambda i, j: (i, j)),
        out_shape=jax.ShapeDtypeStruct((t, d), BF16),
        compiler_params=_cparams(("parallel", "parallel")),
        name="branch_merge",
    )(ya, yb, yc, wb, zmerge, zmerge, zmerge)


def _layer_norm(y, g, b):
    mu = jnp.mean(y, axis=-1, keepdims=True)
    yc = y - mu
    var = jnp.mean(yc * yc, axis=-1, keepdims=True)
    return yc * lax.rsqrt(var + LN_EPS) * g + b


def _outproj_ln_kernel(m_ref, w_ref, x_ref, g_ref, b_ref, o_ref, ob_ref, *, alpha):
    mix = jnp.dot(m_ref[...], w_ref[...], preferred_element_type=F32)
    y = _layer_norm(alpha * x_ref[...] + mix, g_ref[...], b_ref[...])
    o_ref[...] = y
    ob_ref[...] = y.astype(BF16)


def outproj_ln(merged, w_out, x, g, b, alpha, tm=512):
    t, d = x.shape
    row = pl.BlockSpec((tm, d), lambda i: (i, 0))
    vec = pl.BlockSpec((1, d), lambda i: (0, 0))
    return pl.pallas_call(
        functools.partial(_outproj_ln_kernel, alpha=alpha),
        grid=(t // tm,),
        in_specs=[row, pl.BlockSpec((d, d), lambda i: (0, 0)), row, vec, vec],
        out_specs=[row, row],
        out_shape=[jax.ShapeDtypeStruct((t, d), F32), jax.ShapeDtypeStruct((t, d), BF16)],
        compiler_params=_cparams(("parallel",)),
        name="outproj_ln",
    )(merged, w_out, x, g.reshape(1, d), b.reshape(1, d))


def _final_kernel(x_ref, xb_ref, wg_ref, p_ref, wp_ref, y0_ref, y1_ref, gate_ref, g_ref, b_ref, o_ref, ob_ref,
                  *, alpha):
    ple = jax.nn.sigmoid(jnp.dot(xb_ref[...], wg_ref[...], preferred_element_type=F32)) * jnp.dot(
        p_ref[...].astype(BF16), wp_ref[...], preferred_element_type=F32)
    gate = gate_ref[...]
    ffn = gate[:, 0:1] * y0_ref[...].astype(F32) + gate[:, 1:2] * y1_ref[...].astype(F32)
    y = _layer_norm(alpha * x_ref[...] + ffn + ple, g_ref[...], b_ref[...])
    o_ref[...] = y
    ob_ref[...] = y.astype(BF16)


def ple_combine_ln(x, xb, w_ple_gate, p, w_ple, y0, y1, gate, g, b, alpha, tm=512):
    t, d = x.shape
    pd = p.shape[1]
    row = pl.BlockSpec((tm, d), lambda i: (i, 0))
    vec = pl.BlockSpec((1, d), lambda i: (0, 0))
    return pl.pallas_call(
        functools.partial(_final_kernel, alpha=alpha),
        grid=(t // tm,),
        in_specs=[row, row, pl.BlockSpec((d, d), lambda i: (0, 0)),
                  pl.BlockSpec((tm, pd), lambda i: (i, 0)), pl.BlockSpec((pd, d), lambda i: (0, 0)),
                  row, row, pl.BlockSpec((tm, TOP_K), lambda i: (i, 0)), vec, vec],
        out_specs=[row, row],
        out_shape=[jax.ShapeDtypeStruct((t, d), F32), jax.ShapeDtypeStruct((t, d), BF16)],
        compiler_params=_cparams(("parallel",)),
        name="ple_combine_ln",
    )(x, xb, w_ple_gate, p, w_ple, y0, y1, gate, g.reshape(1, d), b.reshape(1, d))


def _router_kernel(x_ref, wr_ref, br_ref, idx_ref, gate_ref):
    tm = x_ref.shape[0]
    nt = (((1,), (1,)), ((), ()))
    x = x_ref[...]
    x_hi = x.astype(BF16)
    x_lo = (x - x_hi.astype(F32)).astype(BF16)
    w = wr_ref[...]
    w_hi = w.astype(BF16)
    w_lo = (w - w_hi.astype(F32)).astype(BF16)
    both = lax.dot_general(jnp.concatenate([w_hi, w_lo], axis=0), x_hi, nt, preferred_element_type=F32)
    logits = both[:N_EXPERTS] + both[N_EXPERTS:] + lax.dot_general(w_hi, x_lo, nt, preferred_element_type=F32)
    aff = jax.nn.sigmoid(logits)
    sel = aff + br_ref[...]
    scores = []
    for g in range(N_GROUPS):
        v = [sel[g * EXPERTS_PER_GROUP + e:g * EXPERTS_PER_GROUP + e + 1, :] for e in range(EXPERTS_PER_GROUP)]
        best = None
        for a in range(EXPERTS_PER_GROUP):
            for c in range(a + 1, EXPERTS_PER_GROUP):
                pair = v[a] + v[c]
                best = pair if best is None else jnp.maximum(best, pair)
        scores.append(best)
    best_score, best_group = scores[0], jnp.zeros((1, tm), jnp.int32)
    for g in range(1, N_GROUPS):
        better = scores[g] > best_score
        best_score = jnp.where(better, scores[g], best_score)
        best_group = jnp.where(better, g, best_group)
    e_row = lax.broadcasted_iota(jnp.int32, (N_EXPERTS, tm), 0)
    cand = jnp.where(e_row // EXPERTS_PER_GROUP == best_group, sel, NEG)
    picks, gates = [], []
    for _ in range(TOP_K):
        mx = jnp.max(cand, axis=0, keepdims=True)
        pick = jnp.min(jnp.where(cand == mx, e_row, N_EXPERTS), axis=0, keepdims=True)
        hit = e_row == pick
        picks.append(pick)
        gates.append(jnp.sum(jnp.where(hit, aff, 0.0), axis=0, keepdims=True))
        cand = jnp.where(hit, -jnp.inf, cand)
    total = gates[0] + gates[1]
    idx_ref[...] = jnp.concatenate(picks, axis=0)
    gate_ref[...] = jnp.concatenate([gates[0] / total, gates[1] / total], axis=0)


def router(x, w_router, b_router, tm=1024):
    t, d = x.shape
    return pl.pallas_call(
        _router_kernel,
        grid=(t // tm,),
        in_specs=[pl.BlockSpec((tm, d), lambda i: (i, 0)),
                  pl.BlockSpec((N_EXPERTS, d), lambda i: (0, 0)),
                  pl.BlockSpec((N_EXPERTS, 1), lambda i: (0, 0))],
        out_specs=[pl.BlockSpec((TOP_K, tm), lambda i: (0, i)), pl.BlockSpec((TOP_K, tm), lambda i: (0, i))],
        out_shape=[jax.ShapeDtypeStruct((TOP_K, t), jnp.int32), jax.ShapeDtypeStruct((TOP_K, t), F32)],
        compiler_params=_cparams(("parallel",)),
        name="router",
    )(x, w_router.T, b_router.reshape(N_EXPERTS, 1))


def _expert_kernel(be_ref, nu_ref, *refs, part_starts):
    n_parts = len(part_starts)
    x_refs = refs[:n_parts]
    wg_ref, wu_ref, wd_ref = refs[n_parts:n_parts + 3]
    prev_ref = refs[n_parts + 3] if len(refs) == n_parts + 8 else None
    o_ref, wgb_ref, wub_ref, wdb_ref = refs[-4:]
    i = pl.program_id(0)

    @pl.when((i == 0) | (be_ref[i] != be_ref[jnp.maximum(i - 1, 0)]))
    def _():
        wgb_ref[...] = wg_ref[0, 0].astype(BF16)
        wub_ref[...] = wu_ref[0, 0].astype(BF16)
        wdb_ref[...] = wd_ref[0, 0].astype(BF16)

    @pl.when(i < nu_ref[0])
    def _():
        x = x_refs[0][...]
        for start, x_ref in zip(part_starts[1:], x_refs[1:]):
            x = jnp.where(i >= start, x_ref[...], x)
        g = jnp.dot(x, wgb_ref[...], preferred_element_type=F32)
        u = jnp.dot(x, wub_ref[...], preferred_element_type=F32)
        out = jnp.dot((jax.nn.silu(g) * u).astype(BF16), wdb_ref[...], preferred_element_type=F32)
        if prev_ref is not None:
            out = out + prev_ref[...].astype(F32)
        o_ref[...] = out.astype(o_ref.dtype)

    @pl.when(i >= nu_ref[0])
    def _():
        o_ref[...] = jnp.zeros_like(o_ref)


def expert_ffn(block_expert, n_used, x_parts, w_gate_up, w_down, layer):
    d = x_parts[0].shape[1]
    n_rows = sum(x.shape[0] for x in x_parts)
    de = w_down.shape[2]
    ce = de // EXPERT_CHUNKS
    part_blocks = [x.shape[0] // MOE_BLK for x in x_parts]
    part_starts = tuple(int(v) for v in np.cumsum([0] + part_blocks[:-1]))
    part_specs = [pl.BlockSpec((MOE_BLK, d), lambda i, be, nu, o=o, n=n: (jnp.clip(i - o, 0, n - 1), 0))
                  for o, n in zip(part_starts, part_blocks)]
    out = None
    for c in range(EXPERT_CHUNKS):
        row = pl.BlockSpec((MOE_BLK, d), lambda i, be, nu: (i, 0))
        in_specs = part_specs + [
            pl.BlockSpec((1, 1, d, ce), lambda i, be, nu, c=c: (layer, be[i], 0, c)),
            pl.BlockSpec((1, 1, d, ce), lambda i, be, nu, c=c: (layer, be[i], 0, EXPERT_CHUNKS + c)),
            pl.BlockSpec((1, 1, ce, d), lambda i, be, nu, c=c: (layer, be[i], c, 0))]
        args = list(x_parts) + [w_gate_up, w_gate_up, w_down]
        if out is not None:
            in_specs.append(row)
            args.append(out)
        out = pl.pallas_call(
            functools.partial(_expert_kernel, part_starts=part_starts),
            grid_spec=pltpu.PrefetchScalarGridSpec(
                num_scalar_prefetch=2,
                grid=(n_rows // MOE_BLK,),
                in_specs=in_specs,
                out_specs=row,
                scratch_shapes=[pltpu.VMEM((d, ce), BF16), pltpu.VMEM((d, ce), BF16), pltpu.VMEM((ce, d), BF16)]),
            out_shape=jax.ShapeDtypeStruct((n_rows, d), BF16),
            compiler_params=_cparams(("arbitrary",)),
            name="expert_ffn",
        )(block_expert, n_used, *args)
    return out


def _dispatch_tables(expert_idx_t):
    n_tok = expert_idx_t.shape[1]
    n_assign = n_tok * TOP_K
    e_flat = expert_idx_t.reshape(n_assign)
    onehot = e_flat[:, None] == jnp.arange(N_EXPERTS)[None, :]
    chunk = min(1024, n_assign)
    oh = onehot.astype(F32).reshape(n_assign // chunk, chunk, N_EXPERTS)
    within = jnp.einsum("ij,bjk->bik", jnp.tril(jnp.ones((chunk, chunk), F32)), oh)
    totals = within[:, -1, :]
    csum = (within + (jnp.cumsum(totals, axis=0) - totals)[:, None, :]).reshape(n_assign, N_EXPERTS)
    rank = jnp.sum(jnp.where(onehot, csum, 0.0), axis=1).astype(jnp.int32) - 1
    counts = csum[-1].astype(jnp.int32)
    padded = (counts + MOE_BLK - 1) // MOE_BLK * MOE_BLK
    pad_end = jnp.cumsum(padded)
    pad_start = pad_end - padded
    dest = pad_start[e_flat] + rank
    n_rows = n_assign + N_EXPERTS * MOE_BLK
    block_start = jnp.arange(n_rows // MOE_BLK, dtype=jnp.int32) * MOE_BLK
    block_expert = jnp.minimum(jnp.sum(pad_end[None, :] <= block_start[:, None], axis=1), N_EXPERTS - 1)
    order = jnp.sort(e_flat * n_assign + jnp.arange(n_assign, dtype=jnp.int32)) % n_assign
    row_expert = jnp.repeat(block_expert, MOE_BLK)
    row_rank = jnp.arange(n_rows, dtype=jnp.int32) - pad_start[row_expert]
    source = order[jnp.clip((jnp.cumsum(counts) - counts)[row_expert] + row_rank, 0, n_assign - 1)]
    row_tok = jnp.where(row_rank < counts[row_expert], source % n_tok, 0)
    n_used = (pad_end[-1:] // MOE_BLK).astype(jnp.int32)
    return row_tok.astype(jnp.int32), block_expert.astype(jnp.int32), n_used, dest.reshape(TOP_K, n_tok)


def _token_mixer(xb, bsz, seq, w_in_t, layer, conv_a_w, conv_a_b, lru_conv_w, lru_conv_b, lru_wa, lru_ba, lru_wx, lru_bx,
                 lru_lam, cmp_pos, phi_w1, phi_b1, phi_w2, w_branch):
    d = xb.shape[1]
    n_tok = bsz * seq
    main_w = 3 * BRANCH_W + 2 * BRANCH_W + NSA_W + 6 * KV_W
    ng = 3 * NSA_HEADS
    kv0 = main_w - 6 * KV_W
    zmain = matmul_wres(xb, w_in_t, layer, 0, main_w, BF16, 2048, 768)
    zgate = matmul_wres(xb, w_in_t, layer, main_w, LANES, F32, 2048, LANES)
    zmerge = matmul_wres(xb, w_in_t, layer, main_w + ng, w_in_t.shape[1] - main_w - ng, BF16, 2048, 512)
    z3 = zmain.reshape(bsz, seq, main_w)

    y_a = conv_mixer(z3, conv_a_w, conv_a_b)
    y_b = lru_mixer(z3, 3 * BRANCH_W, lru_conv_w, lru_conv_b, lru_wa, lru_ba, lru_wx, lru_bx, lru_lam)

    cmp = nsa_compress(z3, kv0, cmp_pos, phi_w1, phi_b1, phi_w2)
    gate_logits = zgate[:, :ng].reshape(bsz, seq, NSA_KV_HEADS, NSA_GRP * 3).transpose(0, 2, 3, 1)
    y_c = nsa_attention(z3, 5 * BRANCH_W, kv0 + 2 * KV_W, cmp[0], cmp[1], gate_logits)

    flat = lambda y: y.reshape(n_tok, BRANCH_W)
    return branch_merge(flat(y_a), flat(y_b), flat(y_c), w_branch.astype(BF16), zmerge)


def _moe(x, xb, w_router, b_router, w_gate_up, w_down, layer):
    idx_t, gate_t = router(x, w_router, b_router)
    row_tok, block_expert, n_used, dest = _dispatch_tables(idx_t)
    piece = xb.shape[0] // 2
    x_parts = [xb[row_tok[a:a + piece]] for a in range(0, row_tok.shape[0], piece)]
    rows = expert_ffn(block_expert, n_used, x_parts, w_gate_up, w_down, layer)
    return rows[dest[0]], rows[dest[1]], gate_t.T


def kernel(x, p, w_in, conv_a_w, conv_a_b, lru_conv_w, lru_conv_b, lru_wa, lru_ba, lru_wx, lru_bx, lru_lam, cmp_pos, phi_w1, phi_b1, phi_w2, w_branch, w_out, ln_g, ln_b, w_router, b_router, w_gate_up, w_down, w_ple, w_ple_gate):
    bsz, seq, d = x.shape
    depth = w_in.shape[0]
    n_tok = bsz * seq
    alpha = (2 * depth) ** 0.25
    xf = x.reshape(n_tok, d)
    xb = xf.astype(BF16)
    w_in_t = jnp.swapaxes(w_in, 1, 2)
    for i in range(depth):
        merged = _token_mixer(xb, bsz, seq, w_in_t, i, conv_a_w[i], conv_a_b[i], lru_conv_w[i], lru_conv_b[i],
                              lru_wa[i], lru_ba[i], lru_wx[i], lru_bx[i], lru_lam[i], cmp_pos[i], phi_w1[i],
                              phi_b1[i], phi_w2[i], w_branch[i])
        xf, xb = outproj_ln(merged, w_out[i].astype(BF16), xf, ln_g[i, 0], ln_b[i, 0], alpha)
        y0, y1, gate = _moe(xf, xb, w_router, b_router, w_gate_up, w_down, i)
        xf, xb = ple_combine_ln(xf, xb, w_ple_gate[i].astype(BF16), p[i].reshape(n_tok, -1),
                                w_ple[i].astype(BF16), y0, y1, gate, ln_g[i, 1], ln_b[i, 1], alpha)
    return xf.reshape(bsz, seq, d)
```

```python
import functools

import ml_dtypes
import numpy as np
import jax
import jax.numpy as jnp
from jax import lax
from jax.experimental import pallas as pl
from jax.experimental.pallas import tpu as pltpu

F32 = jnp.float32
BF16 = jnp.bfloat16

BRANCH_W = 1024
CONV_K = 3
LRU_HEADS = 8
LRU_HD = BRANCH_W // LRU_HEADS
LRU_CONV_K = 4
LRU_C = 8.0
NSA_HEADS = 16
NSA_KV_HEADS = 4
NSA_GRP = NSA_HEADS // NSA_KV_HEADS
NSA_HD = 64
NSA_W = NSA_HEADS * NSA_HD
KV_W = NSA_KV_HEADS * NSA_HD
CMP_BLK = 32
CMP_STRIDE = 16
CMP_HIDDEN = 128
SLC_BLK = 64
SLC_TOPN = 16
WINDOW = 512
N_EXPERTS = 16
N_GROUPS = 4
EXPERTS_PER_GROUP = N_EXPERTS // N_GROUPS
TOP_K = 2
LN_EPS = 1e-5
NEG = -1e30
FORCE = 1e9

LANES = 128
SUBLANES = 8
VMEM_LIMIT = 56 * 1024 * 1024

FEAT = 256
MASK_OFF = NSA_HD
ALIBI_OFF = 2 * NSA_HD
PAD_OFF = ALIBI_OFF + 6
V_ROWS = NSA_HD + 16
LOG2E = 1.4426950408889634

ATT_TQ = 256
ATT_TK = 512
EXP_CHUNK = 64
MOE_BLK = 256
EXPERT_CHUNKS = 2


def _cparams(sem):
    return pltpu.CompilerParams(dimension_semantics=sem, vmem_limit_bytes=VMEM_LIMIT)


def _mm_wres_kernel(x_ref, w_ref, *rest, shift):
    o_ref, wb_ref = rest[-2:]

    @pl.when(pl.program_id(1) == 0)
    def _():
        w = w_ref[0]
        if shift:
            tn = w.shape[0]
            w = jnp.concatenate([w, rest[0][0]], axis=0)[shift:shift + tn]
        wb_ref[...] = w.astype(BF16)

    o_ref[...] = lax.dot_general(x_ref[...], wb_ref[...], (((1,), (1,)), ((), ())),
                                 preferred_element_type=F32).astype(o_ref.dtype)


def matmul_wres(x, wt3, layer, row0, n, out_dtype, tm, tn):
    m, k = x.shape
    shift = row0 % tn
    base = row0 - shift
    assert m % tm == 0 and n % tn == 0 and shift % SUBLANES == 0 and shift <= LANES and tn % LANES == 0
    in_specs = [pl.BlockSpec((tm, k), lambda j, i: (i, 0)),
                pl.BlockSpec((1, tn, k), lambda j, i: (layer, base // tn + j, 0))]
    args = [x, wt3]
    if shift:
        in_specs.append(pl.BlockSpec((1, LANES, k), lambda j, i: (layer, (base + (j + 1) * tn) // LANES, 0)))
        args.append(wt3)
    return pl.pallas_call(
        functools.partial(_mm_wres_kernel, shift=shift),
        grid=(n // tn, m // tm),
        in_specs=in_specs,
        out_specs=pl.BlockSpec((tm, tn), lambda j, i: (i, j)),
        out_shape=jax.ShapeDtypeStruct((m, n), out_dtype),
        scratch_shapes=[pltpu.VMEM((tn, k), BF16)],
        compiler_params=_cparams(("parallel", "arbitrary")),
        name="matmul_wres",
    )(*args)


def _causal_conv(ext, cur, w, k):
    acc = cur * w[k - 1:k, :]
    for j in range(k - 1):
        shift = k - 1 - j
        acc = acc + pltpu.roll(ext, shift, 0)[SUBLANES:, :] * w[j:j + 1, :]
    return acc


def _conv_mixer_kernel(ain_ref, ab_ref, ac_ref, w_ref, b_ref, o_ref, halo_ref):
    @pl.when(pl.program_id(2) == 0)
    def _():
        halo_ref[...] = jnp.zeros_like(halo_ref)

    v = ac_ref[0].astype(F32) * ain_ref[0].astype(F32)
    ext = jnp.concatenate([halo_ref[...], v], axis=0)
    y = _causal_conv(ext, v, w_ref[...], CONV_K) + b_ref[...]
    o_ref[0] = (ab_ref[0].astype(F32) * y).astype(o_ref.dtype)
    halo_ref[...] = v[v.shape[0] - SUBLANES:, :]


def conv_mixer(z3, conv_w, conv_b, ts=512, cw=512):
    b, s, _ = z3.shape
    nc = BRANCH_W // cw
    return pl.pallas_call(
        _conv_mixer_kernel,
        grid=(b, nc, s // ts),
        in_specs=[pl.BlockSpec((1, ts, cw), lambda bi, c, si: (bi, si, c)),
                  pl.BlockSpec((1, ts, cw), lambda bi, c, si: (bi, si, c + nc)),
                  pl.BlockSpec((1, ts, cw), lambda bi, c, si: (bi, si, c + 2 * nc)),
                  pl.BlockSpec((CONV_K, cw), lambda bi, c, si: (0, c)),
                  pl.BlockSpec((1, cw), lambda bi, c, si: (0, c))],
        out_specs=pl.BlockSpec((1, ts, cw), lambda bi, c, si: (bi, si, c)),
        out_shape=jax.ShapeDtypeStruct((b, s, BRANCH_W), BF16),
        scratch_shapes=[pltpu.VMEM((SUBLANES, cw), F32)],
        compiler_params=_cparams(("parallel", "parallel", "arbitrary")),
        name="conv_mixer",
    )(z3, z3, z3, conv_w, conv_b.reshape(1, BRANCH_W))


def _lru_kernel(gate_ref, rin_ref, cw_ref, cb_ref, wa_ref, ba_ref, wx_ref, bx_ref, lam_ref, o_ref,
                halo_ref, carry_ref, a_ref, b_ref, h_ref):
    ts, lw = a_ref.shape

    @pl.when(pl.program_id(2) == 0)
    def _():
        halo_ref[...] = jnp.zeros_like(halo_ref)
        carry_ref[...] = jnp.zeros_like(carry_ref)

    u = rin_ref[0].astype(F32)
    ext = jnp.concatenate([halo_ref[...], u], axis=0)
    xc = _causal_conv(ext, u, cw_ref[...], LRU_CONV_K) + cb_ref[...]
    halo_ref[...] = u[ts - SUBLANES:, :]

    xcb = xc.astype(BF16)
    ra, ix = [], []
    for hh in range(lw // LRU_HD):
        xh = xcb[:, hh * LRU_HD:(hh + 1) * LRU_HD]
        ra.append(jnp.dot(xh, wa_ref[hh].astype(BF16), preferred_element_type=F32))
        ix.append(jnp.dot(xh, wx_ref[hh].astype(BF16), preferred_element_type=F32))
    r = jax.nn.sigmoid(jnp.concatenate(ra, axis=1) + ba_ref[...])
    i = jax.nn.sigmoid(jnp.concatenate(ix, axis=1) + bx_ref[...])
    neg_lam = -lam_ref[...]
    softplus = jnp.maximum(neg_lam, 0.0) + jnp.log1p(jnp.exp(-jnp.abs(neg_lam)))
    log_a = -LRU_C * r * softplus
    a_ref[...] = jnp.exp(log_a)
    th = jnp.tanh(log_a)
    b_ref[...] = jnp.sqrt(-2.0 * th / (1.0 - th)) * (i * xc)

    row = lax.broadcasted_iota(jnp.int32, (SUBLANES, lw), 0)

    def group(g, hc):
        sl = pl.ds(pl.multiple_of(g * SUBLANES, SUBLANES), SUBLANES)
        a = a_ref[sl, :]
        bv = b_ref[sl, :]
        for d in (1, 2, 4):
            keep = row >= d
            bv = jnp.where(keep, a * pltpu.roll(bv, d, 0) + bv, bv)
            a = jnp.where(keep, a * pltpu.roll(a, d, 0), a)
        h = a * hc + bv
        h_ref[sl, :] = h
        return jnp.broadcast_to(h[SUBLANES - 1:SUBLANES, :], (SUBLANES, lw))

    carry_ref[...] = lax.fori_loop(0, ts // SUBLANES, group, carry_ref[...])
    o_ref[0] = (h_ref[...] * jax.nn.gelu(gate_ref[0].astype(F32))).astype(o_ref.dtype)


def lru_mixer(z3, col0, conv_w, conv_b, wa, ba, wx, bx, lam, ts=512, lw=256):
    b, s, _ = z3.shape
    nc = BRANCH_W // lw
    c0 = col0 // lw
    hpb = lw // LRU_HD
    row = lambda v: v.reshape(1, BRANCH_W)
    vec = pl.BlockSpec((1, lw), lambda bi, c, si: (0, c))
    mat = pl.BlockSpec((hpb, LRU_HD, LRU_HD), lambda bi, c, si: (c, 0, 0))
    return pl.pallas_call(
        _lru_kernel,
        grid=(b, nc, s // ts),
        in_specs=[pl.BlockSpec((1, ts, lw), lambda bi, c, si: (bi, si, c0 + c)),
                  pl.BlockSpec((1, ts, lw), lambda bi, c, si: (bi, si, c0 + nc + c)),
                  pl.BlockSpec((LRU_CONV_K, lw), lambda bi, c, si: (0, c)),
                  vec, mat, vec, mat, vec, vec],
        out_specs=pl.BlockSpec((1, ts, lw), lambda bi, c, si: (bi, si, c)),
        out_shape=jax.ShapeDtypeStruct((b, s, BRANCH_W), BF16),
        scratch_shapes=[pltpu.VMEM((SUBLANES, lw), F32), pltpu.VMEM((SUBLANES, lw), F32),
                        pltpu.VMEM((ts, lw), F32), pltpu.VMEM((ts, lw), F32), pltpu.VMEM((ts, lw), F32)],
        compiler_params=_cparams(("parallel", "parallel", "arbitrary")),
        name="rg_lru",
    )(z3, z3, conv_w, row(conv_b), wa, row(ba), wx, row(bx), row(lam))


def _compress_kernel(x_ref, pos_ref, w1_ref, b1_ref, w2_ref, o_ref, xs_ref):
    seq = x_ref.shape[1]
    nslot = seq // CMP_STRIDE
    xs_ref[...] = x_ref[0].astype(F32)
    w1 = w1_ref[0].astype(BF16)
    zeros = jnp.zeros((NSA_HD, CMP_HIDDEN), BF16)
    heads = 2
    first = [jnp.zeros((nslot, CMP_HIDDEN), F32) for _ in range(heads)]
    second = [jnp.zeros((nslot, CMP_HIDDEN), F32) for _ in range(heads)]
    for o in range(CMP_STRIDE):
        rows = xs_ref[pl.ds(o, nslot, stride=CMP_STRIDE), :].astype(BF16)
        w_a = w1[o * NSA_HD:(o + 1) * NSA_HD]
        w_b = w1[(CMP_STRIDE + o) * NSA_HD:(CMP_STRIDE + o + 1) * NSA_HD]
        for h in range(heads):
            pad = lambda w: jnp.concatenate([w, zeros] if h == 0 else [zeros, w], axis=0)
            first[h] = first[h] + jnp.dot(rows, pad(w_a), preferred_element_type=F32)
            second[h] = second[h] + jnp.dot(rows, pad(w_b), preferred_element_type=F32)
    pos = jnp.broadcast_to(pos_ref[0], (SUBLANES, CMP_BLK * NSA_HD)).astype(BF16)
    posb = jnp.dot(pos, w1, preferred_element_type=F32)[0:1, :]
    w2 = w2_ref[0].astype(BF16)
    for h in range(heads):
        hidden = first[h] + pltpu.roll(second[h], nslot - 1, 0) + posb + b1_ref[0]
        out = jnp.dot(jax.nn.gelu(hidden).astype(BF16), w2, preferred_element_type=F32)
        o_ref[0, 0, h] = out.astype(o_ref.dtype)


def nsa_compress(z3, col0, cmp_pos, w1, b1, w2):
    b, s, _ = z3.shape
    assert CMP_BLK == 2 * CMP_STRIDE and 2 * NSA_HD == LANES and col0 % LANES == 0 and s % CMP_STRIDE == 0
    nslot = s // CMP_STRIDE
    pos = cmp_pos.reshape(2, 1, CMP_BLK * NSA_HD)
    return pl.pallas_call(
        _compress_kernel,
        grid=(2, b, NSA_KV_HEADS // 2),
        in_specs=[pl.BlockSpec((1, s, LANES), lambda kv, bi, hp: (bi, 0, (col0 + kv * KV_W) // LANES + hp)),
                  pl.BlockSpec((1, 1, CMP_BLK * NSA_HD), lambda kv, bi, hp: (kv, 0, 0)),
                  pl.BlockSpec((1, CMP_BLK * NSA_HD, CMP_HIDDEN), lambda kv, bi, hp: (kv, 0, 0)),
                  pl.BlockSpec((1, 1, CMP_HIDDEN), lambda kv, bi, hp: (kv, 0, 0)),
                  pl.BlockSpec((1, CMP_HIDDEN, NSA_HD), lambda kv, bi, hp: (kv, 0, 0))],
        out_specs=pl.BlockSpec((1, 1, 2, nslot, NSA_HD), lambda kv, bi, hp: (kv, bi, hp, 0, 0)),
        out_shape=jax.ShapeDtypeStruct((2, b, NSA_KV_HEADS, nslot, NSA_HD), BF16),
        scratch_shapes=[pltpu.VMEM((s, LANES), F32)],
        compiler_params=_cparams(("parallel", "parallel", "parallel")),
        name="nsa_compress",
    )(z3, pos, w1, b1.reshape(2, 1, CMP_HIDDEN), w2)


def _exp2_bf16(s, m):
    rows = s.shape[0]
    chunk = min(EXP_CHUNK, rows)
    return jnp.concatenate([jnp.exp2((s[r:r + chunk, :] - m).astype(BF16)) for r in range(0, rows, chunk)], axis=0)


def _nsa_kernel(q_ref, qf_ref, kc_ref, vc_ref, ksb_ref, vsb_ref, kwb_ref, vwb_ref, fs_ref, fw_ref, gl_ref, c2s_ref,
                bias_ref, o_ref, ks_ref, vs_ref, kw_ref, vw_ref, qp_ref, imp_ref, s0_ref, s1_ref, mx_ref, m_ref, acc_ref,
                *, n_cmp):
    tq = q_ref.shape[1]
    lanes = NSA_GRP * tq
    seq = ks_ref.shape[0]
    n_slc = seq // SLC_BLK
    t0 = pl.multiple_of(pl.program_id(2) * tq, tq)
    head = lambda r: slice(r * tq, (r + 1) * tq)

    @pl.when(pl.program_id(2) == 0)
    def _():
        odd = pl.program_id(1) % 2 == 1

        def keys(blk_ref):
            blk = blk_ref[0].astype(F32)
            return jnp.where(odd, pltpu.roll(blk, NSA_HD, 1), blk)[:, :NSA_HD].astype(BF16)

        def values_t(blk_ref):
            vt = blk_ref[0].astype(F32).T
            return jnp.where(odd, vt[NSA_HD:], vt[:NSA_HD]).astype(BF16)

        def ones_row(n):
            first = lax.broadcasted_iota(jnp.int32, (V_ROWS - NSA_HD, n), 0) == 0
            return jnp.where(first, 1.0, 0.0).astype(BF16)

        ks_ref[:, 0:NSA_HD] = keys(ksb_ref)
        ks_ref[:, NSA_HD:] = fs_ref[...]
        vs_ref[0:NSA_HD, :] = values_t(vsb_ref)
        vs_ref[NSA_HD:, :] = ones_row(seq)
        kw_ref[0:WINDOW, 0:NSA_HD] = jnp.zeros((WINDOW, NSA_HD), BF16)
        kw_ref[WINDOW:, 0:NSA_HD] = keys(kwb_ref)
        kw_ref[:, NSA_HD:] = fw_ref[...]
        vw_ref[0:NSA_HD, 0:WINDOW] = jnp.zeros((NSA_HD, WINDOW), BF16)
        vw_ref[0:NSA_HD, WINDOW:] = values_t(vwb_ref)
        vw_ref[NSA_HD:, :] = ones_row(seq + WINDOW)

    def scores(k_tile, bias=None):
        s = jnp.dot(k_tile, qp_ref[...], preferred_element_type=F32)
        return s if bias is None else s + bias

    qt = q_ref[0].astype(F32).T * (NSA_HD ** -0.5 * LOG2E)
    qp_ref[...] = qf_ref[0]
    for r in range(NSA_GRP):
        qp_ref[0:NSA_HD, head(r)] = qt[r * NSA_HD:(r + 1) * NSA_HD, :].astype(BF16)

    s_cmp = scores(kc_ref[0, 0])
    win = []
    for off, size, bias in ((0, tq, bias_ref[1]), (tq, WINDOW - tq, None), (WINDOW, tq, bias_ref[0])):
        k0 = pl.multiple_of(t0 + off, tq)
        win.append((scores(kw_ref[pl.ds(k0, size), :], bias), vw_ref[:, pl.ds(k0, size)]))

    ncp = kc_ref.shape[2]
    n_row = lax.broadcasted_iota(jnp.int32, (ncp, lanes), 0)
    t_lane = t0 + lax.broadcasted_iota(jnp.int32, (ncp, lanes), 1) % tq
    ok = (t_lane >= n_row * CMP_STRIDE + (CMP_BLK - 1)) & (n_row < n_cmp)
    s = jnp.where(ok, s_cmp, NEG)
    p = jnp.where(ok, jnp.exp2(s - jnp.max(s, axis=0, keepdims=True)), 0.0)
    l = jnp.sum(p, axis=0, keepdims=True)
    p_cmp = p * (1.0 / jnp.where(l > 0.0, l, 1.0))
    o_cmp = jnp.dot(vc_ref[0, 0], p_cmp.astype(BF16), preferred_element_type=F32)

    rest = p_cmp[:, head(0)]
    for r in range(1, NSA_GRP):
        rest = rest + p_cmp[:, head(r)]
    imp = jnp.zeros((n_slc, tq), F32)
    for _ in range(3):
        part = rest.astype(BF16)
        rest = rest - part.astype(F32)
        imp = imp + jnp.dot(c2s_ref[...], part, preferred_element_type=F32)
    j_row = lax.broadcasted_iota(jnp.int32, (n_slc, tq), 0)
    t_col = t0 + lax.broadcasted_iota(jnp.int32, (n_slc, tq), 1)
    forced = (j_row == 0) | (j_row == t_col // SLC_BLK)
    causal = j_row * SLC_BLK <= t_col
    imp = jnp.where(forced, FORCE, jnp.where(causal, imp, -FORCE))
    imp_ref[...] = imp

    n_grp = n_slc // SUBLANES
    vals = [imp[v * SUBLANES:(v + 1) * SUBLANES, :] for v in range(n_grp)]
    sub = lax.broadcasted_iota(jnp.int32, (SUBLANES, tq), 0)
    rank = [jnp.zeros((SUBLANES, tq), jnp.int32) for _ in range(n_grp)]
    for i in range(n_slc):
        bi = jnp.broadcast_to(imp_ref[i:i + 1, :], (SUBLANES, tq))
        for v in range(n_grp):
            if v < i // SUBLANES:
                ahead = bi > vals[v]
            elif v > i // SUBLANES:
                ahead = bi >= vals[v]
            else:
                ahead = (bi > vals[v]) | ((bi == vals[v]) & (sub > i % SUBLANES))
            rank[v] = rank[v] + jnp.where(ahead, 1, 0)
    chosen = (jnp.concatenate(rank, axis=0) < min(SLC_TOPN, n_slc)) & causal
    mask = jnp.where(chosen, 0.0, NEG).astype(BF16)
    for r in range(NSA_GRP):
        qp_ref[MASK_OFF:MASK_OFF + n_slc, head(r)] = mask

    tk = min(ATT_TK, seq)
    n_full = t0 // tk
    last = n_full // 2

    s_bufs = (s0_ref, s1_ref)

    def pair_scores(step, buf):
        for par in range(2):
            k0 = pl.multiple_of((2 * step + par) * tk, tk)
            s = scores(ks_ref[pl.ds(k0, tk), :])
            s_bufs[buf][par] = s
            mx_ref[buf, par] = jnp.max(s, axis=0, keepdims=True)

    def pair_update(step, buf, prefetch, tiles=(0, 1)):
        m_new = {par: jnp.maximum(m_ref[par], mx_ref[buf, par]) for par in tiles}
        probs = {par: _exp2_bf16(s_bufs[buf].at[par], m_new[par]) for par in tiles}
        if prefetch:
            pair_scores(step + 1, 1 - buf)
        for par in tiles:
            k0 = pl.multiple_of((2 * step + par) * tk, tk)
            pv = jnp.dot(vs_ref[:, pl.ds(k0, tk)], probs[par], preferred_element_type=F32)
            acc_ref[par] = jnp.exp2(m_ref[par] - m_new[par]) * acc_ref[par] + pv
            m_ref[par] = m_new[par]

    def mask_diagonal(buf):
        off = pl.multiple_of(t0 - n_full * tk, tq)
        for par in range(2):
            @pl.when(n_full % 2 == par)
            def _():
                tile = s_bufs[buf].at[par]
                tile[pl.ds(off, tq), :] = tile[pl.ds(off, tq), :] + bias_ref[0]
                mx_ref[buf, par] = jnp.max(tile[...], axis=0, keepdims=True)

    def half_trip(step, buf):
        @pl.when(step == last)
        def _():
            mask_diagonal(buf)

        @pl.when(step < last)
        def _():
            pair_update(step, buf, prefetch=True)

        @pl.when((step == last) & (n_full % 2 == 1))
        def _():
            pair_update(step, buf, prefetch=False)

        @pl.when((step == last) & (n_full % 2 == 0))
        def _():
            pair_update(step, buf, prefetch=False, tiles=(0,))

    def trip(i, carry):
        half_trip(2 * i, 0)
        half_trip(2 * i + 1, 1)
        return carry

    m_ref[...] = jnp.full(m_ref.shape, NEG, F32)
    acc_ref[...] = jnp.zeros(acc_ref.shape, F32)
    pair_scores(0, 0)

    mx = functools.reduce(jnp.maximum, [jnp.max(s, axis=0, keepdims=True) for s, _ in win])
    o_win = jnp.zeros((V_ROWS, lanes), F32)
    for s, v_tile in win:
        o_win = o_win + jnp.dot(v_tile, _exp2_bf16(s, mx), preferred_element_type=F32)
    o_win = o_win[0:NSA_HD] * (1.0 / o_win[NSA_HD:NSA_HD + 1])

    lax.fori_loop(0, last // 2 + 1, trip, 0)
    m_all = jnp.maximum(m_ref[0], m_ref[1])
    o_slc = jnp.exp2(m_ref[0] - m_all) * acc_ref[0] + jnp.exp2(m_ref[1] - m_all) * acc_ref[1]
    o_slc = o_slc[0:NSA_HD] * (1.0 / o_slc[NSA_HD:NSA_HD + 1])

    gates = jax.nn.sigmoid(gl_ref[0, 0].astype(F32))
    outs = []
    for r in range(NSA_GRP):
        outs.append(gates[3 * r:3 * r + 1] * o_cmp[:, head(r)] + gates[3 * r + 1:3 * r + 2] * o_slc[:, head(r)]
                    + gates[3 * r + 2:3 * r + 3] * o_win[:, head(r)])
    o_ref[0] = jnp.concatenate(outs, axis=0).T.astype(o_ref.dtype)


def _alibi_slopes():
    return (2.0 ** (-8.0 * np.arange(1, NSA_HEADS + 1) / NSA_HEADS)).astype(np.float32)


def _bf16_split3(v):
    parts, rest = [], np.asarray(v, np.float32)
    for _ in range(3):
        part = rest.astype(ml_dtypes.bfloat16).astype(np.float32)
        parts.append(part)
        rest = (rest.astype(np.float64) - part.astype(np.float64)).astype(np.float32)
    return parts


def _query_features(tq):
    feat = np.zeros((NSA_HEADS, FEAT), np.float32)
    hi, mid, lo = _bf16_split3(_alibi_slopes() * np.float32(LOG2E))
    for c, part in enumerate((hi, mid, lo, hi, mid, lo)):
        feat[:, ALIBI_OFF + c] = part
    feat[:, PAD_OFF] = NEG
    feat = np.repeat(feat.reshape(NSA_KV_HEADS, NSA_GRP, FEAT), tq, axis=1)
    return jnp.asarray(feat.transpose(0, 2, 1), BF16)


def _key_features(pos, n_slc):
    pos = np.asarray(pos)
    real = pos >= 0
    feat = np.zeros((pos.shape[0], FEAT - NSA_HD), np.float32)
    if n_slc:
        feat[real, pos[real] // SLC_BLK] = 1.0
    a0 = ALIBI_OFF - NSA_HD
    feat[real, a0:a0 + 3] = ((pos[real] // SLC_BLK) * SLC_BLK)[:, None]
    feat[real, a0 + 3:a0 + 6] = (pos[real] % SLC_BLK)[:, None]
    feat[~real, PAD_OFF - NSA_HD] = 1.0
    return feat


def _with_features(k, feat):
    b, g, _, _ = k.shape
    f = jnp.broadcast_to(jnp.asarray(feat, BF16), (b, g) + feat.shape)
    return jnp.concatenate([k.astype(BF16), f], axis=3)


def nsa_attention(z3, qcol0, kvcol0, kc, vc, gate_logits):
    b, s, _ = z3.shape
    tq = ATT_TQ
    tk = min(ATT_TK, s)
    assert s % (2 * tk) == 0 and tk % tq == 0 and tq % SLC_BLK == 0 and WINDOW % tq == 0 and WINDOW > tq
    n_slc = s // SLC_BLK
    assert n_slc <= ALIBI_OFF - MASK_OFF and n_slc % SUBLANES == 0
    assert 2 * NSA_HD == LANES and NSA_KV_HEADS % 2 == 0 and kvcol0 % LANES == 0
    n_cmp = (s - CMP_BLK) // CMP_STRIDE + 1
    ncp = kc.shape[2]
    gw = NSA_GRP * NSA_HD
    lanes = NSA_GRP * tq

    kc_f = _with_features(kc, _key_features(np.arange(ncp) * CMP_STRIDE + CMP_BLK - 1, 0))
    feat_s = jnp.asarray(_key_features(np.arange(s), n_slc), BF16)
    feat_w = jnp.asarray(_key_features(np.arange(-WINDOW, s), 0), BF16)

    c_start = np.arange(ncp) * CMP_STRIDE
    s_start = np.arange(n_slc) * SLC_BLK
    overlap = np.clip(np.minimum(c_start[:, None] + CMP_BLK, s_start[None, :] + SLC_BLK)
                      - np.maximum(c_start[:, None], s_start[None, :]), 0, None) / CMP_BLK
    overlap[n_cmp:] = 0.0
    c2s_t = jnp.asarray(overlap.T, BF16)

    key = np.arange(tq)[:, None]
    qry = np.tile(np.arange(tq), NSA_GRP)[None, :]
    bias = jnp.asarray(np.stack([np.where(key <= qry, 0.0, NEG), np.where(key > qry, 0.0, NEG)]), F32)

    blk = lambda *shape: pl.BlockSpec((1, 1) + shape, lambda bi, g, qi: (bi, g, 0, 0))
    const = lambda *shape: pl.BlockSpec(shape, lambda bi, g, qi: (0,) * len(shape))
    pair = lambda j: pl.BlockSpec((1, s, LANES), lambda bi, g, qi: (bi, 0, (kvcol0 + j * KV_W) // LANES + g // 2))
    return pl.pallas_call(
        functools.partial(_nsa_kernel, n_cmp=n_cmp),
        grid=(b, NSA_KV_HEADS, s // tq),
        in_specs=[pl.BlockSpec((1, tq, gw), lambda bi, g, qi: (bi, qi, qcol0 // gw + g)),
                  pl.BlockSpec((1, FEAT, lanes), lambda bi, g, qi: (g, 0, 0)),
                  blk(ncp, FEAT), blk(NSA_HD, ncp),
                  pair(0), pair(1), pair(2), pair(3),
                  const(s, FEAT - NSA_HD), const(s + WINDOW, FEAT - NSA_HD),
                  pl.BlockSpec((1, 1, NSA_GRP * 3, tq), lambda bi, g, qi: (bi, g, 0, qi)),
                  const(n_slc, ncp), const(2, tq, lanes)],
        out_specs=pl.BlockSpec((1, tq, gw), lambda bi, g, qi: (bi, qi, g)),
        out_shape=jax.ShapeDtypeStruct((b, s, NSA_W), BF16),
        scratch_shapes=[pltpu.VMEM((s, FEAT), BF16), pltpu.VMEM((V_ROWS, s), BF16),
                        pltpu.VMEM((s + WINDOW, FEAT), BF16), pltpu.VMEM((V_ROWS, s + WINDOW), BF16),
                        pltpu.VMEM((FEAT, lanes), BF16), pltpu.VMEM((n_slc, tq), F32),
                        pltpu.VMEM((2, tk, lanes), F32), pltpu.VMEM((2, tk, lanes), F32),
                        pltpu.VMEM((2, 2, 1, lanes), F32),
                        pltpu.VMEM((2, 1, lanes), F32), pltpu.VMEM((2, V_ROWS, lanes), F32)],
        compiler_params=_cparams(("parallel", "parallel", "arbitrary")),
        name="nsa_attention",
    )(z3, _query_features(tq), kc_f, jnp.swapaxes(vc, 2, 3), z3, z3, z3, z3, feat_s, feat_w, gate_logits, c2s_t,
      bias)


def _merge_kernel(ya_ref, yb_ref, yc_ref, wb_ref, ga_ref, gb_ref, gc_ref, o_ref):
    acc = None
    for i, (y_ref, g_ref) in enumerate(((ya_ref, ga_ref), (yb_ref, gb_ref), (yc_ref, gc_ref))):
        term = jax.nn.sigmoid(g_ref[...].astype(F32)) * jnp.dot(y_ref[...], wb_ref[i], preferred_element_type=F32)
        acc = term if acc is None else acc + term
    o_ref[...] = acc.astype(o_ref.dtype)


def branch_merge(ya, yb, yc, wb, zmerge, tm=1024, tn=1024):
    t = ya.shape[0]
    d = wb.shape[2]
    nj = d // tn
    y_spec = pl.BlockSpec((tm, BRANCH_W), lambda i, j: (i, 0))
    gate = lambda br: pl.BlockSpec((tm, tn), lambda i, j: (i, j + br * nj))
    return pl.pallas_call(
        _merge_kernel,
        grid=(t // tm, nj),
        in_specs=[y_spec, y_spec, y_spec,
                  pl.BlockSpec((3, BRANCH_W, tn), lambda i, j: (0, 0, j)),
                  gate(0), gate(1), gate(2)],
        out_specs=pl.BlockSpec((tm, tn), lambda i, j: (i, j)),
        out_shape=jax.ShapeDtypeStruct((t, d), BF16),
        compiler_params=_cparams(("parallel", "parallel")),
        name="branch_merge",
    )(ya, yb, yc, wb, zmerge, zmerge, zmerge)


def _layer_norm(y, g, b):
    mu = jnp.mean(y, axis=-1, keepdims=True)
    yc = y - mu
    var = jnp.mean(yc * yc, axis=-1, keepdims=True)
    return yc * lax.rsqrt(var + LN_EPS) * g + b


def _outproj_ln_kernel(m_ref, w_ref, x_ref, g_ref, b_ref, o_ref, ob_ref, *, alpha):
    mix = jnp.dot(m_ref[...], w_ref[...], preferred_element_type=F32)
    y = _layer_norm(alpha * x_ref[...] + mix, g_ref[...], b_ref[...])
    o_ref[...] = y
    ob_ref[...] = y.astype(BF16)


def outproj_ln(merged, w_out, x, g, b, alpha, tm=512):
    t, d = x.shape
    row = pl.BlockSpec((tm, d), lambda i: (i, 0))
    vec = pl.BlockSpec((1, d), lambda i: (0, 0))
    return pl.pallas_call(
        functools.partial(_outproj_ln_kernel, alpha=alpha),
        grid=(t // tm,),
        in_specs=[row, pl.BlockSpec((d, d), lambda i: (0, 0)), row, vec, vec],
        out_specs=[row, row],
        out_shape=[jax.ShapeDtypeStruct((t, d), F32), jax.ShapeDtypeStruct((t, d), BF16)],
        compiler_params=_cparams(("parallel",)),
        name="outproj_ln",
    )(merged, w_out, x, g.reshape(1, d), b.reshape(1, d))


def _final_kernel(x_ref, xb_ref, wg_ref, p_ref, wp_ref, y0_ref, y1_ref, gate_ref, g_ref, b_ref, o_ref, ob_ref,
                  *, alpha):
    ple = jax.nn.sigmoid(jnp.dot(xb_ref[...], wg_ref[...], preferred_element_type=F32)) * jnp.dot(
        p_ref[...].astype(BF16), wp_ref[...], preferred_element_type=F32)
    gate = gate_ref[...]
    ffn = gate[:, 0:1] * y0_ref[...].astype(F32) + gate[:, 1:2] * y1_ref[...].astype(F32)
    y = _layer_norm(alpha * x_ref[...] + ffn + ple, g_ref[...], b_ref[...])
    o_ref[...] = y
    ob_ref[...] = y.astype(BF16)


def ple_combine_ln(x, xb, w_ple_gate, p, w_ple, y0, y1, gate, g, b, alpha, tm=512):
    t, d = x.shape
    pd = p.shape[1]
    row = pl.BlockSpec((tm, d), lambda i: (i, 0))
    vec = pl.BlockSpec((1, d), lambda i: (0, 0))
    return pl.pallas_call(
        functools.partial(_final_kernel, alpha=alpha),
        grid=(t // tm,),
        in_specs=[row, row, pl.BlockSpec((d, d), lambda i: (0, 0)),
                  pl.BlockSpec((tm, pd), lambda i: (i, 0)), pl.BlockSpec((pd, d), lambda i: (0, 0)),
                  row, row, pl.BlockSpec((tm, TOP_K), lambda i: (i, 0)), vec, vec],
        out_specs=[row, row],
        out_shape=[jax.ShapeDtypeStruct((t, d), F32), jax.ShapeDtypeStruct((t, d), BF16)],
        compiler_params=_cparams(("parallel",)),
        name="ple_combine_ln",
    )(x, xb, w_ple_gate, p, w_ple, y0, y1, gate, g.reshape(1, d), b.reshape(1, d))


def _router_kernel(x_ref, wr_ref, br_ref, idx_ref, gate_ref):
    tm = x_ref.shape[0]
    nt = (((1,), (1,)), ((), ()))
    x = x_ref[...]
    x_hi = x.astype(BF16)
    x_lo = (x - x_hi.astype(F32)).astype(BF16)
    w = wr_ref[...]
    w_hi = w.astype(BF16)
    w_lo = (w - w_hi.astype(F32)).astype(BF16)
    both = lax.dot_general(jnp.concatenate([w_hi, w_lo], axis=0), x_hi, nt, preferred_element_type=F32)
    logits = both[:N_EXPERTS] + both[N_EXPERTS:] + lax.dot_general(w_hi, x_lo, nt, preferred_element_type=F32)
    aff = jax.nn.sigmoid(logits)
    sel = aff + br_ref[...]
    scores = []
    for g in range(N_GROUPS):
        v = [sel[g * EXPERTS_PER_GROUP + e:g * EXPERTS_PER_GROUP + e + 1, :] for e in range(EXPERTS_PER_GROUP)]
        best = None
        for a in range(EXPERTS_PER_GROUP):
            for c in range(a + 1, EXPERTS_PER_GROUP):
                pair = v[a] + v[c]
                best = pair if best is None else jnp.maximum(best, pair)
        scores.append(best)
    best_score, best_group = scores[0], jnp.zeros((1, tm), jnp.int32)
    for g in range(1, N_GROUPS):
        better = scores[g] > best_score
        best_score = jnp.where(better, scores[g], best_score)
        best_group = jnp.where(better, g, best_group)
    e_row = lax.broadcasted_iota(jnp.int32, (N_EXPERTS, tm), 0)
    cand = jnp.where(e_row // EXPERTS_PER_GROUP == best_group, sel, NEG)
    picks, gates = [], []
    for _ in range(TOP_K):
        mx = jnp.max(cand, axis=0, keepdims=True)
        pick = jnp.min(jnp.where(cand == mx, e_row, N_EXPERTS), axis=0, keepdims=True)
        hit = e_row == pick
        picks.append(pick)
        gates.append(jnp.sum(jnp.where(hit, aff, 0.0), axis=0, keepdims=True))
        cand = jnp.where(hit, -jnp.inf, cand)
    total = gates[0] + gates[1]
    idx_ref[...] = jnp.concatenate(picks, axis=0)
    gate_ref[...] = jnp.concatenate([gates[0] / total, gates[1] / total], axis=0)


def router(x, w_router, b_router, tm=1024):
    t, d = x.shape
    return pl.pallas_call(
        _router_kernel,
        grid=(t // tm,),
        in_specs=[pl.BlockSpec((tm, d), lambda i: (i, 0)),
                  pl.BlockSpec((N_EXPERTS, d), lambda i: (0, 0)),
                  pl.BlockSpec((N_EXPERTS, 1), lambda i: (0, 0))],
        out_specs=[pl.BlockSpec((TOP_K, tm), lambda i: (0, i)), pl.BlockSpec((TOP_K, tm), lambda i: (0, i))],
        out_shape=[jax.ShapeDtypeStruct((TOP_K, t), jnp.int32), jax.ShapeDtypeStruct((TOP_K, t), F32)],
        compiler_params=_cparams(("parallel",)),
        name="router",
    )(x, w_router.T, b_router.reshape(N_EXPERTS, 1))


def _expert_kernel(be_ref, nu_ref, *refs, part_starts):
    n_parts = len(part_starts)
    x_refs = refs[:n_parts]
    wg_ref, wu_ref, wd_ref = refs[n_parts:n_parts + 3]
    prev_ref = refs[n_parts + 3] if len(refs) == n_parts + 10 else None
    o_ref, wgb_ref, wub_ref, wdb_ref, xbuf_ref, sem_ref = refs[-6:]
    i = pl.program_id(0)
    n_blocks = pl.num_programs(0)
    blk = xbuf_ref.shape[1]

    def fetch(j):
        slot = j % 3
        for p, x_ref in enumerate(x_refs):
            hi = part_starts[p + 1] if p + 1 < n_parts else n_blocks

            @pl.when((j >= part_starts[p]) & (j < hi))
            def _():
                rows = pl.ds(pl.multiple_of((j - part_starts[p]) * blk, blk), blk)
                pltpu.make_async_copy(x_ref.at[rows, :], xbuf_ref.at[slot], sem_ref.at[slot]).start()

    @pl.when(i == 0)
    def _():
        fetch(i)
        fetch(i + 1)

    @pl.when(i + 2 < n_blocks)
    def _():
        fetch(i + 2)

    slot = i % 3
    pltpu.make_async_copy(x_refs[0].at[pl.ds(0, blk), :], xbuf_ref.at[slot], sem_ref.at[slot]).wait()

    @pl.when((i == 0) | (be_ref[i] != be_ref[jnp.maximum(i - 1, 0)]))
    def _():
        wgb_ref[...] = wg_ref[0, 0].astype(BF16)
        wub_ref[...] = wu_ref[0, 0].astype(BF16)
        wdb_ref[...] = wd_ref[0, 0].astype(BF16)

    @pl.when(i < nu_ref[0])
    def _():
        x = xbuf_ref[slot]
        g = jnp.dot(x, wgb_ref[...], preferred_element_type=F32)
        u = jnp.dot(x, wub_ref[...], preferred_element_type=F32)
        out = jnp.dot((jax.nn.silu(g) * u).astype(BF16), wdb_ref[...], preferred_element_type=F32)
        if prev_ref is not None:
            out = out + prev_ref[...].astype(F32)
        o_ref[...] = out.astype(o_ref.dtype)

    @pl.when(i >= nu_ref[0])
    def _():
        o_ref[...] = jnp.zeros_like(o_ref)


def expert_ffn(block_expert, n_used, x_parts, w_gate_up, w_down, layer):
    d = x_parts[0].shape[1]
    n_rows = sum(x.shape[0] for x in x_parts)
    de = w_down.shape[2]
    ce = de // EXPERT_CHUNKS
    part_blocks = [x.shape[0] // MOE_BLK for x in x_parts]
    part_starts = tuple(int(v) for v in np.cumsum([0] + part_blocks[:-1]))
    assert n_rows // MOE_BLK >= 2
    part_specs = [pl.BlockSpec(memory_space=pl.ANY) for _ in x_parts]
    out = None
    for c in range(EXPERT_CHUNKS):
        row = pl.BlockSpec((MOE_BLK, d), lambda i, be, nu: (i, 0))
        in_specs = part_specs + [
            pl.BlockSpec((1, 1, d, ce), lambda i, be, nu, c=c: (layer, be[i], 0, c)),
            pl.BlockSpec((1, 1, d, ce), lambda i, be, nu, c=c: (layer, be[i], 0, EXPERT_CHUNKS + c)),
            pl.BlockSpec((1, 1, ce, d), lambda i, be, nu, c=c: (layer, be[i], c, 0))]
        args = list(x_parts) + [w_gate_up, w_gate_up, w_down]
        if out is not None:
            in_specs.append(row)
            args.append(out)
        out = pl.pallas_call(
            functools.partial(_expert_kernel, part_starts=part_starts),
            grid_spec=pltpu.PrefetchScalarGridSpec(
                num_scalar_prefetch=2,
                grid=(n_rows // MOE_BLK,),
                in_specs=in_specs,
                out_specs=row,
                scratch_shapes=[pltpu.VMEM((d, ce), BF16), pltpu.VMEM((d, ce), BF16), pltpu.VMEM((ce, d), BF16),
                                pltpu.VMEM((3, MOE_BLK, d), BF16), pltpu.SemaphoreType.DMA((3,))]),
            out_shape=jax.ShapeDtypeStruct((n_rows, d), BF16),
            compiler_params=_cparams(("arbitrary",)),
            name="expert_ffn",
        )(block_expert, n_used, *args)
    return out


def _dispatch_tables(expert_idx_t):
    n_tok = expert_idx_t.shape[1]
    n_assign = n_tok * TOP_K
    e_flat = expert_idx_t.reshape(n_assign)
    onehot = e_flat[:, None] == jnp.arange(N_EXPERTS)[None, :]
    chunk = min(1024, n_assign)
    oh = onehot.astype(F32).reshape(n_assign // chunk, chunk, N_EXPERTS)
    within = jnp.einsum("ij,bjk->bik", jnp.tril(jnp.ones((chunk, chunk), F32)), oh)
    totals = within[:, -1, :]
    csum = (within + (jnp.cumsum(totals, axis=0) - totals)[:, None, :]).reshape(n_assign, N_EXPERTS)
    rank = jnp.sum(jnp.where(onehot, csum, 0.0), axis=1).astype(jnp.int32) - 1
    counts = csum[-1].astype(jnp.int32)
    padded = (counts + MOE_BLK - 1) // MOE_BLK * MOE_BLK
    pad_end = jnp.cumsum(padded)
    pad_start = pad_end - padded
    dest = pad_start[e_flat] + rank
    n_rows = n_assign + N_EXPERTS * MOE_BLK
    block_start = jnp.arange(n_rows // MOE_BLK, dtype=jnp.int32) * MOE_BLK
    block_expert = jnp.minimum(jnp.sum(pad_end[None, :] <= block_start[:, None], axis=1), N_EXPERTS - 1)
    order = jnp.sort(e_flat * n_assign + jnp.arange(n_assign, dtype=jnp.int32)) % n_assign
    row_expert = jnp.repeat(block_expert, MOE_BLK)
    row_rank = jnp.arange(n_rows, dtype=jnp.int32) - pad_start[row_expert]
    source = order[jnp.clip((jnp.cumsum(counts) - counts)[row_expert] + row_rank, 0, n_assign - 1)]
    row_tok = jnp.where(row_rank < counts[row_expert], source % n_tok, 0)
    n_used = (pad_end[-1:] // MOE_BLK).astype(jnp.int32)
    return row_tok.astype(jnp.int32), block_expert.astype(jnp.int32), n_used, dest.reshape(TOP_K, n_tok)


def _token_mixer(xb, bsz, seq, w_in_t, layer, conv_a_w, conv_a_b, lru_conv_w, lru_conv_b, lru_wa, lru_ba, lru_wx, lru_bx,
                 lru_lam, cmp_pos, phi_w1, phi_b1, phi_w2, w_branch):
    d = xb.shape[1]
    n_tok = bsz * seq
    main_w = 3 * BRANCH_W + 2 * BRANCH_W + NSA_W + 6 * KV_W
    ng = 3 * NSA_HEADS
    kv0 = main_w - 6 * KV_W
    zmain = matmul_wres(xb, w_in_t, layer, 0, main_w, BF16, 2048, 768)
    zgate = matmul_wres(xb, w_in_t, layer, main_w, LANES, F32, 2048, LANES)
    zmerge = matmul_wres(xb, w_in_t, layer, main_w + ng, w_in_t.shape[1] - main_w - ng, BF16, 2048, 512)
    z3 = zmain.reshape(bsz, seq, main_w)

    y_a = conv_mixer(z3, conv_a_w, conv_a_b)
    y_b = lru_mixer(z3, 3 * BRANCH_W, lru_conv_w, lru_conv_b, lru_wa, lru_ba, lru_wx, lru_bx, lru_lam)

    cmp = nsa_compress(z3, kv0, cmp_pos, phi_w1, phi_b1, phi_w2)
    gate_logits = zgate[:, :ng].reshape(bsz, seq, NSA_KV_HEADS, NSA_GRP * 3).transpose(0, 2, 3, 1)
    y_c = nsa_attention(z3, 5 * BRANCH_W, kv0 + 2 * KV_W, cmp[0], cmp[1], gate_logits)

    flat = lambda y: y.reshape(n_tok, BRANCH_W)
    return branch_merge(flat(y_a), flat(y_b), flat(y_c), w_branch.astype(BF16), zmerge)


def _moe(x, xb, w_router, b_router, w_gate_up, w_down, layer):
    idx_t, gate_t = router(x, w_router, b_router)
    row_tok, block_expert, n_used, dest = _dispatch_tables(idx_t)
    n_tok = xb.shape[0]
    x_parts = [xb[row_tok[a:a + n_tok]] for a in range(0, row_tok.shape[0], n_tok)]
    rows = expert_ffn(block_expert, n_used, x_parts, w_gate_up, w_down, layer)
    return rows[dest[0]], rows[dest[1]], gate_t.T


def kernel(x, p, w_in, conv_a_w, conv_a_b, lru_conv_w, lru_conv_b, lru_wa, lru_ba, lru_wx, lru_bx, lru_lam, cmp_pos, phi_w1, phi_b1, phi_w2, w_branch, w_out, ln_g, ln_b, w_router, b_router, w_gate_up, w_down, w_ple, w_ple_gate):
    bsz, seq, d = x.shape
    depth = w_in.shape[0]
    n_tok = bsz * seq
    alpha = (2 * depth) ** 0.25
    xf = x.reshape(n_tok, d)
    xb = xf.astype(BF16)
    w_in_t = jnp.swapaxes(w_in, 1, 2)
    for i in range(depth):
        merged = _token_mixer(xb, bsz, seq, w_in_t, i, conv_a_w[i], conv_a_b[i], lru_conv_w[i], lru_conv_b[i],
                              lru_wa[i], lru_ba[i], lru_wx[i], lru_bx[i], lru_lam[i], cmp_pos[i], phi_w1[i],
                              phi_b1[i], phi_w2[i], w_branch[i])
        xf, xb = outproj_ln(merged, w_out[i].astype(BF16), xf, ln_g[i, 0], ln_b[i, 0], alpha)
        y0, y1, gate = _moe(xf, xb, w_router, b_router, w_gate_up, w_down, i)
        xf, xb = ple_combine_ln(xf, xb, w_ple_gate[i].astype(BF16), p[i].reshape(n_tok, -1),
                                w_ple[i].astype(BF16), y0, y1, gate, ln_g[i, 1], ln_b[i, 1], alpha)
    return xf.reshape(bsz, seq, d)
```
